```python
import math
import jax
import jax.numpy as jnp
from jax import lax
import numpy as np

D_MODEL = 1024
BATCH = 8
SEQ = 4096
DEPTH = 2

N_MIXERS = 2
EPS = 1e-6
NEG_INF = -1e30

MIX_WIDTH = D_MODEL
TOK_WIDTH = 3 * MIX_WIDTH // 4
MEM_WIDTH = MIX_WIDTH // 4

DILATED_GROUPS = ((128, 1), (512, 4), (2048, 16))
ATT_HEAD_DIM = 64
ATT_HEADS = TOK_WIDTH // ATT_HEAD_DIM
HEADS_PER_GROUP = ATT_HEADS // len(DILATED_GROUPS)
BAND_BLOCK = 64
REL_BUCKETS = 32
REL_MAX_DIST = 1024

DN_HEAD_DIM = 128
DN_HEADS = TOK_WIDTH // DN_HEAD_DIM
DN_CONV = 5
DN_CHUNK = 64

MEM_LEN = 256
MEM_HEADS = 4
MEM_HEAD_DIM = MEM_WIDTH // MEM_HEADS

_FF_RAW = -(-8 * D_MODEL // 3)
D_FF = -(-_FF_RAW // 256) * 256

ATT_IN = 3 * TOK_WIDTH + MEM_WIDTH
DN_IN = 4 * TOK_WIDTH + 4 * DN_HEADS + MEM_WIDTH

kernel_name = "hybrid_dilated_attn_gated_deltanet_encoder"


def _rms_norm(x, gain):
    xf = x.astype(jnp.float32)
    y = xf * lax.rsqrt(jnp.mean(xf * xf, axis=-1, keepdims=True) + EPS) * gain.astype(jnp.float32)
    return y.astype(x.dtype)


def _l2norm(t):
    return t * lax.rsqrt(jnp.sum(t * t, axis=-1, keepdims=True) + EPS)


def _t5_bucket(rel):
    half = REL_BUCKETS // 2
    max_exact = half // 2
    n = np.abs(rel)
    large = max_exact + (np.log(np.maximum(n, 1) / max_exact) / math.log(REL_MAX_DIST / max_exact)
                         * (half - max_exact)).astype(np.int64)
    large = np.minimum(large, half - 1)
    return ((rel > 0) * half + np.where(n < max_exact, n, large)).astype(np.int32)


def _dilated_band_attention(q, k, v, bias, dil, half):
    b, s, h, dh = q.shape
    L = s // dil
    nb = -(-L // BAND_BLOCK)
    lp = nb * BAND_BLOCK
    n = b * dil

    def to_sub(t):
        return t.reshape(b, L, dil, h, dh).transpose(0, 2, 1, 3, 4).reshape(n, L, h, dh)

    def key_windows(t):
        t = jnp.pad(to_sub(t), ((0, 0), (BAND_BLOCK, lp - L + BAND_BLOCK), (0, 0), (0, 0)))
        t = t.reshape(n, nb + 2, BAND_BLOCK, h, dh)
        return jnp.concatenate([t[:, :-2], t[:, 1:-1], t[:, 2:]], axis=2)

    qs = jnp.pad(to_sub(q), ((0, 0), (0, lp - L), (0, 0), (0, 0))).reshape(n, nb, BAND_BLOCK, h, dh)
    kw = key_windows(k)
    vw = key_windows(v)

    rel = np.arange(3 * BAND_BLOCK)[None, :] - BAND_BLOCK - np.arange(BAND_BLOCK)[:, None]
    in_band = np.abs(rel) <= half
    kpos = (np.arange(nb)[:, None] - 1) * BAND_BLOCK + np.arange(3 * BAND_BLOCK)[None, :]
    valid = in_band[None] & ((kpos >= 0) & (kpos < L))[:, None, :]
    bias_band = jnp.transpose(bias[np.clip(rel + half, 0, 2 * half)], (2, 0, 1)).astype(jnp.float32)

    logits = jnp.einsum('nbqhd,nbkhd->nbhqk', qs, kw, preferred_element_type=jnp.float32)
    logits = jnp.where(valid[None, :, None], logits + bias_band[None, None], NEG_INF)
    m = jnp.max(logits, axis=-1)
    p = jnp.exp(logits - m[..., None])
    den = jnp.sum(p, axis=-1)
    o = jnp.einsum('nbhqk,nbkhd->nbqhd', p.astype(v.dtype), vw, preferred_element_type=jnp.float32)
    o = o / jnp.swapaxes(den, -1, -2)[..., None]

    def from_sub(t):
        rest = t.shape[3:]
        t = t.reshape((b, dil, lp) + rest)[:, :, :L]
        return jnp.swapaxes(t, 1, 2).reshape((b, s) + rest)

    return from_sub(o), from_sub(jnp.swapaxes(m, -1, -2)), from_sub(jnp.swapaxes(den, -1, -2))


def _dilated_mixture(q, k, v, rel_bias):
    b, s = q.shape[:2]
    outs, lses = [], []
    for gi, (window, dil) in enumerate(DILATED_GROUPS):
        half = window // (2 * dil)
        heads = slice(gi * HEADS_PER_GROUP, (gi + 1) * HEADS_PER_GROUP)
        bias = rel_bias[_t5_bucket(np.arange(-half, half + 1) * dil)][:, heads]
        o, m, den = _dilated_band_attention(q[:, :, heads], k[:, :, heads], v[:, :, heads], bias, dil, half)
        outs.append(o)
        lses.append(m + jnp.log(den))
    wts = jax.nn.softmax(jnp.stack(lses), axis=0)
    mixed = jnp.concatenate([o * w[..., None] for o, w in zip(outs, wts)], axis=2)
    return mixed.reshape(b, s, TOK_WIDTH)


def _memory_attention(q_mem, mem_n, w_kv):
    b, s, _ = q_mem.shape
    q = q_mem.reshape(b, s, MEM_HEADS, MEM_HEAD_DIM) * (MEM_HEAD_DIM ** -0.5)
    k, v = jnp.split(mem_n @ w_kv, 2, axis=-1)
    k = k.reshape(b, -1, MEM_HEADS, MEM_HEAD_DIM)
    v = v.reshape(b, -1, MEM_HEADS, MEM_HEAD_DIM)
    logits = jnp.einsum('bshd,bmhd->bhsm', q, k, preferred_element_type=jnp.float32)
    p = jax.nn.softmax(logits, axis=-1)
    o = jnp.einsum('bhsm,bmhd->bshd', p.astype(v.dtype), v)
    return o.reshape(b, s, MEM_WIDTH)


def _gated_delta_chunked(q, k, v, g, beta):
    b, s, h, dk = q.shape
    dv = v.shape[-1]
    nc = s // DN_CHUNK

    def chunks(t):
        t = t.reshape((b, nc, DN_CHUNK, h) + t.shape[3:])
        return jnp.moveaxis(t, 3, 1)

    q, k, v, g, beta = (chunks(t) for t in (q, k, v, g, beta))
    g = jnp.cumsum(g, axis=-1)
    tril = np.tril(np.ones((DN_CHUNK, DN_CHUNK), dtype=bool))
    strict = np.tril(np.ones((DN_CHUNK, DN_CHUNK), dtype=bool), -1)
    decay = jnp.exp(jnp.where(tril, g[..., :, None] - g[..., None, :], NEG_INF))
    k_beta = k * beta[..., None]
    lmat = jnp.where(strict, jnp.einsum('bhnik,bhnjk->bhnij', k_beta, k) * decay, 0.0)
    a = lmat + jnp.eye(DN_CHUNK, dtype=jnp.float32)
    rhs = jnp.concatenate([v * beta[..., None], k_beta * jnp.exp(g)[..., None]], axis=-1)
    sol = lax.linalg.triangular_solve(a, rhs, left_side=True, lower=True, unit_diagonal=True)
    u, w = sol[..., :dv], sol[..., dv:]
    intra = jnp.where(tril, jnp.einsum('bhnik,bhnjk->bhnij', q, k) * decay, 0.0)

    def step(state, inp):
        qc, kc, uc, wc, gc, ac = inp
        v_new = uc - jnp.einsum('bhck,bhkv->bhcv', wc, state)
        out = (jnp.einsum('bhck,bhkv->bhcv', qc * jnp.exp(gc)[..., None], state)
               + jnp.einsum('bhij,bhjv->bhiv', ac, v_new))
        g_last = gc[..., -1]
        state = (state * jnp.exp(g_last)[..., None, None]
                 + jnp.einsum('bhck,bhcv->bhkv', kc * jnp.exp(g_last[..., None] - gc)[..., None], v_new))
        return state, out

    xs = tuple(jnp.moveaxis(t, 2, 0) for t in (q, k, u, w, g, intra))
    state0 = jnp.zeros((b, h, dk, dv), jnp.float32)
    _, out = lax.scan(step, state0, xs)
    out = jnp.moveaxis(out, 0, 2)
    return jnp.moveaxis(out, 1, 3).reshape(b, s, h, dv)


def _attention_sublayer(h, mem_n, w_in, w_out, rel_bias, w_mem_kv):
    b, s, _ = h.shape
    q, k, v, q_mem = jnp.split(h @ w_in, [TOK_WIDTH, 2 * TOK_WIDTH, 3 * TOK_WIDTH], axis=-1)
    q = q.reshape(b, s, ATT_HEADS, ATT_HEAD_DIM) * (ATT_HEAD_DIM ** -0.5)
    k = k.reshape(b, s, ATT_HEADS, ATT_HEAD_DIM)
    v = v.reshape(b, s, ATT_HEADS, ATT_HEAD_DIM)
    mixed = _dilated_mixture(q, k, v, rel_bias).astype(h.dtype)
    mem_out = _memory_attention(q_mem, mem_n, w_mem_kv)
    return jnp.concatenate([mixed, mem_out], axis=-1) @ w_out


def _deltanet_sublayer(h, mem_n, w_in, conv_w, a_log, dt_bias, out_norm, w_out, w_mem_kv):
    b, s, _ = h.shape
    qkv, z, gate_in, q_mem = jnp.split(
        h @ w_in, [3 * TOK_WIDTH, 4 * TOK_WIDTH, 4 * TOK_WIDTH + 4 * DN_HEADS], axis=-1)
    qkv = lax.conv_general_dilated(
        qkv, conv_w[:, None, :], window_strides=(1,), padding=[(DN_CONV // 2, DN_CONV // 2)],
        dimension_numbers=('NWC', 'WIO', 'NWC'), feature_group_count=3 * TOK_WIDTH)
    q, k, v = jnp.split(jax.nn.silu(qkv).astype(jnp.float32), 3, axis=-1)
    q = _l2norm(q.reshape(b, s, DN_HEADS, DN_HEAD_DIM)) * (DN_HEAD_DIM ** -0.5)
    k = _l2norm(k.reshape(b, s, DN_HEADS, DN_HEAD_DIM))
    v = v.reshape(b, s, DN_HEADS, DN_HEAD_DIM)
    gate_in = gate_in.astype(jnp.float32).reshape(b, s, 2, 2, DN_HEADS)
    g = -jnp.exp(a_log.astype(jnp.float32)) * jax.nn.softplus(gate_in[:, :, :, 0] + dt_bias.astype(jnp.float32))
    beta = jax.nn.sigmoid(gate_in[:, :, :, 1])
    o_fwd = _gated_delta_chunked(q, k, v, g[:, :, 0], beta[:, :, 0])
    rev = lambda t: jnp.flip(t, axis=1)
    o_bwd = rev(_gated_delta_chunked(rev(q), rev(k), rev(v), rev(g[:, :, 1]), rev(beta[:, :, 1])))
    o = o_fwd + o_bwd
    zf = z.astype(jnp.float32).reshape(b, s, DN_HEADS, DN_HEAD_DIM)
    o = (o * lax.rsqrt(jnp.mean(o * o, axis=-1, keepdims=True) + EPS)
         * out_norm.astype(jnp.float32) * jax.nn.silu(zf))
    o = o.reshape(b, s, TOK_WIDTH).astype(h.dtype)
    mem_out = _memory_attention(q_mem, mem_n, w_mem_kv)
    return jnp.concatenate([o, mem_out], axis=-1) @ w_out


def _swiglu(h, w_gate_up, w_down):
    gate, up = jnp.split(h @ w_gate_up, 2, axis=-1)
    return (jax.nn.silu(gate) * up) @ w_down


def setup_inputs(seed: int = 0) -> dict:
    key = jax.random.key(seed)
    ks = jax.random.split(key, 24)
    n_att = (DEPTH + N_MIXERS - 1) // N_MIXERS
    n_dn = DEPTH // N_MIXERS

    def nrm(k, shape, scale):
        return jax.random.normal(k, shape, jnp.float32) * scale

    def gain(k, shape):
        return 1.0 + nrm(k, shape, 0.05)

    dt = jnp.exp(jax.random.uniform(ks[8], (n_dn, 2, DN_HEADS), jnp.float32,
                                    minval=math.log(1e-3), maxval=math.log(1e-1)))
    return {
        "x": nrm(ks[0], (BATCH, SEQ, D_MODEL), 1.0),
        "mem": nrm(ks[1], (BATCH, MEM_LEN, D_MODEL), 1.0),
        "rel_bias": nrm(ks[2], (REL_BUCKETS, ATT_HEADS), 0.5),
        "att_w_in": nrm(ks[3], (n_att, D_MODEL, ATT_IN), D_MODEL ** -0.5),
        "att_w_out": nrm(ks[4], (n_att, MIX_WIDTH, D_MODEL), MIX_WIDTH ** -0.5),
        "dn_w_in": nrm(ks[5], (n_dn, D_MODEL, DN_IN), D_MODEL ** -0.5),
        "dn_conv": nrm(ks[6], (n_dn, DN_CONV, 3 * TOK_WIDTH), DN_CONV ** -0.5),
        "dn_a_log": jnp.log(jax.random.uniform(ks[7], (n_dn, 2, DN_HEADS), jnp.float32, minval=1.0, maxval=16.0)),
        "dn_dt_bias": dt + jnp.log(-jnp.expm1(-dt)),
        "dn_out_norm": gain(ks[9], (n_dn, DN_HEAD_DIM)),
        "dn_w_out": nrm(ks[10], (n_dn, MIX_WIDTH, D_MODEL), MIX_WIDTH ** -0.5),
        "mem_norm": gain(ks[11], (DEPTH, D_MODEL)),
        "mem_w_kv": nrm(ks[12], (DEPTH, D_MODEL, 2 * MEM_WIDTH), D_MODEL ** -0.5),
        "norm_mix_pre": gain(ks[13], (DEPTH, D_MODEL)),
        "norm_mix_post": gain(ks[14], (DEPTH, D_MODEL)),
        "norm_ffn_pre": gain(ks[15], (DEPTH, D_MODEL)),
        "norm_ffn_post": gain(ks[16], (DEPTH, D_MODEL)),
        "ffn_w_gate_up": nrm(ks[17], (DEPTH, D_MODEL, 2 * D_FF), D_MODEL ** -0.5),
        "ffn_w_down": nrm(ks[18], (DEPTH, D_FF, D_MODEL), D_FF ** -0.5),
    }


def reference(x, mem, rel_bias, att_w_in, att_w_out, dn_w_in, dn_conv, dn_a_log, dn_dt_bias,
              dn_out_norm, dn_w_out, mem_norm, mem_w_kv, norm_mix_pre, norm_mix_post,
              norm_ffn_pre, norm_ffn_post, ffn_w_gate_up, ffn_w_down):
    for i in range(DEPTH):
        j = i // N_MIXERS
        h = _rms_norm(x, norm_mix_pre[i])
        mem_n = _rms_norm(mem, mem_norm[i])
        if i % N_MIXERS == 0:
            mixed = _attention_sublayer(h, mem_n, att_w_in[j], att_w_out[j], rel_bias, mem_w_kv[i])
        else:
            mixed = _deltanet_sublayer(h, mem_n, dn_w_in[j], dn_conv[j], dn_a_log[j], dn_dt_bias[j],
                                       dn_out_norm[j], dn_w_out[j], mem_w_kv[i])
        x = x + _rms_norm(mixed, norm_mix_post[i])
        h = _rms_norm(x, norm_ffn_pre[i])
        x = x + _rms_norm(_swiglu(h, ffn_w_gate_up[i], ffn_w_down[i]), norm_ffn_post[i])
    return x
```

```python
import functools
import math

import jax
import jax.numpy as jnp
import numpy as np
from jax import lax
from jax.experimental import pallas as pl
from jax.experimental.pallas import tpu as pltpu

F32 = jnp.float32
BF16 = jnp.bfloat16

EPS = 1e-6
NEG_INF = -1e30

LANES = 128
VMEM_LIMIT_BYTES = 56 * 1024 * 1024

DILATED_GROUPS = ((128, 1), (512, 4), (2048, 16))
ATT_HEAD_DIM = 64
HEADS_PER_GROUP = 4
GROUP_WIDTH = HEADS_PER_GROUP * ATT_HEAD_DIM
BAND_HALF = 64
REL_BUCKETS = 32
REL_MAX_DIST = 1024
MEM_HEADS = 4
MEM_HEAD_DIM = 64
DN_HEAD_DIM = 128
DN_CONV = 5
DN_CHUNK = 128
SOLVE_LEAF = 16

ATT_Q_BLOCK = 128
ATT_K_WINDOW = ATT_Q_BLOCK + 2 * BAND_HALF


def _params(*semantics):
    return pltpu.CompilerParams(dimension_semantics=semantics, vmem_limit_bytes=VMEM_LIMIT_BYTES)


def _rms(x, gain):
    return x * lax.rsqrt(jnp.mean(x * x, axis=-1, keepdims=True) + EPS) * gain


def _silu(x):
    return x / (1.0 + jnp.exp(-x))


def _const_spec(shape):
    return pl.BlockSpec(shape, lambda *_: (0,) * len(shape))


def _norm_proj_kernel(x_ref, g_ref, *refs, n_out):
    h = _rms(x_ref[...], g_ref[...]).astype(BF16)
    for w_ref, o_ref in zip(refs[:n_out], refs[n_out:]):
        o_ref[...] = jnp.dot(h, w_ref[...], preferred_element_type=F32).astype(o_ref.dtype)


def _norm_proj(x2, gain, weights, out_dtypes, tm=512):
    t, d = x2.shape
    n_out = len(weights)
    return pl.pallas_call(
        functools.partial(_norm_proj_kernel, n_out=n_out),
        grid=(t // tm,),
        in_specs=[pl.BlockSpec((tm, d), lambda i: (i, 0)), _const_spec((1, d))]
        + [_const_spec(w.shape) for w in weights],
        out_specs=[pl.BlockSpec((tm, w.shape[1]), lambda i: (i, 0)) for w in weights],
        out_shape=[jax.ShapeDtypeStruct((t, w.shape[1]), dt) for w, dt in zip(weights, out_dtypes)],
        compiler_params=_params("parallel"),
        name="norm_proj",
    )(x2, gain.reshape(1, d), *weights)


def _mem_kv_kernel(mem_ref, g_ref, w_ref, k_ref, v_ref):
    h = _rms(mem_ref[0], g_ref[...]).astype(BF16)
    kv = jnp.dot(h, w_ref[...], preferred_element_type=F32)
    width = k_ref.shape[-1]
    k_ref[0] = kv[:, :width].astype(BF16)
    v_ref[0] = kv[:, width:].astype(BF16)


def _mem_kv(mem, gain, w_kv):
    b, m, d = mem.shape
    width = w_kv.shape[1] // 2
    return pl.pallas_call(
        _mem_kv_kernel,
        grid=(b,),
        in_specs=[pl.BlockSpec((1, m, d), lambda i: (i, 0, 0)), _const_spec((1, d)), _const_spec(w_kv.shape)],
        out_specs=[pl.BlockSpec((1, m, width), lambda i: (i, 0, 0))] * 2,
        out_shape=[jax.ShapeDtypeStruct((b, m, width), BF16)] * 2,
        compiler_params=_params("parallel"),
        name="mem_kv",
    )(mem, gain.reshape(1, d), w_kv)


def _mem_attn_kernel(q_ref, k_ref, v_ref, o_ref):
    q = q_ref[...]
    k = k_ref[0]
    v = v_ref[0]
    lane = lax.broadcasted_iota(jnp.int32, q.shape, 1)
    out = jnp.zeros(q.shape, F32)
    for h in range(MEM_HEADS):
        in_head = (lane >= h * MEM_HEAD_DIM) & (lane < (h + 1) * MEM_HEAD_DIM)
        qh = jnp.where(in_head, q, jnp.zeros_like(q))
        s = lax.dot_general(qh, k, (((1,), (1,)), ((), ())), preferred_element_type=F32)
        s = s * (MEM_HEAD_DIM ** -0.5)
        m = jnp.max(s, axis=-1, keepdims=True)
        p = jnp.exp(s - m)
        den = jnp.sum(p, axis=-1, keepdims=True)
        pv = jnp.dot((p / den).astype(BF16), v, preferred_element_type=F32)
        out = jnp.where(in_head, pv, out)
    o_ref[...] = out.astype(o_ref.dtype)


def _mem_attn(q_mem, k, v, seq, tq=512):
    t, width = q_mem.shape
    m = k.shape[1]
    per_seq = seq // tq
    return pl.pallas_call(
        _mem_attn_kernel,
        grid=(t // tq,),
        in_specs=[pl.BlockSpec((tq, width), lambda i: (i, 0)),
                  pl.BlockSpec((1, m, width), lambda i: (i // per_seq, 0, 0)),
                  pl.BlockSpec((1, m, width), lambda i: (i // per_seq, 0, 0))],
        out_specs=pl.BlockSpec((tq, width), lambda i: (i, 0)),
        out_shape=jax.ShapeDtypeStruct((t, width), BF16),
        compiler_params=_params("parallel"),
        name="mem_attn",
    )(q_mem, k, v)


def _t5_bucket(rel):
    half = REL_BUCKETS // 2
    max_exact = half // 2
    n = np.abs(rel)
    large = max_exact + (np.log(np.maximum(n, 1) / max_exact) / math.log(REL_MAX_DIST / max_exact)
                         * (half - max_exact)).astype(np.int64)
    large = np.minimum(large, half - 1)
    return ((rel > 0) * half + np.where(n < max_exact, n, large)).astype(np.int32)


def _band_bias(rel_bias, group, dil):
    qi = np.arange(ATT_Q_BLOCK)[:, None]
    kj = np.arange(ATT_K_WINDOW)[None, :]
    tables = []
    for variant in range(3):
        rel = kj - qi - variant * BAND_HALF
        in_band = np.abs(rel) <= BAND_HALF
        bucket = _t5_bucket(np.clip(rel, -BAND_HALF, BAND_HALF) * dil)
        heads = slice(group * HEADS_PER_GROUP, (group + 1) * HEADS_PER_GROUP)
        vals = jnp.transpose(rel_bias[bucket][:, :, heads], (2, 0, 1)).astype(F32)
        tables.append(jnp.where(in_band[None], vals, NEG_INF))
    return jnp.stack(tables)


def _band_attn_kernel(q_ref, k_ref, v_ref, bias_ref, o_ref, lse_ref, *, sub_len, q_tile):
    t = pl.program_id(2)
    lane = lax.broadcasted_iota(jnp.int32, (ATT_Q_BLOCK, LANES), 1)
    low_half = lane < ATT_HEAD_DIM
    for blk in range(q_tile // ATT_Q_BLOCK):
        i0 = t * q_tile + blk * ATT_Q_BLOCK
        ws = jnp.clip(i0 - BAND_HALF, 0, sub_len - ATT_K_WINDOW)
        variant = (i0 - ws) // BAND_HALF
        ws = pl.multiple_of(ws, BAND_HALF)
        q_blk = q_ref[0, pl.ds(blk * ATT_Q_BLOCK, ATT_Q_BLOCK), :]
        k_win = k_ref[0, pl.ds(ws, ATT_K_WINDOW), :]
        v_win = v_ref[0, pl.ds(ws, ATT_K_WINDOW), :]
        for pair in range(GROUP_WIDTH // LANES):
            cols = slice(pair * LANES, (pair + 1) * LANES)
            q_pair, k_pair, v_pair = q_blk[:, cols], k_win[:, cols], v_win[:, cols]
            outs, lses = [], []
            for a in range(2):
                keep = low_half if a == 0 else jnp.logical_not(low_half)
                qh = jnp.where(keep, q_pair, jnp.zeros_like(q_pair))
                s = lax.dot_general(qh, k_pair, (((1,), (1,)), ((), ())), preferred_element_type=F32)
                s = s * (ATT_HEAD_DIM ** -0.5) + bias_ref[variant, 2 * pair + a]
                m = jnp.max(s, axis=-1, keepdims=True)
                p = jnp.exp(s - m)
                den = jnp.sum(p, axis=-1, keepdims=True)
                pv = jnp.dot(p.astype(BF16), v_pair, preferred_element_type=F32)
                outs.append(pv / den)
                lses.append(jnp.broadcast_to(m + jnp.log(den), (ATT_Q_BLOCK, LANES)))
            rows = pl.ds(blk * ATT_Q_BLOCK, ATT_Q_BLOCK)
            o_ref[0, rows, cols] = jnp.where(low_half, outs[0], outs[1])
            lse_ref[0, rows, cols] = jnp.where(low_half, lses[0], lses[1])


def _band_attn(q, k, v, bias, batch, seq, group, dil, q_tile=256):
    t, width = q.shape
    sub_len = seq // dil
    n_groups = width // GROUP_WIDTH

    def view(a):
        return a.reshape(batch, sub_len, dil * a.shape[1])

    q_spec = pl.BlockSpec((1, q_tile, GROUP_WIDTH), lambda b, r, i: (b, i, r * n_groups + group))
    kv_spec = pl.BlockSpec((1, sub_len, GROUP_WIDTH), lambda b, r, i: (b, 0, r * n_groups + group))
    o_spec = pl.BlockSpec((1, q_tile, GROUP_WIDTH), lambda b, r, i: (b, i, r))
    o, lse = pl.pallas_call(
        functools.partial(_band_attn_kernel, sub_len=sub_len, q_tile=q_tile),
        grid=(batch, dil, sub_len // q_tile),
        in_specs=[q_spec, kv_spec, kv_spec, _const_spec(bias.shape)],
        out_specs=[o_spec, o_spec],
        out_shape=[jax.ShapeDtypeStruct((batch, sub_len, dil * GROUP_WIDTH), F32)] * 2,
        compiler_params=_params("parallel", "parallel", "arbitrary"),
        name=f"band_attn_g{group}",
    )(view(q), view(k), view(v), bias)
    return o.reshape(t, GROUP_WIDTH), lse.reshape(t, GROUP_WIDTH)


def _att_out_kernel(o0_ref, o1_ref, o2_ref, l0_ref, l1_ref, l2_ref, mem_ref, x_ref, w_ref, g_ref, out_ref):
    lses = [l0_ref[...], l1_ref[...], l2_ref[...]]
    mx = jnp.maximum(jnp.maximum(lses[0], lses[1]), lses[2])
    es = [jnp.exp(l - mx) for l in lses]
    tot = es[0] + es[1] + es[2]
    parts = [(o_ref[...] * (e / tot)).astype(BF16) for o_ref, e in zip((o0_ref, o1_ref, o2_ref), es)]
    mixed = jnp.concatenate(parts + [mem_ref[...]], axis=-1)
    y = jnp.dot(mixed, w_ref[...], preferred_element_type=F32)
    out_ref[...] = x_ref[...] + _rms(y, g_ref[...])


def _att_out(os_, lses, mem_out, x2, w_out, gain, tm=512):
    t, d = x2.shape
    gw = os_[0].shape[1]
    row = lambda i: (i, 0)
    return pl.pallas_call(
        _att_out_kernel,
        grid=(t // tm,),
        in_specs=[pl.BlockSpec((tm, gw), row)] * 6
        + [pl.BlockSpec((tm, mem_out.shape[1]), row), pl.BlockSpec((tm, d), row),
           _const_spec(w_out.shape), _const_spec((1, d))],
        out_specs=pl.BlockSpec((tm, d), row),
        out_shape=jax.ShapeDtypeStruct((t, d), F32),
        compiler_params=_params("parallel"),
        name="att_out",
    )(*os_, *lses, mem_out, x2, w_out, gain.reshape(1, d))


def _ffn_kernel(x_ref, gpre_ref, wg_ref, wu_ref, wd_ref, gpost_ref, out_ref, *, n_chunks):
    x = x_ref[...]
    h = _rms(x, gpre_ref[...]).astype(BF16)
    d_ff = wg_ref.shape[1]
    step = d_ff // n_chunks
    y = jnp.zeros(x.shape, F32)
    for c in range(n_chunks):
        cols = slice(c * step, (c + 1) * step)
        gate = jnp.dot(h, wg_ref[:, cols], preferred_element_type=F32)
        up = jnp.dot(h, wu_ref[:, cols], preferred_element_type=F32)
        act = (_silu(gate) * up).astype(BF16)
        y = y + jnp.dot(act, wd_ref[cols, :], preferred_element_type=F32)
    out_ref[...] = x + _rms(y, gpost_ref[...])


def _ffn(x2, g_pre, w_gate, w_up, w_down, g_post, tm=512, n_chunks=2):
    t, d = x2.shape
    row = lambda i: (i, 0)
    return pl.pallas_call(
        functools.partial(_ffn_kernel, n_chunks=n_chunks),
        grid=(t // tm,),
        in_specs=[pl.BlockSpec((tm, d), row), _const_spec((1, d)), _const_spec(w_gate.shape),
                  _const_spec(w_up.shape), _const_spec(w_down.shape), _const_spec((1, d))],
        out_specs=pl.BlockSpec((tm, d), row),
        out_shape=jax.ShapeDtypeStruct((t, d), F32),
        compiler_params=_params("parallel"),
        name="ffn",
    )(x2, g_pre.reshape(1, d), w_gate, w_up, w_down, g_post.reshape(1, d))


def _dn_prep_kernel(cur_ref, prev_ref, next_ref, conv_ref, gate_ref, alog_ref, dtb_ref,
                    q_ref, k_ref, v_ref, gb_ref, *, halo, n_dir_heads):
    i = pl.program_id(1)
    last = pl.num_programs(1) - 1
    ts = cur_ref.shape[0]
    width = cur_ref.shape[1]
    tok = width // 3
    n_ext = ts + 2 * halo
    for c0 in range(0, width, LANES):
        cols = slice(c0, c0 + LANES)
        prev = jnp.where(i > 0, prev_ref[:, cols].astype(F32), 0.0)
        nxt = jnp.where(i < last, next_ref[:, cols].astype(F32), 0.0)
        ext = jnp.concatenate([prev, cur_ref[:, cols].astype(F32), nxt], axis=0)
        acc = jnp.zeros((ts, LANES), F32)
        for j in range(DN_CONV):
            shift = (DN_CONV // 2 - j) % n_ext
            shifted = ext if shift == 0 else pltpu.roll(ext, shift, 0)
            acc = acc + shifted[halo:halo + ts] * conv_ref[j:j + 1, cols]
        y = _silu(acc)
        which, col = divmod(c0, tok)
        if which == 2:
            v_ref[:, col:col + LANES] = y.astype(BF16)
        else:
            yn = y * lax.rsqrt(jnp.sum(y * y, axis=-1, keepdims=True) + EPS)
            if which == 0:
                q_ref[:, col:col + LANES] = (yn * (DN_HEAD_DIM ** -0.5)).astype(BF16)
            else:
                k_ref[:, col:col + LANES] = yn.astype(BF16)
    gate = gate_ref[...]
    z = gate + dtb_ref[...]
    softplus = jnp.maximum(z, 0.0) + jnp.log(1.0 + jnp.exp(-jnp.abs(z)))
    decay = -jnp.exp(alog_ref[...]) * softplus
    beta = 1.0 / (1.0 + jnp.exp(-gate))
    lane = lax.broadcasted_iota(jnp.int32, gate.shape, 1)
    gb_ref[...] = jnp.where(lane < n_dir_heads, decay, beta)


def _dn_prep(qkv, gate, conv_w, a_log_row, dt_bias_row, batch, seq, n_dir_heads, ts=512, halo=16):
    t, width = qkv.shape
    tok = width // 3
    per_seq = seq // ts
    blocks_per_tile = ts // halo
    n_halo_blocks = t // halo

    def cur_map(b, i):
        return (b * per_seq + i, 0)

    def prev_map(b, i):
        return (jnp.maximum((b * per_seq + i) * blocks_per_tile - 1, 0), 0)

    def next_map(b, i):
        return (jnp.minimum((b * per_seq + i + 1) * blocks_per_tile, n_halo_blocks - 1), 0)

    return pl.pallas_call(
        functools.partial(_dn_prep_kernel, halo=halo, n_dir_heads=n_dir_heads),
        grid=(batch, per_seq),
        in_specs=[pl.BlockSpec((ts, width), cur_map), pl.BlockSpec((halo, width), prev_map),
                  pl.BlockSpec((halo, width), next_map), _const_spec(conv_w.shape),
                  pl.BlockSpec((ts, LANES), cur_map), _const_spec((1, LANES)), _const_spec((1, LANES))],
        out_specs=[pl.BlockSpec((ts, tok), cur_map)] * 3 + [pl.BlockSpec((ts, LANES), cur_map)],
        out_shape=[jax.ShapeDtypeStruct((t, tok), BF16)] * 3 + [jax.ShapeDtypeStruct((t, LANES), F32)],
        compiler_params=_params("parallel", "arbitrary"),
        name="dn_prep",
    )(qkv, qkv, qkv, conv_w, gate, a_log_row, dt_bias_row)


def _solve_unit_triangular(neg_l, rhs, same_block):
    n = neg_l.shape[0]
    ii = lax.broadcasted_iota(jnp.int32, (n, n), 0)
    jj = lax.broadcasted_iota(jnp.int32, (n, n), 1)
    x = jnp.where(same_block[SOLVE_LEAF], neg_l, 0.0)
    inv = jnp.where(ii == jj, 1.0, x)
    xb = x.astype(BF16)
    power = 1
    while 2 * power < SOLVE_LEAF:
        xb = jnp.dot(xb, xb, preferred_element_type=F32).astype(BF16)
        inv = inv + jnp.dot(xb, inv.astype(BF16), preferred_element_type=F32)
        power *= 2
    size = SOLVE_LEAF
    while size < n:
        off_diag = jnp.where(same_block[2 * size] & jnp.logical_not(same_block[size]), neg_l, 0.0)
        inv_b = inv.astype(BF16)
        right = jnp.dot(off_diag.astype(BF16), inv_b, preferred_element_type=F32)
        inv = inv + jnp.dot(inv_b, right.astype(BF16), preferred_element_type=F32)
        size *= 2
    return jnp.dot(inv.astype(BF16), rhs.astype(BF16), preferred_element_type=F32)


def _delta_chunk(q, k, v, g_col, beta_col, g_row, state, reverse, same_block):
    c = q.shape[0]
    ii = lax.broadcasted_iota(jnp.int32, (c, c), 0)
    jj = lax.broadcasted_iota(jnp.int32, (c, c), 1)
    earlier_eq = (jj >= ii) if reverse else (jj <= ii)
    earlier = (jj > ii) if reverse else (jj < ii)
    g_last = g_col[0:1] if reverse else g_col[c - 1:c]
    decay = jnp.exp(jnp.where(earlier_eq, g_col - g_row, NEG_INF))
    gram = lax.dot_general(jnp.concatenate([k, q], axis=0), k, (((1,), (1,)), ((), ())),
                           preferred_element_type=F32)
    kk, qk = gram[:c], gram[c:]
    neg_l = jnp.where(earlier, kk * decay, 0.0) * (-beta_col)
    intra = jnp.where(earlier_eq, qk * decay, 0.0).astype(BF16)
    kf = k.astype(F32)
    e_g = jnp.exp(g_col)
    rhs = jnp.concatenate([v.astype(F32) * beta_col, kf * (beta_col * e_g)], axis=-1)
    sol = _solve_unit_triangular(neg_l, rhs, same_block)
    d = v.shape[1]
    u, w = sol[:, :d], sol[:, d:]
    qg = (q.astype(F32) * e_g).astype(BF16)
    ws = jnp.dot(jnp.concatenate([w.astype(BF16), qg], axis=0), state.astype(BF16),
                 preferred_element_type=F32)
    v_new = u - ws[:c]
    v_new_b = v_new.astype(BF16)
    out = ws[c:] + jnp.dot(intra, v_new_b, preferred_element_type=F32)
    k_dec = (kf * jnp.exp(g_last - g_col)).astype(BF16)
    new_state = state * jnp.exp(g_last) + lax.dot_general(
        k_dec, v_new_b, (((0,), (0,)), ((), ())), preferred_element_type=F32)
    return out, new_state


def _dn_core_kernel(qf_ref, kf_ref, vf_ref, gbf_ref, gtf_ref, qb_ref, kb_ref, vb_ref, gbb_ref, gtb_ref,
                    of_ref, ob_ref, state_ref, *, heads_per_step, n_heads):
    @pl.when(pl.program_id(2) == 0)
    def _():
        state_ref[...] = jnp.zeros_like(state_ref)

    hb = pl.program_id(1)
    c = qf_ref.shape[0]
    ii = lax.broadcasted_iota(jnp.int32, (c, c), 0)
    jj = lax.broadcasted_iota(jnp.int32, (c, c), 1)
    lower = (jj <= ii).astype(F32)
    upper = (jj >= ii).astype(F32)
    hi = lax.Precision.HIGHEST
    same_block = {}
    size = SOLVE_LEAF
    while size <= c:
        shift = int(math.log2(size))
        same_block[size] = (ii >> shift) == (jj >> shift)
        size *= 2
    for direction, (q_ref, k_ref, v_ref, gb_ref, gt_ref, o_ref) in enumerate(
            ((qf_ref, kf_ref, vf_ref, gbf_ref, gtf_ref, of_ref),
             (qb_ref, kb_ref, vb_ref, gbb_ref, gtb_ref, ob_ref))):
        reverse = direction == 1
        gb = gb_ref[...]
        gt = gt_ref[...]
        cum_col = jnp.dot(upper if reverse else lower, gb, precision=hi, preferred_element_type=F32)
        cum_row = jnp.dot(gt, lower if reverse else upper, precision=hi, preferred_element_type=F32)
        lane = lax.broadcasted_iota(jnp.int32, gb.shape, 1)
        sub = lax.broadcasted_iota(jnp.int32, gt.shape, 0)
        for hl in range(heads_per_step):
            head = hb * heads_per_step + hl
            g_idx = direction * n_heads + head
            b_idx = 2 * n_heads + g_idx
            g_col = jnp.sum(jnp.where(lane == g_idx, cum_col, 0.0), axis=-1, keepdims=True)
            beta_col = jnp.sum(jnp.where(lane == b_idx, gb, 0.0), axis=-1, keepdims=True)
            g_row = jnp.sum(jnp.where(sub == g_idx, cum_row, 0.0), axis=0, keepdims=True)
            cols = slice(hl * DN_HEAD_DIM, (hl + 1) * DN_HEAD_DIM)
            out, new_state = _delta_chunk(q_ref[:, cols], k_ref[:, cols], v_ref[:, cols],
                                          g_col, beta_col, g_row, state_ref[direction, hl], reverse, same_block)
            o_ref[:, cols] = out
            state_ref[direction, hl] = new_state


def _dn_core(q, k, v, gb, gbt, batch, seq, heads_per_step=2):
    t, tok = q.shape
    n_heads = tok // DN_HEAD_DIM
    nc = seq // DN_CHUNK
    width = heads_per_step * DN_HEAD_DIM
    n_rows = gbt.shape[0]

    fwd = lambda b, h, i: (b * nc + i, h)
    bwd = lambda b, h, i: (b * nc + nc - 1 - i, h)
    fwd_g = lambda b, h, i: (b * nc + i, 0)
    bwd_g = lambda b, h, i: (b * nc + nc - 1 - i, 0)
    fwd_t = lambda b, h, i: (0, b * nc + i)
    bwd_t = lambda b, h, i: (0, b * nc + nc - 1 - i)

    def side(tok_map, g_map, t_map):
        return [pl.BlockSpec((DN_CHUNK, width), tok_map)] * 3 + [
            pl.BlockSpec((DN_CHUNK, LANES), g_map), pl.BlockSpec((n_rows, DN_CHUNK), t_map)]

    return pl.pallas_call(
        functools.partial(_dn_core_kernel, heads_per_step=heads_per_step, n_heads=n_heads),
        grid=(batch, n_heads // heads_per_step, nc),
        in_specs=side(fwd, fwd_g, fwd_t) + side(bwd, bwd_g, bwd_t),
        out_specs=[pl.BlockSpec((DN_CHUNK, width), fwd), pl.BlockSpec((DN_CHUNK, width), bwd)],
        out_shape=[jax.ShapeDtypeStruct((t, tok), F32)] * 2,
        scratch_shapes=[pltpu.VMEM((2, heads_per_step, DN_HEAD_DIM, DN_HEAD_DIM), F32)],
        compiler_params=_params("parallel", "parallel", "arbitrary"),
        name="dn_core",
    )(q, k, v, gb, gbt, q, k, v, gb, gbt)


def _dn_out_kernel(of_ref, ob_ref, z_ref, mem_ref, x_ref, onorm_ref, w_ref, g_ref, out_ref):
    tok = of_ref.shape[1]
    parts = []
    for c0 in range(0, tok, DN_HEAD_DIM):
        cols = slice(c0, c0 + DN_HEAD_DIM)
        o = of_ref[:, cols] + ob_ref[:, cols]
        o = o * lax.rsqrt(jnp.mean(o * o, axis=-1, keepdims=True) + EPS) * onorm_ref[...]
        parts.append((o * _silu(z_ref[:, cols].astype(F32))).astype(BF16))
    mixed = jnp.concatenate(parts + [mem_ref[...]], axis=-1)
    y = jnp.dot(mixed, w_ref[...], preferred_element_type=F32)
    out_ref[...] = x_ref[...] + _rms(y, g_ref[...])


def _dn_out(o_f, o_b, z, mem_out, x2, out_norm, w_out, gain, tm=512):
    t, d = x2.shape
    tok = o_f.shape[1]
    row = lambda i: (i, 0)
    return pl.pallas_call(
        _dn_out_kernel,
        grid=(t // tm,),
        in_specs=[pl.BlockSpec((tm, tok), row)] * 3
        + [pl.BlockSpec((tm, mem_out.shape[1]), row), pl.BlockSpec((tm, d), row),
           _const_spec((1, DN_HEAD_DIM)), _const_spec(w_out.shape), _const_spec((1, d))],
        out_specs=pl.BlockSpec((tm, d), row),
        out_shape=jax.ShapeDtypeStruct((t, d), F32),
        compiler_params=_params("parallel"),
        name="dn_out",
    )(o_f, o_b, z, mem_out, x2, out_norm.reshape(1, DN_HEAD_DIM), w_out, gain.reshape(1, d))


def _memory_branch(q_mem, mem, mem_gain, w_kv, seq):
    k_mem, v_mem = _mem_kv(mem, mem_gain, w_kv.astype(BF16))
    return _mem_attn(q_mem, k_mem, v_mem, seq)


def _attention_layer(x2, mem, batch, seq, rel_bias, w_in, w_out, mem_gain, w_kv, g_pre, g_post):
    tok = len(DILATED_GROUPS) * GROUP_WIDTH
    w_in = w_in.astype(BF16)
    weights = [w_in[:, :tok], w_in[:, tok:2 * tok], w_in[:, 2 * tok:3 * tok], w_in[:, 3 * tok:]]
    q, k, v, q_mem = _norm_proj(x2, g_pre, weights, [BF16] * 4)
    os_, lses = [], []
    for group, (_, dil) in enumerate(DILATED_GROUPS):
        o, lse = _band_attn(q, k, v, _band_bias(rel_bias, group, dil), batch, seq, group, dil)
        os_.append(o)
        lses.append(lse)
    mem_out = _memory_branch(q_mem, mem, mem_gain, w_kv, seq)
    return _att_out(os_, lses, mem_out, x2, w_out.astype(BF16), g_post)


def _deltanet_layer(x2, mem, batch, seq, w_in, conv_w, a_log, dt_bias, out_norm, w_out, mem_gain, w_kv,
                    g_pre, g_post):
    d = x2.shape[1]
    n_heads = a_log.shape[1]
    tok = n_heads * DN_HEAD_DIM
    n_gate = 4 * n_heads
    w_gate = w_in[:, 4 * tok:4 * tok + n_gate].reshape(d, 2, 2, n_heads).transpose(0, 2, 1, 3).reshape(d, n_gate)
    w_gate = jnp.pad(w_gate, ((0, 0), (0, LANES - n_gate)))
    weights = [w_in[:, :3 * tok].astype(BF16), w_in[:, 3 * tok:4 * tok].astype(BF16), w_gate.astype(BF16),
               w_in[:, 4 * tok + n_gate:].astype(BF16)]
    qkv, z, gate, q_mem = _norm_proj(x2, g_pre, weights, [BF16, BF16, F32, BF16])
    pad_row = lambda p: jnp.pad(p.reshape(1, -1).astype(F32), ((0, 0), (0, LANES - 2 * n_heads)))
    q, k, v, gb = _dn_prep(qkv, gate, conv_w.astype(F32), pad_row(a_log), pad_row(dt_bias), batch, seq,
                           2 * n_heads)
    gbt = jnp.transpose(gb[:, :8 * ((n_gate + 7) // 8)])
    o_f, o_b = _dn_core(q, k, v, gb, gbt, batch, seq)
    mem_out = _memory_branch(q_mem, mem, mem_gain, w_kv, seq)
    return _dn_out(o_f, o_b, z, mem_out, x2, out_norm, w_out.astype(BF16), g_post)


def kernel(x, mem, rel_bias, att_w_in, att_w_out, dn_w_in, dn_conv, dn_a_log, dn_dt_bias, dn_out_norm,
           dn_w_out, mem_norm, mem_w_kv, norm_mix_pre, norm_mix_post, norm_ffn_pre, norm_ffn_post,
           ffn_w_gate_up, ffn_w_down):
    batch, seq, d = x.shape
    depth = norm_mix_pre.shape[0]
    n_mixers = 2
    x2 = x.reshape(batch * seq, d)
    for i in range(depth):
        j = i // n_mixers
        if i % n_mixers == 0:
            x2 = _attention_layer(x2, mem, batch, seq, rel_bias, att_w_in[j], att_w_out[j], mem_norm[i],
                                  mem_w_kv[i], norm_mix_pre[i], norm_mix_post[i])
        else:
            x2 = _deltanet_layer(x2, mem, batch, seq, dn_w_in[j], dn_conv[j], dn_a_log[j], dn_dt_bias[j],
                                 dn_out_norm[j], dn_w_out[j], mem_norm[i], mem_w_kv[i], norm_mix_pre[i],
                                 norm_mix_post[i])
        d_ff = ffn_w_down.shape[1]
        w_gu = ffn_w_gate_up[i].astype(BF16)
        x2 = _ffn(x2, norm_ffn_pre[i], w_gu[:, :d_ff], w_gu[:, d_ff:], ffn_w_down[i].astype(BF16),
                  norm_ffn_post[i])
    return x2.reshape(batch, seq, d)
```

```python
import functools
import math

import jax
import jax.numpy as jnp
import numpy as np
from jax import lax
from jax.experimental import pallas as pl
from jax.experimental.pallas import tpu as pltpu

F32 = jnp.float32
BF16 = jnp.bfloat16

EPS = 1e-6
NEG_INF = -1e30

LANES = 128
VMEM_LIMIT_BYTES = 56 * 1024 * 1024

DILATED_GROUPS = ((128, 1), (512, 4), (2048, 16))
ATT_HEAD_DIM = 64
HEADS_PER_GROUP = 4
GROUP_WIDTH = HEADS_PER_GROUP * ATT_HEAD_DIM
BAND_HALF = 64
REL_BUCKETS = 32
REL_MAX_DIST = 1024
MEM_HEADS = 4
MEM_HEAD_DIM = 64
DN_HEAD_DIM = 128
DN_CONV = 5
DN_CHUNK = 128
SOLVE_LEAF = 16

ATT_Q_BLOCK = 128
ATT_K_WINDOW = ATT_Q_BLOCK + 2 * BAND_HALF


def _params(*semantics):
    return pltpu.CompilerParams(dimension_semantics=semantics, vmem_limit_bytes=VMEM_LIMIT_BYTES)


def _rms(x, gain):
    return x * lax.rsqrt(jnp.mean(x * x, axis=-1, keepdims=True) + EPS) * gain


def _silu(x):
    return x / (1.0 + jnp.exp(-x))


def _const_spec(shape):
    return pl.BlockSpec(shape, lambda *_: (0,) * len(shape))


def _norm_proj_kernel(x_ref, g_ref, *refs, n_out):
    h = _rms(x_ref[...], g_ref[...]).astype(BF16)
    for w_ref, o_ref in zip(refs[:n_out], refs[n_out:]):
        o_ref[...] = jnp.dot(h, w_ref[...], preferred_element_type=F32).astype(o_ref.dtype)


def _norm_proj(x2, gain, weights, out_dtypes, tm=512):
    t, d = x2.shape
    n_out = len(weights)
    return pl.pallas_call(
        functools.partial(_norm_proj_kernel, n_out=n_out),
        grid=(t // tm,),
        in_specs=[pl.BlockSpec((tm, d), lambda i: (i, 0)), _const_spec((1, d))]
        + [_const_spec(w.shape) for w in weights],
        out_specs=[pl.BlockSpec((tm, w.shape[1]), lambda i: (i, 0)) for w in weights],
        out_shape=[jax.ShapeDtypeStruct((t, w.shape[1]), dt) for w, dt in zip(weights, out_dtypes)],
        compiler_params=_params("parallel"),
        name="norm_proj",
    )(x2, gain.reshape(1, d), *weights)


def _mem_kv_kernel(mem_ref, g_ref, w_ref, k_ref, v_ref):
    h = _rms(mem_ref[0], g_ref[...]).astype(BF16)
    kv = jnp.dot(h, w_ref[...], preferred_element_type=F32)
    width = k_ref.shape[-1]
    k_ref[0] = kv[:, :width].astype(BF16)
    v_ref[0] = kv[:, width:].astype(BF16)


def _mem_kv(mem, gain, w_kv):
    b, m, d = mem.shape
    width = w_kv.shape[1] // 2
    return pl.pallas_call(
        _mem_kv_kernel,
        grid=(b,),
        in_specs=[pl.BlockSpec((1, m, d), lambda i: (i, 0, 0)), _const_spec((1, d)), _const_spec(w_kv.shape)],
        out_specs=[pl.BlockSpec((1, m, width), lambda i: (i, 0, 0))] * 2,
        out_shape=[jax.ShapeDtypeStruct((b, m, width), BF16)] * 2,
        compiler_params=_params("parallel"),
        name="mem_kv",
    )(mem, gain.reshape(1, d), w_kv)


def _mem_attn_kernel(q_ref, k_ref, v_ref, o_ref):
    q = q_ref[...]
    k = k_ref[0]
    v = v_ref[0]
    lane = lax.broadcasted_iota(jnp.int32, q.shape, 1)
    out = jnp.zeros(q.shape, F32)
    for h in range(MEM_HEADS):
        in_head = (lane >= h * MEM_HEAD_DIM) & (lane < (h + 1) * MEM_HEAD_DIM)
        qh = jnp.where(in_head, q, jnp.zeros_like(q))
        s = lax.dot_general(qh, k, (((1,), (1,)), ((), ())), preferred_element_type=F32)
        s = s * (MEM_HEAD_DIM ** -0.5)
        m = jnp.max(s, axis=-1, keepdims=True)
        p = jnp.exp(s - m)
        den = jnp.sum(p, axis=-1, keepdims=True)
        pv = jnp.dot((p / den).astype(BF16), v, preferred_element_type=F32)
        out = jnp.where(in_head, pv, out)
    o_ref[...] = out.astype(o_ref.dtype)


def _mem_attn(q_mem, k, v, seq, tq=512):
    t, width = q_mem.shape
    m = k.shape[1]
    per_seq = seq // tq
    return pl.pallas_call(
        _mem_attn_kernel,
        grid=(t // tq,),
        in_specs=[pl.BlockSpec((tq, width), lambda i: (i, 0)),
                  pl.BlockSpec((1, m, width), lambda i: (i // per_seq, 0, 0)),
                  pl.BlockSpec((1, m, width), lambda i: (i // per_seq, 0, 0))],
        out_specs=pl.BlockSpec((tq, width), lambda i: (i, 0)),
        out_shape=jax.ShapeDtypeStruct((t, width), BF16),
        compiler_params=_params("parallel"),
        name="mem_attn",
    )(q_mem, k, v)


def _t5_bucket(rel):
    half = REL_BUCKETS // 2
    max_exact = half // 2
    n = np.abs(rel)
    large = max_exact + (np.log(np.maximum(n, 1) / max_exact) / math.log(REL_MAX_DIST / max_exact)
                         * (half - max_exact)).astype(np.int64)
    large = np.minimum(large, half - 1)
    return ((rel > 0) * half + np.where(n < max_exact, n, large)).astype(np.int32)


def _band_bias(rel_bias, group, dil):
    period = ATT_Q_BLOCK + ATT_K_WINDOW
    heads = slice(group * HEADS_PER_GROUP, (group + 1) * HEADS_PER_GROUP)
    tables = []
    for variant in range(3):
        rel = np.arange(period) - (ATT_Q_BLOCK - 1) - variant * BAND_HALF
        in_band = np.abs(rel) <= BAND_HALF
        bucket = _t5_bucket(np.clip(rel, -BAND_HALF, BAND_HALF) * dil)
        f = jnp.where(in_band[None, :], jnp.transpose(rel_bias[bucket][:, heads]).astype(F32), NEG_INF)
        flat = jnp.tile(f, (1, ATT_Q_BLOCK))[:, :ATT_Q_BLOCK * (period - 1)]
        skew = flat.reshape(HEADS_PER_GROUP, ATT_Q_BLOCK, period - 1)
        tables.append(skew[:, :, ATT_Q_BLOCK - 1:ATT_Q_BLOCK - 1 + ATT_K_WINDOW])
    return jnp.stack(tables)


def _band_attn_kernel(q_ref, k_ref, v_ref, bias_ref, o_ref, lse_ref, *, sub_len, q_tile):
    t = pl.program_id(2)
    lane = lax.broadcasted_iota(jnp.int32, (ATT_Q_BLOCK, LANES), 1)
    low_half = lane < ATT_HEAD_DIM
    for blk in range(q_tile // ATT_Q_BLOCK):
        i0 = t * q_tile + blk * ATT_Q_BLOCK
        ws = jnp.clip(i0 - BAND_HALF, 0, sub_len - ATT_K_WINDOW)
        variant = (i0 - ws) // BAND_HALF
        ws = pl.multiple_of(ws, BAND_HALF)
        q_blk = q_ref[0, pl.ds(blk * ATT_Q_BLOCK, ATT_Q_BLOCK), :]
        k_win = k_ref[0, pl.ds(ws, ATT_K_WINDOW), :]
        v_win = v_ref[0, pl.ds(ws, ATT_K_WINDOW), :]
        for pair in range(GROUP_WIDTH // LANES):
            cols = slice(pair * LANES, (pair + 1) * LANES)
            q_pair, k_pair, v_pair = q_blk[:, cols], k_win[:, cols], v_win[:, cols]
            outs, lses = [], []
            for a in range(2):
                keep = low_half if a == 0 else jnp.logical_not(low_half)
                qh = jnp.where(keep, q_pair, jnp.zeros_like(q_pair))
                s = lax.dot_general(qh, k_pair, (((1,), (1,)), ((), ())), preferred_element_type=F32)
                s = s * (ATT_HEAD_DIM ** -0.5) + bias_ref[variant, 2 * pair + a]
                m = jnp.max(s, axis=-1, keepdims=True)
                p = jnp.exp(s - m)
                den = jnp.sum(p, axis=-1, keepdims=True)
                pv = jnp.dot(p.astype(BF16), v_pair, preferred_element_type=F32)
                outs.append(pv / den)
                lses.append(jnp.broadcast_to(m + jnp.log(den), (ATT_Q_BLOCK, LANES)))
            rows = pl.ds(blk * ATT_Q_BLOCK, ATT_Q_BLOCK)
            o_ref[0, rows, cols] = jnp.where(low_half, outs[0], outs[1])
            lse_ref[0, rows, cols] = jnp.where(low_half, lses[0], lses[1])


def _band_attn(q, k, v, bias, batch, seq, group, dil, q_tile=256):
    t, width = q.shape
    sub_len = seq // dil
    n_groups = width // GROUP_WIDTH

    def view(a):
        return a.reshape(batch, sub_len, dil * a.shape[1])

    q_spec = pl.BlockSpec((1, q_tile, GROUP_WIDTH), lambda b, r, i: (b, i, r * n_groups + group))
    kv_spec = pl.BlockSpec((1, sub_len, GROUP_WIDTH), lambda b, r, i: (b, 0, r * n_groups + group))
    o_spec = pl.BlockSpec((1, q_tile, GROUP_WIDTH), lambda b, r, i: (b, i, r))
    o, lse = pl.pallas_call(
        functools.partial(_band_attn_kernel, sub_len=sub_len, q_tile=q_tile),
        grid=(batch, dil, sub_len // q_tile),
        in_specs=[q_spec, kv_spec, kv_spec, _const_spec(bias.shape)],
        out_specs=[o_spec, o_spec],
        out_shape=[jax.ShapeDtypeStruct((batch, sub_len, dil * GROUP_WIDTH), F32)] * 2,
        compiler_params=_params("parallel", "parallel", "arbitrary"),
        name=f"band_attn_g{group}",
    )(view(q), view(k), view(v), bias)
    return o.reshape(t, GROUP_WIDTH), lse.reshape(t, GROUP_WIDTH)


def _att_out_kernel(o0_ref, o1_ref, o2_ref, l0_ref, l1_ref, l2_ref, mem_ref, x_ref, w_ref, g_ref, out_ref):
    lses = [l0_ref[...], l1_ref[...], l2_ref[...]]
    mx = jnp.maximum(jnp.maximum(lses[0], lses[1]), lses[2])
    es = [jnp.exp(l - mx) for l in lses]
    tot = es[0] + es[1] + es[2]
    parts = [(o_ref[...] * (e / tot)).astype(BF16) for o_ref, e in zip((o0_ref, o1_ref, o2_ref), es)]
    mixed = jnp.concatenate(parts + [mem_ref[...]], axis=-1)
    y = jnp.dot(mixed, w_ref[...], preferred_element_type=F32)
    out_ref[...] = x_ref[...] + _rms(y, g_ref[...])


def _att_out(os_, lses, mem_out, x2, w_out, gain, tm=512):
    t, d = x2.shape
    gw = os_[0].shape[1]
    row = lambda i: (i, 0)
    return pl.pallas_call(
        _att_out_kernel,
        grid=(t // tm,),
        in_specs=[pl.BlockSpec((tm, gw), row)] * 6
        + [pl.BlockSpec((tm, mem_out.shape[1]), row), pl.BlockSpec((tm, d), row),
           _const_spec(w_out.shape), _const_spec((1, d))],
        out_specs=pl.BlockSpec((tm, d), row),
        out_shape=jax.ShapeDtypeStruct((t, d), F32),
        compiler_params=_params("parallel"),
        name="att_out",
    )(*os_, *lses, mem_out, x2, w_out, gain.reshape(1, d))


def _ffn_kernel(x_ref, gpre_ref, wg_ref, wu_ref, wd_ref, gpost_ref, out_ref, *, n_chunks):
    x = x_ref[...]
    h = _rms(x, gpre_ref[...]).astype(BF16)
    d_ff = wg_ref.shape[1]
    step = d_ff // n_chunks
    y = jnp.zeros(x.shape, F32)
    for c in range(n_chunks):
        cols = slice(c * step, (c + 1) * step)
        gate = jnp.dot(h, wg_ref[:, cols], preferred_element_type=F32)
        up = jnp.dot(h, wu_ref[:, cols], preferred_element_type=F32)
        act = (_silu(gate) * up).astype(BF16)
        y = y + jnp.dot(act, wd_ref[cols, :], preferred_element_type=F32)
    out_ref[...] = x + _rms(y, gpost_ref[...])


def _ffn(x2, g_pre, w_gate, w_up, w_down, g_post, tm=512, n_chunks=2):
    t, d = x2.shape
    row = lambda i: (i, 0)
    return pl.pallas_call(
        functools.partial(_ffn_kernel, n_chunks=n_chunks),
        grid=(t // tm,),
        in_specs=[pl.BlockSpec((tm, d), row), _const_spec((1, d)), _const_spec(w_gate.shape),
                  _const_spec(w_up.shape), _const_spec(w_down.shape), _const_spec((1, d))],
        out_specs=pl.BlockSpec((tm, d), row),
        out_shape=jax.ShapeDtypeStruct((t, d), F32),
        compiler_params=_params("parallel"),
        name="ffn",
    )(x2, g_pre.reshape(1, d), w_gate, w_up, w_down, g_post.reshape(1, d))


def _dn_prep_kernel(cur_ref, prev_ref, next_ref, conv_ref, gate_ref, alog_ref, dtb_ref,
                    q_ref, k_ref, v_ref, gb_ref, *, halo, n_dir_heads):
    i = pl.program_id(1)
    last = pl.num_programs(1) - 1
    ts = cur_ref.shape[0]
    width = cur_ref.shape[1]
    tok = width // 3
    n_ext = ts + 2 * halo
    for c0 in range(0, width, LANES):
        cols = slice(c0, c0 + LANES)
        prev = jnp.where(i > 0, prev_ref[:, cols].astype(F32), 0.0)
        nxt = jnp.where(i < last, next_ref[:, cols].astype(F32), 0.0)
        ext = jnp.concatenate([prev, cur_ref[:, cols].astype(F32), nxt], axis=0)
        acc = jnp.zeros((ts, LANES), F32)
        for j in range(DN_CONV):
            shift = (DN_CONV // 2 - j) % n_ext
            shifted = ext if shift == 0 else pltpu.roll(ext, shift, 0)
            acc = acc + shifted[halo:halo + ts] * conv_ref[j:j + 1, cols]
        y = _silu(acc)
        which, col = divmod(c0, tok)
        if which == 2:
            v_ref[:, col:col + LANES] = y.astype(BF16)
        else:
            yn = y * lax.rsqrt(jnp.sum(y * y, axis=-1, keepdims=True) + EPS)
            if which == 0:
                q_ref[:, col:col + LANES] = (yn * (DN_HEAD_DIM ** -0.5)).astype(BF16)
            else:
                k_ref[:, col:col + LANES] = yn.astype(BF16)
    gate = gate_ref[...]
    z = gate + dtb_ref[...]
    softplus = jnp.maximum(z, 0.0) + jnp.log(1.0 + jnp.exp(-jnp.abs(z)))
    decay = -jnp.exp(alog_ref[...]) * softplus
    beta = 1.0 / (1.0 + jnp.exp(-gate))
    lane = lax.broadcasted_iota(jnp.int32, gate.shape, 1)
    gb_ref[...] = jnp.where(lane < n_dir_heads, decay, beta)


def _dn_prep(qkv, gate, conv_w, a_log_row, dt_bias_row, batch, seq, n_dir_heads, ts=512, halo=16):
    t, width = qkv.shape
    tok = width // 3
    per_seq = seq // ts
    blocks_per_tile = ts // halo
    n_halo_blocks = t // halo

    def cur_map(b, i):
        return (b * per_seq + i, 0)

    def prev_map(b, i):
        return (jnp.maximum((b * per_seq + i) * blocks_per_tile - 1, 0), 0)

    def next_map(b, i):
        return (jnp.minimum((b * per_seq + i + 1) * blocks_per_tile, n_halo_blocks - 1), 0)

    return pl.pallas_call(
        functools.partial(_dn_prep_kernel, halo=halo, n_dir_heads=n_dir_heads),
        grid=(batch, per_seq),
        in_specs=[pl.BlockSpec((ts, width), cur_map), pl.BlockSpec((halo, width), prev_map),
                  pl.BlockSpec((halo, width), next_map), _const_spec(conv_w.shape),
                  pl.BlockSpec((ts, LANES), cur_map), _const_spec((1, LANES)), _const_spec((1, LANES))],
        out_specs=[pl.BlockSpec((ts, tok), cur_map)] * 3 + [pl.BlockSpec((ts, LANES), cur_map)],
        out_shape=[jax.ShapeDtypeStruct((t, tok), BF16)] * 3 + [jax.ShapeDtypeStruct((t, LANES), F32)],
        compiler_params=_params("parallel", "arbitrary"),
        name="dn_prep",
    )(qkv, qkv, qkv, conv_w, gate, a_log_row, dt_bias_row)


def _mm(a, b):
    return jnp.dot(a, b, preferred_element_type=F32)


def _mm_nt(a, b):
    return lax.dot_general(a, b, (((1,), (1,)), ((), ())), preferred_element_type=F32)


def _mm_tn(a, b):
    return lax.dot_general(a, b, (((0,), (0,)), ((), ())), preferred_element_type=F32)


def _unit_triangular_inverses(neg_ls, same_block):
    n = neg_ls[0].shape[0]
    ii = lax.broadcasted_iota(jnp.int32, (n, n), 0)
    jj = lax.broadcasted_iota(jnp.int32, (n, n), 1)
    xs = [jnp.where(same_block[SOLVE_LEAF], nl, 0.0) for nl in neg_ls]
    invs = [jnp.where(ii == jj, 1.0, x) for x in xs]
    xbs = [x.astype(BF16) for x in xs]
    power = 1
    while 2 * power < SOLVE_LEAF:
        xbs = [_mm(xb, xb).astype(BF16) for xb in xbs]
        invs = [inv + _mm(xb, inv.astype(BF16)) for xb, inv in zip(xbs, invs)]
        power *= 2
    size = SOLVE_LEAF
    while size < n:
        merge = same_block[2 * size] & jnp.logical_not(same_block[size])
        offs = [jnp.where(merge, nl, 0.0).astype(BF16) for nl in neg_ls]
        inv_bs = [inv.astype(BF16) for inv in invs]
        rights = [_mm(off, inv_b).astype(BF16) for off, inv_b in zip(offs, inv_bs)]
        invs = [inv + _mm(inv_b, right) for inv, inv_b, right in zip(invs, inv_bs, rights)]
        size *= 2
    return invs


def _dn_core_kernel(qf_ref, kf_ref, vf_ref, gbf_ref, gtf_ref, qb_ref, kb_ref, vb_ref, gbb_ref, gtb_ref,
                    of_ref, ob_ref, state_ref, *, n_heads):
    @pl.when(pl.program_id(1) == 0)
    def _():
        state_ref[...] = jnp.zeros_like(state_ref)

    c = qf_ref.shape[0]
    d = DN_HEAD_DIM
    ii = lax.broadcasted_iota(jnp.int32, (c, c), 0)
    jj = lax.broadcasted_iota(jnp.int32, (c, c), 1)
    lower_eq, upper_eq = jj <= ii, jj >= ii
    lower, upper = jj < ii, jj > ii
    hi = lax.Precision.HIGHEST
    same_block = {}
    size = SOLVE_LEAF
    while size <= c:
        shift = int(math.log2(size))
        same_block[size] = (ii >> shift) == (jj >> shift)
        size *= 2

    items = []
    for direction, (q_ref, k_ref, v_ref, gb_ref, gt_ref, o_ref) in enumerate(
            ((qf_ref, kf_ref, vf_ref, gbf_ref, gtf_ref, of_ref),
             (qb_ref, kb_ref, vb_ref, gbb_ref, gtb_ref, ob_ref))):
        reverse = direction == 1
        gb = gb_ref[...]
        tri_col = (upper_eq if reverse else lower_eq).astype(F32)
        tri_row = (lower_eq if reverse else upper_eq).astype(F32)
        cum_col = jnp.dot(tri_col, gb, precision=hi, preferred_element_type=F32)
        cum_row = jnp.dot(gt_ref[...], tri_row, precision=hi, preferred_element_type=F32)
        for head in range(n_heads):
            g_idx = direction * n_heads + head
            b_idx = 2 * n_heads + g_idx
            cols = slice(head * d, (head + 1) * d)
            items.append(dict(
                direction=direction, head=head, o_ref=o_ref, cols=cols,
                earlier_eq=upper_eq if reverse else lower_eq,
                earlier=upper if reverse else lower,
                q=q_ref[:, cols], k=k_ref[:, cols], v=v_ref[:, cols],
                g_col=cum_col[:, g_idx:g_idx + 1], beta_col=gb[:, b_idx:b_idx + 1],
                g_row=cum_row[g_idx:g_idx + 1, :],
                g_last=cum_col[0:1, g_idx:g_idx + 1] if reverse else cum_col[c - 1:c, g_idx:g_idx + 1]))

    grams = [_mm_nt(jnp.concatenate([it["k"], it["q"]], axis=0), it["k"]) for it in items]
    neg_ls, intras, rhss, qgs, k_decs = [], [], [], [], []
    for it, gram in zip(items, grams):
        decay = jnp.exp(jnp.where(it["earlier_eq"], it["g_col"] - it["g_row"], NEG_INF))
        neg_ls.append(jnp.where(it["earlier"], gram[:c] * decay, 0.0) * (-it["beta_col"]))
        intras.append(jnp.where(it["earlier_eq"], gram[c:] * decay, 0.0).astype(BF16))
        kf = it["k"].astype(F32)
        e_g = jnp.exp(it["g_col"])
        rhss.append(jnp.concatenate([(it["v"].astype(F32) * it["beta_col"]).astype(BF16),
                                     (kf * (it["beta_col"] * e_g)).astype(BF16)], axis=-1))
        qgs.append((it["q"].astype(F32) * e_g).astype(BF16))
        k_decs.append((kf * jnp.exp(it["g_last"] - it["g_col"])).astype(BF16))
    invs = _unit_triangular_inverses(neg_ls, same_block)
    sols = [_mm(inv.astype(BF16), rhs) for inv, rhs in zip(invs, rhss)]
    states = [state_ref[it["direction"], it["head"]] for it in items]
    wss = [_mm(jnp.concatenate([sol[:, d:].astype(BF16), qg], axis=0), state.astype(BF16))
           for sol, qg, state in zip(sols, qgs, states)]
    v_news = [(sol[:, :d] - ws[:c]).astype(BF16) for sol, ws in zip(sols, wss)]
    outs = [ws[c:] + _mm(intra, v_new) for ws, intra, v_new in zip(wss, intras, v_news)]
    updates = [_mm_tn(k_dec, v_new) for k_dec, v_new in zip(k_decs, v_news)]
    for it, out, state, update in zip(items, outs, states, updates):
        it["o_ref"][:, it["cols"]] = out
        state_ref[it["direction"], it["head"]] = state * jnp.exp(it["g_last"]) + update


def _dn_core(q, k, v, gb, gbt, batch, seq):
    t, tok = q.shape
    n_heads = tok // DN_HEAD_DIM
    nc = seq // DN_CHUNK
    n_rows = gbt.shape[0]

    fwd = lambda b, i: (b * nc + i, 0)
    bwd = lambda b, i: (b * nc + nc - 1 - i, 0)
    fwd_t = lambda b, i: (0, b * nc + i)
    bwd_t = lambda b, i: (0, b * nc + nc - 1 - i)

    def side(tok_map, t_map):
        return [pl.BlockSpec((DN_CHUNK, tok), tok_map)] * 3 + [
            pl.BlockSpec((DN_CHUNK, LANES), tok_map), pl.BlockSpec((n_rows, DN_CHUNK), t_map)]

    return pl.pallas_call(
        functools.partial(_dn_core_kernel, n_heads=n_heads),
        grid=(batch, nc),
        in_specs=side(fwd, fwd_t) + side(bwd, bwd_t),
        out_specs=[pl.BlockSpec((DN_CHUNK, tok), fwd), pl.BlockSpec((DN_CHUNK, tok), bwd)],
        out_shape=[jax.ShapeDtypeStruct((t, tok), F32)] * 2,
        scratch_shapes=[pltpu.VMEM((2, n_heads, DN_HEAD_DIM, DN_HEAD_DIM), F32)],
        compiler_params=_params("parallel", "arbitrary"),
        name="dn_core",
    )(q, k, v, gb, gbt, q, k, v, gb, gbt)


def _dn_out_kernel(of_ref, ob_ref, z_ref, mem_ref, x_ref, onorm_ref, w_ref, g_ref, out_ref):
    tok = of_ref.shape[1]
    parts = []
    for c0 in range(0, tok, DN_HEAD_DIM):
        cols = slice(c0, c0 + DN_HEAD_DIM)
        o = of_ref[:, cols] + ob_ref[:, cols]
        o = o * lax.rsqrt(jnp.mean(o * o, axis=-1, keepdims=True) + EPS) * onorm_ref[...]
        parts.append((o * _silu(z_ref[:, cols].astype(F32))).astype(BF16))
    mixed = jnp.concatenate(parts + [mem_ref[...]], axis=-1)
    y = jnp.dot(mixed, w_ref[...], preferred_element_type=F32)
    out_ref[...] = x_ref[...] + _rms(y, g_ref[...])


def _dn_out(o_f, o_b, z, mem_out, x2, out_norm, w_out, gain, tm=512):
    t, d = x2.shape
    tok = o_f.shape[1]
    row = lambda i: (i, 0)
    return pl.pallas_call(
        _dn_out_kernel,
        grid=(t // tm,),
        in_specs=[pl.BlockSpec((tm, tok), row)] * 3
        + [pl.BlockSpec((tm, mem_out.shape[1]), row), pl.BlockSpec((tm, d), row),
           _const_spec((1, DN_HEAD_DIM)), _const_spec(w_out.shape), _const_spec((1, d))],
        out_specs=pl.BlockSpec((tm, d), row),
        out_shape=jax.ShapeDtypeStruct((t, d), F32),
        compiler_params=_params("parallel"),
        name="dn_out",
    )(o_f, o_b, z, mem_out, x2, out_norm.reshape(1, DN_HEAD_DIM), w_out, gain.reshape(1, d))


def _memory_branch(q_mem, mem, mem_gain, w_kv, seq):
    k_mem, v_mem = _mem_kv(mem, mem_gain, w_kv.astype(BF16))
    return _mem_attn(q_mem, k_mem, v_mem, seq)


def _attention_layer(x2, mem, batch, seq, rel_bias, w_in, w_out, mem_gain, w_kv, g_pre, g_post):
    tok = len(DILATED_GROUPS) * GROUP_WIDTH
    w_in = w_in.astype(BF16)
    weights = [w_in[:, :tok], w_in[:, tok:2 * tok], w_in[:, 2 * tok:3 * tok], w_in[:, 3 * tok:]]
    q, k, v, q_mem = _norm_proj(x2, g_pre, weights, [BF16] * 4)
    os_, lses = [], []
    for group, (_, dil) in enumerate(DILATED_GROUPS):
        o, lse = _band_attn(q, k, v, _band_bias(rel_bias, group, dil), batch, seq, group, dil)
        os_.append(o)
        lses.append(lse)
    mem_out = _memory_branch(q_mem, mem, mem_gain, w_kv, seq)
    return _att_out(os_, lses, mem_out, x2, w_out.astype(BF16), g_post)


def _deltanet_layer(x2, mem, batch, seq, w_in, conv_w, a_log, dt_bias, out_norm, w_out, mem_gain, w_kv,
                    g_pre, g_post):
    d = x2.shape[1]
    n_heads = a_log.shape[1]
    tok = n_heads * DN_HEAD_DIM
    n_gate = 4 * n_heads
    w_gate = w_in[:, 4 * tok:4 * tok + n_gate].reshape(d, 2, 2, n_heads).transpose(0, 2, 1, 3).reshape(d, n_gate)
    w_gate = jnp.pad(w_gate, ((0, 0), (0, LANES - n_gate)))
    weights = [w_in[:, :3 * tok].astype(BF16), w_in[:, 3 * tok:4 * tok].astype(BF16), w_gate.astype(BF16),
               w_in[:, 4 * tok + n_gate:].astype(BF16)]
    qkv, z, gate, q_mem = _norm_proj(x2, g_pre, weights, [BF16, BF16, F32, BF16])
    pad_row = lambda p: jnp.pad(p.reshape(1, -1).astype(F32), ((0, 0), (0, LANES - 2 * n_heads)))
    q, k, v, gb = _dn_prep(qkv, gate, conv_w.astype(F32), pad_row(a_log), pad_row(dt_bias), batch, seq,
                           2 * n_heads)
    gbt = jnp.transpose(gb[:, :8 * ((n_gate + 7) // 8)])
    o_f, o_b = _dn_core(q, k, v, gb, gbt, batch, seq)
    mem_out = _memory_branch(q_mem, mem, mem_gain, w_kv, seq)
    return _dn_out(o_f, o_b, z, mem_out, x2, out_norm, w_out.astype(BF16), g_post)


def kernel(x, mem, rel_bias, att_w_in, att_w_out, dn_w_in, dn_conv, dn_a_log, dn_dt_bias, dn_out_norm,
           dn_w_out, mem_norm, mem_w_kv, norm_mix_pre, norm_mix_post, norm_ffn_pre, norm_ffn_post,
           ffn_w_gate_up, ffn_w_down):
    batch, seq, d = x.shape
    depth = norm_mix_pre.shape[0]
    n_mixers = 2
    x2 = x.reshape(batch * seq, d)
    for i in range(depth):
        j = i // n_mixers
        if i % n_mixers == 0:
            x2 = _attention_layer(x2, mem, batch, seq, rel_bias, att_w_in[j], att_w_out[j], mem_norm[i],
                                  mem_w_kv[i], norm_mix_pre[i], norm_mix_post[i])
        else:
            x2 = _deltanet_layer(x2, mem, batch, seq, dn_w_in[j], dn_conv[j], dn_a_log[j], dn_dt_bias[j],
                                 dn_out_norm[j], dn_w_out[j], mem_norm[i], mem_w_kv[i], norm_mix_pre[i],
                                 norm_mix_post[i])
        d_ff = ffn_w_down.shape[1]
        w_gu = ffn_w_gate_up[i].astype(BF16)
        x2 = _ffn(x2, norm_ffn_pre[i], w_gu[:, :d_ff], w_gu[:, d_ff:], ffn_w_down[i].astype(BF16),
                  norm_ffn_post[i])
    return x2.reshape(batch, seq, d)
```

```python
import functools
import math

import jax
import jax.numpy as jnp
import numpy as np
from jax import lax
from jax.experimental import pallas as pl
from jax.experimental.pallas import tpu as pltpu

F32 = jnp.float32
BF16 = jnp.bfloat16

EPS = 1e-6
NEG_INF = -1e30

LANES = 128
VMEM_LIMIT_BYTES = 56 * 1024 * 1024

DILATED_GROUPS = ((128, 1), (512, 4), (2048, 16))
ATT_HEAD_DIM = 64
HEADS_PER_GROUP = 4
GROUP_WIDTH = HEADS_PER_GROUP * ATT_HEAD_DIM
BAND_HALF = 64
REL_BUCKETS = 32
REL_MAX_DIST = 1024
MEM_HEADS = 4
MEM_HEAD_DIM = 64
DN_HEAD_DIM = 128
DN_CONV = 5
DN_CHUNK = 128
SOLVE_LEAF = 16

ATT_Q_BLOCK = 128
ATT_K_WINDOW = ATT_Q_BLOCK + 2 * BAND_HALF


def _params(*semantics):
    return pltpu.CompilerParams(dimension_semantics=semantics, vmem_limit_bytes=VMEM_LIMIT_BYTES)


def _rms(x, gain):
    return x * lax.rsqrt(jnp.mean(x * x, axis=-1, keepdims=True) + EPS) * gain


def _silu(x):
    return x / (1.0 + jnp.exp(-x))


def _const_spec(shape):
    return pl.BlockSpec(shape, lambda *_: (0,) * len(shape))


def _norm_proj_kernel(x_ref, g_ref, *refs, n_out):
    h = _rms(x_ref[...], g_ref[...]).astype(BF16)
    for w_ref, o_ref in zip(refs[:n_out], refs[n_out:]):
        o_ref[...] = jnp.dot(h, w_ref[...], preferred_element_type=F32).astype(o_ref.dtype)


def _norm_proj(x2, gain, weights, out_dtypes, tm=512):
    t, d = x2.shape
    n_out = len(weights)
    return pl.pallas_call(
        functools.partial(_norm_proj_kernel, n_out=n_out),
        grid=(t // tm,),
        in_specs=[pl.BlockSpec((tm, d), lambda i: (i, 0)), _const_spec((1, d))]
        + [_const_spec(w.shape) for w in weights],
        out_specs=[pl.BlockSpec((tm, w.shape[1]), lambda i: (i, 0)) for w in weights],
        out_shape=[jax.ShapeDtypeStruct((t, w.shape[1]), dt) for w, dt in zip(weights, out_dtypes)],
        compiler_params=_params("parallel"),
        name="norm_proj",
    )(x2, gain.reshape(1, d), *weights)


def _mem_kv_kernel(mem_ref, g_ref, w_ref, k_ref, v_ref):
    h = _rms(mem_ref[0], g_ref[...]).astype(BF16)
    kv = jnp.dot(h, w_ref[...], preferred_element_type=F32)
    width = k_ref.shape[-1]
    k_ref[0] = kv[:, :width].astype(BF16)
    v_ref[0] = kv[:, width:].astype(BF16)


def _mem_kv(mem, gain, w_kv):
    b, m, d = mem.shape
    width = w_kv.shape[1] // 2
    return pl.pallas_call(
        _mem_kv_kernel,
        grid=(b,),
        in_specs=[pl.BlockSpec((1, m, d), lambda i: (i, 0, 0)), _const_spec((1, d)), _const_spec(w_kv.shape)],
        out_specs=[pl.BlockSpec((1, m, width), lambda i: (i, 0, 0))] * 2,
        out_shape=[jax.ShapeDtypeStruct((b, m, width), BF16)] * 2,
        compiler_params=_params("parallel"),
        name="mem_kv",
    )(mem, gain.reshape(1, d), w_kv)


def _mem_attn_kernel(q_ref, k_ref, v_ref, o_ref):
    q = q_ref[...]
    k = k_ref[0]
    v = v_ref[0]
    lane = lax.broadcasted_iota(jnp.int32, q.shape, 1)
    out = jnp.zeros(q.shape, F32)
    for h in range(MEM_HEADS):
        in_head = (lane >= h * MEM_HEAD_DIM) & (lane < (h + 1) * MEM_HEAD_DIM)
        qh = jnp.where(in_head, q, jnp.zeros_like(q))
        s = lax.dot_general(qh, k, (((1,), (1,)), ((), ())), preferred_element_type=F32)
        s = s * (MEM_HEAD_DIM ** -0.5)
        m = jnp.max(s, axis=-1, keepdims=True)
        p = jnp.exp(s - m)
        den = jnp.sum(p, axis=-1, keepdims=True)
        pv = jnp.dot((p / den).astype(BF16), v, preferred_element_type=F32)
        out = jnp.where(in_head, pv, out)
    o_ref[...] = out.astype(o_ref.dtype)


def _mem_attn(q_mem, k, v, seq, tq=512):
    t, width = q_mem.shape
    m = k.shape[1]
    per_seq = seq // tq
    return pl.pallas_call(
        _mem_attn_kernel,
        grid=(t // tq,),
        in_specs=[pl.BlockSpec((tq, width), lambda i: (i, 0)),
                  pl.BlockSpec((1, m, width), lambda i: (i // per_seq, 0, 0)),
                  pl.BlockSpec((1, m, width), lambda i: (i // per_seq, 0, 0))],
        out_specs=pl.BlockSpec((tq, width), lambda i: (i, 0)),
        out_shape=jax.ShapeDtypeStruct((t, width), BF16),
        compiler_params=_params("parallel"),
        name="mem_attn",
    )(q_mem, k, v)


def _t5_bucket(rel):
    half = REL_BUCKETS // 2
    max_exact = half // 2
    n = np.abs(rel)
    large = max_exact + (np.log(np.maximum(n, 1) / max_exact) / math.log(REL_MAX_DIST / max_exact)
                         * (half - max_exact)).astype(np.int64)
    large = np.minimum(large, half - 1)
    return ((rel > 0) * half + np.where(n < max_exact, n, large)).astype(np.int32)


def _band_bias(rel_bias, group, dil):
    period = ATT_Q_BLOCK + ATT_K_WINDOW
    heads = slice(group * HEADS_PER_GROUP, (group + 1) * HEADS_PER_GROUP)
    tables = []
    for variant in range(3):
        rel = np.arange(period) - (ATT_Q_BLOCK - 1) - variant * BAND_HALF
        in_band = np.abs(rel) <= BAND_HALF
        bucket = _t5_bucket(np.clip(rel, -BAND_HALF, BAND_HALF) * dil)
        f = jnp.where(in_band[None, :], jnp.transpose(rel_bias[bucket][:, heads]).astype(F32), NEG_INF)
        flat = jnp.tile(f, (1, ATT_Q_BLOCK))[:, :ATT_Q_BLOCK * (period - 1)]
        skew = flat.reshape(HEADS_PER_GROUP, ATT_Q_BLOCK, period - 1)
        tables.append(skew[:, :, ATT_Q_BLOCK - 1:ATT_Q_BLOCK - 1 + ATT_K_WINDOW])
    return jnp.stack(tables)


def _band_attn_kernel(q_ref, k_ref, v_ref, bias_ref, o_ref, lse_ref, *, sub_len, q_tile):
    t = pl.program_id(2)
    lane = lax.broadcasted_iota(jnp.int32, (ATT_Q_BLOCK, LANES), 1)
    low_half = lane < ATT_HEAD_DIM
    for blk in range(q_tile // ATT_Q_BLOCK):
        i0 = t * q_tile + blk * ATT_Q_BLOCK
        ws = jnp.clip(i0 - BAND_HALF, 0, sub_len - ATT_K_WINDOW)
        variant = (i0 - ws) // BAND_HALF
        ws = pl.multiple_of(ws, BAND_HALF)
        q_blk = q_ref[0, pl.ds(blk * ATT_Q_BLOCK, ATT_Q_BLOCK), :]
        k_win = k_ref[0, pl.ds(ws, ATT_K_WINDOW), :]
        v_win = v_ref[0, pl.ds(ws, ATT_K_WINDOW), :]
        for pair in range(GROUP_WIDTH // LANES):
            cols = slice(pair * LANES, (pair + 1) * LANES)
            q_pair, k_pair, v_pair = q_blk[:, cols], k_win[:, cols], v_win[:, cols]
            outs, lses = [], []
            for a in range(2):
                keep = low_half if a == 0 else jnp.logical_not(low_half)
                qh = jnp.where(keep, q_pair, jnp.zeros_like(q_pair))
                s = lax.dot_general(qh, k_pair, (((1,), (1,)), ((), ())), preferred_element_type=F32)
                s = s * (ATT_HEAD_DIM ** -0.5) + bias_ref[variant, 2 * pair + a]
                m = jnp.max(s, axis=-1, keepdims=True)
                p = jnp.exp(s - m)
                den = jnp.sum(p, axis=-1, keepdims=True)
                pv = jnp.dot(p.astype(BF16), v_pair, preferred_element_type=F32)
                outs.append(pv / den)
                lses.append(jnp.broadcast_to(m + jnp.log(den), (ATT_Q_BLOCK, LANES)))
            rows = pl.ds(blk * ATT_Q_BLOCK, ATT_Q_BLOCK)
            o_ref[0, rows, cols] = jnp.where(low_half, outs[0], outs[1])
            lse_ref[0, rows, cols] = jnp.where(low_half, lses[0], lses[1])


def _band_attn(qkv, bias, batch, seq, dil, q_tile=256):
    rows, width = qkv.shape
    sub_len = seq // dil
    qkv3 = qkv.reshape(batch, sub_len, width)
    q_spec = pl.BlockSpec((1, q_tile, GROUP_WIDTH), lambda b, r, i: (b, i, 3 * r))
    k_spec = pl.BlockSpec((1, sub_len, GROUP_WIDTH), lambda b, r, i: (b, 0, 3 * r + 1))
    v_spec = pl.BlockSpec((1, sub_len, GROUP_WIDTH), lambda b, r, i: (b, 0, 3 * r + 2))
    o_spec = pl.BlockSpec((1, q_tile, GROUP_WIDTH), lambda b, r, i: (b, i, r))
    o, lse = pl.pallas_call(
        functools.partial(_band_attn_kernel, sub_len=sub_len, q_tile=q_tile),
        grid=(batch, dil, sub_len // q_tile),
        in_specs=[q_spec, k_spec, v_spec, _const_spec(bias.shape)],
        out_specs=[o_spec, o_spec],
        out_shape=[jax.ShapeDtypeStruct((batch, sub_len, dil * GROUP_WIDTH), F32)] * 2,
        compiler_params=_params("parallel", "parallel", "arbitrary"),
        name=f"band_attn_d{dil}",
    )(qkv3, qkv3, qkv3, bias)
    return o.reshape(rows, dil * GROUP_WIDTH), lse.reshape(rows, dil * GROUP_WIDTH)


def _att_proj_kernel(x_ref, g_ref, *refs, dils):
    n = len(dils)
    w_refs, wmem_ref = refs[:n], refs[n]
    out_refs, qmem_ref, stage_ref = refs[n + 1:2 * n + 1], refs[2 * n + 1], refs[2 * n + 2]
    h = _rms(x_ref[...], g_ref[...]).astype(BF16)
    tm = h.shape[0]
    for w_ref, o_ref, dil in zip(w_refs, out_refs, dils):
        y = jnp.dot(h, w_ref[...], preferred_element_type=F32)
        width = y.shape[1]
        if dil == 1:
            o_ref[...] = y.astype(BF16)
        else:
            for c in range(width // LANES):
                stage_ref[c] = y[:, c * LANES:(c + 1) * LANES]
            for r in range(dil):
                for c in range(width // LANES):
                    lanes = pl.ds(r * width + c * LANES, LANES)
                    o_ref[:, lanes] = stage_ref[c, pl.ds(r, tm // dil, stride=dil), :].astype(BF16)
    qmem_ref[...] = jnp.dot(h, wmem_ref[...], preferred_element_type=F32).astype(BF16)


def _att_proj(x2, gain, group_weights, w_mem, dils, tm=512):
    t, d = x2.shape
    width = group_weights[0].shape[1]
    row = lambda i: (i, 0)
    return pl.pallas_call(
        functools.partial(_att_proj_kernel, dils=dils),
        grid=(t // tm,),
        in_specs=[pl.BlockSpec((tm, d), row), _const_spec((1, d))]
        + [_const_spec(w.shape) for w in group_weights] + [_const_spec(w_mem.shape)],
        out_specs=[pl.BlockSpec((tm // dil, dil * width), row) for dil in dils]
        + [pl.BlockSpec((tm, w_mem.shape[1]), row)],
        out_shape=[jax.ShapeDtypeStruct((t // dil, dil * width), BF16) for dil in dils]
        + [jax.ShapeDtypeStruct((t, w_mem.shape[1]), BF16)],
        scratch_shapes=[pltpu.VMEM((width // LANES, tm, LANES), F32)],
        compiler_params=_params("parallel"),
        name="att_proj",
    )(x2, gain.reshape(1, d), *group_weights, w_mem)


def _att_out_kernel(*refs, dils):
    n = len(dils)
    o_refs, l_refs = refs[:n], refs[n:2 * n]
    mem_ref, x_ref, w_ref, g_ref, out_ref, stage_ref = refs[2 * n:]
    tm = x_ref.shape[0]

    def token_major(ref, dil, slot):
        if dil == 1:
            return ref[...]
        tiles = GROUP_WIDTH // LANES
        for r in range(dil):
            for c in range(tiles):
                lanes = pl.ds(r * GROUP_WIDTH + c * LANES, LANES)
                stage_ref[slot * tiles + c, pl.ds(r, tm // dil, stride=dil), :] = ref[:, lanes]
        return jnp.concatenate([stage_ref[slot * tiles + c] for c in range(tiles)], axis=-1)

    lses = [token_major(ref, dil, 2 * i) for i, (ref, dil) in enumerate(zip(l_refs, dils))]
    outs = [token_major(ref, dil, 2 * i + 1) for i, (ref, dil) in enumerate(zip(o_refs, dils))]
    mx = functools.reduce(jnp.maximum, lses)
    es = [jnp.exp(l - mx) for l in lses]
    tot = functools.reduce(lambda a, b: a + b, es)
    parts = [(o * (e / tot)).astype(BF16) for o, e in zip(outs, es)]
    mixed = jnp.concatenate(parts + [mem_ref[...]], axis=-1)
    y = jnp.dot(mixed, w_ref[...], preferred_element_type=F32)
    out_ref[...] = x_ref[...] + _rms(y, g_ref[...])


def _att_out(os_, lses, dils, mem_out, x2, w_out, gain, tm=512):
    t, d = x2.shape
    row = lambda i: (i, 0)
    group_specs = [pl.BlockSpec((tm // dil, dil * GROUP_WIDTH), row) for dil in dils]
    return pl.pallas_call(
        functools.partial(_att_out_kernel, dils=dils),
        grid=(t // tm,),
        in_specs=group_specs + group_specs
        + [pl.BlockSpec((tm, mem_out.shape[1]), row), pl.BlockSpec((tm, d), row),
           _const_spec(w_out.shape), _const_spec((1, d))],
        out_specs=pl.BlockSpec((tm, d), row),
        out_shape=jax.ShapeDtypeStruct((t, d), F32),
        scratch_shapes=[pltpu.VMEM((2 * len(dils) * (GROUP_WIDTH // LANES), tm, LANES), F32)],
        compiler_params=_params("parallel"),
        name="att_out",
    )(*os_, *lses, mem_out, x2, w_out, gain.reshape(1, d))


def _ffn_kernel(x_ref, gpre_ref, wg_ref, wu_ref, wd_ref, gpost_ref, out_ref, *, n_chunks):
    x = x_ref[...]
    h = _rms(x, gpre_ref[...]).astype(BF16)
    d_ff = wg_ref.shape[1]
    step = d_ff // n_chunks
    y = jnp.zeros(x.shape, F32)
    for c in range(n_chunks):
        cols = slice(c * step, (c + 1) * step)
        gate = jnp.dot(h, wg_ref[:, cols], preferred_element_type=F32)
        up = jnp.dot(h, wu_ref[:, cols], preferred_element_type=F32)
        act = (_silu(gate) * up).astype(BF16)
        y = y + jnp.dot(act, wd_ref[cols, :], preferred_element_type=F32)
    out_ref[...] = x + _rms(y, gpost_ref[...])


def _ffn(x2, g_pre, w_gate, w_up, w_down, g_post, tm=512, n_chunks=2):
    t, d = x2.shape
    row = lambda i: (i, 0)
    return pl.pallas_call(
        functools.partial(_ffn_kernel, n_chunks=n_chunks),
        grid=(t // tm,),
        in_specs=[pl.BlockSpec((tm, d), row), _const_spec((1, d)), _const_spec(w_gate.shape),
                  _const_spec(w_up.shape), _const_spec(w_down.shape), _const_spec((1, d))],
        out_specs=pl.BlockSpec((tm, d), row),
        out_shape=jax.ShapeDtypeStruct((t, d), F32),
        compiler_params=_params("parallel"),
        name="ffn",
    )(x2, g_pre.reshape(1, d), w_gate, w_up, w_down, g_post.reshape(1, d))


def _dn_prep_kernel(cur_ref, prev_ref, next_ref, conv_ref, gate_ref, alog_ref, dtb_ref,
                    q_ref, k_ref, v_ref, gb_ref, *, halo, n_dir_heads):
    i = pl.program_id(1)
    last = pl.num_programs(1) - 1
    ts = cur_ref.shape[0]
    width = cur_ref.shape[1]
    tok = width // 3
    n_ext = ts + 2 * halo
    for c0 in range(0, width, LANES):
        cols = slice(c0, c0 + LANES)
        prev = jnp.where(i > 0, prev_ref[:, cols].astype(F32), 0.0)
        nxt = jnp.where(i < last, next_ref[:, cols].astype(F32), 0.0)
        ext = jnp.concatenate([prev, cur_ref[:, cols].astype(F32), nxt], axis=0)
        acc = jnp.zeros((ts, LANES), F32)
        for j in range(DN_CONV):
            shift = (DN_CONV // 2 - j) % n_ext
            shifted = ext if shift == 0 else pltpu.roll(ext, shift, 0)
            acc = acc + shifted[halo:halo + ts] * conv_ref[j:j + 1, cols]
        y = _silu(acc)
        which, col = divmod(c0, tok)
        if which == 2:
            v_ref[:, col:col + LANES] = y.astype(BF16)
        else:
            yn = y * lax.rsqrt(jnp.sum(y * y, axis=-1, keepdims=True) + EPS)
            if which == 0:
                q_ref[:, col:col + LANES] = (yn * (DN_HEAD_DIM ** -0.5)).astype(BF16)
            else:
                k_ref[:, col:col + LANES] = yn.astype(BF16)
    gate = gate_ref[...]
    z = gate + dtb_ref[...]
    softplus = jnp.maximum(z, 0.0) + jnp.log(1.0 + jnp.exp(-jnp.abs(z)))
    decay = -jnp.exp(alog_ref[...]) * softplus
    beta = 1.0 / (1.0 + jnp.exp(-gate))
    lane = lax.broadcasted_iota(jnp.int32, gate.shape, 1)
    gb_ref[...] = jnp.where(lane < n_dir_heads, decay, beta)


def _dn_prep(qkv, gate, conv_w, a_log_row, dt_bias_row, batch, seq, n_dir_heads, ts=512, halo=16):
    t, width = qkv.shape
    tok = width // 3
    per_seq = seq // ts
    blocks_per_tile = ts // halo
    n_halo_blocks = t // halo

    def cur_map(b, i):
        return (b * per_seq + i, 0)

    def prev_map(b, i):
        return (jnp.maximum((b * per_seq + i) * blocks_per_tile - 1, 0), 0)

    def next_map(b, i):
        return (jnp.minimum((b * per_seq + i + 1) * blocks_per_tile, n_halo_blocks - 1), 0)

    return pl.pallas_call(
        functools.partial(_dn_prep_kernel, halo=halo, n_dir_heads=n_dir_heads),
        grid=(batch, per_seq),
        in_specs=[pl.BlockSpec((ts, width), cur_map), pl.BlockSpec((halo, width), prev_map),
                  pl.BlockSpec((halo, width), next_map), _const_spec(conv_w.shape),
                  pl.BlockSpec((ts, LANES), cur_map), _const_spec((1, LANES)), _const_spec((1, LANES))],
        out_specs=[pl.BlockSpec((ts, tok), cur_map)] * 3 + [pl.BlockSpec((ts, LANES), cur_map)],
        out_shape=[jax.ShapeDtypeStruct((t, tok), BF16)] * 3 + [jax.ShapeDtypeStruct((t, LANES), F32)],
        compiler_params=_params("parallel", "arbitrary"),
        name="dn_prep",
    )(qkv, qkv, qkv, conv_w, gate, a_log_row, dt_bias_row)


def _mm(a, b):
    return jnp.dot(a, b, preferred_element_type=F32)


def _mm_nt(a, b):
    return lax.dot_general(a, b, (((1,), (1,)), ((), ())), preferred_element_type=F32)


def _mm_tn(a, b):
    return lax.dot_general(a, b, (((0,), (0,)), ((), ())), preferred_element_type=F32)


def _unit_triangular_inverses(neg_ls, same_block):
    n = neg_ls[0].shape[0]
    ii = lax.broadcasted_iota(jnp.int32, (n, n), 0)
    jj = lax.broadcasted_iota(jnp.int32, (n, n), 1)
    xs = [jnp.where(same_block[SOLVE_LEAF], nl, 0.0) for nl in neg_ls]
    invs = [jnp.where(ii == jj, 1.0, x) for x in xs]
    xbs = [x.astype(BF16) for x in xs]
    power = 1
    while 2 * power < SOLVE_LEAF:
        xbs = [_mm(xb, xb).astype(BF16) for xb in xbs]
        invs = [inv + _mm(xb, inv.astype(BF16)) for xb, inv in zip(xbs, invs)]
        power *= 2
    size = SOLVE_LEAF
    while size < n:
        merge = same_block[2 * size] & jnp.logical_not(same_block[size])
        offs = [jnp.where(merge, nl, 0.0).astype(BF16) for nl in neg_ls]
        inv_bs = [inv.astype(BF16) for inv in invs]
        rights = [_mm(off, inv_b).astype(BF16) for off, inv_b in zip(offs, inv_bs)]
        invs = [inv + _mm(inv_b, right) for inv, inv_b, right in zip(invs, inv_bs, rights)]
        size *= 2
    return invs


def _dn_core_kernel(qf_ref, kf_ref, vf_ref, gbf_ref, gtf_ref, qb_ref, kb_ref, vb_ref, gbb_ref, gtb_ref,
                    of_ref, ob_ref, state_ref, *, n_heads):
    @pl.when(pl.program_id(1) == 0)
    def _():
        state_ref[...] = jnp.zeros_like(state_ref)

    c = qf_ref.shape[0]
    d = DN_HEAD_DIM
    ii = lax.broadcasted_iota(jnp.int32, (c, c), 0)
    jj = lax.broadcasted_iota(jnp.int32, (c, c), 1)
    lower_eq, upper_eq = jj <= ii, jj >= ii
    lower, upper = jj < ii, jj > ii
    hi = lax.Precision.HIGHEST
    same_block = {}
    size = SOLVE_LEAF
    while size <= c:
        shift = int(math.log2(size))
        same_block[size] = (ii >> shift) == (jj >> shift)
        size *= 2

    items = []
    for direction, (q_ref, k_ref, v_ref, gb_ref, gt_ref, o_ref) in enumerate(
            ((qf_ref, kf_ref, vf_ref, gbf_ref, gtf_ref, of_ref),
             (qb_ref, kb_ref, vb_ref, gbb_ref, gtb_ref, ob_ref))):
        reverse = direction == 1
        gb = gb_ref[...]
        tri_col = (upper_eq if reverse else lower_eq).astype(F32)
        tri_row = (lower_eq if reverse else upper_eq).astype(F32)
        cum_col = jnp.dot(tri_col, gb, precision=hi, preferred_element_type=F32)
        cum_row = jnp.dot(gt_ref[...], tri_row, precision=hi, preferred_element_type=F32)
        for head in range(n_heads):
            g_idx = direction * n_heads + head
            b_idx = 2 * n_heads + g_idx
            cols = slice(head * d, (head + 1) * d)
            items.append(dict(
                direction=direction, head=head, o_ref=o_ref, cols=cols,
                earlier_eq=upper_eq if reverse else lower_eq,
                earlier=upper if reverse else lower,
                q=q_ref[:, cols], k=k_ref[:, cols], v=v_ref[:, cols],
                g_col=cum_col[:, g_idx:g_idx + 1], beta_col=gb[:, b_idx:b_idx + 1],
                g_row=cum_row[g_idx:g_idx + 1, :],
                g_last=cum_col[0:1, g_idx:g_idx + 1] if reverse else cum_col[c - 1:c, g_idx:g_idx + 1]))

    grams = [_mm_nt(jnp.concatenate([it["k"], it["q"]], axis=0), it["k"]) for it in items]
    neg_ls, intras, rhss, qgs, k_decs = [], [], [], [], []
    for it, gram in zip(items, grams):
        decay = jnp.exp(jnp.where(it["earlier_eq"], it["g_col"] - it["g_row"], NEG_INF))
        neg_ls.append(jnp.where(it["earlier"], gram[:c] * decay, 0.0) * (-it["beta_col"]))
        intras.append(jnp.where(it["earlier_eq"], gram[c:] * decay, 0.0).astype(BF16))
        kf = it["k"].astype(F32)
        e_g = jnp.exp(it["g_col"])
        rhss.append(jnp.concatenate([(it["v"].astype(F32) * it["beta_col"]).astype(BF16),
                                     (kf * (it["beta_col"] * e_g)).astype(BF16)], axis=-1))
        qgs.append((it["q"].astype(F32) * e_g).astype(BF16))
        k_decs.append((kf * jnp.exp(it["g_last"] - it["g_col"])).astype(BF16))
    invs = _unit_triangular_inverses(neg_ls, same_block)
    sols = [_mm(inv.astype(BF16), rhs) for inv, rhs in zip(invs, rhss)]
    states = [state_ref[it["direction"], it["head"]] for it in items]
    wss = [_mm(jnp.concatenate([sol[:, d:].astype(BF16), qg], axis=0), state.astype(BF16))
           for sol, qg, state in zip(sols, qgs, states)]
    v_news = [(sol[:, :d] - ws[:c]).astype(BF16) for sol, ws in zip(sols, wss)]
    outs = [ws[c:] + _mm(intra, v_new) for ws, intra, v_new in zip(wss, intras, v_news)]
    updates = [_mm_tn(k_dec, v_new) for k_dec, v_new in zip(k_decs, v_news)]
    for it, out, state, update in zip(items, outs, states, updates):
        it["o_ref"][:, it["cols"]] = out
        state_ref[it["direction"], it["head"]] = state * jnp.exp(it["g_last"]) + update


def _dn_core(q, k, v, gb, gbt, batch, seq):
    t, tok = q.shape
    n_heads = tok // DN_HEAD_DIM
    nc = seq // DN_CHUNK
    n_rows = gbt.shape[0]

    fwd = lambda b, i: (b * nc + i, 0)
    bwd = lambda b, i: (b * nc + nc - 1 - i, 0)
    fwd_t = lambda b, i: (0, b * nc + i)
    bwd_t = lambda b, i: (0, b * nc + nc - 1 - i)

    def side(tok_map, t_map):
        return [pl.BlockSpec((DN_CHUNK, tok), tok_map)] * 3 + [
            pl.BlockSpec((DN_CHUNK, LANES), tok_map), pl.BlockSpec((n_rows, DN_CHUNK), t_map)]

    return pl.pallas_call(
        functools.partial(_dn_core_kernel, n_heads=n_heads),
        grid=(batch, nc),
        in_specs=side(fwd, fwd_t) + side(bwd, bwd_t),
        out_specs=[pl.BlockSpec((DN_CHUNK, tok), fwd), pl.BlockSpec((DN_CHUNK, tok), bwd)],
        out_shape=[jax.ShapeDtypeStruct((t, tok), F32)] * 2,
        scratch_shapes=[pltpu.VMEM((2, n_heads, DN_HEAD_DIM, DN_HEAD_DIM), F32)],
        compiler_params=_params("parallel", "arbitrary"),
        name="dn_core",
    )(q, k, v, gb, gbt, q, k, v, gb, gbt)


def _dn_out_kernel(of_ref, ob_ref, z_ref, mem_ref, x_ref, onorm_ref, w_ref, g_ref, out_ref):
    tok = of_ref.shape[1]
    parts = []
    for c0 in range(0, tok, DN_HEAD_DIM):
        cols = slice(c0, c0 + DN_HEAD_DIM)
        o = of_ref[:, cols] + ob_ref[:, cols]
        o = o * lax.rsqrt(jnp.mean(o * o, axis=-1, keepdims=True) + EPS) * onorm_ref[...]
        parts.append((o * _silu(z_ref[:, cols].astype(F32))).astype(BF16))
    mixed = jnp.concatenate(parts + [mem_ref[...]], axis=-1)
    y = jnp.dot(mixed, w_ref[...], preferred_element_type=F32)
    out_ref[...] = x_ref[...] + _rms(y, g_ref[...])


def _dn_out(o_f, o_b, z, mem_out, x2, out_norm, w_out, gain, tm=512):
    t, d = x2.shape
    tok = o_f.shape[1]
    row = lambda i: (i, 0)
    return pl.pallas_call(
        _dn_out_kernel,
        grid=(t // tm,),
        in_specs=[pl.BlockSpec((tm, tok), row)] * 3
        + [pl.BlockSpec((tm, mem_out.shape[1]), row), pl.BlockSpec((tm, d), row),
           _const_spec((1, DN_HEAD_DIM)), _const_spec(w_out.shape), _const_spec((1, d))],
        out_specs=pl.BlockSpec((tm, d), row),
        out_shape=jax.ShapeDtypeStruct((t, d), F32),
        compiler_params=_params("parallel"),
        name="dn_out",
    )(o_f, o_b, z, mem_out, x2, out_norm.reshape(1, DN_HEAD_DIM), w_out, gain.reshape(1, d))


def _memory_branch(q_mem, mem, mem_gain, w_kv, seq):
    k_mem, v_mem = _mem_kv(mem, mem_gain, w_kv.astype(BF16))
    return _mem_attn(q_mem, k_mem, v_mem, seq)


def _attention_layer(x2, mem, batch, seq, rel_bias, w_in, w_out, mem_gain, w_kv, g_pre, g_post):
    n_groups = len(DILATED_GROUPS)
    tok = n_groups * GROUP_WIDTH
    dils = tuple(dil for _, dil in DILATED_GROUPS)
    w_qkv = w_in[:, :3 * tok].reshape(-1, 3, n_groups, GROUP_WIDTH).transpose(0, 2, 1, 3).astype(BF16)
    group_weights = [w_qkv[:, g].reshape(-1, 3 * GROUP_WIDTH) for g in range(n_groups)]
    *qkvs, q_mem = _att_proj(x2, g_pre, group_weights, w_in[:, 3 * tok:].astype(BF16), dils)
    os_, lses = [], []
    for group, dil in enumerate(dils):
        o, lse = _band_attn(qkvs[group], _band_bias(rel_bias, group, dil), batch, seq, dil)
        os_.append(o)
        lses.append(lse)
    mem_out = _memory_branch(q_mem, mem, mem_gain, w_kv, seq)
    return _att_out(os_, lses, dils, mem_out, x2, w_out.astype(BF16), g_post)


def _deltanet_layer(x2, mem, batch, seq, w_in, conv_w, a_log, dt_bias, out_norm, w_out, mem_gain, w_kv,
                    g_pre, g_post):
    d = x2.shape[1]
    n_heads = a_log.shape[1]
    tok = n_heads * DN_HEAD_DIM
    n_gate = 4 * n_heads
    w_gate = w_in[:, 4 * tok:4 * tok + n_gate].reshape(d, 2, 2, n_heads).transpose(0, 2, 1, 3).reshape(d, n_gate)
    w_gate = jnp.pad(w_gate, ((0, 0), (0, LANES - n_gate)))
    weights = [w_in[:, :3 * tok].astype(BF16), w_in[:, 3 * tok:4 * tok].astype(BF16), w_gate.astype(BF16),
               w_in[:, 4 * tok + n_gate:].astype(BF16)]
    qkv, z, gate, q_mem = _norm_proj(x2, g_pre, weights, [BF16, BF16, F32, BF16])
    pad_row = lambda p: jnp.pad(p.reshape(1, -1).astype(F32), ((0, 0), (0, LANES - 2 * n_heads)))
    q, k, v, gb = _dn_prep(qkv, gate, conv_w.astype(F32), pad_row(a_log), pad_row(dt_bias), batch, seq,
                           2 * n_heads)
    gbt = jnp.transpose(gb[:, :8 * ((n_gate + 7) // 8)])
    o_f, o_b = _dn_core(q, k, v, gb, gbt, batch, seq)
    mem_out = _memory_branch(q_mem, mem, mem_gain, w_kv, seq)
    return _dn_out(o_f, o_b, z, mem_out, x2, out_norm, w_out.astype(BF16), g_post)


def kernel(x, mem, rel_bias, att_w_in, att_w_out, dn_w_in, dn_conv, dn_a_log, dn_dt_bias, dn_out_norm,
           dn_w_out, mem_norm, mem_w_kv, norm_mix_pre, norm_mix_post, norm_ffn_pre, norm_ffn_post,
           ffn_w_gate_up, ffn_w_down):
    batch, seq, d = x.shape
    depth = norm_mix_pre.shape[0]
    n_mixers = 2
    x2 = x.reshape(batch * seq, d)
    for i in range(depth):
        j = i // n_mixers
        if i % n_mixers == 0:
            x2 = _attention_layer(x2, mem, batch, seq, rel_bias, att_w_in[j], att_w_out[j], mem_norm[i],
                                  mem_w_kv[i], norm_mix_pre[i], norm_mix_post[i])
        else:
            x2 = _deltanet_layer(x2, mem, batch, seq, dn_w_in[j], dn_conv[j], dn_a_log[j], dn_dt_bias[j],
                                 dn_out_norm[j], dn_w_out[j], mem_norm[i], mem_w_kv[i], norm_mix_pre[i],
                                 norm_mix_post[i])
        d_ff = ffn_w_down.shape[1]
        w_gu = ffn_w_gate_up[i].astype(BF16)
        x2 = _ffn(x2, norm_ffn_pre[i], w_gu[:, :d_ff], w_gu[:, d_ff:], ffn_w_down[i].astype(BF16),
                  norm_ffn_post[i])
    return x2.reshape(batch, seq, d)
```

```python
import functools
import math

import jax
import jax.numpy as jnp
import numpy as np
from jax import lax
from jax.experimental import pallas as pl
from jax.experimental.pallas import tpu as pltpu

F32 = jnp.float32
BF16 = jnp.bfloat16

EPS = 1e-6
NEG_INF = -1e30

LANES = 128
VMEM_LIMIT_BYTES = 56 * 1024 * 1024

DILATED_GROUPS = ((128, 1), (512, 4), (2048, 16))
ATT_HEAD_DIM = 64
HEADS_PER_GROUP = 4
GROUP_WIDTH = HEADS_PER_GROUP * ATT_HEAD_DIM
BAND_HALF = 64
REL_BUCKETS = 32
REL_MAX_DIST = 1024
MEM_HEADS = 4
MEM_HEAD_DIM = 64
DN_HEAD_DIM = 128
DN_CONV = 5
DN_CHUNK = 128
SOLVE_LEAF = 16

ATT_Q_BLOCK = 128
ATT_K_WINDOW = ATT_Q_BLOCK + 2 * BAND_HALF


def _params(*semantics):
    return pltpu.CompilerParams(dimension_semantics=semantics, vmem_limit_bytes=VMEM_LIMIT_BYTES)


def _rms(x, gain):
    return x * lax.rsqrt(jnp.mean(x * x, axis=-1, keepdims=True) + EPS) * gain


def _silu(x):
    return x / (1.0 + jnp.exp(-x))


def _const_spec(shape):
    return pl.BlockSpec(shape, lambda *_: (0,) * len(shape))


def _norm_proj_kernel(x_ref, g_ref, *refs, n_out):
    h = _rms(x_ref[...], g_ref[...]).astype(BF16)
    for w_ref, o_ref in zip(refs[:n_out], refs[n_out:]):
        o_ref[...] = jnp.dot(h, w_ref[...], preferred_element_type=F32).astype(o_ref.dtype)


def _norm_proj(x2, gain, weights, out_dtypes, tm=512):
    t, d = x2.shape
    n_out = len(weights)
    return pl.pallas_call(
        functools.partial(_norm_proj_kernel, n_out=n_out),
        grid=(t // tm,),
        in_specs=[pl.BlockSpec((tm, d), lambda i: (i, 0)), _const_spec((1, d))]
        + [_const_spec(w.shape) for w in weights],
        out_specs=[pl.BlockSpec((tm, w.shape[1]), lambda i: (i, 0)) for w in weights],
        out_shape=[jax.ShapeDtypeStruct((t, w.shape[1]), dt) for w, dt in zip(weights, out_dtypes)],
        compiler_params=_params("parallel"),
        name="norm_proj",
    )(x2, gain.reshape(1, d), *weights)


def _mem_kv_kernel(mem_ref, g_ref, w_ref, k_ref, v_ref):
    h = _rms(mem_ref[0], g_ref[...]).astype(BF16)
    kv = jnp.dot(h, w_ref[...], preferred_element_type=F32)
    width = k_ref.shape[-1]
    k_ref[0] = kv[:, :width].astype(BF16)
    v_ref[0] = kv[:, width:].astype(BF16)


def _mem_kv(mem, gain, w_kv):
    b, m, d = mem.shape
    width = w_kv.shape[1] // 2
    return pl.pallas_call(
        _mem_kv_kernel,
        grid=(b,),
        in_specs=[pl.BlockSpec((1, m, d), lambda i: (i, 0, 0)), _const_spec((1, d)), _const_spec(w_kv.shape)],
        out_specs=[pl.BlockSpec((1, m, width), lambda i: (i, 0, 0))] * 2,
        out_shape=[jax.ShapeDtypeStruct((b, m, width), BF16)] * 2,
        compiler_params=_params("parallel"),
        name="mem_kv",
    )(mem, gain.reshape(1, d), w_kv)


def _mem_attn_kernel(q_ref, k_ref, v_ref, o_ref):
    q = q_ref[...]
    k = k_ref[0]
    v = v_ref[0]
    lane = lax.broadcasted_iota(jnp.int32, q.shape, 1)
    out = jnp.zeros(q.shape, F32)
    for h in range(MEM_HEADS):
        in_head = (lane >= h * MEM_HEAD_DIM) & (lane < (h + 1) * MEM_HEAD_DIM)
        qh = jnp.where(in_head, q, jnp.zeros_like(q))
        s = lax.dot_general(qh, k, (((1,), (1,)), ((), ())), preferred_element_type=F32)
        s = s * (MEM_HEAD_DIM ** -0.5)
        m = jnp.max(s, axis=-1, keepdims=True)
        p = jnp.exp(s - m)
        den = jnp.sum(p, axis=-1, keepdims=True)
        pv = jnp.dot((p / den).astype(BF16), v, preferred_element_type=F32)
        out = jnp.where(in_head, pv, out)
    o_ref[...] = out.astype(o_ref.dtype)


def _mem_attn(q_mem, k, v, seq, tq=512):
    t, width = q_mem.shape
    m = k.shape[1]
    per_seq = seq // tq
    return pl.pallas_call(
        _mem_attn_kernel,
        grid=(t // tq,),
        in_specs=[pl.BlockSpec((tq, width), lambda i: (i, 0)),
                  pl.BlockSpec((1, m, width), lambda i: (i // per_seq, 0, 0)),
                  pl.BlockSpec((1, m, width), lambda i: (i // per_seq, 0, 0))],
        out_specs=pl.BlockSpec((tq, width), lambda i: (i, 0)),
        out_shape=jax.ShapeDtypeStruct((t, width), BF16),
        compiler_params=_params("parallel"),
        name="mem_attn",
    )(q_mem, k, v)


def _t5_bucket(rel):
    half = REL_BUCKETS // 2
    max_exact = half // 2
    n = np.abs(rel)
    large = max_exact + (np.log(np.maximum(n, 1) / max_exact) / math.log(REL_MAX_DIST / max_exact)
                         * (half - max_exact)).astype(np.int64)
    large = np.minimum(large, half - 1)
    return ((rel > 0) * half + np.where(n < max_exact, n, large)).astype(np.int32)


def _band_bias(rel_bias, group, dil):
    period = ATT_Q_BLOCK + ATT_K_WINDOW
    heads = slice(group * HEADS_PER_GROUP, (group + 1) * HEADS_PER_GROUP)
    tables = []
    for variant in range(3):
        rel = np.arange(period) - (ATT_Q_BLOCK - 1) - variant * BAND_HALF
        in_band = np.abs(rel) <= BAND_HALF
        bucket = _t5_bucket(np.clip(rel, -BAND_HALF, BAND_HALF) * dil)
        f = jnp.where(in_band[None, :], jnp.transpose(rel_bias[bucket][:, heads]).astype(F32), NEG_INF)
        flat = jnp.tile(f, (1, ATT_Q_BLOCK))[:, :ATT_Q_BLOCK * (period - 1)]
        skew = flat.reshape(HEADS_PER_GROUP, ATT_Q_BLOCK, period - 1)
        tables.append(skew[:, :, ATT_Q_BLOCK - 1:ATT_Q_BLOCK - 1 + ATT_K_WINDOW])
    return jnp.stack(tables)


def _band_attn_kernel(q_ref, k_ref, v_ref, bias_ref, o_ref, lse_ref, *, sub_len, q_tile):
    t = pl.program_id(2)
    lane = lax.broadcasted_iota(jnp.int32, (ATT_Q_BLOCK, LANES), 1)
    low_half = lane < ATT_HEAD_DIM
    for blk in range(q_tile // ATT_Q_BLOCK):
        i0 = t * q_tile + blk * ATT_Q_BLOCK
        ws = jnp.clip(i0 - BAND_HALF, 0, sub_len - ATT_K_WINDOW)
        variant = (i0 - ws) // BAND_HALF
        ws = pl.multiple_of(ws, BAND_HALF)
        q_blk = q_ref[0, pl.ds(blk * ATT_Q_BLOCK, ATT_Q_BLOCK), :]
        k_win = k_ref[0, pl.ds(ws, ATT_K_WINDOW), :]
        v_win = v_ref[0, pl.ds(ws, ATT_K_WINDOW), :]
        for pair in range(GROUP_WIDTH // LANES):
            cols = slice(pair * LANES, (pair + 1) * LANES)
            q_pair, k_pair, v_pair = q_blk[:, cols], k_win[:, cols], v_win[:, cols]
            outs, lses = [], []
            for a in range(2):
                keep = low_half if a == 0 else jnp.logical_not(low_half)
                qh = jnp.where(keep, q_pair, jnp.zeros_like(q_pair))
                s = lax.dot_general(qh, k_pair, (((1,), (1,)), ((), ())), preferred_element_type=F32)
                s = s * (ATT_HEAD_DIM ** -0.5) + bias_ref[variant, 2 * pair + a]
                m = jnp.max(s, axis=-1, keepdims=True)
                p = jnp.exp(s - m)
                den = jnp.sum(p, axis=-1, keepdims=True)
                pv = jnp.dot(p.astype(BF16), v_pair, preferred_element_type=F32)
                outs.append(pv / den)
                lses.append(jnp.broadcast_to(m + jnp.log(den), (ATT_Q_BLOCK, LANES)))
            rows = pl.ds(blk * ATT_Q_BLOCK, ATT_Q_BLOCK)
            o_ref[0, rows, cols] = jnp.where(low_half, outs[0], outs[1])
            lse_ref[0, rows, cols] = jnp.where(low_half, lses[0], lses[1])


def _band_attn(qkv, bias, batch, seq, dil, q_tile=256):
    rows, width = qkv.shape
    sub_len = seq // dil
    qkv3 = qkv.reshape(batch, sub_len, width)
    q_spec = pl.BlockSpec((1, q_tile, GROUP_WIDTH), lambda b, r, i: (b, i, 3 * r))
    k_spec = pl.BlockSpec((1, sub_len, GROUP_WIDTH), lambda b, r, i: (b, 0, 3 * r + 1))
    v_spec = pl.BlockSpec((1, sub_len, GROUP_WIDTH), lambda b, r, i: (b, 0, 3 * r + 2))
    o_spec = pl.BlockSpec((1, q_tile, GROUP_WIDTH), lambda b, r, i: (b, i, r))
    o, lse = pl.pallas_call(
        functools.partial(_band_attn_kernel, sub_len=sub_len, q_tile=q_tile),
        grid=(batch, dil, sub_len // q_tile),
        in_specs=[q_spec, k_spec, v_spec, _const_spec(bias.shape)],
        out_specs=[o_spec, o_spec],
        out_shape=[jax.ShapeDtypeStruct((batch, sub_len, dil * GROUP_WIDTH), F32)] * 2,
        compiler_params=_params("parallel", "parallel", "arbitrary"),
        name=f"band_attn_d{dil}",
    )(qkv3, qkv3, qkv3, bias)
    return o.reshape(rows, dil * GROUP_WIDTH), lse.reshape(rows, dil * GROUP_WIDTH)


def _att_proj_kernel(x_ref, g_ref, *refs, dils):
    n = len(dils)
    w_refs, wmem_ref = refs[:n], refs[n]
    out_refs, qmem_ref, stage_ref = refs[n + 1:2 * n + 1], refs[2 * n + 1], refs[2 * n + 2]
    h = _rms(x_ref[...], g_ref[...]).astype(BF16)
    tm = h.shape[0]
    for w_ref, o_ref, dil in zip(w_refs, out_refs, dils):
        y = jnp.dot(h, w_ref[...], preferred_element_type=F32)
        width = y.shape[1]
        if dil == 1:
            o_ref[...] = y.astype(BF16)
        else:
            for c in range(width // LANES):
                stage_ref[c] = y[:, c * LANES:(c + 1) * LANES]
            for r in range(dil):
                for c in range(width // LANES):
                    lanes = pl.ds(r * width + c * LANES, LANES)
                    o_ref[:, lanes] = stage_ref[c, pl.ds(r, tm // dil, stride=dil), :].astype(BF16)
    qmem_ref[...] = jnp.dot(h, wmem_ref[...], preferred_element_type=F32).astype(BF16)


def _att_proj(x2, gain, group_weights, w_mem, dils, tm=512):
    t, d = x2.shape
    width = group_weights[0].shape[1]
    row = lambda i: (i, 0)
    return pl.pallas_call(
        functools.partial(_att_proj_kernel, dils=dils),
        grid=(t // tm,),
        in_specs=[pl.BlockSpec((tm, d), row), _const_spec((1, d))]
        + [_const_spec(w.shape) for w in group_weights] + [_const_spec(w_mem.shape)],
        out_specs=[pl.BlockSpec((tm // dil, dil * width), row) for dil in dils]
        + [pl.BlockSpec((tm, w_mem.shape[1]), row)],
        out_shape=[jax.ShapeDtypeStruct((t // dil, dil * width), BF16) for dil in dils]
        + [jax.ShapeDtypeStruct((t, w_mem.shape[1]), BF16)],
        scratch_shapes=[pltpu.VMEM((width // LANES, tm, LANES), F32)],
        compiler_params=_params("parallel"),
        name="att_proj",
    )(x2, gain.reshape(1, d), *group_weights, w_mem)


def _att_out_kernel(*refs, dils):
    n = len(dils)
    o_refs, l_refs = refs[:n], refs[n:2 * n]
    mem_ref, x_ref, w_ref, g_ref, out_ref, stage_ref = refs[2 * n:]
    tm = x_ref.shape[0]

    def token_major(ref, dil, slot):
        if dil == 1:
            return ref[...]
        tiles = GROUP_WIDTH // LANES
        for r in range(dil):
            for c in range(tiles):
                lanes = pl.ds(r * GROUP_WIDTH + c * LANES, LANES)
                stage_ref[slot * tiles + c, pl.ds(r, tm // dil, stride=dil), :] = ref[:, lanes]
        return jnp.concatenate([stage_ref[slot * tiles + c] for c in range(tiles)], axis=-1)

    lses = [token_major(ref, dil, 2 * i) for i, (ref, dil) in enumerate(zip(l_refs, dils))]
    outs = [token_major(ref, dil, 2 * i + 1) for i, (ref, dil) in enumerate(zip(o_refs, dils))]
    mx = functools.reduce(jnp.maximum, lses)
    es = [jnp.exp(l - mx) for l in lses]
    tot = functools.reduce(lambda a, b: a + b, es)
    parts = [(o * (e / tot)).astype(BF16) for o, e in zip(outs, es)]
    mixed = jnp.concatenate(parts + [mem_ref[...]], axis=-1)
    y = jnp.dot(mixed, w_ref[...], preferred_element_type=F32)
    out_ref[...] = x_ref[...] + _rms(y, g_ref[...])


def _att_out(os_, lses, dils, mem_out, x2, w_out, gain, tm=512):
    t, d = x2.shape
    row = lambda i: (i, 0)
    group_specs = [pl.BlockSpec((tm // dil, dil * GROUP_WIDTH), row) for dil in dils]
    return pl.pallas_call(
        functools.partial(_att_out_kernel, dils=dils),
        grid=(t // tm,),
        in_specs=group_specs + group_specs
        + [pl.BlockSpec((tm, mem_out.shape[1]), row), pl.BlockSpec((tm, d), row),
           _const_spec(w_out.shape), _const_spec((1, d))],
        out_specs=pl.BlockSpec((tm, d), row),
        out_shape=jax.ShapeDtypeStruct((t, d), F32),
        scratch_shapes=[pltpu.VMEM((2 * len(dils) * (GROUP_WIDTH // LANES), tm, LANES), F32)],
        compiler_params=_params("parallel"),
        name="att_out",
    )(*os_, *lses, mem_out, x2, w_out, gain.reshape(1, d))


def _ffn_kernel(x_ref, gpre_ref, wg_ref, wu_ref, wd_ref, gpost_ref, out_ref, *, n_chunks):
    x = x_ref[...]
    h = _rms(x, gpre_ref[...]).astype(BF16)
    d_ff = wg_ref.shape[1]
    step = d_ff // n_chunks
    y = jnp.zeros(x.shape, F32)
    for c in range(n_chunks):
        cols = slice(c * step, (c + 1) * step)
        gate = jnp.dot(h, wg_ref[:, cols], preferred_element_type=F32)
        up = jnp.dot(h, wu_ref[:, cols], preferred_element_type=F32)
        act = (_silu(gate) * up).astype(BF16)
        y = y + jnp.dot(act, wd_ref[cols, :], preferred_element_type=F32)
    out_ref[...] = x + _rms(y, gpost_ref[...])


def _ffn(x2, g_pre, w_gate, w_up, w_down, g_post, tm=512, n_chunks=2):
    t, d = x2.shape
    row = lambda i: (i, 0)
    return pl.pallas_call(
        functools.partial(_ffn_kernel, n_chunks=n_chunks),
        grid=(t // tm,),
        in_specs=[pl.BlockSpec((tm, d), row), _const_spec((1, d)), _const_spec(w_gate.shape),
                  _const_spec(w_up.shape), _const_spec(w_down.shape), _const_spec((1, d))],
        out_specs=pl.BlockSpec((tm, d), row),
        out_shape=jax.ShapeDtypeStruct((t, d), F32),
        compiler_params=_params("parallel"),
        name="ffn",
    )(x2, g_pre.reshape(1, d), w_gate, w_up, w_down, g_post.reshape(1, d))


def _dn_prep_kernel(cur_ref, prev_ref, next_ref, conv_ref, gate_ref, alog_ref, dtb_ref,
                    q_ref, k_ref, v_ref, gb_ref, ext_ref, *, halo, n_dir_heads):
    i = pl.program_id(1)
    last = pl.num_programs(1) - 1
    ts = cur_ref.shape[0]
    width = cur_ref.shape[1]
    tok = width // 3
    n_ext = ts + 2 * halo
    for c0 in range(0, width, LANES):
        cols = slice(c0, c0 + LANES)
        ext = ext_ref.at[(c0 // LANES) % ext_ref.shape[0]]
        ext[0:halo, :] = jnp.where(i > 0, prev_ref[:, cols].astype(F32), 0.0)
        ext[halo:halo + ts, :] = cur_ref[:, cols].astype(F32)
        ext[halo + ts:n_ext, :] = jnp.where(i < last, next_ref[:, cols].astype(F32), 0.0)
        acc = jnp.zeros((ts, LANES), F32)
        for j in range(DN_CONV):
            first = halo + j - DN_CONV // 2
            acc = acc + ext[first:first + ts, :] * conv_ref[j:j + 1, cols]
        y = _silu(acc)
        which, col = divmod(c0, tok)
        if which == 2:
            v_ref[:, col:col + LANES] = y.astype(BF16)
        else:
            yn = y * lax.rsqrt(jnp.sum(y * y, axis=-1, keepdims=True) + EPS)
            if which == 0:
                q_ref[:, col:col + LANES] = (yn * (DN_HEAD_DIM ** -0.5)).astype(BF16)
            else:
                k_ref[:, col:col + LANES] = yn.astype(BF16)
    gate = gate_ref[...]
    z = gate + dtb_ref[...]
    softplus = jnp.maximum(z, 0.0) + jnp.log(1.0 + jnp.exp(-jnp.abs(z)))
    decay = -jnp.exp(alog_ref[...]) * softplus
    beta = 1.0 / (1.0 + jnp.exp(-gate))
    lane = lax.broadcasted_iota(jnp.int32, gate.shape, 1)
    gb_ref[...] = jnp.where(lane < n_dir_heads, decay, beta)


def _dn_prep(qkv, gate, conv_w, a_log_row, dt_bias_row, batch, seq, n_dir_heads, ts=512, halo=16):
    t, width = qkv.shape
    tok = width // 3
    per_seq = seq // ts
    blocks_per_tile = ts // halo
    n_halo_blocks = t // halo

    def cur_map(b, i):
        return (b * per_seq + i, 0)

    def prev_map(b, i):
        return (jnp.maximum((b * per_seq + i) * blocks_per_tile - 1, 0), 0)

    def next_map(b, i):
        return (jnp.minimum((b * per_seq + i + 1) * blocks_per_tile, n_halo_blocks - 1), 0)

    return pl.pallas_call(
        functools.partial(_dn_prep_kernel, halo=halo, n_dir_heads=n_dir_heads),
        grid=(batch, per_seq),
        in_specs=[pl.BlockSpec((ts, width), cur_map), pl.BlockSpec((halo, width), prev_map),
                  pl.BlockSpec((halo, width), next_map), _const_spec(conv_w.shape),
                  pl.BlockSpec((ts, LANES), cur_map), _const_spec((1, LANES)), _const_spec((1, LANES))],
        out_specs=[pl.BlockSpec((ts, tok), cur_map)] * 3 + [pl.BlockSpec((ts, LANES), cur_map)],
        out_shape=[jax.ShapeDtypeStruct((t, tok), BF16)] * 3 + [jax.ShapeDtypeStruct((t, LANES), F32)],
        scratch_shapes=[pltpu.VMEM((2, ts + 2 * halo, LANES), F32)],
        compiler_params=_params("parallel", "arbitrary"),
        name="dn_prep",
    )(qkv, qkv, qkv, conv_w, gate, a_log_row, dt_bias_row)


def _mm(a, b):
    return jnp.dot(a, b, preferred_element_type=F32)


def _mm_nt(a, b):
    return lax.dot_general(a, b, (((1,), (1,)), ((), ())), preferred_element_type=F32)


def _mm_tn(a, b):
    return lax.dot_general(a, b, (((0,), (0,)), ((), ())), preferred_element_type=F32)


def _row_blocks(x, size, parity):
    return jnp.concatenate([x[b * size:(b + 1) * size] for b in range(parity, x.shape[0] // size, 2)], axis=0)


def _interleave_row_blocks(even, odd, size):
    pieces = []
    for b in range(even.shape[0] // size):
        pieces += [even[b * size:(b + 1) * size], odd[b * size:(b + 1) * size]]
    return jnp.concatenate(pieces, axis=0)


def _unit_triangular_inverses(neg_ls, uppers, same_block):
    n = neg_ls[0].shape[0]
    ii = lax.broadcasted_iota(jnp.int32, (n, n), 0)
    jj = lax.broadcasted_iota(jnp.int32, (n, n), 1)
    xs = [jnp.where(same_block[SOLVE_LEAF], nl, 0.0) for nl in neg_ls]
    invs = [jnp.where(ii == jj, 1.0, x) for x in xs]
    xbs = [x.astype(BF16) for x in xs]
    power = 1
    while 2 * power < SOLVE_LEAF:
        xbs = [_mm(xb, xb).astype(BF16) for xb in xbs]
        invs = [inv + _mm(xb, inv.astype(BF16)) for xb, inv in zip(xbs, invs)]
        power *= 2
    size = SOLVE_LEAF
    while size < n:
        merge = same_block[2 * size] & jnp.logical_not(same_block[size])
        hot = [0 if upper else 1 for upper in uppers]
        offs = [_row_blocks(jnp.where(merge, nl, 0.0).astype(BF16), size, p) for nl, p in zip(neg_ls, hot)]
        inv_bs = [inv.astype(BF16) for inv in invs]
        rights = [_mm(off, inv_b).astype(BF16) for off, inv_b in zip(offs, inv_bs)]
        zero = jnp.zeros((n // 2, n), BF16)
        rights = [_interleave_row_blocks(*((r, zero) if p == 0 else (zero, r)), size) for r, p in zip(rights, hot)]
        changes = [_mm(_row_blocks(inv_b, size, p), right) for inv_b, right, p in zip(inv_bs, rights, hot)]
        merged = []
        for inv, change, p in zip(invs, changes, hot):
            kept = _row_blocks(inv, size, 1 - p)
            moved = _row_blocks(inv, size, p) + change
            merged.append(_interleave_row_blocks(*((moved, kept) if p == 0 else (kept, moved)), size))
        invs = merged
        size *= 2
    return invs


def _exact_prefix_sums(x, tri, rows):
    hi = x.astype(BF16).astype(F32)
    mid = (x - hi).astype(BF16).astype(F32)
    lo = (x - hi - mid).astype(BF16).astype(F32)
    if rows:
        n = x.shape[0]
        res = _mm(jnp.concatenate([hi, mid, lo], axis=0).astype(BF16), tri)
        return res[:n] + res[n:2 * n] + res[2 * n:]
    quarter = LANES // 4
    packed = hi + pltpu.roll(mid, quarter, 1) + pltpu.roll(lo, 2 * quarter, 1)
    res = _mm(tri, packed.astype(BF16))
    return res + pltpu.roll(res, LANES - quarter, 1) + pltpu.roll(res, LANES - 2 * quarter, 1)


def _dn_core_kernel(qf_ref, kf_ref, vf_ref, gbf_ref, gtf_ref, qb_ref, kb_ref, vb_ref, gbb_ref, gtb_ref,
                    of_ref, ob_ref, state_ref, *, n_heads):
    @pl.when(pl.program_id(1) == 0)
    def _():
        state_ref[...] = jnp.zeros_like(state_ref)

    c = DN_CHUNK
    d = DN_HEAD_DIM
    n_sub = qf_ref.shape[0] // c
    ii = lax.broadcasted_iota(jnp.int32, (c, c), 0)
    jj = lax.broadcasted_iota(jnp.int32, (c, c), 1)
    lower_eq, upper_eq = jj <= ii, jj >= ii
    lower, upper = jj < ii, jj > ii
    same_block = {}
    size = SOLVE_LEAF
    while size <= c:
        shift = int(math.log2(size))
        same_block[size] = (ii >> shift) == (jj >> shift)
        size *= 2

    items = []
    for direction, (q_ref, k_ref, v_ref, gb_ref, gt_ref, o_ref) in enumerate(
            ((qf_ref, kf_ref, vf_ref, gbf_ref, gtf_ref, of_ref),
             (qb_ref, kb_ref, vb_ref, gbb_ref, gtb_ref, ob_ref))):
        reverse = direction == 1
        tri_col = (upper_eq if reverse else lower_eq).astype(BF16)
        tri_row = (lower_eq if reverse else upper_eq).astype(BF16)
        for sub in range(n_sub):
            rows = slice(sub * c, (sub + 1) * c)
            gb = gb_ref[rows, :]
            lane = lax.broadcasted_iota(jnp.int32, gb.shape, 1)
            cum_col = _exact_prefix_sums(jnp.where(lane < LANES // 4, gb, 0.0), tri_col, rows=False)
            cum_row = _exact_prefix_sums(gt_ref[:, rows], tri_row, rows=True)
            for head in range(n_heads):
                g_idx = direction * n_heads + head
                b_idx = 2 * n_heads + g_idx
                cols = slice(head * d, (head + 1) * d)
                items.append(dict(
                    direction=direction, head=head, sub=sub, o_ref=o_ref, rows=rows, cols=cols, reverse=reverse,
                    earlier_eq=upper_eq if reverse else lower_eq,
                    earlier=upper if reverse else lower,
                    q=q_ref[rows, cols], k=k_ref[rows, cols], v=v_ref[rows, cols],
                    g_col=cum_col[:, g_idx:g_idx + 1], beta_col=gb[:, b_idx:b_idx + 1],
                    g_row=cum_row[g_idx:g_idx + 1, :],
                    g_last=cum_col[0:1, g_idx:g_idx + 1] if reverse else cum_col[c - 1:c, g_idx:g_idx + 1]))

    grams = [_mm_nt(jnp.concatenate([it["k"], it["q"]], axis=0), it["k"]) for it in items]
    neg_ls, rhss = [], []
    for it, gram in zip(items, grams):
        g_wide = jnp.broadcast_to(it["g_col"], (c, d))
        beta_wide = jnp.broadcast_to(it["beta_col"], (c, d))
        it["g_last_wide"] = jnp.broadcast_to(it["g_last"], (c, d))
        decay = jnp.exp(jnp.where(it["earlier_eq"], g_wide - it["g_row"], NEG_INF))
        neg_ls.append(jnp.where(it["earlier"], gram[:c] * decay, 0.0) * (-beta_wide))
        it["intra"] = jnp.where(it["earlier_eq"], gram[c:] * decay, 0.0).astype(BF16)
        kf = it["k"].astype(F32)
        e_g = jnp.exp(g_wide)
        rhss.append(jnp.concatenate([(it["v"].astype(F32) * beta_wide).astype(BF16),
                                     (kf * (beta_wide * e_g)).astype(BF16)], axis=-1))
        it["qg"] = (it["q"].astype(F32) * e_g).astype(BF16)
        it["k_dec"] = (kf * jnp.exp(it["g_last_wide"] - g_wide)).astype(BF16)
    invs = _unit_triangular_inverses(neg_ls, [it["reverse"] for it in items], same_block)
    for it, inv, rhs in zip(items, invs, rhss):
        it["sol"] = _mm(inv.astype(BF16), rhs)

    for step in range(n_sub):
        live = [it for it in items if it["sub"] == (n_sub - 1 - step if it["reverse"] else step)]
        states = [state_ref[it["direction"], it["head"]] for it in live]
        wss = [_mm(jnp.concatenate([it["sol"][:, d:].astype(BF16), it["qg"]], axis=0), state.astype(BF16))
               for it, state in zip(live, states)]
        v_news = [(it["sol"][:, :d] - ws[:c]).astype(BF16) for it, ws in zip(live, wss)]
        outs = [ws[c:] + _mm(it["intra"], v_new) for it, ws, v_new in zip(live, wss, v_news)]
        updates = [_mm_tn(it["k_dec"], v_new) for it, v_new in zip(live, v_news)]
        for it, out, state, update in zip(live, outs, states, updates):
            it["o_ref"][it["rows"], it["cols"]] = out
            state_ref[it["direction"], it["head"]] = state * jnp.exp(it["g_last_wide"]) + update


def _dn_core(q, k, v, gb, gbt, batch, seq, chunks_per_step=2):
    t, tok = q.shape
    n_heads = tok // DN_HEAD_DIM
    rows = chunks_per_step * DN_CHUNK
    steps = seq // rows
    n_rows = gbt.shape[0]

    fwd = lambda b, i: (b * steps + i, 0)
    bwd = lambda b, i: (b * steps + steps - 1 - i, 0)
    fwd_t = lambda b, i: (0, b * steps + i)
    bwd_t = lambda b, i: (0, b * steps + steps - 1 - i)

    def side(tok_map, t_map):
        return [pl.BlockSpec((rows, tok), tok_map)] * 3 + [
            pl.BlockSpec((rows, LANES), tok_map), pl.BlockSpec((n_rows, rows), t_map)]

    return pl.pallas_call(
        functools.partial(_dn_core_kernel, n_heads=n_heads),
        grid=(batch, steps),
        in_specs=side(fwd, fwd_t) + side(bwd, bwd_t),
        out_specs=[pl.BlockSpec((rows, tok), fwd), pl.BlockSpec((rows, tok), bwd)],
        out_shape=[jax.ShapeDtypeStruct((t, tok), F32)] * 2,
        scratch_shapes=[pltpu.VMEM((2, n_heads, DN_HEAD_DIM, DN_HEAD_DIM), F32)],
        compiler_params=_params("parallel", "arbitrary"),
        name="dn_core",
    )(q, k, v, gb, gbt, q, k, v, gb, gbt)


def _dn_out_kernel(of_ref, ob_ref, z_ref, mem_ref, x_ref, onorm_ref, w_ref, g_ref, out_ref):
    tok = of_ref.shape[1]
    parts = []
    for c0 in range(0, tok, DN_HEAD_DIM):
        cols = slice(c0, c0 + DN_HEAD_DIM)
        o = of_ref[:, cols] + ob_ref[:, cols]
        o = o * lax.rsqrt(jnp.mean(o * o, axis=-1, keepdims=True) + EPS) * onorm_ref[...]
        parts.append((o * _silu(z_ref[:, cols].astype(F32))).astype(BF16))
    mixed = jnp.concatenate(parts + [mem_ref[...]], axis=-1)
    y = jnp.dot(mixed, w_ref[...], preferred_element_type=F32)
    out_ref[...] = x_ref[...] + _rms(y, g_ref[...])


def _dn_out(o_f, o_b, z, mem_out, x2, out_norm, w_out, gain, tm=512):
    t, d = x2.shape
    tok = o_f.shape[1]
    row = lambda i: (i, 0)
    return pl.pallas_call(
        _dn_out_kernel,
        grid=(t // tm,),
        in_specs=[pl.BlockSpec((tm, tok), row)] * 3
        + [pl.BlockSpec((tm, mem_out.shape[1]), row), pl.BlockSpec((tm, d), row),
           _const_spec((1, DN_HEAD_DIM)), _const_spec(w_out.shape), _const_spec((1, d))],
        out_specs=pl.BlockSpec((tm, d), row),
        out_shape=jax.ShapeDtypeStruct((t, d), F32),
        compiler_params=_params("parallel"),
        name="dn_out",
    )(o_f, o_b, z, mem_out, x2, out_norm.reshape(1, DN_HEAD_DIM), w_out, gain.reshape(1, d))


def _memory_branch(q_mem, mem, mem_gain, w_kv, seq):
    k_mem, v_mem = _mem_kv(mem, mem_gain, w_kv.astype(BF16))
    return _mem_attn(q_mem, k_mem, v_mem, seq)


def _attention_layer(x2, mem, batch, seq, rel_bias, w_in, w_out, mem_gain, w_kv, g_pre, g_post):
    n_groups = len(DILATED_GROUPS)
    tok = n_groups * GROUP_WIDTH
    dils = tuple(dil for _, dil in DILATED_GROUPS)
    w_qkv = w_in[:, :3 * tok].reshape(-1, 3, n_groups, GROUP_WIDTH).transpose(0, 2, 1, 3).astype(BF16)
    group_weights = [w_qkv[:, g].reshape(-1, 3 * GROUP_WIDTH) for g in range(n_groups)]
    *qkvs, q_mem = _att_proj(x2, g_pre, group_weights, w_in[:, 3 * tok:].astype(BF16), dils)
    os_, lses = [], []
    for group, dil in enumerate(dils):
        o, lse = _band_attn(qkvs[group], _band_bias(rel_bias, group, dil), batch, seq, dil)
        os_.append(o)
        lses.append(lse)
    mem_out = _memory_branch(q_mem, mem, mem_gain, w_kv, seq)
    return _att_out(os_, lses, dils, mem_out, x2, w_out.astype(BF16), g_post)


def _deltanet_layer(x2, mem, batch, seq, w_in, conv_w, a_log, dt_bias, out_norm, w_out, mem_gain, w_kv,
                    g_pre, g_post):
    d = x2.shape[1]
    n_heads = a_log.shape[1]
    tok = n_heads * DN_HEAD_DIM
    n_gate = 4 * n_heads
    w_gate = w_in[:, 4 * tok:4 * tok + n_gate].reshape(d, 2, 2, n_heads).transpose(0, 2, 1, 3).reshape(d, n_gate)
    w_gate = jnp.pad(w_gate, ((0, 0), (0, LANES - n_gate)))
    weights = [w_in[:, :3 * tok].astype(BF16), w_in[:, 3 * tok:4 * tok].astype(BF16), w_gate.astype(BF16),
               w_in[:, 4 * tok + n_gate:].astype(BF16)]
    qkv, z, gate, q_mem = _norm_proj(x2, g_pre, weights, [BF16, BF16, F32, BF16])
    pad_row = lambda p: jnp.pad(p.reshape(1, -1).astype(F32), ((0, 0), (0, LANES - 2 * n_heads)))
    q, k, v, gb = _dn_prep(qkv, gate, conv_w.astype(F32), pad_row(a_log), pad_row(dt_bias), batch, seq,
                           2 * n_heads)
    gbt = jnp.transpose(gb[:, :8 * ((n_gate + 7) // 8)])
    o_f, o_b = _dn_core(q, k, v, gb, gbt, batch, seq)
    mem_out = _memory_branch(q_mem, mem, mem_gain, w_kv, seq)
    return _dn_out(o_f, o_b, z, mem_out, x2, out_norm, w_out.astype(BF16), g_post)


def kernel(x, mem, rel_bias, att_w_in, att_w_out, dn_w_in, dn_conv, dn_a_log, dn_dt_bias, dn_out_norm,
           dn_w_out, mem_norm, mem_w_kv, norm_mix_pre, norm_mix_post, norm_ffn_pre, norm_ffn_post,
           ffn_w_gate_up, ffn_w_down):
    batch, seq, d = x.shape
    depth = norm_mix_pre.shape[0]
    n_mixers = 2
    x2 = x.reshape(batch * seq, d)
    for i in range(depth):
        j = i // n_mixers
        if i % n_mixers == 0:
            x2 = _attention_layer(x2, mem, batch, seq, rel_bias, att_w_in[j], att_w_out[j], mem_norm[i],
                                  mem_w_kv[i], norm_mix_pre[i], norm_mix_post[i])
        else:
            x2 = _deltanet_layer(x2, mem, batch, seq, dn_w_in[j], dn_conv[j], dn_a_log[j], dn_dt_bias[j],
                                 dn_out_norm[j], dn_w_out[j], mem_norm[i], mem_w_kv[i], norm_mix_pre[i],
                                 norm_mix_post[i])
        d_ff = ffn_w_down.shape[1]
        w_gu = ffn_w_gate_up[i].astype(BF16)
        x2 = _ffn(x2, norm_ffn_pre[i], w_gu[:, :d_ff], w_gu[:, d_ff:], ffn_w_down[i].astype(BF16),
                  norm_ffn_post[i])
    return x2.reshape(batch, seq, d)
```

```python
import functools
import math

import jax
import jax.numpy as jnp
import numpy as np
from jax import lax
from jax.experimental import pallas as pl
from jax.experimental.pallas import tpu as pltpu

F32 = jnp.float32
BF16 = jnp.bfloat16

EPS = 1e-6
NEG_INF = -1e30

LANES = 128
VMEM_LIMIT_BYTES = 56 * 1024 * 1024

DILATED_GROUPS = ((128, 1), (512, 4), (2048, 16))
ATT_HEAD_DIM = 64
HEADS_PER_GROUP = 4
GROUP_WIDTH = HEADS_PER_GROUP * ATT_HEAD_DIM
BAND_HALF = 64
REL_BUCKETS = 32
REL_MAX_DIST = 1024
MEM_HEADS = 4
MEM_HEAD_DIM = 64
DN_HEAD_DIM = 128
DN_CONV = 5
DN_CHUNK = 128
SOLVE_LEAF = 16

ATT_Q_BLOCK = 128
ATT_K_WINDOW = ATT_Q_BLOCK + 2 * BAND_HALF


def _params(*semantics):
    return pltpu.CompilerParams(dimension_semantics=semantics, vmem_limit_bytes=VMEM_LIMIT_BYTES)


def _rms(x, gain):
    return x * lax.rsqrt(jnp.mean(x * x, axis=-1, keepdims=True) + EPS) * gain


def _silu(x):
    return x / (1.0 + jnp.exp(-x))


def _const_spec(shape):
    return pl.BlockSpec(shape, lambda *_: (0,) * len(shape))


def _norm_proj_kernel(x_ref, g_ref, *refs, n_out):
    h = _rms(x_ref[...], g_ref[...]).astype(BF16)
    for w_ref, o_ref in zip(refs[:n_out], refs[n_out:]):
        o_ref[...] = jnp.dot(h, w_ref[...], preferred_element_type=F32).astype(o_ref.dtype)


def _norm_proj(x2, gain, weights, out_dtypes, tm=512):
    t, d = x2.shape
    n_out = len(weights)
    return pl.pallas_call(
        functools.partial(_norm_proj_kernel, n_out=n_out),
        grid=(t // tm,),
        in_specs=[pl.BlockSpec((tm, d), lambda i: (i, 0)), _const_spec((1, d))]
        + [pl.BlockSpec((d, width), functools.partial(lambda i, blk: (0, blk), blk=blk)) for _, width, blk in weights],
        out_specs=[pl.BlockSpec((tm, width), lambda i: (i, 0)) for _, width, _ in weights],
        out_shape=[jax.ShapeDtypeStruct((t, width), dt) for (_, width, _), dt in zip(weights, out_dtypes)],
        compiler_params=_params("parallel"),
        name="norm_proj",
    )(x2, gain.reshape(1, d), *[w for w, _, _ in weights])


def _mem_kv_kernel(mem_ref, g_ref, w_ref, k_ref, v_ref):
    h = _rms(mem_ref[0], g_ref[...]).astype(BF16)
    kv = jnp.dot(h, w_ref[...], preferred_element_type=F32)
    width = k_ref.shape[-1]
    k_ref[0] = kv[:, :width].astype(BF16)
    v_ref[0] = kv[:, width:].astype(BF16)


def _mem_kv(mem, gain, w_kv):
    b, m, d = mem.shape
    width = w_kv.shape[1] // 2
    return pl.pallas_call(
        _mem_kv_kernel,
        grid=(b,),
        in_specs=[pl.BlockSpec((1, m, d), lambda i: (i, 0, 0)), _const_spec((1, d)), _const_spec(w_kv.shape)],
        out_specs=[pl.BlockSpec((1, m, width), lambda i: (i, 0, 0))] * 2,
        out_shape=[jax.ShapeDtypeStruct((b, m, width), BF16)] * 2,
        compiler_params=_params("parallel"),
        name="mem_kv",
    )(mem, gain.reshape(1, d), w_kv)


def _mem_attn_kernel(q_ref, k_ref, v_ref, o_ref):
    q = q_ref[...]
    k = k_ref[0]
    v = v_ref[0]
    lane = lax.broadcasted_iota(jnp.int32, q.shape, 1)
    in_head = [(lane >= h * MEM_HEAD_DIM) & (lane < (h + 1) * MEM_HEAD_DIM) for h in range(MEM_HEADS)]
    logits = [lax.dot_general(jnp.where(mask, q, jnp.zeros_like(q)), k, (((1,), (1,)), ((), ())),
                              preferred_element_type=F32) for mask in in_head]
    probs = []
    for s in logits:
        s = s * (MEM_HEAD_DIM ** -0.5)
        p = jnp.exp(s - jnp.max(s, axis=-1, keepdims=True))
        probs.append((p / jnp.sum(p, axis=-1, keepdims=True)).astype(BF16))
    pvs = [jnp.dot(p, v, preferred_element_type=F32) for p in probs]
    out = pvs[-1]
    for mask, pv in zip(in_head[:-1], pvs[:-1]):
        out = jnp.where(mask, pv, out)
    o_ref[...] = out.astype(o_ref.dtype)


def _mem_attn(q_mem, k, v, seq, tq=512):
    t, width = q_mem.shape
    m = k.shape[1]
    per_seq = seq // tq
    return pl.pallas_call(
        _mem_attn_kernel,
        grid=(t // tq,),
        in_specs=[pl.BlockSpec((tq, width), lambda i: (i, 0)),
                  pl.BlockSpec((1, m, width), lambda i: (i // per_seq, 0, 0)),
                  pl.BlockSpec((1, m, width), lambda i: (i // per_seq, 0, 0))],
        out_specs=pl.BlockSpec((tq, width), lambda i: (i, 0)),
        out_shape=jax.ShapeDtypeStruct((t, width), BF16),
        compiler_params=_params("parallel"),
        name="mem_attn",
    )(q_mem, k, v)


def _t5_bucket(rel):
    half = REL_BUCKETS // 2
    max_exact = half // 2
    n = np.abs(rel)
    large = max_exact + (np.log(np.maximum(n, 1) / max_exact) / math.log(REL_MAX_DIST / max_exact)
                         * (half - max_exact)).astype(np.int64)
    large = np.minimum(large, half - 1)
    return ((rel > 0) * half + np.where(n < max_exact, n, large)).astype(np.int32)


def _band_bias(rel_bias, group, dil):
    period = ATT_Q_BLOCK + ATT_K_WINDOW
    heads = slice(group * HEADS_PER_GROUP, (group + 1) * HEADS_PER_GROUP)
    tables = []
    for variant in range(3):
        rel = np.arange(period) - (ATT_Q_BLOCK - 1) - variant * BAND_HALF
        in_band = np.abs(rel) <= BAND_HALF
        bucket = _t5_bucket(np.clip(rel, -BAND_HALF, BAND_HALF) * dil)
        f = jnp.where(in_band[None, :], jnp.transpose(rel_bias[bucket][:, heads]).astype(F32), NEG_INF)
        flat = jnp.tile(f, (1, ATT_Q_BLOCK))[:, :ATT_Q_BLOCK * (period - 1)]
        skew = flat.reshape(HEADS_PER_GROUP, ATT_Q_BLOCK, period - 1)
        tables.append(skew[:, :, ATT_Q_BLOCK - 1:ATT_Q_BLOCK - 1 + ATT_K_WINDOW])
    return jnp.stack(tables)


def _band_attn_kernel(x_ref, bias_ref, o_ref, lse_ref, *, sub_len, n_res, q_tile):
    lane = lax.broadcasted_iota(jnp.int32, (ATT_Q_BLOCK, LANES), 1)
    low_half = lane < ATT_HEAD_DIM
    n_pairs = GROUP_WIDTH // LANES

    def tile(t, carry):
        blocks = []
        for res in range(n_res):
            for blk in range(q_tile // ATT_Q_BLOCK):
                i0 = pl.multiple_of(t * q_tile + blk * ATT_Q_BLOCK, ATT_Q_BLOCK)
                ws = jnp.clip(i0 - BAND_HALF, 0, sub_len - ATT_K_WINDOW)
                variant = (i0 - ws) // BAND_HALF
                ws = pl.multiple_of(ws, BAND_HALF)
                base = res * 3 * GROUP_WIDTH
                blocks.append(dict(
                    res=res, rows=pl.ds(i0, ATT_Q_BLOCK), variant=variant,
                    q=x_ref[0, pl.ds(i0, ATT_Q_BLOCK), base:base + GROUP_WIDTH],
                    k=x_ref[0, pl.ds(ws, ATT_K_WINDOW), base + GROUP_WIDTH:base + 2 * GROUP_WIDTH],
                    v=x_ref[0, pl.ds(ws, ATT_K_WINDOW), base + 2 * GROUP_WIDTH:base + 3 * GROUP_WIDTH]))
        heads = [(b, pair, a) for b in blocks for pair in range(n_pairs) for a in range(2)]
        logits = []
        for b, pair, a in heads:
            cols = slice(pair * LANES, (pair + 1) * LANES)
            q_pair = b["q"][:, cols]
            keep = low_half if a == 0 else jnp.logical_not(low_half)
            qh = jnp.where(keep, q_pair, jnp.zeros_like(q_pair))
            logits.append(_mm_nt(qh, b["k"][:, cols]))
        probs, dens, lses = [], [], []
        for (b, pair, a), s in zip(heads, logits):
            s = s * (ATT_HEAD_DIM ** -0.5) + bias_ref[b["variant"], 2 * pair + a]
            m = jnp.max(s, axis=-1, keepdims=True)
            p = jnp.exp(s - m)
            den = jnp.sum(p, axis=-1, keepdims=True)
            probs.append(p.astype(BF16))
            dens.append(den)
            lses.append(m + jnp.log(den))
        pvs = [_mm(p, b["v"][:, pair * LANES:(pair + 1) * LANES]) for (b, pair, a), p in zip(heads, probs)]
        for idx in range(0, len(heads), 2):
            b, pair, _ = heads[idx]
            cols = pl.ds(b["res"] * GROUP_WIDTH + pair * LANES, LANES)
            out = jnp.where(low_half, pvs[idx] / dens[idx], pvs[idx + 1] / dens[idx + 1])
            lse = jnp.where(low_half, jnp.broadcast_to(lses[idx], (ATT_Q_BLOCK, LANES)),
                            jnp.broadcast_to(lses[idx + 1], (ATT_Q_BLOCK, LANES)))
            o_ref[0, b["rows"], cols] = out
            lse_ref[0, b["rows"], cols] = lse
        return carry

    lax.fori_loop(0, sub_len // q_tile, tile, 0)


def _band_attn(qkv, bias, batch, seq, dil, n_res, q_tile=256):
    rows, width = qkv.shape
    sub_len = seq // dil
    qkv3 = qkv.reshape(batch, sub_len, width)
    x_spec = pl.BlockSpec((1, sub_len, n_res * 3 * GROUP_WIDTH), lambda b, r: (b, 0, r))
    o_spec = pl.BlockSpec((1, sub_len, n_res * GROUP_WIDTH), lambda b, r: (b, 0, r))
    o, lse = pl.pallas_call(
        functools.partial(_band_attn_kernel, sub_len=sub_len, n_res=n_res, q_tile=q_tile),
        grid=(batch, dil // n_res),
        in_specs=[x_spec, _const_spec(bias.shape)],
        out_specs=[o_spec, o_spec],
        out_shape=[jax.ShapeDtypeStruct((batch, sub_len, dil * GROUP_WIDTH), F32)] * 2,
        compiler_params=_params("parallel", "parallel"),
        name=f"band_attn_d{dil}",
    )(qkv3, bias)
    return o.reshape(rows, dil * GROUP_WIDTH), lse.reshape(rows, dil * GROUP_WIDTH)


def _att_proj_kernel(x_ref, g_ref, *refs, dils):
    n = len(dils)
    w_refs, wmem_ref = refs[:n], refs[n]
    out_refs, qmem_ref, stage_ref = refs[n + 1:2 * n + 1], refs[2 * n + 1], refs[2 * n + 2]
    h = _rms(x_ref[...], g_ref[...]).astype(BF16)
    tm = h.shape[0]
    for w_ref, o_ref, dil in zip(w_refs, out_refs, dils):
        y = jnp.dot(h, w_ref[...], preferred_element_type=F32)
        width = y.shape[1]
        if dil == 1:
            o_ref[...] = y.astype(BF16)
        else:
            for c in range(width // LANES):
                stage_ref[c] = y[:, c * LANES:(c + 1) * LANES]
            for r in range(dil):
                for c in range(width // LANES):
                    lanes = pl.ds(r * width + c * LANES, LANES)
                    o_ref[:, lanes] = stage_ref[c, pl.ds(r, tm // dil, stride=dil), :].astype(BF16)
    qmem_ref[...] = jnp.dot(h, wmem_ref[...], preferred_element_type=F32).astype(BF16)


def _att_proj(x2, gain, group_weights, w_mem, dils, tm=512):
    t, d = x2.shape
    width = group_weights[0].shape[1]
    row = lambda i: (i, 0)
    return pl.pallas_call(
        functools.partial(_att_proj_kernel, dils=dils),
        grid=(t // tm,),
        in_specs=[pl.BlockSpec((tm, d), row), _const_spec((1, d))]
        + [_const_spec(w.shape) for w in group_weights] + [_const_spec(w_mem.shape)],
        out_specs=[pl.BlockSpec((tm // dil, dil * width), row) for dil in dils]
        + [pl.BlockSpec((tm, w_mem.shape[1]), row)],
        out_shape=[jax.ShapeDtypeStruct((t // dil, dil * width), BF16) for dil in dils]
        + [jax.ShapeDtypeStruct((t, w_mem.shape[1]), BF16)],
        scratch_shapes=[pltpu.VMEM((width // LANES, tm, LANES), F32)],
        compiler_params=_params("parallel"),
        name="att_proj",
    )(x2, gain.reshape(1, d), *group_weights, w_mem)


def _att_out_kernel(*refs, dils):
    n = len(dils)
    o_refs, l_refs = refs[:n], refs[n:2 * n]
    mem_ref, x_ref, w_ref, g_ref, out_ref, stage_ref = refs[2 * n:]
    tm = x_ref.shape[0]

    def token_major(ref, dil, slot):
        if dil == 1:
            return ref[...]
        tiles = GROUP_WIDTH // LANES
        for r in range(dil):
            for c in range(tiles):
                lanes = pl.ds(r * GROUP_WIDTH + c * LANES, LANES)
                stage_ref[slot * tiles + c, pl.ds(r, tm // dil, stride=dil), :] = ref[:, lanes]
        return jnp.concatenate([stage_ref[slot * tiles + c] for c in range(tiles)], axis=-1)

    lses = [token_major(ref, dil, 2 * i) for i, (ref, dil) in enumerate(zip(l_refs, dils))]
    outs = [token_major(ref, dil, 2 * i + 1) for i, (ref, dil) in enumerate(zip(o_refs, dils))]
    mx = functools.reduce(jnp.maximum, lses)
    es = [jnp.exp(l - mx) for l in lses]
    tot = functools.reduce(lambda a, b: a + b, es)
    parts = [(o * (e / tot)).astype(BF16) for o, e in zip(outs, es)]
    mixed = jnp.concatenate(parts + [mem_ref[...]], axis=-1)
    y = jnp.dot(mixed, w_ref[...], preferred_element_type=F32)
    out_ref[...] = x_ref[...] + _rms(y, g_ref[...])


def _att_out(os_, lses, dils, mem_out, x2, w_out, gain, tm=512):
    t, d = x2.shape
    row = lambda i: (i, 0)
    group_specs = [pl.BlockSpec((tm // dil, dil * GROUP_WIDTH), row) for dil in dils]
    return pl.pallas_call(
        functools.partial(_att_out_kernel, dils=dils),
        grid=(t // tm,),
        in_specs=group_specs + group_specs
        + [pl.BlockSpec((tm, mem_out.shape[1]), row), pl.BlockSpec((tm, d), row),
           _const_spec(w_out.shape), _const_spec((1, d))],
        out_specs=pl.BlockSpec((tm, d), row),
        out_shape=jax.ShapeDtypeStruct((t, d), F32),
        scratch_shapes=[pltpu.VMEM((2 * len(dils) * (GROUP_WIDTH // LANES), tm, LANES), F32)],
        compiler_params=_params("parallel"),
        name="att_out",
    )(*os_, *lses, mem_out, x2, w_out, gain.reshape(1, d))


def _ffn_kernel(x_ref, gpre_ref, wg_ref, wu_ref, wd_ref, gpost_ref, out_ref, *, n_chunks):
    x = x_ref[...]
    h = _rms(x, gpre_ref[...]).astype(BF16)
    d_ff = wg_ref.shape[1]
    step = d_ff // n_chunks
    y = jnp.zeros(x.shape, F32)
    for c in range(n_chunks):
        cols = slice(c * step, (c + 1) * step)
        gate = jnp.dot(h, wg_ref[:, cols], preferred_element_type=F32)
        up = jnp.dot(h, wu_ref[:, cols], preferred_element_type=F32)
        act = (_silu(gate) * up).astype(BF16)
        y = y + jnp.dot(act, wd_ref[cols, :], preferred_element_type=F32)
    out_ref[...] = x + _rms(y, gpost_ref[...])


def _ffn(x2, g_pre, w_gate_up, w_down, g_post, tm=512, n_chunks=2):
    t, d = x2.shape
    d_ff = w_down.shape[0]
    row = lambda i: (i, 0)
    return pl.pallas_call(
        functools.partial(_ffn_kernel, n_chunks=n_chunks),
        grid=(t // tm,),
        in_specs=[pl.BlockSpec((tm, d), row), _const_spec((1, d)),
                  pl.BlockSpec((d, d_ff), lambda i: (0, 0)), pl.BlockSpec((d, d_ff), lambda i: (0, 1)),
                  _const_spec(w_down.shape), _const_spec((1, d))],
        out_specs=pl.BlockSpec((tm, d), row),
        out_shape=jax.ShapeDtypeStruct((t, d), F32),
        compiler_params=_params("parallel"),
        name="ffn",
    )(x2, g_pre.reshape(1, d), w_gate_up, w_gate_up, w_down, g_post.reshape(1, d))


def _dn_prep_kernel(cur_ref, prev_ref, next_ref, conv_ref, gate_ref, alog_ref, dtb_ref,
                    q_ref, k_ref, v_ref, gb_ref, gbt_ref, ext_ref, *, halo, n_dir_heads):
    i = pl.program_id(1)
    last = pl.num_programs(1) - 1
    ts = cur_ref.shape[0]
    width = cur_ref.shape[1]
    tok = width // 3
    n_ext = ts + 2 * halo
    for c0 in range(0, width, LANES):
        cols = slice(c0, c0 + LANES)
        ext = ext_ref.at[(c0 // LANES) % ext_ref.shape[0]]
        ext[0:halo, :] = jnp.where(i > 0, prev_ref[:, cols].astype(F32), 0.0)
        ext[halo:halo + ts, :] = cur_ref[:, cols].astype(F32)
        ext[halo + ts:n_ext, :] = jnp.where(i < last, next_ref[:, cols].astype(F32), 0.0)
        acc = jnp.zeros((ts, LANES), F32)
        for j in range(DN_CONV):
            first = halo + j - DN_CONV // 2
            acc = acc + ext[first:first + ts, :] * conv_ref[j:j + 1, cols]
        y = _silu(acc)
        which, col = divmod(c0, tok)
        if which == 2:
            v_ref[:, col:col + LANES] = y.astype(BF16)
        else:
            yn = y * lax.rsqrt(jnp.sum(y * y, axis=-1, keepdims=True) + EPS)
            if which == 0:
                q_ref[:, col:col + LANES] = (yn * (DN_HEAD_DIM ** -0.5)).astype(BF16)
            else:
                k_ref[:, col:col + LANES] = yn.astype(BF16)
    gate = gate_ref[...]
    z = gate + dtb_ref[...]
    softplus = jnp.maximum(z, 0.0) + jnp.log(1.0 + jnp.exp(-jnp.abs(z)))
    decay = -jnp.exp(alog_ref[...]) * softplus
    beta = 1.0 / (1.0 + jnp.exp(-gate))
    lane = lax.broadcasted_iota(jnp.int32, gate.shape, 1)
    gb = jnp.where(lane < n_dir_heads, decay, beta)
    gb_ref[...] = gb
    gbt_ref[...] = jnp.transpose(gb)[:gbt_ref.shape[0], :]


def _dn_prep(qkv, gate, conv_w, a_log_row, dt_bias_row, batch, seq, n_dir_heads, ts=512, halo=16):
    t, width = qkv.shape
    n_gate_rows = 8 * ((2 * n_dir_heads + 7) // 8)
    tok = width // 3
    per_seq = seq // ts
    blocks_per_tile = ts // halo
    n_halo_blocks = t // halo

    def cur_map(b, i):
        return (b * per_seq + i, 0)

    def prev_map(b, i):
        return (jnp.maximum((b * per_seq + i) * blocks_per_tile - 1, 0), 0)

    def next_map(b, i):
        return (jnp.minimum((b * per_seq + i + 1) * blocks_per_tile, n_halo_blocks - 1), 0)

    return pl.pallas_call(
        functools.partial(_dn_prep_kernel, halo=halo, n_dir_heads=n_dir_heads),
        grid=(batch, per_seq),
        in_specs=[pl.BlockSpec((ts, width), cur_map), pl.BlockSpec((halo, width), prev_map),
                  pl.BlockSpec((halo, width), next_map), _const_spec(conv_w.shape),
                  pl.BlockSpec((ts, LANES), cur_map), _const_spec((1, LANES)), _const_spec((1, LANES))],
        out_specs=[pl.BlockSpec((ts, tok), cur_map)] * 3
        + [pl.BlockSpec((ts, LANES), cur_map), pl.BlockSpec((n_gate_rows, ts), lambda b, i: (0, b * per_seq + i))],
        out_shape=[jax.ShapeDtypeStruct((t, tok), BF16)] * 3
        + [jax.ShapeDtypeStruct((t, LANES), F32), jax.ShapeDtypeStruct((n_gate_rows, t), F32)],
        scratch_shapes=[pltpu.VMEM((2, ts + 2 * halo, LANES), F32)],
        compiler_params=_params("parallel", "arbitrary"),
        name="dn_prep",
    )(qkv, qkv, qkv, conv_w, gate, a_log_row, dt_bias_row)


def _mm(a, b):
    return jnp.dot(a, b, preferred_element_type=F32)


def _mm_nt(a, b):
    return lax.dot_general(a, b, (((1,), (1,)), ((), ())), preferred_element_type=F32)


def _mm_tn(a, b):
    return lax.dot_general(a, b, (((0,), (0,)), ((), ())), preferred_element_type=F32)


def _row_blocks(x, size, parity):
    return jnp.concatenate([x[b * size:(b + 1) * size] for b in range(parity, x.shape[0] // size, 2)], axis=0)


def _interleave_row_blocks(even, odd, size):
    pieces = []
    for b in range(even.shape[0] // size):
        pieces += [even[b * size:(b + 1) * size], odd[b * size:(b + 1) * size]]
    return jnp.concatenate(pieces, axis=0)


def _unit_triangular_inverses(neg_ls, uppers, same_block):
    n = neg_ls[0].shape[0]
    ii = lax.broadcasted_iota(jnp.int32, (n, n), 0)
    jj = lax.broadcasted_iota(jnp.int32, (n, n), 1)
    xs = [jnp.where(same_block[SOLVE_LEAF], nl, 0.0) for nl in neg_ls]
    invs = [jnp.where(ii == jj, 1.0, x) for x in xs]
    xbs = [x.astype(BF16) for x in xs]
    power = 1
    while 2 * power < SOLVE_LEAF:
        xbs = [_mm(xb, xb).astype(BF16) for xb in xbs]
        invs = [inv + _mm(xb, inv.astype(BF16)) for xb, inv in zip(xbs, invs)]
        power *= 2
    size = SOLVE_LEAF
    while size < n:
        merge = same_block[2 * size] & jnp.logical_not(same_block[size])
        hot = [0 if upper else 1 for upper in uppers]
        offs = [_row_blocks(jnp.where(merge, nl, 0.0).astype(BF16), size, p) for nl, p in zip(neg_ls, hot)]
        inv_bs = [inv.astype(BF16) for inv in invs]
        rights = [_mm(off, inv_b).astype(BF16) for off, inv_b in zip(offs, inv_bs)]
        zero = jnp.zeros((n // 2, n), BF16)
        rights = [_interleave_row_blocks(*((r, zero) if p == 0 else (zero, r)), size) for r, p in zip(rights, hot)]
        changes = [_mm(_row_blocks(inv_b, size, p), right) for inv_b, right, p in zip(inv_bs, rights, hot)]
        merged = []
        for inv, change, p in zip(invs, changes, hot):
            kept = _row_blocks(inv, size, 1 - p)
            moved = _row_blocks(inv, size, p) + change
            merged.append(_interleave_row_blocks(*((moved, kept) if p == 0 else (kept, moved)), size))
        invs = merged
        size *= 2
    return invs


def _exact_prefix_sums(x, tri, rows):
    hi = x.astype(BF16).astype(F32)
    mid = (x - hi).astype(BF16).astype(F32)
    lo = (x - hi - mid).astype(BF16).astype(F32)
    if rows:
        n = x.shape[0]
        res = _mm(jnp.concatenate([hi, mid, lo], axis=0).astype(BF16), tri)
        return res[:n] + res[n:2 * n] + res[2 * n:]
    quarter = LANES // 4
    packed = hi + pltpu.roll(mid, quarter, 1) + pltpu.roll(lo, 2 * quarter, 1)
    res = _mm(tri, packed.astype(BF16))
    return res + pltpu.roll(res, LANES - quarter, 1) + pltpu.roll(res, LANES - 2 * quarter, 1)


def _dn_core_kernel(qf_ref, kf_ref, vf_ref, gbf_ref, gtf_ref, qb_ref, kb_ref, vb_ref, gbb_ref, gtb_ref,
                    of_ref, ob_ref, state_ref, *, n_heads):
    @pl.when(pl.program_id(1) == 0)
    def _():
        state_ref[...] = jnp.zeros_like(state_ref)

    c = DN_CHUNK
    d = DN_HEAD_DIM
    n_sub = qf_ref.shape[0] // c
    ii = lax.broadcasted_iota(jnp.int32, (c, c), 0)
    jj = lax.broadcasted_iota(jnp.int32, (c, c), 1)
    lower_eq, upper_eq = jj <= ii, jj >= ii
    lower, upper = jj < ii, jj > ii
    same_block = {}
    size = SOLVE_LEAF
    while size <= c:
        shift = int(math.log2(size))
        same_block[size] = (ii >> shift) == (jj >> shift)
        size *= 2

    items = []
    for direction, (q_ref, k_ref, v_ref, gb_ref, gt_ref, o_ref) in enumerate(
            ((qf_ref, kf_ref, vf_ref, gbf_ref, gtf_ref, of_ref),
             (qb_ref, kb_ref, vb_ref, gbb_ref, gtb_ref, ob_ref))):
        reverse = direction == 1
        tri_col = (upper_eq if reverse else lower_eq).astype(BF16)
        tri_row = (lower_eq if reverse else upper_eq).astype(BF16)
        for sub in range(n_sub):
            rows = slice(sub * c, (sub + 1) * c)
            gb = gb_ref[rows, :]
            lane = lax.broadcasted_iota(jnp.int32, gb.shape, 1)
            cum_col = _exact_prefix_sums(jnp.where(lane < LANES // 4, gb, 0.0), tri_col, rows=False)
            cum_row = _exact_prefix_sums(gt_ref[:, rows], tri_row, rows=True)
            for head in range(n_heads):
                g_idx = direction * n_heads + head
                b_idx = 2 * n_heads + g_idx
                cols = slice(head * d, (head + 1) * d)
                items.append(dict(
                    direction=direction, head=head, sub=sub, o_ref=o_ref, rows=rows, cols=cols, reverse=reverse,
                    earlier_eq=upper_eq if reverse else lower_eq,
                    earlier=upper if reverse else lower,
                    q=q_ref[rows, cols], k=k_ref[rows, cols], v=v_ref[rows, cols],
                    g_col=cum_col[:, g_idx:g_idx + 1], beta_col=gb[:, b_idx:b_idx + 1],
                    g_row=cum_row[g_idx:g_idx + 1, :],
                    g_last=cum_col[0:1, g_idx:g_idx + 1] if reverse else cum_col[c - 1:c, g_idx:g_idx + 1]))

    grams = [_mm_nt(jnp.concatenate([it["k"], it["q"]], axis=0), it["k"]) for it in items]
    neg_ls, rhss = [], []
    for it, gram in zip(items, grams):
        g_wide = jnp.broadcast_to(it["g_col"], (c, d))
        beta_wide = jnp.broadcast_to(it["beta_col"], (c, d))
        it["g_last_wide"] = jnp.broadcast_to(it["g_last"], (c, d))
        decay = jnp.exp(jnp.where(it["earlier_eq"], g_wide - it["g_row"], NEG_INF))
        neg_ls.append(jnp.where(it["earlier"], gram[:c] * decay, 0.0) * (-beta_wide))
        it["intra"] = jnp.where(it["earlier_eq"], gram[c:] * decay, 0.0).astype(BF16)
        kf = it["k"].astype(F32)
        e_g = jnp.exp(g_wide)
        rhss.append(jnp.concatenate([(it["v"].astype(F32) * beta_wide).astype(BF16),
                                     (kf * (beta_wide * e_g)).astype(BF16)], axis=-1))
        it["qg"] = (it["q"].astype(F32) * e_g).astype(BF16)
        it["k_dec"] = (kf * jnp.exp(it["g_last_wide"] - g_wide)).astype(BF16)
    invs = _unit_triangular_inverses(neg_ls, [it["reverse"] for it in items], same_block)
    for it, inv, rhs in zip(items, invs, rhss):
        it["sol"] = _mm(inv.astype(BF16), rhs)

    for step in range(n_sub):
        live = [it for it in items if it["sub"] == (n_sub - 1 - step if it["reverse"] else step)]
        states = [state_ref[it["direction"], it["head"]] for it in live]
        wss = [_mm(jnp.concatenate([it["sol"][:, d:].astype(BF16), it["qg"]], axis=0), state.astype(BF16))
               for it, state in zip(live, states)]
        v_news = [(it["sol"][:, :d] - ws[:c]).astype(BF16) for it, ws in zip(live, wss)]
        outs = [ws[c:] + _mm(it["intra"], v_new) for it, ws, v_new in zip(live, wss, v_news)]
        updates = [_mm_tn(it["k_dec"], v_new) for it, v_new in zip(live, v_news)]
        for it, out, state, update in zip(live, outs, states, updates):
            it["o_ref"][it["rows"], it["cols"]] = out
            state_ref[it["direction"], it["head"]] = state * jnp.exp(it["g_last_wide"]) + update


def _dn_core(q, k, v, gb, gbt, batch, seq, chunks_per_step=2):
    t, tok = q.shape
    n_heads = tok // DN_HEAD_DIM
    rows = chunks_per_step * DN_CHUNK
    steps = seq // rows
    n_rows = gbt.shape[0]

    fwd = lambda b, i: (b * steps + i, 0)
    bwd = lambda b, i: (b * steps + steps - 1 - i, 0)
    fwd_t = lambda b, i: (0, b * steps + i)
    bwd_t = lambda b, i: (0, b * steps + steps - 1 - i)

    def side(tok_map, t_map):
        return [pl.BlockSpec((rows, tok), tok_map)] * 3 + [
            pl.BlockSpec((rows, LANES), tok_map), pl.BlockSpec((n_rows, rows), t_map)]

    return pl.pallas_call(
        functools.partial(_dn_core_kernel, n_heads=n_heads),
        grid=(batch, steps),
        in_specs=side(fwd, fwd_t) + side(bwd, bwd_t),
        out_specs=[pl.BlockSpec((rows, tok), fwd), pl.BlockSpec((rows, tok), bwd)],
        out_shape=[jax.ShapeDtypeStruct((t, tok), F32)] * 2,
        scratch_shapes=[pltpu.VMEM((2, n_heads, DN_HEAD_DIM, DN_HEAD_DIM), F32)],
        compiler_params=_params("parallel", "arbitrary"),
        name="dn_core",
    )(q, k, v, gb, gbt, q, k, v, gb, gbt)


def _dn_out_kernel(of_ref, ob_ref, z_ref, mem_ref, x_ref, onorm_ref, w_ref, g_ref, out_ref):
    tok = of_ref.shape[1]
    parts = []
    for c0 in range(0, tok, DN_HEAD_DIM):
        cols = slice(c0, c0 + DN_HEAD_DIM)
        o = of_ref[:, cols] + ob_ref[:, cols]
        o = o * lax.rsqrt(jnp.mean(o * o, axis=-1, keepdims=True) + EPS) * onorm_ref[...]
        parts.append((o * _silu(z_ref[:, cols].astype(F32))).astype(BF16))
    mixed = jnp.concatenate(parts + [mem_ref[...]], axis=-1)
    y = jnp.dot(mixed, w_ref[...], preferred_element_type=F32)
    out_ref[...] = x_ref[...] + _rms(y, g_ref[...])


def _dn_out(o_f, o_b, z, mem_out, x2, out_norm, w_out, gain, tm=512):
    t, d = x2.shape
    tok = o_f.shape[1]
    row = lambda i: (i, 0)
    return pl.pallas_call(
        _dn_out_kernel,
        grid=(t // tm,),
        in_specs=[pl.BlockSpec((tm, tok), row)] * 3
        + [pl.BlockSpec((tm, mem_out.shape[1]), row), pl.BlockSpec((tm, d), row),
           _const_spec((1, DN_HEAD_DIM)), _const_spec(w_out.shape), _const_spec((1, d))],
        out_specs=pl.BlockSpec((tm, d), row),
        out_shape=jax.ShapeDtypeStruct((t, d), F32),
        compiler_params=_params("parallel"),
        name="dn_out",
    )(o_f, o_b, z, mem_out, x2, out_norm.reshape(1, DN_HEAD_DIM), w_out, gain.reshape(1, d))


def _memory_branch(q_mem, mem, mem_gain, w_kv, seq):
    k_mem, v_mem = _mem_kv(mem, mem_gain, w_kv.astype(BF16))
    return _mem_attn(q_mem, k_mem, v_mem, seq)


def _attention_layer(x2, mem, batch, seq, rel_bias, w_in, w_out, mem_gain, w_kv, g_pre, g_post):
    n_groups = len(DILATED_GROUPS)
    tok = n_groups * GROUP_WIDTH
    dils = tuple(dil for _, dil in DILATED_GROUPS)
    w_qkv = w_in[:, :3 * tok].reshape(-1, 3, n_groups, GROUP_WIDTH).transpose(0, 2, 1, 3).astype(BF16)
    group_weights = [w_qkv[:, g].reshape(-1, 3 * GROUP_WIDTH) for g in range(n_groups)]
    *qkvs, q_mem = _att_proj(x2, g_pre, group_weights, w_in[:, 3 * tok:].astype(BF16), dils)
    os_, lses = [], []
    for group, dil in enumerate(dils):
        n_res = min(dil, 2)
        o, lse = _band_attn(qkvs[group], _band_bias(rel_bias, group, dil), batch, seq, dil, n_res,
                            q_tile=512 // n_res)
        os_.append(o)
        lses.append(lse)
    mem_out = _memory_branch(q_mem, mem, mem_gain, w_kv, seq)
    return _att_out(os_, lses, dils, mem_out, x2, w_out.astype(BF16), g_post)


def _deltanet_layer(x2, mem, batch, seq, w_in, conv_w, a_log, dt_bias, out_norm, w_out, mem_gain, w_kv,
                    g_pre, g_post):
    d = x2.shape[1]
    n_heads = a_log.shape[1]
    tok = n_heads * DN_HEAD_DIM
    n_gate = 4 * n_heads
    w_gate = w_in[:, 4 * tok:4 * tok + n_gate].reshape(d, 2, 2, n_heads).transpose(0, 2, 1, 3).reshape(d, n_gate)
    w_gate = jnp.pad(w_gate, ((0, 0), (0, LANES - n_gate)))
    w_in_b = w_in.astype(BF16)
    w_mem = w_in_b[:, 4 * tok + n_gate:]
    weights = [(w_in_b, 3 * tok, 0), (w_in_b, tok, 3), (w_gate.astype(BF16), LANES, 0), (w_mem, w_mem.shape[1], 0)]
    qkv, z, gate, q_mem = _norm_proj(x2, g_pre, weights, [BF16, BF16, F32, BF16])
    pad_row = lambda p: jnp.pad(p.reshape(1, -1).astype(F32), ((0, 0), (0, LANES - 2 * n_heads)))
    q, k, v, gb, gbt = _dn_prep(qkv, gate, conv_w.astype(F32), pad_row(a_log), pad_row(dt_bias), batch, seq,
                                2 * n_heads)
    o_f, o_b = _dn_core(q, k, v, gb, gbt, batch, seq)
    mem_out = _memory_branch(q_mem, mem, mem_gain, w_kv, seq)
    return _dn_out(o_f, o_b, z, mem_out, x2, out_norm, w_out.astype(BF16), g_post)


def kernel(x, mem, rel_bias, att_w_in, att_w_out, dn_w_in, dn_conv, dn_a_log, dn_dt_bias, dn_out_norm,
           dn_w_out, mem_norm, mem_w_kv, norm_mix_pre, norm_mix_post, norm_ffn_pre, norm_ffn_post,
           ffn_w_gate_up, ffn_w_down):
    batch, seq, d = x.shape
    depth = norm_mix_pre.shape[0]
    n_mixers = 2
    x2 = x.reshape(batch * seq, d)
    for i in range(depth):
        j = i // n_mixers
        if i % n_mixers == 0:
            x2 = _attention_layer(x2, mem, batch, seq, rel_bias, att_w_in[j], att_w_out[j], mem_norm[i],
                                  mem_w_kv[i], norm_mix_pre[i], norm_mix_post[i])
        else:
            x2 = _deltanet_layer(x2, mem, batch, seq, dn_w_in[j], dn_conv[j], dn_a_log[j], dn_dt_bias[j],
                                 dn_out_norm[j], dn_w_out[j], mem_norm[i], mem_w_kv[i], norm_mix_pre[i],
                                 norm_mix_post[i])
        x2 = _ffn(x2, norm_ffn_pre[i], ffn_w_gate_up[i].astype(BF16), ffn_w_down[i].astype(BF16),
                  norm_ffn_post[i])
    return x2.reshape(batch, seq, d)
```

```python
import functools
import math

import jax
import jax.numpy as jnp
import numpy as np
from jax import lax
from jax.experimental import pallas as pl
from jax.experimental.pallas import tpu as pltpu

F32 = jnp.float32
BF16 = jnp.bfloat16

EPS = 1e-6
NEG_INF = -1e30

LANES = 128
VMEM_LIMIT_BYTES = 56 * 1024 * 1024

DILATED_GROUPS = ((128, 1), (512, 4), (2048, 16))
ATT_HEAD_DIM = 64
HEADS_PER_GROUP = 4
GROUP_WIDTH = HEADS_PER_GROUP * ATT_HEAD_DIM
BAND_HALF = 64
REL_BUCKETS = 32
REL_MAX_DIST = 1024
MEM_HEADS = 4
MEM_HEAD_DIM = 64
DN_HEAD_DIM = 128
DN_CONV = 5
DN_CHUNK = 128
SOLVE_LEAF = 16

ATT_Q_BLOCK = 128
ATT_K_WINDOW = ATT_Q_BLOCK + 2 * BAND_HALF


def _params(*semantics):
    return pltpu.CompilerParams(dimension_semantics=semantics, vmem_limit_bytes=VMEM_LIMIT_BYTES)


def _rms(x, gain):
    return x * lax.rsqrt(jnp.mean(x * x, axis=-1, keepdims=True) + EPS) * gain


def _silu(x):
    return x / (1.0 + jnp.exp(-x))


def _const_spec(shape):
    return pl.BlockSpec(shape, lambda *_: (0,) * len(shape))


def _norm_proj_kernel(x_ref, g_ref, *refs, n_out):
    h = _rms(x_ref[...], g_ref[...]).astype(BF16)
    for w_ref, o_ref in zip(refs[:n_out], refs[n_out:]):
        o_ref[...] = jnp.dot(h, w_ref[...], preferred_element_type=F32).astype(o_ref.dtype)


def _norm_proj(x2, gain, weights, out_dtypes, tm=512):
    t, d = x2.shape
    n_out = len(weights)
    return pl.pallas_call(
        functools.partial(_norm_proj_kernel, n_out=n_out),
        grid=(t // tm,),
        in_specs=[pl.BlockSpec((tm, d), lambda i: (i, 0)), _const_spec((1, d))]
        + [pl.BlockSpec((d, width), functools.partial(lambda i, blk: (0, blk), blk=blk)) for _, width, blk in weights],
        out_specs=[pl.BlockSpec((tm, width), lambda i: (i, 0)) for _, width, _ in weights],
        out_shape=[jax.ShapeDtypeStruct((t, width), dt) for (_, width, _), dt in zip(weights, out_dtypes)],
        compiler_params=_params("parallel"),
        name="norm_proj",
    )(x2, gain.reshape(1, d), *[w for w, _, _ in weights])


def _mem_kv_kernel(mem_ref, g_ref, w_ref, k_ref, v_ref):
    h = _rms(mem_ref[0], g_ref[...]).astype(BF16)
    kv = jnp.dot(h, w_ref[...], preferred_element_type=F32)
    width = k_ref.shape[-1]
    k_ref[0] = kv[:, :width].astype(BF16)
    v_ref[0] = kv[:, width:].astype(BF16)


def _mem_kv(mem, gain, w_kv):
    b, m, d = mem.shape
    width = w_kv.shape[1] // 2
    return pl.pallas_call(
        _mem_kv_kernel,
        grid=(b,),
        in_specs=[pl.BlockSpec((1, m, d), lambda i: (i, 0, 0)), _const_spec((1, d)), _const_spec(w_kv.shape)],
        out_specs=[pl.BlockSpec((1, m, width), lambda i: (i, 0, 0))] * 2,
        out_shape=[jax.ShapeDtypeStruct((b, m, width), BF16)] * 2,
        compiler_params=_params("parallel"),
        name="mem_kv",
    )(mem, gain.reshape(1, d), w_kv)


def _mem_attn_kernel(q_ref, k_ref, v_ref, o_ref):
    q = q_ref[...]
    k = k_ref[0]
    v = v_ref[0]
    lane = lax.broadcasted_iota(jnp.int32, q.shape, 1)
    in_head = [(lane >= h * MEM_HEAD_DIM) & (lane < (h + 1) * MEM_HEAD_DIM) for h in range(MEM_HEADS)]
    logits = [lax.dot_general(jnp.where(mask, q, jnp.zeros_like(q)), k, (((1,), (1,)), ((), ())),
                              preferred_element_type=F32) for mask in in_head]
    probs = []
    for s in logits:
        s = s * (MEM_HEAD_DIM ** -0.5)
        p = jnp.exp(s - jnp.max(s, axis=-1, keepdims=True))
        probs.append((p / jnp.sum(p, axis=-1, keepdims=True)).astype(BF16))
    pvs = [jnp.dot(p, v, preferred_element_type=F32) for p in probs]
    out = pvs[-1]
    for mask, pv in zip(in_head[:-1], pvs[:-1]):
        out = jnp.where(mask, pv, out)
    o_ref[...] = out.astype(o_ref.dtype)


def _mem_attn(q_mem, k, v, seq, tq=512):
    t, width = q_mem.shape
    m = k.shape[1]
    per_seq = seq // tq
    return pl.pallas_call(
        _mem_attn_kernel,
        grid=(t // tq,),
        in_specs=[pl.BlockSpec((tq, width), lambda i: (i, 0)),
                  pl.BlockSpec((1, m, width), lambda i: (i // per_seq, 0, 0)),
                  pl.BlockSpec((1, m, width), lambda i: (i // per_seq, 0, 0))],
        out_specs=pl.BlockSpec((tq, width), lambda i: (i, 0)),
        out_shape=jax.ShapeDtypeStruct((t, width), BF16),
        compiler_params=_params("parallel"),
        name="mem_attn",
    )(q_mem, k, v)


def _t5_bucket(rel):
    half = REL_BUCKETS // 2
    max_exact = half // 2
    n = np.abs(rel)
    large = max_exact + (np.log(np.maximum(n, 1) / max_exact) / math.log(REL_MAX_DIST / max_exact)
                         * (half - max_exact)).astype(np.int64)
    large = np.minimum(large, half - 1)
    return ((rel > 0) * half + np.where(n < max_exact, n, large)).astype(np.int32)


def _band_bias(rel_bias, group, dil):
    period = ATT_Q_BLOCK + ATT_K_WINDOW
    heads = slice(group * HEADS_PER_GROUP, (group + 1) * HEADS_PER_GROUP)
    tables = []
    for variant in range(3):
        rel = np.arange(period) - (ATT_Q_BLOCK - 1) - variant * BAND_HALF
        in_band = np.abs(rel) <= BAND_HALF
        bucket = _t5_bucket(np.clip(rel, -BAND_HALF, BAND_HALF) * dil)
        f = jnp.where(in_band[None, :], jnp.transpose(rel_bias[bucket][:, heads]).astype(F32), NEG_INF)
        flat = jnp.tile(f, (1, ATT_Q_BLOCK))[:, :ATT_Q_BLOCK * (period - 1)]
        skew = flat.reshape(HEADS_PER_GROUP, ATT_Q_BLOCK, period - 1)
        tables.append(skew[:, :, ATT_Q_BLOCK - 1:ATT_Q_BLOCK - 1 + ATT_K_WINDOW])
    return jnp.stack(tables)


def _band_attn_kernel(x_ref, bias_ref, o_ref, lse_ref, *, sub_len, n_res, q_tile):
    lane = lax.broadcasted_iota(jnp.int32, (ATT_Q_BLOCK, LANES), 1)
    low_half = lane < ATT_HEAD_DIM
    n_pairs = GROUP_WIDTH // LANES

    def tile(t, carry):
        blocks = []
        for res in range(n_res):
            for blk in range(q_tile // ATT_Q_BLOCK):
                i0 = pl.multiple_of(t * q_tile + blk * ATT_Q_BLOCK, ATT_Q_BLOCK)
                ws = jnp.clip(i0 - BAND_HALF, 0, sub_len - ATT_K_WINDOW)
                variant = (i0 - ws) // BAND_HALF
                ws = pl.multiple_of(ws, BAND_HALF)
                base = res * 3 * GROUP_WIDTH
                blocks.append(dict(
                    res=res, rows=pl.ds(i0, ATT_Q_BLOCK), variant=variant,
                    q=x_ref[0, pl.ds(i0, ATT_Q_BLOCK), base:base + GROUP_WIDTH],
                    k=x_ref[0, pl.ds(ws, ATT_K_WINDOW), base + GROUP_WIDTH:base + 2 * GROUP_WIDTH],
                    v=x_ref[0, pl.ds(ws, ATT_K_WINDOW), base + 2 * GROUP_WIDTH:base + 3 * GROUP_WIDTH]))
        heads = [(b, pair, a) for b in blocks for pair in range(n_pairs) for a in range(2)]
        logits = []
        for b, pair, a in heads:
            cols = slice(pair * LANES, (pair + 1) * LANES)
            q_pair = b["q"][:, cols]
            keep = low_half if a == 0 else jnp.logical_not(low_half)
            qh = jnp.where(keep, q_pair, jnp.zeros_like(q_pair))
            logits.append(_mm_nt(qh, b["k"][:, cols]))
        probs, dens, lses = [], [], []
        for (b, pair, a), s in zip(heads, logits):
            s = s * (ATT_HEAD_DIM ** -0.5) + bias_ref[b["variant"], 2 * pair + a]
            m = jnp.max(s, axis=-1, keepdims=True)
            p = jnp.exp(s - m)
            den = jnp.sum(p, axis=-1, keepdims=True)
            probs.append(p.astype(BF16))
            dens.append(den)
            lses.append(m + jnp.log(den))
        pvs = [_mm(p, b["v"][:, pair * LANES:(pair + 1) * LANES]) for (b, pair, a), p in zip(heads, probs)]
        for idx in range(0, len(heads), 2):
            b, pair, _ = heads[idx]
            cols = pl.ds(b["res"] * GROUP_WIDTH + pair * LANES, LANES)
            out = jnp.where(low_half, pvs[idx] / dens[idx], pvs[idx + 1] / dens[idx + 1])
            lse = jnp.where(low_half, jnp.broadcast_to(lses[idx], (ATT_Q_BLOCK, LANES)),
                            jnp.broadcast_to(lses[idx + 1], (ATT_Q_BLOCK, LANES)))
            o_ref[0, b["rows"], cols] = out
            lse_ref[0, b["rows"], cols] = lse
        return carry

    lax.fori_loop(0, sub_len // q_tile, tile, 0)


def _band_attn(qkv, bias, batch, seq, dil, n_res, q_tile=256):
    rows, width = qkv.shape
    sub_len = seq // dil
    qkv3 = qkv.reshape(batch, sub_len, width)
    x_spec = pl.BlockSpec((1, sub_len, n_res * 3 * GROUP_WIDTH), lambda b, r: (b, 0, r))
    o_spec = pl.BlockSpec((1, sub_len, n_res * GROUP_WIDTH), lambda b, r: (b, 0, r))
    o, lse = pl.pallas_call(
        functools.partial(_band_attn_kernel, sub_len=sub_len, n_res=n_res, q_tile=q_tile),
        grid=(batch, dil // n_res),
        in_specs=[x_spec, _const_spec(bias.shape)],
        out_specs=[o_spec, o_spec],
        out_shape=[jax.ShapeDtypeStruct((batch, sub_len, dil * GROUP_WIDTH), F32)] * 2,
        compiler_params=_params("parallel", "parallel"),
        name=f"band_attn_d{dil}",
    )(qkv3, bias)
    return o.reshape(rows, dil * GROUP_WIDTH), lse.reshape(rows, dil * GROUP_WIDTH)


def _att_proj_kernel(x_ref, g_ref, *refs, dils):
    n = len(dils)
    w_refs, wmem_ref = refs[:n], refs[n]
    out_refs, qmem_ref, stage_ref = refs[n + 1:2 * n + 1], refs[2 * n + 1], refs[2 * n + 2]
    h = _rms(x_ref[...], g_ref[...]).astype(BF16)
    tm = h.shape[0]
    for w_ref, o_ref, dil in zip(w_refs, out_refs, dils):
        y = jnp.dot(h, w_ref[...], preferred_element_type=F32)
        width = y.shape[1]
        if dil == 1:
            o_ref[...] = y.astype(BF16)
        else:
            for c in range(width // LANES):
                stage_ref[c] = y[:, c * LANES:(c + 1) * LANES]
            for r in range(dil):
                for c in range(width // LANES):
                    lanes = pl.ds(r * width + c * LANES, LANES)
                    o_ref[:, lanes] = stage_ref[c, pl.ds(r, tm // dil, stride=dil), :].astype(BF16)
    qmem_ref[...] = jnp.dot(h, wmem_ref[...], preferred_element_type=F32).astype(BF16)


def _att_proj(x2, gain, group_weights, w_mem, dils, tm=512):
    t, d = x2.shape
    width = group_weights[0].shape[1]
    row = lambda i: (i, 0)
    return pl.pallas_call(
        functools.partial(_att_proj_kernel, dils=dils),
        grid=(t // tm,),
        in_specs=[pl.BlockSpec((tm, d), row), _const_spec((1, d))]
        + [_const_spec(w.shape) for w in group_weights] + [_const_spec(w_mem.shape)],
        out_specs=[pl.BlockSpec((tm // dil, dil * width), row) for dil in dils]
        + [pl.BlockSpec((tm, w_mem.shape[1]), row)],
        out_shape=[jax.ShapeDtypeStruct((t // dil, dil * width), BF16) for dil in dils]
        + [jax.ShapeDtypeStruct((t, w_mem.shape[1]), BF16)],
        scratch_shapes=[pltpu.VMEM((width // LANES, tm, LANES), F32)],
        compiler_params=_params("parallel"),
        name="att_proj",
    )(x2, gain.reshape(1, d), *group_weights, w_mem)


def _att_out_kernel(*refs, dils):
    n = len(dils)
    o_refs, l_refs = refs[:n], refs[n:2 * n]
    mem_ref, x_ref, w_ref, g_ref, out_ref, stage_ref = refs[2 * n:]
    tm = x_ref.shape[0]

    def token_major(ref, dil, slot):
        if dil == 1:
            return ref[...]
        tiles = GROUP_WIDTH // LANES
        for r in range(dil):
            for c in range(tiles):
                lanes = pl.ds(r * GROUP_WIDTH + c * LANES, LANES)
                stage_ref[slot * tiles + c, pl.ds(r, tm // dil, stride=dil), :] = ref[:, lanes]
        return jnp.concatenate([stage_ref[slot * tiles + c] for c in range(tiles)], axis=-1)

    lses = [token_major(ref, dil, 2 * i) for i, (ref, dil) in enumerate(zip(l_refs, dils))]
    outs = [token_major(ref, dil, 2 * i + 1) for i, (ref, dil) in enumerate(zip(o_refs, dils))]
    mx = functools.reduce(jnp.maximum, lses)
    es = [jnp.exp(l - mx) for l in lses]
    tot = functools.reduce(lambda a, b: a + b, es)
    parts = [(o * (e / tot)).astype(BF16) for o, e in zip(outs, es)]
    mixed = jnp.concatenate(parts + [mem_ref[...]], axis=-1)
    y = jnp.dot(mixed, w_ref[...], preferred_element_type=F32)
    out_ref[...] = x_ref[...] + _rms(y, g_ref[...])


def _att_out(os_, lses, dils, mem_out, x2, w_out, gain, tm=512):
    t, d = x2.shape
    row = lambda i: (i, 0)
    group_specs = [pl.BlockSpec((tm // dil, dil * GROUP_WIDTH), row) for dil in dils]
    return pl.pallas_call(
        functools.partial(_att_out_kernel, dils=dils),
        grid=(t // tm,),
        in_specs=group_specs + group_specs
        + [pl.BlockSpec((tm, mem_out.shape[1]), row), pl.BlockSpec((tm, d), row),
           _const_spec(w_out.shape), _const_spec((1, d))],
        out_specs=pl.BlockSpec((tm, d), row),
        out_shape=jax.ShapeDtypeStruct((t, d), F32),
        scratch_shapes=[pltpu.VMEM((2 * len(dils) * (GROUP_WIDTH // LANES), tm, LANES), F32)],
        compiler_params=_params("parallel"),
        name="att_out",
    )(*os_, *lses, mem_out, x2, w_out, gain.reshape(1, d))


def _ffn_kernel(x_ref, gpre_ref, wg_ref, wu_ref, wd_ref, gpost_ref, out_ref, *, n_chunks):
    x = x_ref[...]
    h = _rms(x, gpre_ref[...]).astype(BF16)
    d_ff = wg_ref.shape[1]
    step = d_ff // n_chunks
    y = jnp.zeros(x.shape, F32)
    for c in range(n_chunks):
        cols = slice(c * step, (c + 1) * step)
        gate = jnp.dot(h, wg_ref[:, cols], preferred_element_type=F32)
        up = jnp.dot(h, wu_ref[:, cols], preferred_element_type=F32)
        act = (_silu(gate) * up).astype(BF16)
        y = y + jnp.dot(act, wd_ref[cols, :], preferred_element_type=F32)
    out_ref[...] = x + _rms(y, gpost_ref[...])


def _ffn(x2, g_pre, w_gate_up, w_down, g_post, tm=512, n_chunks=2):
    t, d = x2.shape
    d_ff = w_down.shape[0]
    row = lambda i: (i, 0)
    return pl.pallas_call(
        functools.partial(_ffn_kernel, n_chunks=n_chunks),
        grid=(t // tm,),
        in_specs=[pl.BlockSpec((tm, d), row), _const_spec((1, d)),
                  pl.BlockSpec((d, d_ff), lambda i: (0, 0)), pl.BlockSpec((d, d_ff), lambda i: (0, 1)),
                  _const_spec(w_down.shape), _const_spec((1, d))],
        out_specs=pl.BlockSpec((tm, d), row),
        out_shape=jax.ShapeDtypeStruct((t, d), F32),
        compiler_params=_params("parallel"),
        name="ffn",
    )(x2, g_pre.reshape(1, d), w_gate_up, w_gate_up, w_down, g_post.reshape(1, d))


def _dn_prep_kernel(cur_ref, prev_ref, next_ref, conv_ref, gate_ref, alog_ref, dtb_ref,
                    q_ref, k_ref, v_ref, gb_ref, gbt_ref, ext_ref, *, halo, n_dir_heads):
    i = pl.program_id(1)
    last = pl.num_programs(1) - 1
    ts = cur_ref.shape[0]
    width = cur_ref.shape[1]
    tok = width // 3
    n_ext = ts + 2 * halo
    for c0 in range(0, width, LANES):
        cols = slice(c0, c0 + LANES)
        ext = ext_ref.at[(c0 // LANES) % ext_ref.shape[0]]
        ext[0:halo, :] = jnp.where(i > 0, prev_ref[:, cols].astype(F32), 0.0)
        ext[halo:halo + ts, :] = cur_ref[:, cols].astype(F32)
        ext[halo + ts:n_ext, :] = jnp.where(i < last, next_ref[:, cols].astype(F32), 0.0)
        acc = jnp.zeros((ts, LANES), F32)
        for j in range(DN_CONV):
            first = halo + j - DN_CONV // 2
            acc = acc + ext[first:first + ts, :] * conv_ref[j:j + 1, cols]
        y = _silu(acc)
        which, col = divmod(c0, tok)
        if which == 2:
            v_ref[:, col:col + LANES] = y.astype(BF16)
        else:
            yn = y * lax.rsqrt(jnp.sum(y * y, axis=-1, keepdims=True) + EPS)
            if which == 0:
                q_ref[:, col:col + LANES] = (yn * (DN_HEAD_DIM ** -0.5)).astype(BF16)
            else:
                k_ref[:, col:col + LANES] = yn.astype(BF16)
    gate = gate_ref[...]
    z = gate + dtb_ref[...]
    softplus = jnp.maximum(z, 0.0) + jnp.log(1.0 + jnp.exp(-jnp.abs(z)))
    decay = -jnp.exp(alog_ref[...]) * softplus
    beta = 1.0 / (1.0 + jnp.exp(-gate))
    lane = lax.broadcasted_iota(jnp.int32, gate.shape, 1)
    gb = jnp.where(lane < n_dir_heads, decay, beta)
    gb_ref[...] = gb
    gbt_ref[...] = jnp.transpose(gb)[:gbt_ref.shape[0], :]


def _dn_prep(qkv, gate, conv_w, a_log_row, dt_bias_row, batch, seq, n_dir_heads, ts=512, halo=16):
    t, width = qkv.shape
    n_gate_rows = 8 * ((2 * n_dir_heads + 7) // 8)
    tok = width // 3
    per_seq = seq // ts
    blocks_per_tile = ts // halo
    n_halo_blocks = t // halo

    def cur_map(b, i):
        return (b * per_seq + i, 0)

    def prev_map(b, i):
        return (jnp.maximum((b * per_seq + i) * blocks_per_tile - 1, 0), 0)

    def next_map(b, i):
        return (jnp.minimum((b * per_seq + i + 1) * blocks_per_tile, n_halo_blocks - 1), 0)

    return pl.pallas_call(
        functools.partial(_dn_prep_kernel, halo=halo, n_dir_heads=n_dir_heads),
        grid=(batch, per_seq),
        in_specs=[pl.BlockSpec((ts, width), cur_map), pl.BlockSpec((halo, width), prev_map),
                  pl.BlockSpec((halo, width), next_map), _const_spec(conv_w.shape),
                  pl.BlockSpec((ts, LANES), cur_map), _const_spec((1, LANES)), _const_spec((1, LANES))],
        out_specs=[pl.BlockSpec((ts, tok), cur_map)] * 3
        + [pl.BlockSpec((ts, LANES), cur_map), pl.BlockSpec((n_gate_rows, ts), lambda b, i: (0, b * per_seq + i))],
        out_shape=[jax.ShapeDtypeStruct((t, tok), BF16)] * 3
        + [jax.ShapeDtypeStruct((t, LANES), F32), jax.ShapeDtypeStruct((n_gate_rows, t), F32)],
        scratch_shapes=[pltpu.VMEM((2, ts + 2 * halo, LANES), F32)],
        compiler_params=_params("parallel", "arbitrary"),
        name="dn_prep",
    )(qkv, qkv, qkv, conv_w, gate, a_log_row, dt_bias_row)


def _mm(a, b):
    return jnp.dot(a, b, preferred_element_type=F32)


def _mm_nt(a, b):
    return lax.dot_general(a, b, (((1,), (1,)), ((), ())), preferred_element_type=F32)


def _mm_tn(a, b):
    return lax.dot_general(a, b, (((0,), (0,)), ((), ())), preferred_element_type=F32)


def _row_blocks(x, size, parity):
    return jnp.concatenate([x[b * size:(b + 1) * size] for b in range(parity, x.shape[0] // size, 2)], axis=0)


def _interleave_row_blocks(even, odd, size):
    pieces = []
    for b in range(even.shape[0] // size):
        pieces += [even[b * size:(b + 1) * size], odd[b * size:(b + 1) * size]]
    return jnp.concatenate(pieces, axis=0)


def _unit_triangular_inverses(neg_ls, uppers, same_block):
    n = neg_ls[0].shape[0]
    ii = lax.broadcasted_iota(jnp.int32, (n, n), 0)
    jj = lax.broadcasted_iota(jnp.int32, (n, n), 1)
    identity = (ii == jj).astype(F32).astype(BF16)
    zero = jnp.zeros((n, n), BF16)
    plus_identity = lambda m: m + identity
    leaf = same_block[SOLVE_LEAF].astype(F32).astype(BF16)
    xbs = [nl * leaf for nl in neg_ls]
    invs = [plus_identity(xb) for xb in xbs]
    power = 1
    while 2 * power < SOLVE_LEAF:
        xbs = [_mm(xb, xb).astype(BF16) for xb in xbs]
        invs = [_mm(plus_identity(xb), inv).astype(BF16) for xb, inv in zip(xbs, invs)]
        power *= 2
    size = SOLVE_LEAF
    while size < n:
        merge = (same_block[2 * size] & jnp.logical_not(same_block[size])).astype(F32).astype(BF16)
        hot = [0 if upper else 1 for upper in uppers]
        offs = [_row_blocks(nl, size, p) * _row_blocks(merge, size, p) for nl, p in zip(neg_ls, hot)]
        rights = [_mm(off, inv).astype(BF16) for off, inv in zip(offs, invs)]
        half_zero = zero[:n // 2]
        rights = [_interleave_row_blocks(*((r, half_zero) if p == 0 else (half_zero, r)), size)
                  for r, p in zip(rights, hot)]
        moved = [_mm(_row_blocks(inv, size, p), plus_identity(right)).astype(BF16)
                 for inv, right, p in zip(invs, rights, hot)]
        invs = [_interleave_row_blocks(*((m, _row_blocks(inv, size, 1)) if p == 0 else
                                         (_row_blocks(inv, size, 0), m)), size)
                for inv, m, p in zip(invs, moved, hot)]
        size *= 2
    return invs


def _exact_prefix_sums(x, tri, rows):
    hi = x.astype(BF16).astype(F32)
    mid = (x - hi).astype(BF16).astype(F32)
    lo = (x - hi - mid).astype(BF16).astype(F32)
    if rows:
        n = x.shape[0]
        res = _mm(jnp.concatenate([hi, mid, lo], axis=0).astype(BF16), tri)
        return res[:n] + res[n:2 * n] + res[2 * n:]
    quarter = LANES // 4
    packed = hi + pltpu.roll(mid, quarter, 1) + pltpu.roll(lo, 2 * quarter, 1)
    res = _mm(tri, packed.astype(BF16))
    return res + pltpu.roll(res, LANES - quarter, 1) + pltpu.roll(res, LANES - 2 * quarter, 1)


def _dn_core_kernel(qf_ref, kf_ref, vf_ref, gbf_ref, gtf_ref, qb_ref, kb_ref, vb_ref, gbb_ref, gtb_ref,
                    of_ref, ob_ref, state_ref, negl_ref, rhs_ref, intra_ref, qg_ref, kdec_ref, glast_ref,
                    *, n_heads):
    staged = (negl_ref, rhs_ref, intra_ref, qg_ref, kdec_ref, glast_ref)

    @pl.when(pl.program_id(1) == 0)
    def _():
        state_ref[...] = jnp.zeros_like(state_ref)
        for ref in staged:
            ref[...] = jnp.zeros_like(ref)

    c = DN_CHUNK
    d = DN_HEAD_DIM
    n_sub = qf_ref.shape[0] // c
    ii = lax.broadcasted_iota(jnp.int32, (c, c), 0)
    jj = lax.broadcasted_iota(jnp.int32, (c, c), 1)
    lower_eq, upper_eq = jj <= ii, jj >= ii
    lower, upper = jj < ii, jj > ii
    same_block = {}
    size = SOLVE_LEAF
    while size <= c:
        shift = int(math.log2(size))
        same_block[size] = (ii >> shift) == (jj >> shift)
        size *= 2

    ready = []
    for direction, o_ref in enumerate((of_ref, ob_ref)):
        for sub in range(n_sub):
            for head in range(n_heads):
                slot = len(ready)
                ready.append(dict(
                    direction=direction, head=head, sub=sub, reverse=direction == 1, o_ref=o_ref,
                    rows=slice(sub * c, (sub + 1) * c), cols=slice(head * d, (head + 1) * d),
                    neg_l=negl_ref[slot], rhs=rhs_ref[slot], intra=intra_ref[slot], qg=qg_ref[slot],
                    k_dec=kdec_ref[slot], g_last_row=glast_ref[slot, 0:1, :]))

    items = []
    for direction, (q_ref, k_ref, v_ref, gb_ref, gt_ref) in enumerate(
            ((qf_ref, kf_ref, vf_ref, gbf_ref, gtf_ref), (qb_ref, kb_ref, vb_ref, gbb_ref, gtb_ref))):
        reverse = direction == 1
        tri_col = (upper_eq if reverse else lower_eq).astype(BF16)
        tri_row = (lower_eq if reverse else upper_eq).astype(BF16)
        for sub in range(n_sub):
            rows = slice(sub * c, (sub + 1) * c)
            gb = gb_ref[rows, :]
            lane = lax.broadcasted_iota(jnp.int32, gb.shape, 1)
            cum_col = _exact_prefix_sums(jnp.where(lane < LANES // 4, gb, 0.0), tri_col, rows=False)
            cum_row = _exact_prefix_sums(gt_ref[:, rows], tri_row, rows=True)
            for head in range(n_heads):
                g_idx = direction * n_heads + head
                b_idx = 2 * n_heads + g_idx
                cols = slice(head * d, (head + 1) * d)
                items.append(dict(
                    earlier_eq=upper_eq if reverse else lower_eq,
                    earlier=upper if reverse else lower,
                    q=q_ref[rows, cols], k=k_ref[rows, cols], v=v_ref[rows, cols],
                    g_col=cum_col[:, g_idx:g_idx + 1], beta_col=gb[:, b_idx:b_idx + 1],
                    g_row=cum_row[g_idx:g_idx + 1, :],
                    g_last=cum_col[0:1, g_idx:g_idx + 1] if reverse else cum_col[c - 1:c, g_idx:g_idx + 1]))

    grams = [_mm_nt(jnp.concatenate([it["k"], it["q"]], axis=0), it["k"]) for it in items]
    for slot, (it, gram) in enumerate(zip(items, grams)):
        g_wide = jnp.broadcast_to(it["g_col"], (c, d))
        beta_wide = jnp.broadcast_to(it["beta_col"], (c, d))
        g_last_wide = jnp.broadcast_to(it["g_last"], (c, d))
        decay = jnp.exp(jnp.where(it["earlier_eq"], g_wide - it["g_row"], NEG_INF))
        kf = it["k"].astype(F32)
        e_g = jnp.exp(g_wide)
        negl_ref[slot] = (jnp.where(it["earlier"], gram[:c] * decay, 0.0) * (-beta_wide)).astype(BF16)
        intra_ref[slot] = jnp.where(it["earlier_eq"], gram[c:] * decay, 0.0).astype(BF16)
        rhs_ref[slot] = jnp.concatenate([(it["v"].astype(F32) * beta_wide).astype(BF16),
                                         (kf * (beta_wide * e_g)).astype(BF16)], axis=-1)
        qg_ref[slot] = (it["q"].astype(F32) * e_g).astype(BF16)
        kdec_ref[slot] = (kf * jnp.exp(g_last_wide - g_wide)).astype(BF16)
        glast_ref[slot] = g_last_wide[:glast_ref.shape[1]]

    invs = _unit_triangular_inverses([it["neg_l"] for it in ready], [it["reverse"] for it in ready], same_block)
    for it, inv in zip(ready, invs):
        sol = _mm(inv, it["rhs"])
        it["u"], it["w"] = sol[:, :d], sol[:, d:].astype(BF16)

    for step in range(n_sub):
        live = [it for it in ready if it["sub"] == (n_sub - 1 - step if it["reverse"] else step)]
        states = [state_ref[it["direction"], it["head"]] for it in live]
        wss = [_mm(jnp.concatenate([it["w"], it["qg"]], axis=0), state.astype(BF16))
               for it, state in zip(live, states)]
        v_news = [(it["u"] - ws[:c]).astype(BF16) for it, ws in zip(live, wss)]
        outs = [ws[c:] + _mm(it["intra"], v_new) for it, ws, v_new in zip(live, wss, v_news)]
        updates = [_mm_tn(it["k_dec"], v_new) for it, v_new in zip(live, v_news)]
        for it, out, state, update in zip(live, outs, states, updates):
            it["o_ref"][it["rows"], it["cols"]] = out
            state_ref[it["direction"], it["head"]] = state * jnp.exp(it["g_last_row"]) + update


def _dn_core(q, k, v, gb, gbt, batch, seq, chunks_per_step=2):
    t, tok = q.shape
    n_heads = tok // DN_HEAD_DIM
    rows = chunks_per_step * DN_CHUNK
    steps = seq // rows
    n_rows = gbt.shape[0]

    prepared = lambda j: jnp.minimum(j, steps - 1)
    finished = lambda j: jnp.maximum(j - 1, 0)
    fwd = lambda b, j: (b * steps + prepared(j), 0)
    bwd = lambda b, j: (b * steps + steps - 1 - prepared(j), 0)
    fwd_t = lambda b, j: (0, b * steps + prepared(j))
    bwd_t = lambda b, j: (0, b * steps + steps - 1 - prepared(j))
    fwd_out = lambda b, j: (b * steps + finished(j), 0)
    bwd_out = lambda b, j: (b * steps + steps - 1 - finished(j), 0)

    def side(tok_map, t_map):
        return [pl.BlockSpec((rows, tok), tok_map)] * 3 + [
            pl.BlockSpec((rows, LANES), tok_map), pl.BlockSpec((n_rows, rows), t_map)]

    c, d = DN_CHUNK, DN_HEAD_DIM
    n_chains = 2 * chunks_per_step * n_heads
    return pl.pallas_call(
        functools.partial(_dn_core_kernel, n_heads=n_heads),
        grid=(batch, steps + 1),
        in_specs=side(fwd, fwd_t) + side(bwd, bwd_t),
        out_specs=[pl.BlockSpec((rows, tok), fwd_out), pl.BlockSpec((rows, tok), bwd_out)],
        out_shape=[jax.ShapeDtypeStruct((t, tok), F32)] * 2,
        scratch_shapes=[pltpu.VMEM((2, n_heads, d, d), F32),
                        pltpu.VMEM((n_chains, c, c), BF16),
                        pltpu.VMEM((n_chains, c, 2 * d), BF16),
                        pltpu.VMEM((n_chains, c, c), BF16),
                        pltpu.VMEM((n_chains, c, d), BF16),
                        pltpu.VMEM((n_chains, c, d), BF16),
                        pltpu.VMEM((n_chains, 8, d), F32)],
        compiler_params=_params("parallel", "arbitrary"),
        name="dn_core",
    )(q, k, v, gb, gbt, q, k, v, gb, gbt)


def _dn_out_kernel(of_ref, ob_ref, z_ref, mem_ref, x_ref, onorm_ref, w_ref, g_ref, out_ref):
    tok = of_ref.shape[1]
    parts = []
    for c0 in range(0, tok, DN_HEAD_DIM):
        cols = slice(c0, c0 + DN_HEAD_DIM)
        o = of_ref[:, cols] + ob_ref[:, cols]
        o = o * lax.rsqrt(jnp.mean(o * o, axis=-1, keepdims=True) + EPS) * onorm_ref[...]
        parts.append((o * _silu(z_ref[:, cols].astype(F32))).astype(BF16))
    mixed = jnp.concatenate(parts + [mem_ref[...]], axis=-1)
    y = jnp.dot(mixed, w_ref[...], preferred_element_type=F32)
    out_ref[...] = x_ref[...] + _rms(y, g_ref[...])


def _dn_out(o_f, o_b, z, mem_out, x2, out_norm, w_out, gain, tm=512):
    t, d = x2.shape
    tok = o_f.shape[1]
    row = lambda i: (i, 0)
    return pl.pallas_call(
        _dn_out_kernel,
        grid=(t // tm,),
        in_specs=[pl.BlockSpec((tm, tok), row)] * 3
        + [pl.BlockSpec((tm, mem_out.shape[1]), row), pl.BlockSpec((tm, d), row),
           _const_spec((1, DN_HEAD_DIM)), _const_spec(w_out.shape), _const_spec((1, d))],
        out_specs=pl.BlockSpec((tm, d), row),
        out_shape=jax.ShapeDtypeStruct((t, d), F32),
        compiler_params=_params("parallel"),
        name="dn_out",
    )(o_f, o_b, z, mem_out, x2, out_norm.reshape(1, DN_HEAD_DIM), w_out, gain.reshape(1, d))


def _memory_branch(q_mem, mem, mem_gain, w_kv, seq):
    k_mem, v_mem = _mem_kv(mem, mem_gain, w_kv.astype(BF16))
    return _mem_attn(q_mem, k_mem, v_mem, seq)


def _attention_layer(x2, mem, batch, seq, rel_bias, w_in, w_out, mem_gain, w_kv, g_pre, g_post):
    n_groups = len(DILATED_GROUPS)
    tok = n_groups * GROUP_WIDTH
    dils = tuple(dil for _, dil in DILATED_GROUPS)
    w_qkv = w_in[:, :3 * tok].reshape(-1, 3, n_groups, GROUP_WIDTH).transpose(0, 2, 1, 3).astype(BF16)
    group_weights = [w_qkv[:, g].reshape(-1, 3 * GROUP_WIDTH) for g in range(n_groups)]
    *qkvs, q_mem = _att_proj(x2, g_pre, group_weights, w_in[:, 3 * tok:].astype(BF16), dils)
    os_, lses = [], []
    for group, dil in enumerate(dils):
        n_res = min(dil, 2)
        o, lse = _band_attn(qkvs[group], _band_bias(rel_bias, group, dil), batch, seq, dil, n_res,
                            q_tile=512 // n_res)
        os_.append(o)
        lses.append(lse)
    mem_out = _memory_branch(q_mem, mem, mem_gain, w_kv, seq)
    return _att_out(os_, lses, dils, mem_out, x2, w_out.astype(BF16), g_post)


def _deltanet_layer(x2, mem, batch, seq, w_in, conv_w, a_log, dt_bias, out_norm, w_out, mem_gain, w_kv,
                    g_pre, g_post):
    d = x2.shape[1]
    n_heads = a_log.shape[1]
    tok = n_heads * DN_HEAD_DIM
    n_gate = 4 * n_heads
    w_gate = w_in[:, 4 * tok:4 * tok + n_gate].reshape(d, 2, 2, n_heads).transpose(0, 2, 1, 3).reshape(d, n_gate)
    w_gate = jnp.pad(w_gate, ((0, 0), (0, LANES - n_gate)))
    w_in_b = w_in.astype(BF16)
    w_mem = w_in_b[:, 4 * tok + n_gate:]
    weights = [(w_in_b, 3 * tok, 0), (w_in_b, tok, 3), (w_gate.astype(BF16), LANES, 0), (w_mem, w_mem.shape[1], 0)]
    qkv, z, gate, q_mem = _norm_proj(x2, g_pre, weights, [BF16, BF16, F32, BF16])
    pad_row = lambda p: jnp.pad(p.reshape(1, -1).astype(F32), ((0, 0), (0, LANES - 2 * n_heads)))
    q, k, v, gb, gbt = _dn_prep(qkv, gate, conv_w.astype(F32), pad_row(a_log), pad_row(dt_bias), batch, seq,
                                2 * n_heads)
    o_f, o_b = _dn_core(q, k, v, gb, gbt, batch, seq)
    mem_out = _memory_branch(q_mem, mem, mem_gain, w_kv, seq)
    return _dn_out(o_f, o_b, z, mem_out, x2, out_norm, w_out.astype(BF16), g_post)


def kernel(x, mem, rel_bias, att_w_in, att_w_out, dn_w_in, dn_conv, dn_a_log, dn_dt_bias, dn_out_norm,
           dn_w_out, mem_norm, mem_w_kv, norm_mix_pre, norm_mix_post, norm_ffn_pre, norm_ffn_post,
           ffn_w_gate_up, ffn_w_down):
    batch, seq, d = x.shape
    depth = norm_mix_pre.shape[0]
    n_mixers = 2
    x2 = x.reshape(batch * seq, d)
    for i in range(depth):
        j = i // n_mixers
        if i % n_mixers == 0:
            x2 = _attention_layer(x2, mem, batch, seq, rel_bias, att_w_in[j], att_w_out[j], mem_norm[i],
                                  mem_w_kv[i], norm_mix_pre[i], norm_mix_post[i])
        else:
            x2 = _deltanet_layer(x2, mem, batch, seq, dn_w_in[j], dn_conv[j], dn_a_log[j], dn_dt_bias[j],
                                 dn_out_norm[j], dn_w_out[j], mem_norm[i], mem_w_kv[i], norm_mix_pre[i],
                                 norm_mix_post[i])
        x2 = _ffn(x2, norm_ffn_pre[i], ffn_w_gate_up[i].astype(BF16), ffn_w_down[i].astype(BF16),
                  norm_ffn_post[i])
    return x2.reshape(batch, seq, d)
```

```python
import functools
import math

import jax
import jax.numpy as jnp
import numpy as np
from jax import lax
from jax.experimental import pallas as pl
from jax.experimental.pallas import tpu as pltpu

F32 = jnp.float32
BF16 = jnp.bfloat16

EPS = 1e-6
NEG_INF = -1e30

LANES = 128
VMEM_LIMIT_BYTES = 56 * 1024 * 1024

DILATED_GROUPS = ((128, 1), (512, 4), (2048, 16))
ATT_HEAD_DIM = 64
HEADS_PER_GROUP = 4
GROUP_WIDTH = HEADS_PER_GROUP * ATT_HEAD_DIM
BAND_HALF = 64
REL_BUCKETS = 32
REL_MAX_DIST = 1024
MEM_HEADS = 4
MEM_HEAD_DIM = 64
DN_HEAD_DIM = 128
DN_CONV = 5
DN_CHUNK = 128
SOLVE_LEAF = 16

ATT_Q_BLOCK = 128
ATT_K_WINDOW = ATT_Q_BLOCK + 2 * BAND_HALF


def _params(*semantics):
    return pltpu.CompilerParams(dimension_semantics=semantics, vmem_limit_bytes=VMEM_LIMIT_BYTES)


def _rms(x, gain):
    return x * lax.rsqrt(jnp.mean(x * x, axis=-1, keepdims=True) + EPS) * gain


def _silu(x):
    return x / (1.0 + jnp.exp(-x))


def _const_spec(shape):
    return pl.BlockSpec(shape, lambda *_: (0,) * len(shape))


def _mem_kv_kernel(mem_ref, g_ref, w_ref, k_ref, v_ref):
    h = _rms(mem_ref[0], g_ref[...]).astype(BF16)
    kv = jnp.dot(h, w_ref[...], preferred_element_type=F32)
    width = k_ref.shape[-1]
    k_ref[0] = kv[:, :width].astype(BF16)
    v_ref[0] = kv[:, width:].astype(BF16)


def _mem_kv(mem, gain, w_kv):
    b, m, d = mem.shape
    width = w_kv.shape[1] // 2
    return pl.pallas_call(
        _mem_kv_kernel,
        grid=(b,),
        in_specs=[pl.BlockSpec((1, m, d), lambda i: (i, 0, 0)), _const_spec((1, d)), _const_spec(w_kv.shape)],
        out_specs=[pl.BlockSpec((1, m, width), lambda i: (i, 0, 0))] * 2,
        out_shape=[jax.ShapeDtypeStruct((b, m, width), BF16)] * 2,
        compiler_params=_params("parallel"),
        name="mem_kv",
    )(mem, gain.reshape(1, d), w_kv)


def _mem_attn_kernel(q_ref, k_ref, v_ref, o_ref):
    q = q_ref[...]
    k = k_ref[0]
    v = v_ref[0]
    lane = lax.broadcasted_iota(jnp.int32, q.shape, 1)
    in_head = [(lane >= h * MEM_HEAD_DIM) & (lane < (h + 1) * MEM_HEAD_DIM) for h in range(MEM_HEADS)]
    logits = [lax.dot_general(jnp.where(mask, q, jnp.zeros_like(q)), k, (((1,), (1,)), ((), ())),
                              preferred_element_type=F32) for mask in in_head]
    probs = []
    for s in logits:
        s = s * (MEM_HEAD_DIM ** -0.5)
        p = jnp.exp(s - jnp.max(s, axis=-1, keepdims=True))
        probs.append((p / jnp.sum(p, axis=-1, keepdims=True)).astype(BF16))
    pvs = [jnp.dot(p, v, preferred_element_type=F32) for p in probs]
    out = pvs[-1]
    for mask, pv in zip(in_head[:-1], pvs[:-1]):
        out = jnp.where(mask, pv, out)
    o_ref[...] = out.astype(o_ref.dtype)


def _mem_attn(q_mem, k, v, seq, tq=512):
    t, width = q_mem.shape
    m = k.shape[1]
    per_seq = seq // tq
    return pl.pallas_call(
        _mem_attn_kernel,
        grid=(t // tq,),
        in_specs=[pl.BlockSpec((tq, width), lambda i: (i, 0)),
                  pl.BlockSpec((1, m, width), lambda i: (i // per_seq, 0, 0)),
                  pl.BlockSpec((1, m, width), lambda i: (i // per_seq, 0, 0))],
        out_specs=pl.BlockSpec((tq, width), lambda i: (i, 0)),
        out_shape=jax.ShapeDtypeStruct((t, width), BF16),
        compiler_params=_params("parallel"),
        name="mem_attn",
    )(q_mem, k, v)


def _t5_bucket(rel):
    half = REL_BUCKETS // 2
    max_exact = half // 2
    n = np.abs(rel)
    large = max_exact + (np.log(np.maximum(n, 1) / max_exact) / math.log(REL_MAX_DIST / max_exact)
                         * (half - max_exact)).astype(np.int64)
    large = np.minimum(large, half - 1)
    return ((rel > 0) * half + np.where(n < max_exact, n, large)).astype(np.int32)


def _band_bias(rel_bias, group, dil):
    period = ATT_Q_BLOCK + ATT_K_WINDOW
    heads = slice(group * HEADS_PER_GROUP, (group + 1) * HEADS_PER_GROUP)
    tables = []
    for variant in range(3):
        rel = np.arange(period) - (ATT_Q_BLOCK - 1) - variant * BAND_HALF
        in_band = np.abs(rel) <= BAND_HALF
        bucket = _t5_bucket(np.clip(rel, -BAND_HALF, BAND_HALF) * dil)
        f = jnp.where(in_band[None, :], jnp.transpose(rel_bias[bucket][:, heads]).astype(F32), NEG_INF)
        flat = jnp.tile(f, (1, ATT_Q_BLOCK))[:, :ATT_Q_BLOCK * (period - 1)]
        skew = flat.reshape(HEADS_PER_GROUP, ATT_Q_BLOCK, period - 1)
        tables.append(skew[:, :, ATT_Q_BLOCK - 1:ATT_Q_BLOCK - 1 + ATT_K_WINDOW])
    return jnp.stack(tables)


def _band_attn_kernel(x_ref, bias_ref, o_ref, lse_ref, *, sub_len, n_res, q_tile):
    lane = lax.broadcasted_iota(jnp.int32, (ATT_Q_BLOCK, LANES), 1)
    low_half = lane < ATT_HEAD_DIM
    n_pairs = GROUP_WIDTH // LANES

    def tile(t, carry):
        blocks = []
        for res in range(n_res):
            for blk in range(q_tile // ATT_Q_BLOCK):
                i0 = pl.multiple_of(t * q_tile + blk * ATT_Q_BLOCK, ATT_Q_BLOCK)
                ws = jnp.clip(i0 - BAND_HALF, 0, sub_len - ATT_K_WINDOW)
                variant = (i0 - ws) // BAND_HALF
                ws = pl.multiple_of(ws, BAND_HALF)
                base = res * 3 * GROUP_WIDTH
                blocks.append(dict(
                    res=res, rows=pl.ds(i0, ATT_Q_BLOCK), variant=variant,
                    q=x_ref[0, pl.ds(i0, ATT_Q_BLOCK), base:base + GROUP_WIDTH],
                    k=x_ref[0, pl.ds(ws, ATT_K_WINDOW), base + GROUP_WIDTH:base + 2 * GROUP_WIDTH],
                    v=x_ref[0, pl.ds(ws, ATT_K_WINDOW), base + 2 * GROUP_WIDTH:base + 3 * GROUP_WIDTH]))
        heads = [(b, pair, a) for b in blocks for pair in range(n_pairs) for a in range(2)]
        logits = []
        for b, pair, a in heads:
            cols = slice(pair * LANES, (pair + 1) * LANES)
            q_pair = b["q"][:, cols]
            keep = low_half if a == 0 else jnp.logical_not(low_half)
            qh = jnp.where(keep, q_pair, jnp.zeros_like(q_pair))
            logits.append(_mm_nt(qh, b["k"][:, cols]))
        probs, dens, lses = [], [], []
        for (b, pair, a), s in zip(heads, logits):
            s = s * (ATT_HEAD_DIM ** -0.5) + bias_ref[b["variant"], 2 * pair + a]
            m = jnp.max(s, axis=-1, keepdims=True)
            p = jnp.exp(s - m)
            den = jnp.sum(p, axis=-1, keepdims=True)
            probs.append(p.astype(BF16))
            dens.append(den)
            lses.append(m + jnp.log(den))
        pvs = [_mm(p, b["v"][:, pair * LANES:(pair + 1) * LANES]) for (b, pair, a), p in zip(heads, probs)]
        for idx in range(0, len(heads), 2):
            b, pair, _ = heads[idx]
            cols = pl.ds(b["res"] * GROUP_WIDTH + pair * LANES, LANES)
            out = jnp.where(low_half, pvs[idx] / dens[idx], pvs[idx + 1] / dens[idx + 1])
            lse = jnp.where(low_half, jnp.broadcast_to(lses[idx], (ATT_Q_BLOCK, LANES)),
                            jnp.broadcast_to(lses[idx + 1], (ATT_Q_BLOCK, LANES)))
            o_ref[0, b["rows"], cols] = out
            lse_ref[0, b["rows"], cols] = lse
        return carry

    lax.fori_loop(0, sub_len // q_tile, tile, 0)


def _band_attn(qkv, bias, batch, seq, dil, n_res, q_tile=256):
    rows, width = qkv.shape
    sub_len = seq // dil
    qkv3 = qkv.reshape(batch, sub_len, width)
    x_spec = pl.BlockSpec((1, sub_len, n_res * 3 * GROUP_WIDTH), lambda b, r: (b, 0, r))
    o_spec = pl.BlockSpec((1, sub_len, n_res * GROUP_WIDTH), lambda b, r: (b, 0, r))
    o, lse = pl.pallas_call(
        functools.partial(_band_attn_kernel, sub_len=sub_len, n_res=n_res, q_tile=q_tile),
        grid=(batch, dil // n_res),
        in_specs=[x_spec, _const_spec(bias.shape)],
        out_specs=[o_spec, o_spec],
        out_shape=[jax.ShapeDtypeStruct((batch, sub_len, dil * GROUP_WIDTH), F32)] * 2,
        compiler_params=_params("parallel", "parallel"),
        name=f"band_attn_d{dil}",
    )(qkv3, bias)
    return o.reshape(rows, dil * GROUP_WIDTH), lse.reshape(rows, dil * GROUP_WIDTH)


def _att_proj_kernel(x_ref, g_ref, *refs, dils):
    n = len(dils)
    w_refs, wmem_ref = refs[:n], refs[n]
    out_refs, qmem_ref, stage_ref = refs[n + 1:2 * n + 1], refs[2 * n + 1], refs[2 * n + 2]
    h = _rms(x_ref[...], g_ref[...]).astype(BF16)
    tm = h.shape[0]
    for w_ref, o_ref, dil in zip(w_refs, out_refs, dils):
        y = jnp.dot(h, w_ref[...], preferred_element_type=F32)
        width = y.shape[1]
        if dil == 1:
            o_ref[...] = y.astype(BF16)
        else:
            for c in range(width // LANES):
                stage_ref[c] = y[:, c * LANES:(c + 1) * LANES]
            for r in range(dil):
                for c in range(width // LANES):
                    lanes = pl.ds(r * width + c * LANES, LANES)
                    o_ref[:, lanes] = stage_ref[c, pl.ds(r, tm // dil, stride=dil), :].astype(BF16)
    qmem_ref[...] = jnp.dot(h, wmem_ref[...], preferred_element_type=F32).astype(BF16)


def _att_proj(x2, gain, group_weights, w_mem, dils, tm=512):
    t, d = x2.shape
    width = group_weights[0].shape[1]
    row = lambda i: (i, 0)
    return pl.pallas_call(
        functools.partial(_att_proj_kernel, dils=dils),
        grid=(t // tm,),
        in_specs=[pl.BlockSpec((tm, d), row), _const_spec((1, d))]
        + [_const_spec(w.shape) for w in group_weights] + [_const_spec(w_mem.shape)],
        out_specs=[pl.BlockSpec((tm // dil, dil * width), row) for dil in dils]
        + [pl.BlockSpec((tm, w_mem.shape[1]), row)],
        out_shape=[jax.ShapeDtypeStruct((t // dil, dil * width), BF16) for dil in dils]
        + [jax.ShapeDtypeStruct((t, w_mem.shape[1]), BF16)],
        scratch_shapes=[pltpu.VMEM((width // LANES, tm, LANES), F32)],
        compiler_params=_params("parallel"),
        name="att_proj",
    )(x2, gain.reshape(1, d), *group_weights, w_mem)


def _att_out_kernel(*refs, dils):
    n = len(dils)
    o_refs, l_refs = refs[:n], refs[n:2 * n]
    mem_ref, x_ref, w_ref, g_ref, out_ref, stage_ref = refs[2 * n:]
    tm = x_ref.shape[0]

    def token_major(ref, dil, slot):
        if dil == 1:
            return ref[...]
        tiles = GROUP_WIDTH // LANES
        for r in range(dil):
            for c in range(tiles):
                lanes = pl.ds(r * GROUP_WIDTH + c * LANES, LANES)
                stage_ref[slot * tiles + c, pl.ds(r, tm // dil, stride=dil), :] = ref[:, lanes]
        return jnp.concatenate([stage_ref[slot * tiles + c] for c in range(tiles)], axis=-1)

    lses = [token_major(ref, dil, 2 * i) for i, (ref, dil) in enumerate(zip(l_refs, dils))]
    outs = [token_major(ref, dil, 2 * i + 1) for i, (ref, dil) in enumerate(zip(o_refs, dils))]
    mx = functools.reduce(jnp.maximum, lses)
    es = [jnp.exp(l - mx) for l in lses]
    tot = functools.reduce(lambda a, b: a + b, es)
    parts = [(o * (e / tot)).astype(BF16) for o, e in zip(outs, es)]
    mixed = jnp.concatenate(parts + [mem_ref[...]], axis=-1)
    y = jnp.dot(mixed, w_ref[...], preferred_element_type=F32)
    out_ref[...] = x_ref[...] + _rms(y, g_ref[...])


def _att_out(os_, lses, dils, mem_out, x2, w_out, gain, tm=512):
    t, d = x2.shape
    row = lambda i: (i, 0)
    group_specs = [pl.BlockSpec((tm // dil, dil * GROUP_WIDTH), row) for dil in dils]
    return pl.pallas_call(
        functools.partial(_att_out_kernel, dils=dils),
        grid=(t // tm,),
        in_specs=group_specs + group_specs
        + [pl.BlockSpec((tm, mem_out.shape[1]), row), pl.BlockSpec((tm, d), row),
           _const_spec(w_out.shape), _const_spec((1, d))],
        out_specs=pl.BlockSpec((tm, d), row),
        out_shape=jax.ShapeDtypeStruct((t, d), F32),
        scratch_shapes=[pltpu.VMEM((2 * len(dils) * (GROUP_WIDTH // LANES), tm, LANES), F32)],
        compiler_params=_params("parallel"),
        name="att_out",
    )(*os_, *lses, mem_out, x2, w_out, gain.reshape(1, d))


def _ffn_kernel(x_ref, gpre_ref, wg_ref, wu_ref, wd_ref, gpost_ref, out_ref, *, n_chunks):
    x = x_ref[...]
    h = _rms(x, gpre_ref[...]).astype(BF16)
    d_ff = wg_ref.shape[1]
    step = d_ff // n_chunks
    y = jnp.zeros(x.shape, F32)
    for c in range(n_chunks):
        cols = slice(c * step, (c + 1) * step)
        gate = jnp.dot(h, wg_ref[:, cols], preferred_element_type=F32)
        up = jnp.dot(h, wu_ref[:, cols], preferred_element_type=F32)
        act = (_silu(gate) * up).astype(BF16)
        y = y + jnp.dot(act, wd_ref[cols, :], preferred_element_type=F32)
    out_ref[...] = x + _rms(y, gpost_ref[...])


def _ffn(x2, g_pre, w_gate_up, w_down, g_post, tm=512, n_chunks=2):
    t, d = x2.shape
    d_ff = w_down.shape[0]
    row = lambda i: (i, 0)
    return pl.pallas_call(
        functools.partial(_ffn_kernel, n_chunks=n_chunks),
        grid=(t // tm,),
        in_specs=[pl.BlockSpec((tm, d), row), _const_spec((1, d)),
                  pl.BlockSpec((d, d_ff), lambda i: (0, 0)), pl.BlockSpec((d, d_ff), lambda i: (0, 1)),
                  _const_spec(w_down.shape), _const_spec((1, d))],
        out_specs=pl.BlockSpec((tm, d), row),
        out_shape=jax.ShapeDtypeStruct((t, d), F32),
        compiler_params=_params("parallel"),
        name="ffn",
    )(x2, g_pre.reshape(1, d), w_gate_up, w_gate_up, w_down, g_post.reshape(1, d))


def _dn_proj_kernel(x_ref, xprev_ref, xnext_ref, g_ref, wqkv_ref, wz_ref, wgate_ref, wmem_ref, conv_ref,
                    alog_ref, dtb_ref, q_ref, k_ref, v_ref, z_ref, qmem_ref, gb_ref, gbt_ref, ext_ref,
                    *, halo, n_dir_heads):
    i = pl.program_id(1)
    last = pl.num_programs(1) - 1
    tm = x_ref.shape[0]
    width = wqkv_ref.shape[1]
    tok = width // 3
    gain = g_ref[...]
    h = _rms(x_ref[...], gain).astype(BF16)
    h_prev = jnp.where(i > 0, _rms(xprev_ref[...], gain), 0.0).astype(BF16)
    h_next = jnp.where(i < last, _rms(xnext_ref[...], gain), 0.0).astype(BF16)
    h_ext = jnp.concatenate([h_prev, h, h_next], axis=0)
    chunk = 2 * LANES
    for c0 in range(0, width, chunk):
        y_ext = jnp.dot(h_ext, wqkv_ref[:, c0:c0 + chunk], preferred_element_type=F32)
        for half in range(chunk // LANES):
            c1 = c0 + half * LANES
            cols = slice(c1, c1 + LANES)
            ext = ext_ref.at[(c1 // LANES) % ext_ref.shape[0]]
            ext[...] = y_ext[:, half * LANES:(half + 1) * LANES]
            acc = jnp.zeros((tm, LANES), F32)
            for j in range(DN_CONV):
                first = halo + j - DN_CONV // 2
                acc = acc + ext[first:first + tm, :] * conv_ref[j:j + 1, cols]
            y = _silu(acc)
            which, col = divmod(c1, tok)
            if which == 2:
                v_ref[:, col:col + LANES] = y.astype(BF16)
            else:
                yn = y * lax.rsqrt(jnp.sum(y * y, axis=-1, keepdims=True) + EPS)
                if which == 0:
                    q_ref[:, col:col + LANES] = (yn * (DN_HEAD_DIM ** -0.5)).astype(BF16)
                else:
                    k_ref[:, col:col + LANES] = yn.astype(BF16)
    z_ref[...] = jnp.dot(h, wz_ref[...], preferred_element_type=F32).astype(BF16)
    qmem_ref[...] = jnp.dot(h, wmem_ref[...], preferred_element_type=F32).astype(BF16)
    gate = jnp.dot(h, wgate_ref[...], preferred_element_type=F32)
    shifted = gate + dtb_ref[...]
    softplus = jnp.maximum(shifted, 0.0) + jnp.log(1.0 + jnp.exp(-jnp.abs(shifted)))
    decay = -jnp.exp(alog_ref[...]) * softplus
    beta = 1.0 / (1.0 + jnp.exp(-gate))
    lane = lax.broadcasted_iota(jnp.int32, gate.shape, 1)
    gb = jnp.where(lane < n_dir_heads, decay, beta)
    gb_ref[...] = gb
    gbt_ref[...] = jnp.transpose(gb)[:gbt_ref.shape[0], :]


def _dn_proj(x2, gain, w_in, w_gate, w_mem, conv_w, a_log_row, dt_bias_row, batch, seq, tok, n_dir_heads,
             tm=512, halo=16):
    t, d = x2.shape
    n_gate_rows = 8 * ((2 * n_dir_heads + 7) // 8)
    per_seq = seq // tm
    blocks_per_tile = tm // halo
    n_halo_blocks = t // halo
    mem_width = w_mem.shape[1]

    def cur_map(b, i):
        return (b * per_seq + i, 0)

    def prev_map(b, i):
        return (jnp.maximum((b * per_seq + i) * blocks_per_tile - 1, 0), 0)

    def next_map(b, i):
        return (jnp.minimum((b * per_seq + i + 1) * blocks_per_tile, n_halo_blocks - 1), 0)

    return pl.pallas_call(
        functools.partial(_dn_proj_kernel, halo=halo, n_dir_heads=n_dir_heads),
        grid=(batch, per_seq),
        in_specs=[pl.BlockSpec((tm, d), cur_map), pl.BlockSpec((halo, d), prev_map),
                  pl.BlockSpec((halo, d), next_map), _const_spec((1, d)),
                  pl.BlockSpec((d, 3 * tok), lambda b, i: (0, 0)), pl.BlockSpec((d, tok), lambda b, i: (0, 3)),
                  _const_spec(w_gate.shape), _const_spec(w_mem.shape), _const_spec(conv_w.shape),
                  _const_spec((1, LANES)), _const_spec((1, LANES))],
        out_specs=[pl.BlockSpec((tm, tok), cur_map)] * 4
        + [pl.BlockSpec((tm, mem_width), cur_map), pl.BlockSpec((tm, LANES), cur_map),
           pl.BlockSpec((n_gate_rows, tm), lambda b, i: (0, b * per_seq + i))],
        out_shape=[jax.ShapeDtypeStruct((t, tok), BF16)] * 4
        + [jax.ShapeDtypeStruct((t, mem_width), BF16), jax.ShapeDtypeStruct((t, LANES), F32),
           jax.ShapeDtypeStruct((n_gate_rows, t), F32)],
        scratch_shapes=[pltpu.VMEM((2, tm + 2 * halo, LANES), F32)],
        compiler_params=_params("parallel", "arbitrary"),
        name="dn_proj",
    )(x2, x2, x2, gain.reshape(1, d), w_in, w_in, w_gate, w_mem, conv_w, a_log_row, dt_bias_row)


def _mm(a, b):
    return jnp.dot(a, b, preferred_element_type=F32)


def _mm_nt(a, b):
    return lax.dot_general(a, b, (((1,), (1,)), ((), ())), preferred_element_type=F32)


def _mm_tn(a, b):
    return lax.dot_general(a, b, (((0,), (0,)), ((), ())), preferred_element_type=F32)


def _row_blocks(x, size, parity):
    return jnp.concatenate([x[b * size:(b + 1) * size] for b in range(parity, x.shape[0] // size, 2)], axis=0)


def _interleave_row_blocks(even, odd, size):
    pieces = []
    for b in range(even.shape[0] // size):
        pieces += [even[b * size:(b + 1) * size], odd[b * size:(b + 1) * size]]
    return jnp.concatenate(pieces, axis=0)


def _unit_triangular_inverses(neg_ls, uppers, same_block):
    n = neg_ls[0].shape[0]
    ii = lax.broadcasted_iota(jnp.int32, (n, n), 0)
    jj = lax.broadcasted_iota(jnp.int32, (n, n), 1)
    identity = (ii == jj).astype(F32).astype(BF16)
    zero = jnp.zeros((n, n), BF16)
    plus_identity = lambda m: m + identity
    leaf = same_block[SOLVE_LEAF].astype(F32).astype(BF16)
    xbs = [nl * leaf for nl in neg_ls]
    invs = [plus_identity(xb) for xb in xbs]
    power = 1
    while 2 * power < SOLVE_LEAF:
        xbs = [_mm(xb, xb).astype(BF16) for xb in xbs]
        invs = [_mm(plus_identity(xb), inv).astype(BF16) for xb, inv in zip(xbs, invs)]
        power *= 2
    size = SOLVE_LEAF
    while size < n:
        merge = (same_block[2 * size] & jnp.logical_not(same_block[size])).astype(F32).astype(BF16)
        hot = [0 if upper else 1 for upper in uppers]
        offs = [_row_blocks(nl, size, p) * _row_blocks(merge, size, p) for nl, p in zip(neg_ls, hot)]
        rights = [_mm(off, inv).astype(BF16) for off, inv in zip(offs, invs)]
        half_zero = zero[:n // 2]
        rights = [_interleave_row_blocks(*((r, half_zero) if p == 0 else (half_zero, r)), size)
                  for r, p in zip(rights, hot)]
        moved = [_mm(_row_blocks(inv, size, p), plus_identity(right)).astype(BF16)
                 for inv, right, p in zip(invs, rights, hot)]
        invs = [_interleave_row_blocks(*((m, _row_blocks(inv, size, 1)) if p == 0 else
                                         (_row_blocks(inv, size, 0), m)), size)
                for inv, m, p in zip(invs, moved, hot)]
        size *= 2
    return invs


def _exact_prefix_sums(x, tri, rows):
    hi = x.astype(BF16).astype(F32)
    mid = (x - hi).astype(BF16).astype(F32)
    lo = (x - hi - mid).astype(BF16).astype(F32)
    if rows:
        n = x.shape[0]
        res = _mm(jnp.concatenate([hi, mid, lo], axis=0).astype(BF16), tri)
        return res[:n] + res[n:2 * n] + res[2 * n:]
    quarter = LANES // 4
    packed = hi + pltpu.roll(mid, quarter, 1) + pltpu.roll(lo, 2 * quarter, 1)
    res = _mm(tri, packed.astype(BF16))
    return res + pltpu.roll(res, LANES - quarter, 1) + pltpu.roll(res, LANES - 2 * quarter, 1)


def _dn_core_kernel(qf_ref, kf_ref, vf_ref, gbf_ref, gtf_ref, qb_ref, kb_ref, vb_ref, gbb_ref, gtb_ref,
                    of_ref, ob_ref, state_ref, *, n_heads):
    @pl.when(pl.program_id(1) == 0)
    def _():
        state_ref[...] = jnp.zeros_like(state_ref)

    c = DN_CHUNK
    d = DN_HEAD_DIM
    n_sub = qf_ref.shape[0] // c
    ii = lax.broadcasted_iota(jnp.int32, (c, c), 0)
    jj = lax.broadcasted_iota(jnp.int32, (c, c), 1)
    lower_eq, upper_eq = jj <= ii, jj >= ii
    lower, upper = jj < ii, jj > ii
    same_block = {}
    size = SOLVE_LEAF
    while size <= c:
        shift = int(math.log2(size))
        same_block[size] = (ii >> shift) == (jj >> shift)
        size *= 2

    items = []
    for direction, (q_ref, k_ref, v_ref, gb_ref, gt_ref, o_ref) in enumerate(
            ((qf_ref, kf_ref, vf_ref, gbf_ref, gtf_ref, of_ref),
             (qb_ref, kb_ref, vb_ref, gbb_ref, gtb_ref, ob_ref))):
        reverse = direction == 1
        tri_col = (upper_eq if reverse else lower_eq).astype(BF16)
        tri_row = (lower_eq if reverse else upper_eq).astype(BF16)
        for sub in range(n_sub):
            rows = slice(sub * c, (sub + 1) * c)
            gb = gb_ref[rows, :]
            lane = lax.broadcasted_iota(jnp.int32, gb.shape, 1)
            cum_col = _exact_prefix_sums(jnp.where(lane < LANES // 4, gb, 0.0), tri_col, rows=False)
            cum_row = _exact_prefix_sums(gt_ref[:, rows], tri_row, rows=True)
            for head in range(n_heads):
                g_idx = direction * n_heads + head
                b_idx = 2 * n_heads + g_idx
                cols = slice(head * d, (head + 1) * d)
                items.append(dict(
                    direction=direction, head=head, sub=sub, o_ref=o_ref, rows=rows, cols=cols, reverse=reverse,
                    earlier_eq=upper_eq if reverse else lower_eq,
                    earlier=upper if reverse else lower,
                    q=q_ref[rows, cols], k=k_ref[rows, cols], v=v_ref[rows, cols],
                    g_col=cum_col[:, g_idx:g_idx + 1], beta_col=gb[:, b_idx:b_idx + 1],
                    g_row=cum_row[g_idx:g_idx + 1, :],
                    g_last=cum_col[0:1, g_idx:g_idx + 1] if reverse else cum_col[c - 1:c, g_idx:g_idx + 1]))

    grams = [_mm_nt(jnp.concatenate([it["k"], it["q"]], axis=0), it["k"]) for it in items]
    neg_ls, rhss = [], []
    for it, gram in zip(items, grams):
        g_wide = jnp.broadcast_to(it["g_col"], (c, d))
        beta_wide = jnp.broadcast_to(it["beta_col"], (c, d))
        it["g_last_wide"] = jnp.broadcast_to(it["g_last"], (c, d))
        decay = jnp.exp(jnp.where(it["earlier_eq"], g_wide - it["g_row"], NEG_INF))
        neg_ls.append((jnp.where(it["earlier"], gram[:c] * decay, 0.0) * (-beta_wide)).astype(BF16))
        it["intra"] = jnp.where(it["earlier_eq"], gram[c:] * decay, 0.0).astype(BF16)
        kf = it["k"].astype(F32)
        e_g = jnp.exp(g_wide)
        rhss.append(jnp.concatenate([(it["v"].astype(F32) * beta_wide).astype(BF16),
                                     (kf * (beta_wide * e_g)).astype(BF16)], axis=-1))
        it["qg"] = (it["q"].astype(F32) * e_g).astype(BF16)
        it["k_dec"] = (kf * jnp.exp(it["g_last_wide"] - g_wide)).astype(BF16)
    invs = _unit_triangular_inverses(neg_ls, [it["reverse"] for it in items], same_block)
    for it, inv, rhs in zip(items, invs, rhss):
        sol = _mm(inv, rhs)
        it["u"], it["w"] = sol[:, :d], sol[:, d:].astype(BF16)

    for step in range(n_sub):
        live = [it for it in items if it["sub"] == (n_sub - 1 - step if it["reverse"] else step)]
        states = [state_ref[it["direction"], it["head"]] for it in live]
        wss = [_mm(jnp.concatenate([it["w"], it["qg"]], axis=0), state.astype(BF16))
               for it, state in zip(live, states)]
        v_news = [(it["u"] - ws[:c]).astype(BF16) for it, ws in zip(live, wss)]
        outs = [ws[c:] + _mm(it["intra"], v_new) for it, ws, v_new in zip(live, wss, v_news)]
        updates = [_mm_tn(it["k_dec"], v_new) for it, v_new in zip(live, v_news)]
        for it, out, state, update in zip(live, outs, states, updates):
            it["o_ref"][it["rows"], it["cols"]] = out
            state_ref[it["direction"], it["head"]] = state * jnp.exp(it["g_last_wide"]) + update


def _dn_core(q, k, v, gb, gbt, batch, seq, chunks_per_step=2):
    t, tok = q.shape
    n_heads = tok // DN_HEAD_DIM
    rows = chunks_per_step * DN_CHUNK
    steps = seq // rows
    n_rows = gbt.shape[0]

    fwd = lambda b, i: (b * steps + i, 0)
    bwd = lambda b, i: (b * steps + steps - 1 - i, 0)
    fwd_t = lambda b, i: (0, b * steps + i)
    bwd_t = lambda b, i: (0, b * steps + steps - 1 - i)

    def side(tok_map, t_map):
        return [pl.BlockSpec((rows, tok), tok_map)] * 3 + [
            pl.BlockSpec((rows, LANES), tok_map), pl.BlockSpec((n_rows, rows), t_map)]

    return pl.pallas_call(
        functools.partial(_dn_core_kernel, n_heads=n_heads),
        grid=(batch, steps),
        in_specs=side(fwd, fwd_t) + side(bwd, bwd_t),
        out_specs=[pl.BlockSpec((rows, tok), fwd), pl.BlockSpec((rows, tok), bwd)],
        out_shape=[jax.ShapeDtypeStruct((t, tok), F32)] * 2,
        scratch_shapes=[pltpu.VMEM((2, n_heads, DN_HEAD_DIM, DN_HEAD_DIM), F32)],
        compiler_params=_params("parallel", "arbitrary"),
        name="dn_core",
    )(q, k, v, gb, gbt, q, k, v, gb, gbt)


def _dn_out_kernel(of_ref, ob_ref, z_ref, mem_ref, x_ref, onorm_ref, w_ref, g_ref, out_ref):
    tok = of_ref.shape[1]
    parts = []
    for c0 in range(0, tok, DN_HEAD_DIM):
        cols = slice(c0, c0 + DN_HEAD_DIM)
        o = of_ref[:, cols] + ob_ref[:, cols]
        o = o * lax.rsqrt(jnp.mean(o * o, axis=-1, keepdims=True) + EPS) * onorm_ref[...]
        parts.append((o * _silu(z_ref[:, cols].astype(F32))).astype(BF16))
    mixed = jnp.concatenate(parts + [mem_ref[...]], axis=-1)
    y = jnp.dot(mixed, w_ref[...], preferred_element_type=F32)
    out_ref[...] = x_ref[...] + _rms(y, g_ref[...])


def _dn_out(o_f, o_b, z, mem_out, x2, out_norm, w_out, gain, tm=512):
    t, d = x2.shape
    tok = o_f.shape[1]
    row = lambda i: (i, 0)
    return pl.pallas_call(
        _dn_out_kernel,
        grid=(t // tm,),
        in_specs=[pl.BlockSpec((tm, tok), row)] * 3
        + [pl.BlockSpec((tm, mem_out.shape[1]), row), pl.BlockSpec((tm, d), row),
           _const_spec((1, DN_HEAD_DIM)), _const_spec(w_out.shape), _const_spec((1, d))],
        out_specs=pl.BlockSpec((tm, d), row),
        out_shape=jax.ShapeDtypeStruct((t, d), F32),
        compiler_params=_params("parallel"),
        name="dn_out",
    )(o_f, o_b, z, mem_out, x2, out_norm.reshape(1, DN_HEAD_DIM), w_out, gain.reshape(1, d))


def _memory_branch(q_mem, mem, mem_gain, w_kv, seq):
    k_mem, v_mem = _mem_kv(mem, mem_gain, w_kv.astype(BF16))
    return _mem_attn(q_mem, k_mem, v_mem, seq)


def _attention_layer(x2, mem, batch, seq, rel_bias, w_in, w_out, mem_gain, w_kv, g_pre, g_post):
    n_groups = len(DILATED_GROUPS)
    tok = n_groups * GROUP_WIDTH
    dils = tuple(dil for _, dil in DILATED_GROUPS)
    w_qkv = w_in[:, :3 * tok].reshape(-1, 3, n_groups, GROUP_WIDTH).transpose(0, 2, 1, 3).astype(BF16)
    group_weights = [w_qkv[:, g].reshape(-1, 3 * GROUP_WIDTH) for g in range(n_groups)]
    *qkvs, q_mem = _att_proj(x2, g_pre, group_weights, w_in[:, 3 * tok:].astype(BF16), dils)
    os_, lses = [], []
    for group, dil in enumerate(dils):
        n_res = min(dil, 2)
        o, lse = _band_attn(qkvs[group], _band_bias(rel_bias, group, dil), batch, seq, dil, n_res,
                            q_tile=512 // n_res)
        os_.append(o)
        lses.append(lse)
    mem_out = _memory_branch(q_mem, mem, mem_gain, w_kv, seq)
    return _att_out(os_, lses, dils, mem_out, x2, w_out.astype(BF16), g_post)


def _deltanet_layer(x2, mem, batch, seq, w_in, conv_w, a_log, dt_bias, out_norm, w_out, mem_gain, w_kv,
                    g_pre, g_post):
    d = x2.shape[1]
    n_heads = a_log.shape[1]
    tok = n_heads * DN_HEAD_DIM
    n_gate = 4 * n_heads
    w_gate = w_in[:, 4 * tok:4 * tok + n_gate].reshape(d, 2, 2, n_heads).transpose(0, 2, 1, 3).reshape(d, n_gate)
    w_gate = jnp.pad(w_gate, ((0, 0), (0, LANES - n_gate)))
    w_in_b = w_in.astype(BF16)
    pad_row = lambda p: jnp.pad(p.reshape(1, -1).astype(F32), ((0, 0), (0, LANES - 2 * n_heads)))
    q, k, v, z, q_mem, gb, gbt = _dn_proj(x2, g_pre, w_in_b, w_gate.astype(BF16), w_in_b[:, 4 * tok + n_gate:],
                                          conv_w.astype(F32), pad_row(a_log), pad_row(dt_bias), batch, seq, tok,
                                          2 * n_heads)
    o_f, o_b = _dn_core(q, k, v, gb, gbt, batch, seq)
    mem_out = _memory_branch(q_mem, mem, mem_gain, w_kv, seq)
    return _dn_out(o_f, o_b, z, mem_out, x2, out_norm, w_out.astype(BF16), g_post)


def kernel(x, mem, rel_bias, att_w_in, att_w_out, dn_w_in, dn_conv, dn_a_log, dn_dt_bias, dn_out_norm,
           dn_w_out, mem_norm, mem_w_kv, norm_mix_pre, norm_mix_post, norm_ffn_pre, norm_ffn_post,
           ffn_w_gate_up, ffn_w_down):
    batch, seq, d = x.shape
    depth = norm_mix_pre.shape[0]
    n_mixers = 2
    x2 = x.reshape(batch * seq, d)
    for i in range(depth):
        j = i // n_mixers
        if i % n_mixers == 0:
            x2 = _attention_layer(x2, mem, batch, seq, rel_bias, att_w_in[j], att_w_out[j], mem_norm[i],
                                  mem_w_kv[i], norm_mix_pre[i], norm_mix_post[i])
        else:
            x2 = _deltanet_layer(x2, mem, batch, seq, dn_w_in[j], dn_conv[j], dn_a_log[j], dn_dt_bias[j],
                                 dn_out_norm[j], dn_w_out[j], mem_norm[i], mem_w_kv[i], norm_mix_pre[i],
                                 norm_mix_post[i])
        x2 = _ffn(x2, norm_ffn_pre[i], ffn_w_gate_up[i].astype(BF16), ffn_w_down[i].astype(BF16),
                  norm_ffn_post[i])
    return x2.reshape(batch, seq, d)
```

```python
import functools
import math

import jax
import jax.numpy as jnp
import numpy as np
from jax import lax
from jax.experimental import pallas as pl
from jax.experimental.pallas import tpu as pltpu

F32 = jnp.float32
BF16 = jnp.bfloat16

EPS = 1e-6
NEG_INF = -1e30

LANES = 128
VMEM_LIMIT_BYTES = 56 * 1024 * 1024

DILATED_GROUPS = ((128, 1), (512, 4), (2048, 16))
ATT_HEAD_DIM = 64
HEADS_PER_GROUP = 4
GROUP_WIDTH = HEADS_PER_GROUP * ATT_HEAD_DIM
BAND_HALF = 64
REL_BUCKETS = 32
REL_MAX_DIST = 1024
MEM_HEADS = 4
MEM_HEAD_DIM = 64
DN_HEAD_DIM = 128
DN_CONV = 5
DN_CHUNK = 128
SOLVE_LEAF = 16

ATT_Q_BLOCK = 128
ATT_K_WINDOW = ATT_Q_BLOCK + 2 * BAND_HALF


def _params(*semantics):
    return pltpu.CompilerParams(dimension_semantics=semantics, vmem_limit_bytes=VMEM_LIMIT_BYTES)


def _rms(x, gain):
    return x * lax.rsqrt(jnp.mean(x * x, axis=-1, keepdims=True) + EPS) * gain


def _silu(x):
    return x / (1.0 + jnp.exp(-x))


def _const_spec(shape):
    return pl.BlockSpec(shape, lambda *_: (0,) * len(shape))


def _mem_kv_kernel(mem_ref, g_ref, w_ref, k_ref, v_ref):
    h = _rms(mem_ref[0], g_ref[...]).astype(BF16)
    kv = jnp.dot(h, w_ref[...], preferred_element_type=F32)
    width = k_ref.shape[-1]
    k_ref[0] = kv[:, :width].astype(BF16)
    v_ref[0] = kv[:, width:].astype(BF16)


def _mem_kv(mem, gain, w_kv):
    b, m, d = mem.shape
    width = w_kv.shape[1] // 2
    return pl.pallas_call(
        _mem_kv_kernel,
        grid=(b,),
        in_specs=[pl.BlockSpec((1, m, d), lambda i: (i, 0, 0)), _const_spec((1, d)), _const_spec(w_kv.shape)],
        out_specs=[pl.BlockSpec((1, m, width), lambda i: (i, 0, 0))] * 2,
        out_shape=[jax.ShapeDtypeStruct((b, m, width), BF16)] * 2,
        compiler_params=_params("parallel"),
        name="mem_kv",
    )(mem, gain.reshape(1, d), w_kv)


def _mem_attn_kernel(q_ref, k_ref, v_ref, o_ref):
    q = q_ref[...]
    k = k_ref[0]
    v = v_ref[0]
    lane = lax.broadcasted_iota(jnp.int32, q.shape, 1)
    in_head = [(lane >= h * MEM_HEAD_DIM) & (lane < (h + 1) * MEM_HEAD_DIM) for h in range(MEM_HEADS)]
    logits = [lax.dot_general(jnp.where(mask, q, jnp.zeros_like(q)), k, (((1,), (1,)), ((), ())),
                              preferred_element_type=F32) for mask in in_head]
    probs = []
    for s in logits:
        s = s * (MEM_HEAD_DIM ** -0.5)
        p = jnp.exp(s - jnp.max(s, axis=-1, keepdims=True))
        probs.append((p / jnp.sum(p, axis=-1, keepdims=True)).astype(BF16))
    pvs = [jnp.dot(p, v, preferred_element_type=F32) for p in probs]
    out = pvs[-1]
    for mask, pv in zip(in_head[:-1], pvs[:-1]):
        out = jnp.where(mask, pv, out)
    o_ref[...] = out.astype(o_ref.dtype)


def _mem_attn(q_mem, k, v, seq, tq=512):
    t, width = q_mem.shape
    m = k.shape[1]
    per_seq = seq // tq
    return pl.pallas_call(
        _mem_attn_kernel,
        grid=(t // tq,),
        in_specs=[pl.BlockSpec((tq, width), lambda i: (i, 0)),
                  pl.BlockSpec((1, m, width), lambda i: (i // per_seq, 0, 0)),
                  pl.BlockSpec((1, m, width), lambda i: (i // per_seq, 0, 0))],
        out_specs=pl.BlockSpec((tq, width), lambda i: (i, 0)),
        out_shape=jax.ShapeDtypeStruct((t, width), BF16),
        compiler_params=_params("parallel"),
        name="mem_attn",
    )(q_mem, k, v)


def _t5_bucket(rel):
    half = REL_BUCKETS // 2
    max_exact = half // 2
    n = np.abs(rel)
    large = max_exact + (np.log(np.maximum(n, 1) / max_exact) / math.log(REL_MAX_DIST / max_exact)
                         * (half - max_exact)).astype(np.int64)
    large = np.minimum(large, half - 1)
    return ((rel > 0) * half + np.where(n < max_exact, n, large)).astype(np.int32)


def _band_bias(rel_bias, group, dil):
    period = ATT_Q_BLOCK + ATT_K_WINDOW
    heads = slice(group * HEADS_PER_GROUP, (group + 1) * HEADS_PER_GROUP)
    tables = []
    for variant in range(3):
        rel = np.arange(period) - (ATT_Q_BLOCK - 1) - variant * BAND_HALF
        in_band = np.abs(rel) <= BAND_HALF
        bucket = _t5_bucket(np.clip(rel, -BAND_HALF, BAND_HALF) * dil)
        f = jnp.where(in_band[None, :], jnp.transpose(rel_bias[bucket][:, heads]).astype(F32), NEG_INF)
        flat = jnp.tile(f, (1, ATT_Q_BLOCK))[:, :ATT_Q_BLOCK * (period - 1)]
        skew = flat.reshape(HEADS_PER_GROUP, ATT_Q_BLOCK, period - 1)
        tables.append(skew[:, :, ATT_Q_BLOCK - 1:ATT_Q_BLOCK - 1 + ATT_K_WINDOW])
    return jnp.stack(tables)


def _band_attn_kernel(x_ref, bias_ref, o_ref, lse_ref, *, sub_len, n_res, q_tile):
    lane = lax.broadcasted_iota(jnp.int32, (ATT_Q_BLOCK, LANES), 1)
    low_half = lane < ATT_HEAD_DIM
    n_pairs = GROUP_WIDTH // LANES

    def tile(t, carry):
        blocks = []
        for res in range(n_res):
            for blk in range(q_tile // ATT_Q_BLOCK):
                i0 = pl.multiple_of(t * q_tile + blk * ATT_Q_BLOCK, ATT_Q_BLOCK)
                ws = jnp.clip(i0 - BAND_HALF, 0, sub_len - ATT_K_WINDOW)
                variant = (i0 - ws) // BAND_HALF
                ws = pl.multiple_of(ws, BAND_HALF)
                base = res * 3 * GROUP_WIDTH
                blocks.append(dict(
                    res=res, rows=pl.ds(i0, ATT_Q_BLOCK), variant=variant,
                    q=x_ref[0, pl.ds(i0, ATT_Q_BLOCK), base:base + GROUP_WIDTH],
                    k=x_ref[0, pl.ds(ws, ATT_K_WINDOW), base + GROUP_WIDTH:base + 2 * GROUP_WIDTH],
                    v=x_ref[0, pl.ds(ws, ATT_K_WINDOW), base + 2 * GROUP_WIDTH:base + 3 * GROUP_WIDTH]))
        heads = [(b, pair, a) for b in blocks for pair in range(n_pairs) for a in range(2)]
        logits = []
        for b, pair, a in heads:
            cols = slice(pair * LANES, (pair + 1) * LANES)
            q_pair = b["q"][:, cols]
            keep = low_half if a == 0 else jnp.logical_not(low_half)
            qh = jnp.where(keep, q_pair, jnp.zeros_like(q_pair))
            logits.append(_mm_nt(qh, b["k"][:, cols]))
        probs, dens, lses = [], [], []
        for (b, pair, a), s in zip(heads, logits):
            s = s * (ATT_HEAD_DIM ** -0.5) + bias_ref[b["variant"], 2 * pair + a]
            m = jnp.max(s, axis=-1, keepdims=True)
            p = jnp.exp(s - m)
            den = jnp.sum(p, axis=-1, keepdims=True)
            probs.append(p.astype(BF16))
            dens.append(den)
            lses.append(m + jnp.log(den))
        pvs = [_mm(p, b["v"][:, pair * LANES:(pair + 1) * LANES]) for (b, pair, a), p in zip(heads, probs)]
        for idx in range(0, len(heads), 2):
            b, pair, _ = heads[idx]
            cols = pl.ds(b["res"] * GROUP_WIDTH + pair * LANES, LANES)
            out = jnp.where(low_half, pvs[idx] / dens[idx], pvs[idx + 1] / dens[idx + 1])
            lse = jnp.where(low_half, jnp.broadcast_to(lses[idx], (ATT_Q_BLOCK, LANES)),
                            jnp.broadcast_to(lses[idx + 1], (ATT_Q_BLOCK, LANES)))
            o_ref[0, b["rows"], cols] = out
            lse_ref[0, b["rows"], cols] = lse
        return carry

    lax.fori_loop(0, sub_len // q_tile, tile, 0)


def _band_attn(qkv, bias, batch, seq, dil, n_res, q_tile=256):
    rows, width = qkv.shape
    sub_len = seq // dil
    qkv3 = qkv.reshape(batch, sub_len, width)
    x_spec = pl.BlockSpec((1, sub_len, n_res * 3 * GROUP_WIDTH), lambda b, r: (b, 0, r))
    o_spec = pl.BlockSpec((1, sub_len, n_res * GROUP_WIDTH), lambda b, r: (b, 0, r))
    o, lse = pl.pallas_call(
        functools.partial(_band_attn_kernel, sub_len=sub_len, n_res=n_res, q_tile=q_tile),
        grid=(batch, dil // n_res),
        in_specs=[x_spec, _const_spec(bias.shape)],
        out_specs=[o_spec, o_spec],
        out_shape=[jax.ShapeDtypeStruct((batch, sub_len, dil * GROUP_WIDTH), F32)] * 2,
        compiler_params=_params("parallel", "parallel"),
        name=f"band_attn_d{dil}",
    )(qkv3, bias)
    return o.reshape(rows, dil * GROUP_WIDTH), lse.reshape(rows, dil * GROUP_WIDTH)


def _att_proj_kernel(x_ref, g_ref, *refs, dils):
    n = len(dils)
    w_refs, wmem_ref = refs[:n], refs[n]
    out_refs, qmem_ref, stage_ref = refs[n + 1:2 * n + 1], refs[2 * n + 1], refs[2 * n + 2]
    h = _rms(x_ref[...], g_ref[...]).astype(BF16)
    tm = h.shape[0]
    for w_ref, o_ref, dil in zip(w_refs, out_refs, dils):
        y = jnp.dot(h, w_ref[...], preferred_element_type=F32)
        width = y.shape[1]
        if dil == 1:
            o_ref[...] = y.astype(BF16)
        else:
            for c in range(width // LANES):
                stage_ref[c] = y[:, c * LANES:(c + 1) * LANES]
            for r in range(dil):
                for c in range(width // LANES):
                    lanes = pl.ds(r * width + c * LANES, LANES)
                    o_ref[:, lanes] = stage_ref[c, pl.ds(r, tm // dil, stride=dil), :].astype(BF16)
    qmem_ref[...] = jnp.dot(h, wmem_ref[...], preferred_element_type=F32).astype(BF16)


def _att_proj(x2, gain, group_weights, w_mem, dils, tm=512):
    t, d = x2.shape
    width = group_weights[0].shape[1]
    row = lambda i: (i, 0)
    return pl.pallas_call(
        functools.partial(_att_proj_kernel, dils=dils),
        grid=(t // tm,),
        in_specs=[pl.BlockSpec((tm, d), row), _const_spec((1, d))]
        + [_const_spec(w.shape) for w in group_weights] + [_const_spec(w_mem.shape)],
        out_specs=[pl.BlockSpec((tm // dil, dil * width), row) for dil in dils]
        + [pl.BlockSpec((tm, w_mem.shape[1]), row)],
        out_shape=[jax.ShapeDtypeStruct((t // dil, dil * width), BF16) for dil in dils]
        + [jax.ShapeDtypeStruct((t, w_mem.shape[1]), BF16)],
        scratch_shapes=[pltpu.VMEM((width // LANES, tm, LANES), F32)],
        compiler_params=_params("parallel"),
        name="att_proj",
    )(x2, gain.reshape(1, d), *group_weights, w_mem)


def _att_out_kernel(*refs, dils):
    n = len(dils)
    o_refs, l_refs = refs[:n], refs[n:2 * n]
    mem_ref, x_ref, w_ref, g_ref, out_ref, stage_ref = refs[2 * n:]
    tm = x_ref.shape[0]

    def token_major(ref, dil, slot):
        if dil == 1:
            return ref[...]
        tiles = GROUP_WIDTH // LANES
        for r in range(dil):
            for c in range(tiles):
                lanes = pl.ds(r * GROUP_WIDTH + c * LANES, LANES)
                stage_ref[slot * tiles + c, pl.ds(r, tm // dil, stride=dil), :] = ref[:, lanes]
        return jnp.concatenate([stage_ref[slot * tiles + c] for c in range(tiles)], axis=-1)

    lses = [token_major(ref, dil, 2 * i) for i, (ref, dil) in enumerate(zip(l_refs, dils))]
    outs = [token_major(ref, dil, 2 * i + 1) for i, (ref, dil) in enumerate(zip(o_refs, dils))]
    mx = functools.reduce(jnp.maximum, lses)
    es = [jnp.exp(l - mx) for l in lses]
    tot = functools.reduce(lambda a, b: a + b, es)
    parts = [(o * (e / tot)).astype(BF16) for o, e in zip(outs, es)]
    mixed = jnp.concatenate(parts + [mem_ref[...]], axis=-1)
    y = jnp.dot(mixed, w_ref[...], preferred_element_type=F32)
    out_ref[...] = x_ref[...] + _rms(y, g_ref[...])


def _att_out(os_, lses, dils, mem_out, x2, w_out, gain, tm=512):
    t, d = x2.shape
    row = lambda i: (i, 0)
    group_specs = [pl.BlockSpec((tm // dil, dil * GROUP_WIDTH), row) for dil in dils]
    return pl.pallas_call(
        functools.partial(_att_out_kernel, dils=dils),
        grid=(t // tm,),
        in_specs=group_specs + group_specs
        + [pl.BlockSpec((tm, mem_out.shape[1]), row), pl.BlockSpec((tm, d), row),
           _const_spec(w_out.shape), _const_spec((1, d))],
        out_specs=pl.BlockSpec((tm, d), row),
        out_shape=jax.ShapeDtypeStruct((t, d), F32),
        scratch_shapes=[pltpu.VMEM((2 * len(dils) * (GROUP_WIDTH // LANES), tm, LANES), F32)],
        compiler_params=_params("parallel"),
        name="att_out",
    )(*os_, *lses, mem_out, x2, w_out, gain.reshape(1, d))


def _ffn_kernel(x_ref, gpre_ref, wg_ref, wu_ref, wd_ref, gpost_ref, out_ref, *, n_chunks):
    x = x_ref[...]
    h = _rms(x, gpre_ref[...]).astype(BF16)
    d_ff = wg_ref.shape[1]
    step = d_ff // n_chunks
    y = jnp.zeros(x.shape, F32)
    for c in range(n_chunks):
        cols = slice(c * step, (c + 1) * step)
        gate = jnp.dot(h, wg_ref[:, cols], preferred_element_type=F32)
        up = jnp.dot(h, wu_ref[:, cols], preferred_element_type=F32)
        act = (_silu(gate) * up).astype(BF16)
        y = y + jnp.dot(act, wd_ref[cols, :], preferred_element_type=F32)
    out_ref[...] = x + _rms(y, gpost_ref[...])


def _ffn(x2, g_pre, w_gate_up, w_down, g_post, tm=256, n_chunks=1):
    t, d = x2.shape
    d_ff = w_down.shape[0]
    row = lambda i: (i, 0)
    return pl.pallas_call(
        functools.partial(_ffn_kernel, n_chunks=n_chunks),
        grid=(t // tm,),
        in_specs=[pl.BlockSpec((tm, d), row), _const_spec((1, d)),
                  pl.BlockSpec((d, d_ff), lambda i: (0, 0)), pl.BlockSpec((d, d_ff), lambda i: (0, 1)),
                  _const_spec(w_down.shape), _const_spec((1, d))],
        out_specs=pl.BlockSpec((tm, d), row),
        out_shape=jax.ShapeDtypeStruct((t, d), F32),
        compiler_params=_params("parallel"),
        name="ffn",
    )(x2, g_pre.reshape(1, d), w_gate_up, w_gate_up, w_down, g_post.reshape(1, d))


def _dn_proj_kernel(x_ref, xprev_ref, xnext_ref, g_ref, wqkv_ref, wz_ref, wgate_ref, wmem_ref, conv_ref,
                    alog_ref, dtb_ref, q_ref, k_ref, v_ref, z_ref, qmem_ref, gb_ref, gbt_ref, ext_ref,
                    *, halo, n_dir_heads):
    i = pl.program_id(1)
    last = pl.num_programs(1) - 1
    tm = x_ref.shape[0]
    width = wqkv_ref.shape[1]
    tok = width // 3
    gain = g_ref[...]
    h = _rms(x_ref[...], gain).astype(BF16)
    h_prev = jnp.where(i > 0, _rms(xprev_ref[...], gain), 0.0).astype(BF16)
    h_next = jnp.where(i < last, _rms(xnext_ref[...], gain), 0.0).astype(BF16)
    h_ext = jnp.concatenate([h_prev, h, h_next], axis=0)
    chunk = 2 * LANES
    for c0 in range(0, width, chunk):
        y_ext = jnp.dot(h_ext, wqkv_ref[:, c0:c0 + chunk], preferred_element_type=F32)
        for half in range(chunk // LANES):
            c1 = c0 + half * LANES
            cols = slice(c1, c1 + LANES)
            ext = ext_ref.at[(c1 // LANES) % ext_ref.shape[0]]
            ext[...] = y_ext[:, half * LANES:(half + 1) * LANES]
            acc = jnp.zeros((tm, LANES), F32)
            for j in range(DN_CONV):
                first = halo + j - DN_CONV // 2
                acc = acc + ext[first:first + tm, :] * conv_ref[j:j + 1, cols]
            y = _silu(acc)
            which, col = divmod(c1, tok)
            if which == 2:
                v_ref[:, col:col + LANES] = y.astype(BF16)
            else:
                yn = y * lax.rsqrt(jnp.sum(y * y, axis=-1, keepdims=True) + EPS)
                if which == 0:
                    q_ref[:, col:col + LANES] = (yn * (DN_HEAD_DIM ** -0.5)).astype(BF16)
                else:
                    k_ref[:, col:col + LANES] = yn.astype(BF16)
    z_ref[...] = jnp.dot(h, wz_ref[...], preferred_element_type=F32).astype(BF16)
    qmem_ref[...] = jnp.dot(h, wmem_ref[...], preferred_element_type=F32).astype(BF16)
    gate = jnp.dot(h, wgate_ref[...], preferred_element_type=F32)
    shifted = gate + dtb_ref[...]
    softplus = jnp.maximum(shifted, 0.0) + jnp.log(1.0 + jnp.exp(-jnp.abs(shifted)))
    decay = -jnp.exp(alog_ref[...]) * softplus
    beta = 1.0 / (1.0 + jnp.exp(-gate))
    lane = lax.broadcasted_iota(jnp.int32, gate.shape, 1)
    gb = jnp.where(lane < n_dir_heads, decay, beta)
    gb_ref[...] = gb
    gbt_ref[...] = jnp.transpose(gb)[:gbt_ref.shape[0], :]


def _dn_proj(x2, gain, w_in, w_gate, w_mem, conv_w, a_log_row, dt_bias_row, batch, seq, tok, n_dir_heads,
             tm=512, halo=16):
    t, d = x2.shape
    n_gate_rows = 8 * ((2 * n_dir_heads + 7) // 8)
    per_seq = seq // tm
    blocks_per_tile = tm // halo
    n_halo_blocks = t // halo
    mem_width = w_mem.shape[1]

    def cur_map(b, i):
        return (b * per_seq + i, 0)

    def prev_map(b, i):
        return (jnp.maximum((b * per_seq + i) * blocks_per_tile - 1, 0), 0)

    def next_map(b, i):
        return (jnp.minimum((b * per_seq + i + 1) * blocks_per_tile, n_halo_blocks - 1), 0)

    return pl.pallas_call(
        functools.partial(_dn_proj_kernel, halo=halo, n_dir_heads=n_dir_heads),
        grid=(batch, per_seq),
        in_specs=[pl.BlockSpec((tm, d), cur_map), pl.BlockSpec((halo, d), prev_map),
                  pl.BlockSpec((halo, d), next_map), _const_spec((1, d)),
                  pl.BlockSpec((d, 3 * tok), lambda b, i: (0, 0)), pl.BlockSpec((d, tok), lambda b, i: (0, 3)),
                  _const_spec(w_gate.shape), _const_spec(w_mem.shape), _const_spec(conv_w.shape),
                  _const_spec((1, LANES)), _const_spec((1, LANES))],
        out_specs=[pl.BlockSpec((tm, tok), cur_map)] * 4
        + [pl.BlockSpec((tm, mem_width), cur_map), pl.BlockSpec((tm, LANES), cur_map),
           pl.BlockSpec((n_gate_rows, tm), lambda b, i: (0, b * per_seq + i))],
        out_shape=[jax.ShapeDtypeStruct((t, tok), BF16)] * 4
        + [jax.ShapeDtypeStruct((t, mem_width), BF16), jax.ShapeDtypeStruct((t, LANES), F32),
           jax.ShapeDtypeStruct((n_gate_rows, t), F32)],
        scratch_shapes=[pltpu.VMEM((2, tm + 2 * halo, LANES), F32)],
        compiler_params=_params("parallel", "arbitrary"),
        name="dn_proj",
    )(x2, x2, x2, gain.reshape(1, d), w_in, w_in, w_gate, w_mem, conv_w, a_log_row, dt_bias_row)


def _mm(a, b):
    return jnp.dot(a, b, preferred_element_type=F32)


def _mm_nt(a, b):
    return lax.dot_general(a, b, (((1,), (1,)), ((), ())), preferred_element_type=F32)


def _mm_tn(a, b):
    return lax.dot_general(a, b, (((0,), (0,)), ((), ())), preferred_element_type=F32)


def _row_blocks(x, size, parity):
    return jnp.concatenate([x[b * size:(b + 1) * size] for b in range(parity, x.shape[0] // size, 2)], axis=0)


def _interleave_row_blocks(even, odd, size):
    pieces = []
    for b in range(even.shape[0] // size):
        pieces += [even[b * size:(b + 1) * size], odd[b * size:(b + 1) * size]]
    return jnp.concatenate(pieces, axis=0)


def _unit_triangular_inverses(neg_ls, uppers, same_block):
    n = neg_ls[0].shape[0]
    ii = lax.broadcasted_iota(jnp.int32, (n, n), 0)
    jj = lax.broadcasted_iota(jnp.int32, (n, n), 1)
    identity = (ii == jj).astype(F32).astype(BF16)
    zero = jnp.zeros((n, n), BF16)
    plus_identity = lambda m: m + identity
    leaf = same_block[SOLVE_LEAF].astype(F32).astype(BF16)
    xbs = [nl * leaf for nl in neg_ls]
    invs = [plus_identity(xb) for xb in xbs]
    power = 1
    while 2 * power < SOLVE_LEAF:
        xbs = [_mm(xb, xb).astype(BF16) for xb in xbs]
        invs = [_mm(plus_identity(xb), inv).astype(BF16) for xb, inv in zip(xbs, invs)]
        power *= 2
    size = SOLVE_LEAF
    while size < n:
        merge = (same_block[2 * size] & jnp.logical_not(same_block[size])).astype(F32).astype(BF16)
        hot = [0 if upper else 1 for upper in uppers]
        offs = [_row_blocks(nl, size, p) * _row_blocks(merge, size, p) for nl, p in zip(neg_ls, hot)]
        rights = [_mm(off, inv).astype(BF16) for off, inv in zip(offs, invs)]
        half_zero = zero[:n // 2]
        rights = [_interleave_row_blocks(*((r, half_zero) if p == 0 else (half_zero, r)), size)
                  for r, p in zip(rights, hot)]
        moved = [_mm(_row_blocks(inv, size, p), plus_identity(right)).astype(BF16)
                 for inv, right, p in zip(invs, rights, hot)]
        invs = [_interleave_row_blocks(*((m, _row_blocks(inv, size, 1)) if p == 0 else
                                         (_row_blocks(inv, size, 0), m)), size)
                for inv, m, p in zip(invs, moved, hot)]
        size *= 2
    return invs


def _exact_prefix_sums(x, tri, rows):
    hi = x.astype(BF16).astype(F32)
    mid = (x - hi).astype(BF16).astype(F32)
    lo = (x - hi - mid).astype(BF16).astype(F32)
    if rows:
        n = x.shape[0]
        res = _mm(jnp.concatenate([hi, mid, lo], axis=0).astype(BF16), tri)
        return res[:n] + res[n:2 * n] + res[2 * n:]
    quarter = LANES // 4
    packed = hi + pltpu.roll(mid, quarter, 1) + pltpu.roll(lo, 2 * quarter, 1)
    res = _mm(tri, packed.astype(BF16))
    return res + pltpu.roll(res, LANES - quarter, 1) + pltpu.roll(res, LANES - 2 * quarter, 1)


def _dn_core_kernel(qf_ref, kf_ref, vf_ref, gbf_ref, gtf_ref, qb_ref, kb_ref, vb_ref, gbb_ref, gtb_ref,
                    of_ref, ob_ref, state_ref, *, n_heads):
    @pl.when(pl.program_id(1) == 0)
    def _():
        state_ref[...] = jnp.zeros_like(state_ref)

    c = DN_CHUNK
    d = DN_HEAD_DIM
    n_sub = qf_ref.shape[0] // c
    ii = lax.broadcasted_iota(jnp.int32, (c, c), 0)
    jj = lax.broadcasted_iota(jnp.int32, (c, c), 1)
    lower_eq, upper_eq = jj <= ii, jj >= ii
    lower, upper = jj < ii, jj > ii
    same_block = {}
    size = SOLVE_LEAF
    while size <= c:
        shift = int(math.log2(size))
        same_block[size] = (ii >> shift) == (jj >> shift)
        size *= 2

    items = []
    for direction, (q_ref, k_ref, v_ref, gb_ref, gt_ref, o_ref) in enumerate(
            ((qf_ref, kf_ref, vf_ref, gbf_ref, gtf_ref, of_ref),
             (qb_ref, kb_ref, vb_ref, gbb_ref, gtb_ref, ob_ref))):
        reverse = direction == 1
        tri_col = (upper_eq if reverse else lower_eq).astype(BF16)
        tri_row = (lower_eq if reverse else upper_eq).astype(BF16)
        for sub in range(n_sub):
            rows = slice(sub * c, (sub + 1) * c)
            gb = gb_ref[rows, :]
            lane = lax.broadcasted_iota(jnp.int32, gb.shape, 1)
            cum_col = _exact_prefix_sums(jnp.where(lane < LANES // 4, gb, 0.0), tri_col, rows=False)
            cum_row = _exact_prefix_sums(gt_ref[:, rows], tri_row, rows=True)
            for head in range(n_heads):
                g_idx = direction * n_heads + head
                b_idx = 2 * n_heads + g_idx
                cols = slice(head * d, (head + 1) * d)
                items.append(dict(
                    direction=direction, head=head, sub=sub, o_ref=o_ref, rows=rows, cols=cols, reverse=reverse,
                    earlier_eq=upper_eq if reverse else lower_eq,
                    earlier=upper if reverse else lower,
                    q=q_ref[rows, cols], k=k_ref[rows, cols], v=v_ref[rows, cols],
                    g_col=cum_col[:, g_idx:g_idx + 1], beta_col=gb[:, b_idx:b_idx + 1],
                    g_row=cum_row[g_idx:g_idx + 1, :],
                    g_last=cum_col[0:1, g_idx:g_idx + 1] if reverse else cum_col[c - 1:c, g_idx:g_idx + 1]))

    grams = [_mm_nt(jnp.concatenate([it["k"], it["q"]], axis=0), it["k"]) for it in items]
    neg_ls, rhss = [], []
    for it, gram in zip(items, grams):
        g_wide = jnp.broadcast_to(it["g_col"], (c, d))
        beta_wide = jnp.broadcast_to(it["beta_col"], (c, d))
        it["g_last_wide"] = jnp.broadcast_to(it["g_last"], (c, d))
        decay = jnp.exp(jnp.where(it["earlier_eq"], g_wide - it["g_row"], NEG_INF))
        neg_ls.append((jnp.where(it["earlier"], gram[:c] * decay, 0.0) * (-beta_wide)).astype(BF16))
        it["intra"] = jnp.where(it["earlier_eq"], gram[c:] * decay, 0.0).astype(BF16)
        kf = it["k"].astype(F32)
        e_g = jnp.exp(g_wide)
        rhss.append(jnp.concatenate([(it["v"].astype(F32) * beta_wide).astype(BF16),
                                     (kf * (beta_wide * e_g)).astype(BF16)], axis=-1))
        it["qg"] = (it["q"].astype(F32) * e_g).astype(BF16)
        it["k_dec"] = (kf * jnp.exp(it["g_last_wide"] - g_wide)).astype(BF16)
    invs = _unit_triangular_inverses(neg_ls, [it["reverse"] for it in items], same_block)
    for it, inv, rhs in zip(items, invs, rhss):
        sol = _mm(inv, rhs)
        it["u"], it["w"] = sol[:, :d], sol[:, d:].astype(BF16)

    for step in range(n_sub):
        live = [it for it in items if it["sub"] == (n_sub - 1 - step if it["reverse"] else step)]
        states = [state_ref[it["direction"], it["head"]] for it in live]
        wss = [_mm(jnp.concatenate([it["w"], it["qg"]], axis=0), state.astype(BF16))
               for it, state in zip(live, states)]
        v_news = [(it["u"] - ws[:c]).astype(BF16) for it, ws in zip(live, wss)]
        outs = [ws[c:] + _mm(it["intra"], v_new) for it, ws, v_new in zip(live, wss, v_news)]
        updates = [_mm_tn(it["k_dec"], v_new) for it, v_new in zip(live, v_news)]
        for it, out, state, update in zip(live, outs, states, updates):
            it["o_ref"][it["rows"], it["cols"]] = out
            state_ref[it["direction"], it["head"]] = state * jnp.exp(it["g_last_wide"]) + update


def _dn_core(q, k, v, gb, gbt, batch, seq, chunks_per_step=2):
    t, tok = q.shape
    n_heads = tok // DN_HEAD_DIM
    rows = chunks_per_step * DN_CHUNK
    steps = seq // rows
    n_rows = gbt.shape[0]

    fwd = lambda b, i: (b * steps + i, 0)
    bwd = lambda b, i: (b * steps + steps - 1 - i, 0)
    fwd_t = lambda b, i: (0, b * steps + i)
    bwd_t = lambda b, i: (0, b * steps + steps - 1 - i)

    def side(tok_map, t_map):
        return [pl.BlockSpec((rows, tok), tok_map)] * 3 + [
            pl.BlockSpec((rows, LANES), tok_map), pl.BlockSpec((n_rows, rows), t_map)]

    return pl.pallas_call(
        functools.partial(_dn_core_kernel, n_heads=n_heads),
        grid=(batch, steps),
        in_specs=side(fwd, fwd_t) + side(bwd, bwd_t),
        out_specs=[pl.BlockSpec((rows, tok), fwd), pl.BlockSpec((rows, tok), bwd)],
        out_shape=[jax.ShapeDtypeStruct((t, tok), F32)] * 2,
        scratch_shapes=[pltpu.VMEM((2, n_heads, DN_HEAD_DIM, DN_HEAD_DIM), F32)],
        compiler_params=_params("parallel", "arbitrary"),
        name="dn_core",
    )(q, k, v, gb, gbt, q, k, v, gb, gbt)


def _dn_out_kernel(of_ref, ob_ref, z_ref, mem_ref, x_ref, onorm_ref, w_ref, g_ref, out_ref):
    tok = of_ref.shape[1]
    parts = []
    for c0 in range(0, tok, DN_HEAD_DIM):
        cols = slice(c0, c0 + DN_HEAD_DIM)
        o = of_ref[:, cols] + ob_ref[:, cols]
        o = o * lax.rsqrt(jnp.mean(o * o, axis=-1, keepdims=True) + EPS) * onorm_ref[...]
        parts.append((o * _silu(z_ref[:, cols].astype(F32))).astype(BF16))
    mixed = jnp.concatenate(parts + [mem_ref[...]], axis=-1)
    y = jnp.dot(mixed, w_ref[...], preferred_element_type=F32)
    out_ref[...] = x_ref[...] + _rms(y, g_ref[...])


def _dn_out(o_f, o_b, z, mem_out, x2, out_norm, w_out, gain, tm=512):
    t, d = x2.shape
    tok = o_f.shape[1]
    row = lambda i: (i, 0)
    return pl.pallas_call(
        _dn_out_kernel,
        grid=(t // tm,),
        in_specs=[pl.BlockSpec((tm, tok), row)] * 3
        + [pl.BlockSpec((tm, mem_out.shape[1]), row), pl.BlockSpec((tm, d), row),
           _const_spec((1, DN_HEAD_DIM)), _const_spec(w_out.shape), _const_spec((1, d))],
        out_specs=pl.BlockSpec((tm, d), row),
        out_shape=jax.ShapeDtypeStruct((t, d), F32),
        compiler_params=_params("parallel"),
        name="dn_out",
    )(o_f, o_b, z, mem_out, x2, out_norm.reshape(1, DN_HEAD_DIM), w_out, gain.reshape(1, d))


def _memory_branch(q_mem, mem, mem_gain, w_kv, seq):
    k_mem, v_mem = _mem_kv(mem, mem_gain, w_kv.astype(BF16))
    return _mem_attn(q_mem, k_mem, v_mem, seq)


def _attention_layer(x2, mem, batch, seq, rel_bias, w_in, w_out, mem_gain, w_kv, g_pre, g_post):
    n_groups = len(DILATED_GROUPS)
    tok = n_groups * GROUP_WIDTH
    dils = tuple(dil for _, dil in DILATED_GROUPS)
    w_qkv = w_in[:, :3 * tok].reshape(-1, 3, n_groups, GROUP_WIDTH).transpose(0, 2, 1, 3).astype(BF16)
    group_weights = [w_qkv[:, g].reshape(-1, 3 * GROUP_WIDTH) for g in range(n_groups)]
    *qkvs, q_mem = _att_proj(x2, g_pre, group_weights, w_in[:, 3 * tok:].astype(BF16), dils)
    os_, lses = [], []
    for group, dil in enumerate(dils):
        n_res = min(dil, 2)
        o, lse = _band_attn(qkvs[group], _band_bias(rel_bias, group, dil), batch, seq, dil, n_res,
                            q_tile=512 // n_res)
        os_.append(o)
        lses.append(lse)
    mem_out = _memory_branch(q_mem, mem, mem_gain, w_kv, seq)
    return _att_out(os_, lses, dils, mem_out, x2, w_out.astype(BF16), g_post)


def _deltanet_layer(x2, mem, batch, seq, w_in, conv_w, a_log, dt_bias, out_norm, w_out, mem_gain, w_kv,
                    g_pre, g_post):
    d = x2.shape[1]
    n_heads = a_log.shape[1]
    tok = n_heads * DN_HEAD_DIM
    n_gate = 4 * n_heads
    w_gate = w_in[:, 4 * tok:4 * tok + n_gate].reshape(d, 2, 2, n_heads).transpose(0, 2, 1, 3).reshape(d, n_gate)
    w_gate = jnp.pad(w_gate, ((0, 0), (0, LANES - n_gate)))
    w_in_b = w_in.astype(BF16)
    pad_row = lambda p: jnp.pad(p.reshape(1, -1).astype(F32), ((0, 0), (0, LANES - 2 * n_heads)))
    q, k, v, z, q_mem, gb, gbt = _dn_proj(x2, g_pre, w_in_b, w_gate.astype(BF16), w_in_b[:, 4 * tok + n_gate:],
                                          conv_w.astype(F32), pad_row(a_log), pad_row(dt_bias), batch, seq, tok,
                                          2 * n_heads)
    o_f, o_b = _dn_core(q, k, v, gb, gbt, batch, seq)
    mem_out = _memory_branch(q_mem, mem, mem_gain, w_kv, seq)
    return _dn_out(o_f, o_b, z, mem_out, x2, out_norm, w_out.astype(BF16), g_post)


def kernel(x, mem, rel_bias, att_w_in, att_w_out, dn_w_in, dn_conv, dn_a_log, dn_dt_bias, dn_out_norm,
           dn_w_out, mem_norm, mem_w_kv, norm_mix_pre, norm_mix_post, norm_ffn_pre, norm_ffn_post,
           ffn_w_gate_up, ffn_w_down):
    batch, seq, d = x.shape
    depth = norm_mix_pre.shape[0]
    n_mixers = 2
    x2 = x.reshape(batch * seq, d)
    for i in range(depth):
        j = i // n_mixers
        if i % n_mixers == 0:
            x2 = _attention_layer(x2, mem, batch, seq, rel_bias, att_w_in[j], att_w_out[j], mem_norm[i],
                                  mem_w_kv[i], norm_mix_pre[i], norm_mix_post[i])
        else:
            x2 = _deltanet_layer(x2, mem, batch, seq, dn_w_in[j], dn_conv[j], dn_a_log[j], dn_dt_bias[j],
                                 dn_out_norm[j], dn_w_out[j], mem_norm[i], mem_w_kv[i], norm_mix_pre[i],
                                 norm_mix_post[i])
        x2 = _ffn(x2, norm_ffn_pre[i], ffn_w_gate_up[i].astype(BF16), ffn_w_down[i].astype(BF16),
                  norm_ffn_post[i])
    return x2.reshape(batch, seq, d)
```

```python
import functools
import math

import jax
import jax.numpy as jnp
import numpy as np
from jax import lax
from jax.experimental import pallas as pl
from jax.experimental.pallas import tpu as pltpu

F32 = jnp.float32
BF16 = jnp.bfloat16

EPS = 1e-6
NEG_INF = -1e30

LANES = 128
VMEM_LIMIT_BYTES = 56 * 1024 * 1024

DILATED_GROUPS = ((128, 1), (512, 4), (2048, 16))
ATT_HEAD_DIM = 64
HEADS_PER_GROUP = 4
GROUP_WIDTH = HEADS_PER_GROUP * ATT_HEAD_DIM
BAND_HALF = 64
REL_BUCKETS = 32
REL_MAX_DIST = 1024
MEM_HEADS = 4
MEM_HEAD_DIM = 64
DN_HEAD_DIM = 128
DN_CONV = 5
DN_CHUNK = 128
SOLVE_LEAF = 16

ATT_Q_BLOCK = 128
ATT_K_WINDOW = ATT_Q_BLOCK + 2 * BAND_HALF


def _params(*semantics):
    return pltpu.CompilerParams(dimension_semantics=semantics, vmem_limit_bytes=VMEM_LIMIT_BYTES)


def _rms(x, gain):
    return x * lax.rsqrt(jnp.mean(x * x, axis=-1, keepdims=True) + EPS) * gain


def _silu(x):
    return x / (1.0 + jnp.exp(-x))


def _const_spec(shape):
    return pl.BlockSpec(shape, lambda *_: (0,) * len(shape))


def _mem_kv_kernel(mem_ref, g_ref, w_ref, k_ref, v_ref):
    h = _rms(mem_ref[0], g_ref[...]).astype(BF16)
    kv = jnp.dot(h, w_ref[...], preferred_element_type=F32)
    width = k_ref.shape[-1]
    k_ref[0] = kv[:, :width].astype(BF16)
    v_ref[0] = kv[:, width:].astype(BF16)


def _mem_kv(mem, gain, w_kv):
    b, m, d = mem.shape
    width = w_kv.shape[1] // 2
    return pl.pallas_call(
        _mem_kv_kernel,
        grid=(b,),
        in_specs=[pl.BlockSpec((1, m, d), lambda i: (i, 0, 0)), _const_spec((1, d)), _const_spec(w_kv.shape)],
        out_specs=[pl.BlockSpec((1, m, width), lambda i: (i, 0, 0))] * 2,
        out_shape=[jax.ShapeDtypeStruct((b, m, width), BF16)] * 2,
        compiler_params=_params("parallel"),
        name="mem_kv",
    )(mem, gain.reshape(1, d), w_kv)


def _mem_attn_kernel(q_ref, k_ref, v_ref, o_ref):
    q = q_ref[...]
    k = k_ref[0]
    v = v_ref[0]
    lane = lax.broadcasted_iota(jnp.int32, q.shape, 1)
    in_head = [(lane >= h * MEM_HEAD_DIM) & (lane < (h + 1) * MEM_HEAD_DIM) for h in range(MEM_HEADS)]
    logits = [lax.dot_general(jnp.where(mask, q, jnp.zeros_like(q)), k, (((1,), (1,)), ((), ())),
                              preferred_element_type=F32) for mask in in_head]
    probs = []
    for s in logits:
        s = s * (MEM_HEAD_DIM ** -0.5)
        p = jnp.exp(s - jnp.max(s, axis=-1, keepdims=True))
        probs.append((p / jnp.sum(p, axis=-1, keepdims=True)).astype(BF16))
    pvs = [jnp.dot(p, v, preferred_element_type=F32) for p in probs]
    out = pvs[-1]
    for mask, pv in zip(in_head[:-1], pvs[:-1]):
        out = jnp.where(mask, pv, out)
    o_ref[...] = out.astype(o_ref.dtype)


def _mem_attn(q_mem, k, v, seq, tq=512):
    t, width = q_mem.shape
    m = k.shape[1]
    per_seq = seq // tq
    return pl.pallas_call(
        _mem_attn_kernel,
        grid=(t // tq,),
        in_specs=[pl.BlockSpec((tq, width), lambda i: (i, 0)),
                  pl.BlockSpec((1, m, width), lambda i: (i // per_seq, 0, 0)),
                  pl.BlockSpec((1, m, width), lambda i: (i // per_seq, 0, 0))],
        out_specs=pl.BlockSpec((tq, width), lambda i: (i, 0)),
        out_shape=jax.ShapeDtypeStruct((t, width), BF16),
        compiler_params=_params("parallel"),
        name="mem_attn",
    )(q_mem, k, v)


def _t5_bucket(rel):
    half = REL_BUCKETS // 2
    max_exact = half // 2
    n = np.abs(rel)
    large = max_exact + (np.log(np.maximum(n, 1) / max_exact) / math.log(REL_MAX_DIST / max_exact)
                         * (half - max_exact)).astype(np.int64)
    large = np.minimum(large, half - 1)
    return ((rel > 0) * half + np.where(n < max_exact, n, large)).astype(np.int32)


def _band_bias(rel_bias, group, dil):
    period = ATT_Q_BLOCK + ATT_K_WINDOW
    heads = slice(group * HEADS_PER_GROUP, (group + 1) * HEADS_PER_GROUP)
    tables = []
    for variant in range(3):
        rel = np.arange(period) - (ATT_Q_BLOCK - 1) - variant * BAND_HALF
        in_band = np.abs(rel) <= BAND_HALF
        bucket = _t5_bucket(np.clip(rel, -BAND_HALF, BAND_HALF) * dil)
        f = jnp.where(in_band[None, :], jnp.transpose(rel_bias[bucket][:, heads]).astype(F32), NEG_INF)
        flat = jnp.tile(f, (1, ATT_Q_BLOCK))[:, :ATT_Q_BLOCK * (period - 1)]
        skew = flat.reshape(HEADS_PER_GROUP, ATT_Q_BLOCK, period - 1)
        tables.append(skew[:, :, ATT_Q_BLOCK - 1:ATT_Q_BLOCK - 1 + ATT_K_WINDOW])
    return jnp.stack(tables)


def _band_attn_kernel(x_ref, bias_ref, o_ref, lse_ref, *, sub_len, n_res, q_tile):
    lane = lax.broadcasted_iota(jnp.int32, (ATT_Q_BLOCK, LANES), 1)
    low_half = lane < ATT_HEAD_DIM
    n_pairs = GROUP_WIDTH // LANES

    def tile(t, carry):
        blocks = []
        for res in range(n_res):
            for blk in range(q_tile // ATT_Q_BLOCK):
                i0 = pl.multiple_of(t * q_tile + blk * ATT_Q_BLOCK, ATT_Q_BLOCK)
                ws = jnp.clip(i0 - BAND_HALF, 0, sub_len - ATT_K_WINDOW)
                variant = (i0 - ws) // BAND_HALF
                ws = pl.multiple_of(ws, BAND_HALF)
                base = res * 3 * GROUP_WIDTH
                blocks.append(dict(
                    res=res, rows=pl.ds(i0, ATT_Q_BLOCK), variant=variant,
                    q=x_ref[0, pl.ds(i0, ATT_Q_BLOCK), base:base + GROUP_WIDTH],
                    k=x_ref[0, pl.ds(ws, ATT_K_WINDOW), base + GROUP_WIDTH:base + 2 * GROUP_WIDTH],
                    v=x_ref[0, pl.ds(ws, ATT_K_WINDOW), base + 2 * GROUP_WIDTH:base + 3 * GROUP_WIDTH]))
        heads = [(b, pair, a) for b in blocks for pair in range(n_pairs) for a in range(2)]
        logits = []
        for b, pair, a in heads:
            cols = slice(pair * LANES, (pair + 1) * LANES)
            q_pair = b["q"][:, cols]
            keep = low_half if a == 0 else jnp.logical_not(low_half)
            qh = jnp.where(keep, q_pair, jnp.zeros_like(q_pair))
            logits.append(_mm_nt(qh, b["k"][:, cols]))
        probs, dens, lses = [], [], []
        for (b, pair, a), s in zip(heads, logits):
            s = s * (ATT_HEAD_DIM ** -0.5) + bias_ref[b["variant"], 2 * pair + a]
            m = jnp.max(s, axis=-1, keepdims=True)
            p = jnp.exp(s - m)
            den = jnp.sum(p, axis=-1, keepdims=True)
            probs.append(p.astype(BF16))
            dens.append(den)
            lses.append(m + jnp.log(den))
        pvs = [_mm(p, b["v"][:, pair * LANES:(pair + 1) * LANES]) for (b, pair, a), p in zip(heads, probs)]
        for idx in range(0, len(heads), 2):
            b, pair, _ = heads[idx]
            cols = pl.ds(b["res"] * GROUP_WIDTH + pair * LANES, LANES)
            out = jnp.where(low_half, pvs[idx] / dens[idx], pvs[idx + 1] / dens[idx + 1])
            lse = jnp.where(low_half, jnp.broadcast_to(lses[idx], (ATT_Q_BLOCK, LANES)),
                            jnp.broadcast_to(lses[idx + 1], (ATT_Q_BLOCK, LANES)))
            o_ref[0, b["rows"], cols] = out.astype(o_ref.dtype)
            lse_ref[0, b["rows"], cols] = lse
        return carry

    lax.fori_loop(0, sub_len // q_tile, tile, 0)


def _band_attn(qkv, bias, batch, seq, dil, n_res, q_tile=256):
    rows, width = qkv.shape
    sub_len = seq // dil
    qkv3 = qkv.reshape(batch, sub_len, width)
    x_spec = pl.BlockSpec((1, sub_len, n_res * 3 * GROUP_WIDTH), lambda b, r: (b, 0, r))
    o_spec = pl.BlockSpec((1, sub_len, n_res * GROUP_WIDTH), lambda b, r: (b, 0, r))
    o, lse = pl.pallas_call(
        functools.partial(_band_attn_kernel, sub_len=sub_len, n_res=n_res, q_tile=q_tile),
        grid=(batch, dil // n_res),
        in_specs=[x_spec, _const_spec(bias.shape)],
        out_specs=[o_spec, o_spec],
        out_shape=[jax.ShapeDtypeStruct((batch, sub_len, dil * GROUP_WIDTH), BF16),
                   jax.ShapeDtypeStruct((batch, sub_len, dil * GROUP_WIDTH), F32)],
        compiler_params=_params("parallel", "parallel"),
        name=f"band_attn_d{dil}",
    )(qkv3, bias)
    return o.reshape(rows, dil * GROUP_WIDTH), lse.reshape(rows, dil * GROUP_WIDTH)


def _att_proj_kernel(x_ref, g_ref, *refs, dils):
    n = len(dils)
    w_refs, wmem_ref = refs[:n], refs[n]
    out_refs, qmem_ref, stage_ref = refs[n + 1:2 * n + 1], refs[2 * n + 1], refs[2 * n + 2]
    h = _rms(x_ref[...], g_ref[...]).astype(BF16)
    tm = h.shape[0]
    for w_ref, o_ref, dil in zip(w_refs, out_refs, dils):
        y = jnp.dot(h, w_ref[...], preferred_element_type=F32)
        width = y.shape[1]
        if dil == 1:
            o_ref[...] = y.astype(BF16)
        else:
            for c in range(width // LANES):
                stage_ref[c] = y[:, c * LANES:(c + 1) * LANES]
            for r in range(dil):
                for c in range(width // LANES):
                    lanes = pl.ds(r * width + c * LANES, LANES)
                    o_ref[:, lanes] = stage_ref[c, pl.ds(r, tm // dil, stride=dil), :].astype(BF16)
    qmem_ref[...] = jnp.dot(h, wmem_ref[...], preferred_element_type=F32).astype(BF16)


def _att_proj(x2, gain, group_weights, w_mem, dils, tm=512):
    t, d = x2.shape
    width = group_weights[0].shape[1]
    row = lambda i: (i, 0)
    return pl.pallas_call(
        functools.partial(_att_proj_kernel, dils=dils),
        grid=(t // tm,),
        in_specs=[pl.BlockSpec((tm, d), row), _const_spec((1, d))]
        + [_const_spec(w.shape) for w in group_weights] + [_const_spec(w_mem.shape)],
        out_specs=[pl.BlockSpec((tm // dil, dil * width), row) for dil in dils]
        + [pl.BlockSpec((tm, w_mem.shape[1]), row)],
        out_shape=[jax.ShapeDtypeStruct((t // dil, dil * width), BF16) for dil in dils]
        + [jax.ShapeDtypeStruct((t, w_mem.shape[1]), BF16)],
        scratch_shapes=[pltpu.VMEM((width // LANES, tm, LANES), F32)],
        compiler_params=_params("parallel"),
        name="att_proj",
    )(x2, gain.reshape(1, d), *group_weights, w_mem)


def _att_out_kernel(*refs, dils):
    n = len(dils)
    o_refs, l_refs = refs[:n], refs[n:2 * n]
    mem_ref, x_ref, w_ref, g_ref, out_ref, stage_ref = refs[2 * n:]
    tm = x_ref.shape[0]

    def token_major(ref, dil, slot):
        if dil == 1:
            return ref[...].astype(F32)
        tiles = GROUP_WIDTH // LANES
        for r in range(dil):
            for c in range(tiles):
                lanes = pl.ds(r * GROUP_WIDTH + c * LANES, LANES)
                stage_ref[slot * tiles + c, pl.ds(r, tm // dil, stride=dil), :] = ref[:, lanes].astype(F32)
        return jnp.concatenate([stage_ref[slot * tiles + c] for c in range(tiles)], axis=-1)

    lses = [token_major(ref, dil, 2 * i) for i, (ref, dil) in enumerate(zip(l_refs, dils))]
    outs = [token_major(ref, dil, 2 * i + 1) for i, (ref, dil) in enumerate(zip(o_refs, dils))]
    mx = functools.reduce(jnp.maximum, lses)
    es = [jnp.exp(l - mx) for l in lses]
    tot = functools.reduce(lambda a, b: a + b, es)
    parts = [(o * (e / tot)).astype(BF16) for o, e in zip(outs, es)]
    mixed = jnp.concatenate(parts + [mem_ref[...]], axis=-1)
    y = jnp.dot(mixed, w_ref[...], preferred_element_type=F32)
    out_ref[...] = x_ref[...] + _rms(y, g_ref[...])


def _att_out(os_, lses, dils, mem_out, x2, w_out, gain, tm=1024):
    t, d = x2.shape
    row = lambda i: (i, 0)
    group_specs = [pl.BlockSpec((tm // dil, dil * GROUP_WIDTH), row) for dil in dils]
    return pl.pallas_call(
        functools.partial(_att_out_kernel, dils=dils),
        grid=(t // tm,),
        in_specs=group_specs + group_specs
        + [pl.BlockSpec((tm, mem_out.shape[1]), row), pl.BlockSpec((tm, d), row),
           _const_spec(w_out.shape), _const_spec((1, d))],
        out_specs=pl.BlockSpec((tm, d), row),
        out_shape=jax.ShapeDtypeStruct((t, d), F32),
        scratch_shapes=[pltpu.VMEM((2 * len(dils) * (GROUP_WIDTH // LANES), tm, LANES), F32)],
        compiler_params=_params("parallel"),
        name="att_out",
    )(*os_, *lses, mem_out, x2, w_out, gain.reshape(1, d))


def _ffn_kernel(x_ref, gpre_ref, wg_ref, wu_ref, wd_ref, gpost_ref, out_ref, *, n_chunks):
    x = x_ref[...]
    h = _rms(x, gpre_ref[...]).astype(BF16)
    d_ff = wg_ref.shape[1]
    step = d_ff // n_chunks
    y = jnp.zeros(x.shape, F32)
    for c in range(n_chunks):
        cols = slice(c * step, (c + 1) * step)
        gate = jnp.dot(h, wg_ref[:, cols], preferred_element_type=F32)
        up = jnp.dot(h, wu_ref[:, cols], preferred_element_type=F32)
        act = (_silu(gate) * up).astype(BF16)
        y = y + jnp.dot(act, wd_ref[cols, :], preferred_element_type=F32)
    out_ref[...] = x + _rms(y, gpost_ref[...])


def _ffn(x2, g_pre, w_gate_up, w_down, g_post, tm=256, n_chunks=1):
    t, d = x2.shape
    d_ff = w_down.shape[0]
    row = lambda i: (i, 0)
    return pl.pallas_call(
        functools.partial(_ffn_kernel, n_chunks=n_chunks),
        grid=(t // tm,),
        in_specs=[pl.BlockSpec((tm, d), row), _const_spec((1, d)),
                  pl.BlockSpec((d, d_ff), lambda i: (0, 0)), pl.BlockSpec((d, d_ff), lambda i: (0, 1)),
                  _const_spec(w_down.shape), _const_spec((1, d))],
        out_specs=pl.BlockSpec((tm, d), row),
        out_shape=jax.ShapeDtypeStruct((t, d), F32),
        compiler_params=_params("parallel"),
        name="ffn",
    )(x2, g_pre.reshape(1, d), w_gate_up, w_gate_up, w_down, g_post.reshape(1, d))


def _dn_proj_kernel(x_ref, xprev_ref, xnext_ref, g_ref, wqkv_ref, wz_ref, wgate_ref, wmem_ref, conv_ref,
                    alog_ref, dtb_ref, q_ref, k_ref, v_ref, z_ref, qmem_ref, gb_ref, gbt_ref, ext_ref,
                    *, halo, n_dir_heads):
    i = pl.program_id(1)
    last = pl.num_programs(1) - 1
    tm = x_ref.shape[0]
    width = wqkv_ref.shape[1]
    tok = width // 3
    gain = g_ref[...]
    h = _rms(x_ref[...], gain).astype(BF16)
    h_prev = jnp.where(i > 0, _rms(xprev_ref[...], gain), 0.0).astype(BF16)
    h_next = jnp.where(i < last, _rms(xnext_ref[...], gain), 0.0).astype(BF16)
    h_ext = jnp.concatenate([h_prev, h, h_next], axis=0)
    chunk = 2 * LANES
    for c0 in range(0, width, chunk):
        y_ext = jnp.dot(h_ext, wqkv_ref[:, c0:c0 + chunk], preferred_element_type=F32)
        for half in range(chunk // LANES):
            c1 = c0 + half * LANES
            cols = slice(c1, c1 + LANES)
            ext = ext_ref.at[(c1 // LANES) % ext_ref.shape[0]]
            ext[...] = y_ext[:, half * LANES:(half + 1) * LANES]
            acc = jnp.zeros((tm, LANES), F32)
            for j in range(DN_CONV):
                first = halo + j - DN_CONV // 2
                acc = acc + ext[first:first + tm, :] * conv_ref[j:j + 1, cols]
            y = _silu(acc)
            which, col = divmod(c1, tok)
            if which == 2:
                v_ref[:, col:col + LANES] = y.astype(BF16)
            else:
                yn = y * lax.rsqrt(jnp.sum(y * y, axis=-1, keepdims=True) + EPS)
                if which == 0:
                    q_ref[:, col:col + LANES] = (yn * (DN_HEAD_DIM ** -0.5)).astype(BF16)
                else:
                    k_ref[:, col:col + LANES] = yn.astype(BF16)
    z_ref[...] = jnp.dot(h, wz_ref[...], preferred_element_type=F32).astype(BF16)
    qmem_ref[...] = jnp.dot(h, wmem_ref[...], preferred_element_type=F32).astype(BF16)
    gate = jnp.dot(h, wgate_ref[...], preferred_element_type=F32)
    shifted = gate + dtb_ref[...]
    softplus = jnp.maximum(shifted, 0.0) + jnp.log(1.0 + jnp.exp(-jnp.abs(shifted)))
    decay = -jnp.exp(alog_ref[...]) * softplus
    beta = 1.0 / (1.0 + jnp.exp(-gate))
    lane = lax.broadcasted_iota(jnp.int32, gate.shape, 1)
    gb = jnp.where(lane < n_dir_heads, decay, beta)
    gb_ref[...] = gb
    gbt_ref[...] = jnp.transpose(gb)[:gbt_ref.shape[0], :]


def _dn_proj(x2, gain, w_in, w_gate, w_mem, conv_w, a_log_row, dt_bias_row, batch, seq, tok, n_dir_heads,
             tm=512, halo=16):
    t, d = x2.shape
    n_gate_rows = 8 * ((2 * n_dir_heads + 7) // 8)
    per_seq = seq // tm
    blocks_per_tile = tm // halo
    n_halo_blocks = t // halo
    mem_width = w_mem.shape[1]

    def cur_map(b, i):
        return (b * per_seq + i, 0)

    def prev_map(b, i):
        return (jnp.maximum((b * per_seq + i) * blocks_per_tile - 1, 0), 0)

    def next_map(b, i):
        return (jnp.minimum((b * per_seq + i + 1) * blocks_per_tile, n_halo_blocks - 1), 0)

    return pl.pallas_call(
        functools.partial(_dn_proj_kernel, halo=halo, n_dir_heads=n_dir_heads),
        grid=(batch, per_seq),
        in_specs=[pl.BlockSpec((tm, d), cur_map), pl.BlockSpec((halo, d), prev_map),
                  pl.BlockSpec((halo, d), next_map), _const_spec((1, d)),
                  pl.BlockSpec((d, 3 * tok), lambda b, i: (0, 0)), pl.BlockSpec((d, tok), lambda b, i: (0, 3)),
                  _const_spec(w_gate.shape), _const_spec(w_mem.shape), _const_spec(conv_w.shape),
                  _const_spec((1, LANES)), _const_spec((1, LANES))],
        out_specs=[pl.BlockSpec((tm, tok), cur_map)] * 4
        + [pl.BlockSpec((tm, mem_width), cur_map), pl.BlockSpec((tm, LANES), cur_map),
           pl.BlockSpec((n_gate_rows, tm), lambda b, i: (0, b * per_seq + i))],
        out_shape=[jax.ShapeDtypeStruct((t, tok), BF16)] * 4
        + [jax.ShapeDtypeStruct((t, mem_width), BF16), jax.ShapeDtypeStruct((t, LANES), F32),
           jax.ShapeDtypeStruct((n_gate_rows, t), F32)],
        scratch_shapes=[pltpu.VMEM((2, tm + 2 * halo, LANES), F32)],
        compiler_params=_params("parallel", "arbitrary"),
        name="dn_proj",
    )(x2, x2, x2, gain.reshape(1, d), w_in, w_in, w_gate, w_mem, conv_w, a_log_row, dt_bias_row)


def _mm(a, b):
    return jnp.dot(a, b, preferred_element_type=F32)


def _mm_nt(a, b):
    return lax.dot_general(a, b, (((1,), (1,)), ((), ())), preferred_element_type=F32)


def _mm_tn(a, b):
    return lax.dot_general(a, b, (((0,), (0,)), ((), ())), preferred_element_type=F32)


def _row_blocks(x, size, parity):
    return jnp.concatenate([x[b * size:(b + 1) * size] for b in range(parity, x.shape[0] // size, 2)], axis=0)


def _interleave_row_blocks(even, odd, size):
    pieces = []
    for b in range(even.shape[0] // size):
        pieces += [even[b * size:(b + 1) * size], odd[b * size:(b + 1) * size]]
    return jnp.concatenate(pieces, axis=0)


def _unit_triangular_inverses(neg_ls, uppers, same_block):
    n = neg_ls[0].shape[0]
    ii = lax.broadcasted_iota(jnp.int32, (n, n), 0)
    jj = lax.broadcasted_iota(jnp.int32, (n, n), 1)
    identity = (ii == jj).astype(F32).astype(BF16)
    zero = jnp.zeros((n, n), BF16)
    plus_identity = lambda m: m + identity
    leaf = same_block[SOLVE_LEAF].astype(F32).astype(BF16)
    xbs = [nl * leaf for nl in neg_ls]
    invs = [plus_identity(xb) for xb in xbs]
    power = 1
    while 2 * power < SOLVE_LEAF:
        xbs = [_mm(xb, xb).astype(BF16) for xb in xbs]
        invs = [_mm(plus_identity(xb), inv).astype(BF16) for xb, inv in zip(xbs, invs)]
        power *= 2
    size = SOLVE_LEAF
    while size < n:
        merge = (same_block[2 * size] & jnp.logical_not(same_block[size])).astype(F32).astype(BF16)
        hot = [0 if upper else 1 for upper in uppers]
        offs = [_row_blocks(nl, size, p) * _row_blocks(merge, size, p) for nl, p in zip(neg_ls, hot)]
        rights = [_mm(off, inv).astype(BF16) for off, inv in zip(offs, invs)]
        half_zero = zero[:n // 2]
        rights = [_interleave_row_blocks(*((r, half_zero) if p == 0 else (half_zero, r)), size)
                  for r, p in zip(rights, hot)]
        moved = [_mm(_row_blocks(inv, size, p), plus_identity(right)).astype(BF16)
                 for inv, right, p in zip(invs, rights, hot)]
        invs = [_interleave_row_blocks(*((m, _row_blocks(inv, size, 1)) if p == 0 else
                                         (_row_blocks(inv, size, 0), m)), size)
                for inv, m, p in zip(invs, moved, hot)]
        size *= 2
    return invs


def _exact_prefix_sums(x, tri, rows):
    hi = x.astype(BF16).astype(F32)
    mid = (x - hi).astype(BF16).astype(F32)
    lo = (x - hi - mid).astype(BF16).astype(F32)
    if rows:
        n = x.shape[0]
        res = _mm(jnp.concatenate([hi, mid, lo], axis=0).astype(BF16), tri)
        return res[:n] + res[n:2 * n] + res[2 * n:]
    quarter = LANES // 4
    packed = hi + pltpu.roll(mid, quarter, 1) + pltpu.roll(lo, 2 * quarter, 1)
    res = _mm(tri, packed.astype(BF16))
    return res + pltpu.roll(res, LANES - quarter, 1) + pltpu.roll(res, LANES - 2 * quarter, 1)


def _dn_core_kernel(qf_ref, kf_ref, vf_ref, gbf_ref, gtf_ref, qb_ref, kb_ref, vb_ref, gbb_ref, gtb_ref,
                    of_ref, ob_ref, state_ref, *, n_heads):
    @pl.when(pl.program_id(1) == 0)
    def _():
        state_ref[...] = jnp.zeros_like(state_ref)

    c = DN_CHUNK
    d = DN_HEAD_DIM
    n_sub = qf_ref.shape[0] // c
    ii = lax.broadcasted_iota(jnp.int32, (c, c), 0)
    jj = lax.broadcasted_iota(jnp.int32, (c, c), 1)
    lower_eq, upper_eq = jj <= ii, jj >= ii
    lower, upper = jj < ii, jj > ii
    same_block = {}
    size = SOLVE_LEAF
    while size <= c:
        shift = int(math.log2(size))
        same_block[size] = (ii >> shift) == (jj >> shift)
        size *= 2

    items = []
    for direction, (q_ref, k_ref, v_ref, gb_ref, gt_ref, o_ref) in enumerate(
            ((qf_ref, kf_ref, vf_ref, gbf_ref, gtf_ref, of_ref),
             (qb_ref, kb_ref, vb_ref, gbb_ref, gtb_ref, ob_ref))):
        reverse = direction == 1
        tri_col = (upper_eq if reverse else lower_eq).astype(BF16)
        tri_row = (lower_eq if reverse else upper_eq).astype(BF16)
        for sub in range(n_sub):
            rows = slice(sub * c, (sub + 1) * c)
            gb = gb_ref[rows, :]
            lane = lax.broadcasted_iota(jnp.int32, gb.shape, 1)
            cum_col = _exact_prefix_sums(jnp.where(lane < LANES // 4, gb, 0.0), tri_col, rows=False)
            cum_row = _exact_prefix_sums(gt_ref[:, rows], tri_row, rows=True)
            for head in range(n_heads):
                g_idx = direction * n_heads + head
                b_idx = 2 * n_heads + g_idx
                cols = slice(head * d, (head + 1) * d)
                items.append(dict(
                    direction=direction, head=head, sub=sub, o_ref=o_ref, rows=rows, cols=cols, reverse=reverse,
                    earlier_eq=upper_eq if reverse else lower_eq,
                    earlier=upper if reverse else lower,
                    q=q_ref[rows, cols], k=k_ref[rows, cols], v=v_ref[rows, cols],
                    g_col=cum_col[:, g_idx:g_idx + 1], beta_col=gb[:, b_idx:b_idx + 1],
                    g_row=cum_row[g_idx:g_idx + 1, :],
                    g_last=cum_col[0:1, g_idx:g_idx + 1] if reverse else cum_col[c - 1:c, g_idx:g_idx + 1]))

    grams = [_mm_nt(jnp.concatenate([it["k"], it["q"]], axis=0), it["k"]) for it in items]
    neg_ls, rhss = [], []
    for it, gram in zip(items, grams):
        g_wide = jnp.broadcast_to(it["g_col"], (c, d))
        beta_wide = jnp.broadcast_to(it["beta_col"], (c, d))
        it["g_last_wide"] = jnp.broadcast_to(it["g_last"], (c, d))
        decay = jnp.exp(jnp.where(it["earlier_eq"], g_wide - it["g_row"], NEG_INF))
        neg_ls.append((jnp.where(it["earlier"], gram[:c] * decay, 0.0) * (-beta_wide)).astype(BF16))
        it["intra"] = jnp.where(it["earlier_eq"], gram[c:] * decay, 0.0).astype(BF16)
        kf = it["k"].astype(F32)
        e_g = jnp.exp(g_wide)
        rhss.append(jnp.concatenate([(it["v"].astype(F32) * beta_wide).astype(BF16),
                                     (kf * (beta_wide * e_g)).astype(BF16)], axis=-1))
        it["qg"] = (it["q"].astype(F32) * e_g).astype(BF16)
        it["k_dec"] = (kf * jnp.exp(it["g_last_wide"] - g_wide)).astype(BF16)
    invs = _unit_triangular_inverses(neg_ls, [it["reverse"] for it in items], same_block)
    for it, inv, rhs in zip(items, invs, rhss):
        sol = _mm(inv, rhs)
        it["u"], it["w"] = sol[:, :d], sol[:, d:].astype(BF16)

    for step in range(n_sub):
        live = [it for it in items if it["sub"] == (n_sub - 1 - step if it["reverse"] else step)]
        states = [state_ref[it["direction"], it["head"]] for it in live]
        wss = [_mm(jnp.concatenate([it["w"], it["qg"]], axis=0), state.astype(BF16))
               for it, state in zip(live, states)]
        v_news = [(it["u"] - ws[:c]).astype(BF16) for it, ws in zip(live, wss)]
        outs = [ws[c:] + _mm(it["intra"], v_new) for it, ws, v_new in zip(live, wss, v_news)]
        updates = [_mm_tn(it["k_dec"], v_new) for it, v_new in zip(live, v_news)]
        for it, out, state, update in zip(live, outs, states, updates):
            it["o_ref"][it["rows"], it["cols"]] = out.astype(BF16)
            state_ref[it["direction"], it["head"]] = state * jnp.exp(it["g_last_wide"]) + update


def _dn_core(q, k, v, gb, gbt, batch, seq, chunks_per_step=2):
    t, tok = q.shape
    n_heads = tok // DN_HEAD_DIM
    rows = chunks_per_step * DN_CHUNK
    steps = seq // rows
    n_rows = gbt.shape[0]

    fwd = lambda b, i: (b * steps + i, 0)
    bwd = lambda b, i: (b * steps + steps - 1 - i, 0)
    fwd_t = lambda b, i: (0, b * steps + i)
    bwd_t = lambda b, i: (0, b * steps + steps - 1 - i)

    def side(tok_map, t_map):
        return [pl.BlockSpec((rows, tok), tok_map)] * 3 + [
            pl.BlockSpec((rows, LANES), tok_map), pl.BlockSpec((n_rows, rows), t_map)]

    return pl.pallas_call(
        functools.partial(_dn_core_kernel, n_heads=n_heads),
        grid=(batch, steps),
        in_specs=side(fwd, fwd_t) + side(bwd, bwd_t),
        out_specs=[pl.BlockSpec((rows, tok), fwd), pl.BlockSpec((rows, tok), bwd)],
        out_shape=[jax.ShapeDtypeStruct((t, tok), BF16)] * 2,
        scratch_shapes=[pltpu.VMEM((2, n_heads, DN_HEAD_DIM, DN_HEAD_DIM), F32)],
        compiler_params=_params("parallel", "arbitrary"),
        name="dn_core",
    )(q, k, v, gb, gbt, q, k, v, gb, gbt)


def _dn_out_kernel(of_ref, ob_ref, z_ref, mem_ref, x_ref, onorm_ref, w_ref, g_ref, out_ref):
    tok = of_ref.shape[1]
    parts = []
    for c0 in range(0, tok, DN_HEAD_DIM):
        cols = slice(c0, c0 + DN_HEAD_DIM)
        o = of_ref[:, cols].astype(F32) + ob_ref[:, cols].astype(F32)
        o = o * lax.rsqrt(jnp.mean(o * o, axis=-1, keepdims=True) + EPS) * onorm_ref[...]
        parts.append((o * _silu(z_ref[:, cols].astype(F32))).astype(BF16))
    mixed = jnp.concatenate(parts + [mem_ref[...]], axis=-1)
    y = jnp.dot(mixed, w_ref[...], preferred_element_type=F32)
    out_ref[...] = x_ref[...] + _rms(y, g_ref[...])


def _dn_out(o_f, o_b, z, mem_out, x2, out_norm, w_out, gain, tm=1024):
    t, d = x2.shape
    tok = o_f.shape[1]
    row = lambda i: (i, 0)
    return pl.pallas_call(
        _dn_out_kernel,
        grid=(t // tm,),
        in_specs=[pl.BlockSpec((tm, tok), row)] * 3
        + [pl.BlockSpec((tm, mem_out.shape[1]), row), pl.BlockSpec((tm, d), row),
           _const_spec((1, DN_HEAD_DIM)), _const_spec(w_out.shape), _const_spec((1, d))],
        out_specs=pl.BlockSpec((tm, d), row),
        out_shape=jax.ShapeDtypeStruct((t, d), F32),
        compiler_params=_params("parallel"),
        name="dn_out",
    )(o_f, o_b, z, mem_out, x2, out_norm.reshape(1, DN_HEAD_DIM), w_out, gain.reshape(1, d))


def _memory_branch(q_mem, mem, mem_gain, w_kv, seq):
    k_mem, v_mem = _mem_kv(mem, mem_gain, w_kv.astype(BF16))
    return _mem_attn(q_mem, k_mem, v_mem, seq)


def _attention_layer(x2, mem, batch, seq, rel_bias, w_in, w_out, mem_gain, w_kv, g_pre, g_post):
    n_groups = len(DILATED_GROUPS)
    tok = n_groups * GROUP_WIDTH
    dils = tuple(dil for _, dil in DILATED_GROUPS)
    w_qkv = w_in[:, :3 * tok].reshape(-1, 3, n_groups, GROUP_WIDTH).transpose(0, 2, 1, 3).astype(BF16)
    group_weights = [w_qkv[:, g].reshape(-1, 3 * GROUP_WIDTH) for g in range(n_groups)]
    *qkvs, q_mem = _att_proj(x2, g_pre, group_weights, w_in[:, 3 * tok:].astype(BF16), dils)
    os_, lses = [], []
    for group, dil in enumerate(dils):
        n_res = min(dil, 2)
        o, lse = _band_attn(qkvs[group], _band_bias(rel_bias, group, dil), batch, seq, dil, n_res,
                            q_tile=512 // n_res)
        os_.append(o)
        lses.append(lse)
    mem_out = _memory_branch(q_mem, mem, mem_gain, w_kv, seq)
    return _att_out(os_, lses, dils, mem_out, x2, w_out.astype(BF16), g_post)


def _deltanet_layer(x2, mem, batch, seq, w_in, conv_w, a_log, dt_bias, out_norm, w_out, mem_gain, w_kv,
                    g_pre, g_post):
    d = x2.shape[1]
    n_heads = a_log.shape[1]
    tok = n_heads * DN_HEAD_DIM
    n_gate = 4 * n_heads
    w_gate = w_in[:, 4 * tok:4 * tok + n_gate].reshape(d, 2, 2, n_heads).transpose(0, 2, 1, 3).reshape(d, n_gate)
    w_gate = jnp.pad(w_gate, ((0, 0), (0, LANES - n_gate)))
    w_in_b = w_in.astype(BF16)
    pad_row = lambda p: jnp.pad(p.reshape(1, -1).astype(F32), ((0, 0), (0, LANES - 2 * n_heads)))
    q, k, v, z, q_mem, gb, gbt = _dn_proj(x2, g_pre, w_in_b, w_gate.astype(BF16), w_in_b[:, 4 * tok + n_gate:],
                                          conv_w.astype(F32), pad_row(a_log), pad_row(dt_bias), batch, seq, tok,
                                          2 * n_heads)
    o_f, o_b = _dn_core(q, k, v, gb, gbt, batch, seq)
    mem_out = _memory_branch(q_mem, mem, mem_gain, w_kv, seq)
    return _dn_out(o_f, o_b, z, mem_out, x2, out_norm, w_out.astype(BF16), g_post)


def kernel(x, mem, rel_bias, att_w_in, att_w_out, dn_w_in, dn_conv, dn_a_log, dn_dt_bias, dn_out_norm,
           dn_w_out, mem_norm, mem_w_kv, norm_mix_pre, norm_mix_post, norm_ffn_pre, norm_ffn_post,
           ffn_w_gate_up, ffn_w_down):
    batch, seq, d = x.shape
    depth = norm_mix_pre.shape[0]
    n_mixers = 2
    x2 = x.reshape(batch * seq, d)
    for i in range(depth):
        j = i // n_mixers
        if i % n_mixers == 0:
            x2 = _attention_layer(x2, mem, batch, seq, rel_bias, att_w_in[j], att_w_out[j], mem_norm[i],
                                  mem_w_kv[i], norm_mix_pre[i], norm_mix_post[i])
        else:
            x2 = _deltanet_layer(x2, mem, batch, seq, dn_w_in[j], dn_conv[j], dn_a_log[j], dn_dt_bias[j],
                                 dn_out_norm[j], dn_w_out[j], mem_norm[i], mem_w_kv[i], norm_mix_pre[i],
                                 norm_mix_post[i])
        x2 = _ffn(x2, norm_ffn_pre[i], ffn_w_gate_up[i].astype(BF16), ffn_w_down[i].astype(BF16),
                  norm_ffn_post[i])
    return x2.reshape(batch, seq, d)
```

```python
import functools
import math

import jax
import jax.numpy as jnp
import numpy as np
from jax import lax
from jax.experimental import pallas as pl
from jax.experimental.pallas import tpu as pltpu

F32 = jnp.float32
BF16 = jnp.bfloat16

EPS = 1e-6
NEG_INF = -1e30

LANES = 128
VMEM_LIMIT_BYTES = 56 * 1024 * 1024

DILATED_GROUPS = ((128, 1), (512, 4), (2048, 16))
ATT_HEAD_DIM = 64
HEADS_PER_GROUP = 4
GROUP_WIDTH = HEADS_PER_GROUP * ATT_HEAD_DIM
BAND_HALF = 64
REL_BUCKETS = 32
REL_MAX_DIST = 1024
MEM_HEADS = 4
MEM_HEAD_DIM = 64
DN_HEAD_DIM = 128
DN_CONV = 5
DN_CHUNK = 128
SOLVE_LEAF = 16
CONV_ROWS = 64

ATT_Q_BLOCK = 128
ATT_K_WINDOW = ATT_Q_BLOCK + 2 * BAND_HALF


def _params(*semantics):
    return pltpu.CompilerParams(dimension_semantics=semantics, vmem_limit_bytes=VMEM_LIMIT_BYTES)


def _rms(x, gain):
    return x * lax.rsqrt(jnp.mean(x * x, axis=-1, keepdims=True) + EPS) * gain


def _silu(x):
    return x / (1.0 + jnp.exp2(x * (-1.0 / math.log(2.0))))


def _const_spec(shape):
    return pl.BlockSpec(shape, lambda *_: (0,) * len(shape))


def _mem_kv_kernel(mem_ref, g_ref, w_ref, k_ref, v_ref):
    h = _rms(mem_ref[0], g_ref[...]).astype(BF16)
    kv = jnp.dot(h, w_ref[...], preferred_element_type=F32)
    width = k_ref.shape[-1]
    k_ref[0] = kv[:, :width].astype(BF16)
    v_ref[0] = kv[:, width:].astype(BF16)


def _mem_kv(mem, gain, w_kv):
    b, m, d = mem.shape
    width = w_kv.shape[1] // 2
    return pl.pallas_call(
        _mem_kv_kernel,
        grid=(b,),
        in_specs=[pl.BlockSpec((1, m, d), lambda i: (i, 0, 0)), _const_spec((1, d)), _const_spec(w_kv.shape)],
        out_specs=[pl.BlockSpec((1, m, width), lambda i: (i, 0, 0))] * 2,
        out_shape=[jax.ShapeDtypeStruct((b, m, width), BF16)] * 2,
        compiler_params=_params("parallel"),
        name="mem_kv",
    )(mem, gain.reshape(1, d), w_kv)


def _memory_attention(q, k, v):
    lane = lax.broadcasted_iota(jnp.int32, q.shape, 1)
    in_head = [(lane >= h * MEM_HEAD_DIM) & (lane < (h + 1) * MEM_HEAD_DIM) for h in range(MEM_HEADS)]
    logits = [lax.dot_general(jnp.where(mask, q, jnp.zeros_like(q)), k, (((1,), (1,)), ((), ())),
                              preferred_element_type=F32) for mask in in_head]
    probs = []
    for s in logits:
        s = s * (MEM_HEAD_DIM ** -0.5)
        p = jnp.exp(s - jnp.max(s, axis=-1, keepdims=True))
        probs.append((p / jnp.sum(p, axis=-1, keepdims=True)).astype(BF16))
    pvs = [jnp.dot(p, v, preferred_element_type=F32) for p in probs]
    out = pvs[-1]
    for mask, pv in zip(in_head[:-1], pvs[:-1]):
        out = jnp.where(mask, pv, out)
    return out.astype(BF16)


def _memory_specs(q_mem, k_mem, seq, tm):
    per_seq = seq // tm
    kv_spec = pl.BlockSpec((1,) + k_mem.shape[1:], lambda i: (i // per_seq, 0, 0))
    return [pl.BlockSpec((tm, q_mem.shape[1]), lambda i: (i, 0)), kv_spec, kv_spec]


def _t5_bucket(rel):
    half = REL_BUCKETS // 2
    max_exact = half // 2
    n = np.abs(rel)
    large = max_exact + (np.log(np.maximum(n, 1) / max_exact) / math.log(REL_MAX_DIST / max_exact)
                         * (half - max_exact)).astype(np.int64)
    large = np.minimum(large, half - 1)
    return ((rel > 0) * half + np.where(n < max_exact, n, large)).astype(np.int32)


def _band_bias(rel_bias, group, dil):
    period = ATT_Q_BLOCK + ATT_K_WINDOW
    heads = slice(group * HEADS_PER_GROUP, (group + 1) * HEADS_PER_GROUP)
    tables = []
    for variant in range(3):
        rel = np.arange(period) - (ATT_Q_BLOCK - 1) - variant * BAND_HALF
        in_band = np.abs(rel) <= BAND_HALF
        bucket = _t5_bucket(np.clip(rel, -BAND_HALF, BAND_HALF) * dil)
        f = jnp.where(in_band[None, :], jnp.transpose(rel_bias[bucket][:, heads]).astype(F32), NEG_INF)
        flat = jnp.tile(f, (1, ATT_Q_BLOCK))[:, :ATT_Q_BLOCK * (period - 1)]
        skew = flat.reshape(HEADS_PER_GROUP, ATT_Q_BLOCK, period - 1)
        tables.append(skew[:, :, ATT_Q_BLOCK - 1:ATT_Q_BLOCK - 1 + ATT_K_WINDOW])
    return jnp.stack(tables)


def _band_attn_kernel(x_ref, bias_ref, o_ref, lse_ref, *, sub_len, n_res, q_tile):
    lane = lax.broadcasted_iota(jnp.int32, (ATT_Q_BLOCK, LANES), 1)
    low_half = lane < ATT_HEAD_DIM
    n_pairs = GROUP_WIDTH // LANES

    def tile(t, carry):
        blocks = []
        for res in range(n_res):
            for blk in range(q_tile // ATT_Q_BLOCK):
                i0 = pl.multiple_of(t * q_tile + blk * ATT_Q_BLOCK, ATT_Q_BLOCK)
                ws = jnp.clip(i0 - BAND_HALF, 0, sub_len - ATT_K_WINDOW)
                variant = (i0 - ws) // BAND_HALF
                ws = pl.multiple_of(ws, BAND_HALF)
                base = res * 3 * GROUP_WIDTH
                blocks.append(dict(
                    res=res, rows=pl.ds(i0, ATT_Q_BLOCK), variant=variant,
                    q=x_ref[0, pl.ds(i0, ATT_Q_BLOCK), base:base + GROUP_WIDTH],
                    k=x_ref[0, pl.ds(ws, ATT_K_WINDOW), base + GROUP_WIDTH:base + 2 * GROUP_WIDTH],
                    v=x_ref[0, pl.ds(ws, ATT_K_WINDOW), base + 2 * GROUP_WIDTH:base + 3 * GROUP_WIDTH]))
        heads = [(b, pair, a) for b in blocks for pair in range(n_pairs) for a in range(2)]
        logits = []
        for b, pair, a in heads:
            cols = slice(pair * LANES, (pair + 1) * LANES)
            q_pair = b["q"][:, cols]
            keep = low_half if a == 0 else jnp.logical_not(low_half)
            qh = jnp.where(keep, q_pair, jnp.zeros_like(q_pair))
            logits.append(_mm_nt(qh, b["k"][:, cols]))
        probs, dens, lses = [], [], []
        for (b, pair, a), s in zip(heads, logits):
            s = s * (ATT_HEAD_DIM ** -0.5) + bias_ref[b["variant"], 2 * pair + a]
            m = jnp.max(s, axis=-1, keepdims=True)
            p = jnp.exp(s - m)
            den = jnp.sum(p, axis=-1, keepdims=True)
            probs.append(p.astype(BF16))
            dens.append(den)
            lses.append(m + jnp.log(den))
        pvs = [_mm(p, b["v"][:, pair * LANES:(pair + 1) * LANES]) for (b, pair, a), p in zip(heads, probs)]
        for idx in range(0, len(heads), 2):
            b, pair, _ = heads[idx]
            cols = pl.ds(b["res"] * GROUP_WIDTH + pair * LANES, LANES)
            out = jnp.where(low_half, pvs[idx] / dens[idx], pvs[idx + 1] / dens[idx + 1])
            lse = jnp.where(low_half, jnp.broadcast_to(lses[idx], (ATT_Q_BLOCK, LANES)),
                            jnp.broadcast_to(lses[idx + 1], (ATT_Q_BLOCK, LANES)))
            o_ref[0, b["rows"], cols] = out.astype(o_ref.dtype)
            lse_ref[0, b["rows"], cols] = lse
        return carry

    lax.fori_loop(0, sub_len // q_tile, tile, 0)


def _band_attn(qkv, bias, batch, seq, dil, n_res, q_tile=256):
    rows, width = qkv.shape
    sub_len = seq // dil
    qkv3 = qkv.reshape(batch, sub_len, width)
    x_spec = pl.BlockSpec((1, sub_len, n_res * 3 * GROUP_WIDTH), lambda b, r: (b, 0, r))
    o_spec = pl.BlockSpec((1, sub_len, n_res * GROUP_WIDTH), lambda b, r: (b, 0, r))
    o, lse = pl.pallas_call(
        functools.partial(_band_attn_kernel, sub_len=sub_len, n_res=n_res, q_tile=q_tile),
        grid=(batch, dil // n_res),
        in_specs=[x_spec, _const_spec(bias.shape)],
        out_specs=[o_spec, o_spec],
        out_shape=[jax.ShapeDtypeStruct((batch, sub_len, dil * GROUP_WIDTH), BF16),
                   jax.ShapeDtypeStruct((batch, sub_len, dil * GROUP_WIDTH), F32)],
        compiler_params=_params("parallel", "parallel"),
        name=f"band_attn_d{dil}",
    )(qkv3, bias)
    return o.reshape(rows, dil * GROUP_WIDTH), lse.reshape(rows, dil * GROUP_WIDTH)


def _att_proj_kernel(x_ref, g_ref, *refs, dils):
    n = len(dils)
    w_refs, wmem_ref = refs[:n], refs[n]
    out_refs, qmem_ref, stage_ref = refs[n + 1:2 * n + 1], refs[2 * n + 1], refs[2 * n + 2]
    h = _rms(x_ref[...], g_ref[...]).astype(BF16)
    tm = h.shape[0]
    staged = 0
    for w_ref, o_ref, dil in zip(w_refs, out_refs, dils):
        y = jnp.dot(h, w_ref[...], preferred_element_type=F32)
        width = y.shape[1]
        tiles = width // LANES
        if dil == 1:
            o_ref[...] = y.astype(BF16)
        else:
            for c in range(tiles):
                stage_ref[staged + c] = y[:, c * LANES:(c + 1) * LANES]
            for r in range(dil):
                for c in range(tiles):
                    lanes = pl.ds(r * width + c * LANES, LANES)
                    o_ref[:, lanes] = stage_ref[staged + c, pl.ds(r, tm // dil, stride=dil), :].astype(BF16)
            staged += tiles
    qmem_ref[...] = jnp.dot(h, wmem_ref[...], preferred_element_type=F32).astype(BF16)


def _att_proj(x2, gain, group_weights, w_mem, dils, tm=512):
    t, d = x2.shape
    width = group_weights[0].shape[1]
    row = lambda i: (i, 0)
    return pl.pallas_call(
        functools.partial(_att_proj_kernel, dils=dils),
        grid=(t // tm,),
        in_specs=[pl.BlockSpec((tm, d), row), _const_spec((1, d))]
        + [_const_spec(w.shape) for w in group_weights] + [_const_spec(w_mem.shape)],
        out_specs=[pl.BlockSpec((tm // dil, dil * width), row) for dil in dils]
        + [pl.BlockSpec((tm, w_mem.shape[1]), row)],
        out_shape=[jax.ShapeDtypeStruct((t // dil, dil * width), BF16) for dil in dils]
        + [jax.ShapeDtypeStruct((t, w_mem.shape[1]), BF16)],
        scratch_shapes=[pltpu.VMEM((sum(dil > 1 for dil in dils) * (width // LANES), tm, LANES), F32)],
        compiler_params=_params("parallel"),
        name="att_proj",
    )(x2, gain.reshape(1, d), *group_weights, w_mem)


def _att_out_kernel(*refs, dils):
    n = len(dils)
    o_refs, l_refs = refs[:n], refs[n:2 * n]
    qmem_ref, kmem_ref, vmem_ref, x_ref, w_ref, g_ref, out_ref, stage_ref = refs[2 * n:]
    tm = x_ref.shape[0]

    def token_major(ref, dil, slot):
        if dil == 1:
            return ref[...].astype(F32)
        tiles = GROUP_WIDTH // LANES
        for r in range(dil):
            for c in range(tiles):
                lanes = pl.ds(r * GROUP_WIDTH + c * LANES, LANES)
                stage_ref[slot * tiles + c, pl.ds(r, tm // dil, stride=dil), :] = ref[:, lanes].astype(F32)
        return jnp.concatenate([stage_ref[slot * tiles + c] for c in range(tiles)], axis=-1)

    lses = [token_major(ref, dil, 2 * i) for i, (ref, dil) in enumerate(zip(l_refs, dils))]
    outs = [token_major(ref, dil, 2 * i + 1) for i, (ref, dil) in enumerate(zip(o_refs, dils))]
    mx = functools.reduce(jnp.maximum, lses)
    es = [jnp.exp(l - mx) for l in lses]
    tot = functools.reduce(lambda a, b: a + b, es)
    parts = [(o * (e / tot)).astype(BF16) for o, e in zip(outs, es)]
    mem_out = _memory_attention(qmem_ref[...], kmem_ref[0], vmem_ref[0])
    mixed = jnp.concatenate(parts + [mem_out], axis=-1)
    y = jnp.dot(mixed, w_ref[...], preferred_element_type=F32)
    out_ref[...] = x_ref[...] + _rms(y, g_ref[...])


def _att_out(os_, lses, dils, q_mem, k_mem, v_mem, x2, w_out, gain, seq, tm=1024):
    t, d = x2.shape
    row = lambda i: (i, 0)
    group_specs = [pl.BlockSpec((tm // dil, dil * GROUP_WIDTH), row) for dil in dils]
    return pl.pallas_call(
        functools.partial(_att_out_kernel, dils=dils),
        grid=(t // tm,),
        in_specs=group_specs + group_specs + _memory_specs(q_mem, k_mem, seq, tm)
        + [pl.BlockSpec((tm, d), row), _const_spec(w_out.shape), _const_spec((1, d))],
        out_specs=pl.BlockSpec((tm, d), row),
        out_shape=jax.ShapeDtypeStruct((t, d), F32),
        scratch_shapes=[pltpu.VMEM((2 * len(dils) * (GROUP_WIDTH // LANES), tm, LANES), F32)],
        compiler_params=_params("parallel"),
        name="att_out",
    )(*os_, *lses, q_mem, k_mem, v_mem, x2, w_out, gain.reshape(1, d))


def _ffn_kernel(x_ref, gpre_ref, wg_ref, wu_ref, wd_ref, gpost_ref, out_ref, *, n_chunks):
    x = x_ref[...]
    h = _rms(x, gpre_ref[...]).astype(BF16)
    d_ff = wg_ref.shape[1]
    step = d_ff // n_chunks
    y = jnp.zeros(x.shape, F32)
    for c in range(n_chunks):
        cols = slice(c * step, (c + 1) * step)
        gate = jnp.dot(h, wg_ref[:, cols], preferred_element_type=F32)
        up = jnp.dot(h, wu_ref[:, cols], preferred_element_type=F32)
        act = (_silu(gate) * up).astype(BF16)
        y = y + jnp.dot(act, wd_ref[cols, :], preferred_element_type=F32)
    out_ref[...] = x + _rms(y, gpost_ref[...])


def _ffn(x2, g_pre, w_gate_up, w_down, g_post, tm=256, n_chunks=1):
    t, d = x2.shape
    d_ff = w_down.shape[0]
    row = lambda i: (i, 0)
    return pl.pallas_call(
        functools.partial(_ffn_kernel, n_chunks=n_chunks),
        grid=(t // tm,),
        in_specs=[pl.BlockSpec((tm, d), row), _const_spec((1, d)),
                  pl.BlockSpec((d, d_ff), lambda i: (0, 0)), pl.BlockSpec((d, d_ff), lambda i: (0, 1)),
                  _const_spec(w_down.shape), _const_spec((1, d))],
        out_specs=pl.BlockSpec((tm, d), row),
        out_shape=jax.ShapeDtypeStruct((t, d), F32),
        compiler_params=_params("parallel"),
        name="ffn",
    )(x2, g_pre.reshape(1, d), w_gate_up, w_gate_up, w_down, g_post.reshape(1, d))


def _dn_proj_kernel(x_ref, xprev_ref, xnext_ref, g_ref, wqkv_ref, wz_ref, wgate_ref, wmem_ref, conv_ref,
                    alog_ref, dtb_ref, q_ref, k_ref, v_ref, z_ref, qmem_ref, gb_ref, gbt_ref, ext_ref,
                    *, halo, n_dir_heads):
    i = pl.program_id(1)
    last = pl.num_programs(1) - 1
    tm = x_ref.shape[0]
    width = wqkv_ref.shape[1]
    tok = width // 3
    gain = g_ref[...]
    h = _rms(x_ref[...], gain).astype(BF16)
    h_prev = jnp.where(i > 0, _rms(xprev_ref[...], gain), 0.0).astype(BF16)
    h_next = jnp.where(i < last, _rms(xnext_ref[...], gain), 0.0).astype(BF16)
    h_ext = jnp.concatenate([h_prev, h, h_next], axis=0)
    chunk = 2 * LANES
    for c0 in range(0, width, chunk):
        y_ext = jnp.dot(h_ext, wqkv_ref[:, c0:c0 + chunk], preferred_element_type=F32)
        for half in range(chunk // LANES):
            c1 = c0 + half * LANES
            cols = slice(c1, c1 + LANES)
            ext = ext_ref.at[(c1 // LANES) % ext_ref.shape[0]]
            ext[...] = y_ext[:, half * LANES:(half + 1) * LANES]
            taps = [conv_ref[j:j + 1, cols] for j in range(DN_CONV)]
            which, col = divmod(c1, tok)
            out_ref = (q_ref, k_ref, v_ref)[which]
            for r0 in range(0, tm, CONV_ROWS):
                first = halo + r0 - DN_CONV // 2
                acc = ext[first:first + CONV_ROWS, :] * taps[0]
                for j in range(1, DN_CONV):
                    acc = acc + ext[first + j:first + j + CONV_ROWS, :] * taps[j]
                y = _silu(acc)
                if which < 2:
                    scale = lax.rsqrt(jnp.sum(y * y, axis=-1, keepdims=True) + EPS)
                    y = y * (scale * (DN_HEAD_DIM ** -0.5) if which == 0 else scale)
                out_ref[r0:r0 + CONV_ROWS, col:col + LANES] = y.astype(BF16)
    z_ref[...] = jnp.dot(h, wz_ref[...], preferred_element_type=F32).astype(BF16)
    qmem_ref[...] = jnp.dot(h, wmem_ref[...], preferred_element_type=F32).astype(BF16)
    gate = jnp.dot(h, wgate_ref[...], preferred_element_type=F32)
    shifted = gate + dtb_ref[...]
    softplus = jnp.maximum(shifted, 0.0) + jnp.log(1.0 + jnp.exp(-jnp.abs(shifted)))
    decay = -jnp.exp(alog_ref[...]) * softplus
    beta = 1.0 / (1.0 + jnp.exp(-gate))
    lane = lax.broadcasted_iota(jnp.int32, gate.shape, 1)
    gb = jnp.where(lane < n_dir_heads, decay, beta)
    gb_ref[...] = gb
    gbt_ref[...] = jnp.transpose(gb)[:gbt_ref.shape[0], :]


def _dn_proj(x2, gain, w_in, w_gate, w_mem, conv_w, a_log_row, dt_bias_row, batch, seq, tok, n_dir_heads,
             tm=512, halo=16):
    t, d = x2.shape
    n_gate_rows = 8 * ((2 * n_dir_heads + 7) // 8)
    per_seq = seq // tm
    blocks_per_tile = tm // halo
    n_halo_blocks = t // halo
    mem_width = w_mem.shape[1]

    def cur_map(b, i):
        return (b * per_seq + i, 0)

    def prev_map(b, i):
        return (jnp.maximum((b * per_seq + i) * blocks_per_tile - 1, 0), 0)

    def next_map(b, i):
        return (jnp.minimum((b * per_seq + i + 1) * blocks_per_tile, n_halo_blocks - 1), 0)

    return pl.pallas_call(
        functools.partial(_dn_proj_kernel, halo=halo, n_dir_heads=n_dir_heads),
        grid=(batch, per_seq),
        in_specs=[pl.BlockSpec((tm, d), cur_map), pl.BlockSpec((halo, d), prev_map),
                  pl.BlockSpec((halo, d), next_map), _const_spec((1, d)),
                  pl.BlockSpec((d, 3 * tok), lambda b, i: (0, 0)), pl.BlockSpec((d, tok), lambda b, i: (0, 3)),
                  _const_spec(w_gate.shape), _const_spec(w_mem.shape), _const_spec(conv_w.shape),
                  _const_spec((1, LANES)), _const_spec((1, LANES))],
        out_specs=[pl.BlockSpec((tm, tok), cur_map)] * 4
        + [pl.BlockSpec((tm, mem_width), cur_map), pl.BlockSpec((tm, LANES), cur_map),
           pl.BlockSpec((n_gate_rows, tm), lambda b, i: (0, b * per_seq + i))],
        out_shape=[jax.ShapeDtypeStruct((t, tok), BF16)] * 4
        + [jax.ShapeDtypeStruct((t, mem_width), BF16), jax.ShapeDtypeStruct((t, LANES), F32),
           jax.ShapeDtypeStruct((n_gate_rows, t), F32)],
        scratch_shapes=[pltpu.VMEM((4, tm + 2 * halo, LANES), F32)],
        compiler_params=_params("parallel", "arbitrary"),
        name="dn_proj",
    )(x2, x2, x2, gain.reshape(1, d), w_in, w_in, w_gate, w_mem, conv_w, a_log_row, dt_bias_row)


def _mm(a, b):
    return jnp.dot(a, b, preferred_element_type=F32)


def _mm_nt(a, b):
    return lax.dot_general(a, b, (((1,), (1,)), ((), ())), preferred_element_type=F32)


def _mm_tn(a, b):
    return lax.dot_general(a, b, (((0,), (0,)), ((), ())), preferred_element_type=F32)


def _row_blocks(x, size, parity):
    return jnp.concatenate([x[b * size:(b + 1) * size] for b in range(parity, x.shape[0] // size, 2)], axis=0)


def _interleave_row_blocks(even, odd, size):
    pieces = []
    for b in range(even.shape[0] // size):
        pieces += [even[b * size:(b + 1) * size], odd[b * size:(b + 1) * size]]
    return jnp.concatenate(pieces, axis=0)


def _unit_triangular_inverses(neg_ls, uppers, same_block):
    n = neg_ls[0].shape[0]
    ii = lax.broadcasted_iota(jnp.int32, (n, n), 0)
    jj = lax.broadcasted_iota(jnp.int32, (n, n), 1)
    identity = (ii == jj).astype(F32).astype(BF16)
    zero = jnp.zeros((n, n), BF16)
    plus_identity = lambda m: m + identity
    leaf = same_block[SOLVE_LEAF].astype(F32).astype(BF16)
    xbs = [nl * leaf for nl in neg_ls]
    invs = [plus_identity(xb) for xb in xbs]
    power = 1
    while 2 * power < SOLVE_LEAF:
        xbs = [_mm(xb, xb).astype(BF16) for xb in xbs]
        invs = [_mm(plus_identity(xb), inv).astype(BF16) for xb, inv in zip(xbs, invs)]
        power *= 2
    size = SOLVE_LEAF
    while size < n:
        merge = (same_block[2 * size] & jnp.logical_not(same_block[size])).astype(F32).astype(BF16)
        hot = [0 if upper else 1 for upper in uppers]
        offs = [_row_blocks(nl, size, p) * _row_blocks(merge, size, p) for nl, p in zip(neg_ls, hot)]
        rights = [_mm(off, inv).astype(BF16) for off, inv in zip(offs, invs)]
        half_zero = zero[:n // 2]
        rights = [_interleave_row_blocks(*((r, half_zero) if p == 0 else (half_zero, r)), size)
                  for r, p in zip(rights, hot)]
        moved = [_mm(_row_blocks(inv, size, p), plus_identity(right)).astype(BF16)
                 for inv, right, p in zip(invs, rights, hot)]
        invs = [_interleave_row_blocks(*((m, _row_blocks(inv, size, 1)) if p == 0 else
                                         (_row_blocks(inv, size, 0), m)), size)
                for inv, m, p in zip(invs, moved, hot)]
        size *= 2
    return invs


def _exact_prefix_sums(x, tri, rows):
    hi = x.astype(BF16).astype(F32)
    mid = (x - hi).astype(BF16).astype(F32)
    lo = (x - hi - mid).astype(BF16).astype(F32)
    if rows:
        n = x.shape[0]
        res = _mm(jnp.concatenate([hi, mid, lo], axis=0).astype(BF16), tri)
        return res[:n] + res[n:2 * n] + res[2 * n:]
    quarter = LANES // 4
    packed = hi + pltpu.roll(mid, quarter, 1) + pltpu.roll(lo, 2 * quarter, 1)
    res = _mm(tri, packed.astype(BF16))
    return res + pltpu.roll(res, LANES - quarter, 1) + pltpu.roll(res, LANES - 2 * quarter, 1)


def _dn_core_kernel(qf_ref, kf_ref, vf_ref, gbf_ref, gtf_ref, qb_ref, kb_ref, vb_ref, gbb_ref, gtb_ref,
                    of_ref, ob_ref, state_ref, *, n_heads):
    @pl.when(pl.program_id(1) == 0)
    def _():
        state_ref[...] = jnp.zeros_like(state_ref)

    c = DN_CHUNK
    d = DN_HEAD_DIM
    n_sub = qf_ref.shape[0] // c
    ii = lax.broadcasted_iota(jnp.int32, (c, c), 0)
    jj = lax.broadcasted_iota(jnp.int32, (c, c), 1)
    lower_eq, upper_eq = jj <= ii, jj >= ii
    lower, upper = jj < ii, jj > ii
    same_block = {}
    size = SOLVE_LEAF
    while size <= c:
        shift = int(math.log2(size))
        same_block[size] = (ii >> shift) == (jj >> shift)
        size *= 2

    items = []
    for direction, (q_ref, k_ref, v_ref, gb_ref, gt_ref, o_ref) in enumerate(
            ((qf_ref, kf_ref, vf_ref, gbf_ref, gtf_ref, of_ref),
             (qb_ref, kb_ref, vb_ref, gbb_ref, gtb_ref, ob_ref))):
        reverse = direction == 1
        tri_col = (upper_eq if reverse else lower_eq).astype(BF16)
        tri_row = (lower_eq if reverse else upper_eq).astype(BF16)
        for sub in range(n_sub):
            rows = slice(sub * c, (sub + 1) * c)
            gb = gb_ref[rows, :]
            lane = lax.broadcasted_iota(jnp.int32, gb.shape, 1)
            cum_col = _exact_prefix_sums(jnp.where(lane < LANES // 4, gb, 0.0), tri_col, rows=False)
            cum_row = _exact_prefix_sums(gt_ref[:, rows], tri_row, rows=True)
            for head in range(n_heads):
                g_idx = direction * n_heads + head
                b_idx = 2 * n_heads + g_idx
                cols = slice(head * d, (head + 1) * d)
                items.append(dict(
                    direction=direction, head=head, sub=sub, o_ref=o_ref, rows=rows, cols=cols, reverse=reverse,
                    earlier_eq=upper_eq if reverse else lower_eq,
                    earlier=upper if reverse else lower,
                    q=q_ref[rows, cols], k=k_ref[rows, cols], v=v_ref[rows, cols],
                    g_col=cum_col[:, g_idx:g_idx + 1], beta_col=gb[:, b_idx:b_idx + 1],
                    g_row=cum_row[g_idx:g_idx + 1, :],
                    g_last=cum_col[0:1, g_idx:g_idx + 1] if reverse else cum_col[c - 1:c, g_idx:g_idx + 1]))

    grams = [_mm_nt(jnp.concatenate([it["k"], it["q"]], axis=0), it["k"]) for it in items]
    neg_ls, rhss = [], []
    for it, gram in zip(items, grams):
        g_wide = jnp.broadcast_to(it["g_col"], (c, d))
        beta_wide = jnp.broadcast_to(it["beta_col"], (c, d))
        it["g_last_wide"] = jnp.broadcast_to(it["g_last"], (c, d))
        decay = jnp.exp(jnp.where(it["earlier_eq"], g_wide - it["g_row"], NEG_INF))
        neg_ls.append((jnp.where(it["earlier"], gram[:c] * decay, 0.0) * (-beta_wide)).astype(BF16))
        it["intra"] = jnp.where(it["earlier_eq"], gram[c:] * decay, 0.0).astype(BF16)
        kf = it["k"].astype(F32)
        e_g = jnp.exp(g_wide)
        rhss.append(jnp.concatenate([(it["v"].astype(F32) * beta_wide).astype(BF16),
                                     (kf * (beta_wide * e_g)).astype(BF16)], axis=-1))
        it["qg"] = (it["q"].astype(F32) * e_g).astype(BF16)
        it["k_dec"] = (kf * jnp.exp(it["g_last_wide"] - g_wide)).astype(BF16)
    invs = _unit_triangular_inverses(neg_ls, [it["reverse"] for it in items], same_block)
    for it, inv, rhs in zip(items, invs, rhss):
        sol = _mm(inv, rhs)
        it["u"], it["w"] = sol[:, :d], sol[:, d:].astype(BF16)

    for step in range(n_sub):
        live = [it for it in items if it["sub"] == (n_sub - 1 - step if it["reverse"] else step)]
        states = [state_ref[it["direction"], it["head"]] for it in live]
        wss = [_mm(jnp.concatenate([it["w"], it["qg"]], axis=0), state.astype(BF16))
               for it, state in zip(live, states)]
        v_news = [(it["u"] - ws[:c]).astype(BF16) for it, ws in zip(live, wss)]
        outs = [ws[c:] + _mm(it["intra"], v_new) for it, ws, v_new in zip(live, wss, v_news)]
        updates = [_mm_tn(it["k_dec"], v_new) for it, v_new in zip(live, v_news)]
        for it, out, state, update in zip(live, outs, states, updates):
            it["o_ref"][it["rows"], it["cols"]] = out.astype(BF16)
            state_ref[it["direction"], it["head"]] = state * jnp.exp(it["g_last_wide"]) + update


def _dn_core(q, k, v, gb, gbt, batch, seq, chunks_per_step=2):
    t, tok = q.shape
    n_heads = tok // DN_HEAD_DIM
    rows = chunks_per_step * DN_CHUNK
    steps = seq // rows
    n_rows = gbt.shape[0]

    fwd = lambda b, i: (b * steps + i, 0)
    bwd = lambda b, i: (b * steps + steps - 1 - i, 0)
    fwd_t = lambda b, i: (0, b * steps + i)
    bwd_t = lambda b, i: (0, b * steps + steps - 1 - i)

    def side(tok_map, t_map):
        return [pl.BlockSpec((rows, tok), tok_map)] * 3 + [
            pl.BlockSpec((rows, LANES), tok_map), pl.BlockSpec((n_rows, rows), t_map)]

    return pl.pallas_call(
        functools.partial(_dn_core_kernel, n_heads=n_heads),
        grid=(batch, steps),
        in_specs=side(fwd, fwd_t) + side(bwd, bwd_t),
        out_specs=[pl.BlockSpec((rows, tok), fwd), pl.BlockSpec((rows, tok), bwd)],
        out_shape=[jax.ShapeDtypeStruct((t, tok), BF16)] * 2,
        scratch_shapes=[pltpu.VMEM((2, n_heads, DN_HEAD_DIM, DN_HEAD_DIM), F32)],
        compiler_params=_params("parallel", "arbitrary"),
        name="dn_core",
    )(q, k, v, gb, gbt, q, k, v, gb, gbt)


def _dn_out_kernel(of_ref, ob_ref, z_ref, qmem_ref, kmem_ref, vmem_ref, x_ref, onorm_ref, w_ref, g_ref,
                   out_ref):
    tok = of_ref.shape[1]
    parts = []
    for c0 in range(0, tok, DN_HEAD_DIM):
        cols = slice(c0, c0 + DN_HEAD_DIM)
        o = of_ref[:, cols].astype(F32) + ob_ref[:, cols].astype(F32)
        o = o * lax.rsqrt(jnp.mean(o * o, axis=-1, keepdims=True) + EPS) * onorm_ref[...]
        parts.append((o * _silu(z_ref[:, cols].astype(F32))).astype(BF16))
    mem_out = _memory_attention(qmem_ref[...], kmem_ref[0], vmem_ref[0])
    mixed = jnp.concatenate(parts + [mem_out], axis=-1)
    y = jnp.dot(mixed, w_ref[...], preferred_element_type=F32)
    out_ref[...] = x_ref[...] + _rms(y, g_ref[...])


def _dn_out(o_f, o_b, z, q_mem, k_mem, v_mem, x2, out_norm, w_out, gain, seq, tm=1024):
    t, d = x2.shape
    tok = o_f.shape[1]
    row = lambda i: (i, 0)
    return pl.pallas_call(
        _dn_out_kernel,
        grid=(t // tm,),
        in_specs=[pl.BlockSpec((tm, tok), row)] * 3 + _memory_specs(q_mem, k_mem, seq, tm)
        + [pl.BlockSpec((tm, d), row), _const_spec((1, DN_HEAD_DIM)), _const_spec(w_out.shape),
           _const_spec((1, d))],
        out_specs=pl.BlockSpec((tm, d), row),
        out_shape=jax.ShapeDtypeStruct((t, d), F32),
        compiler_params=_params("parallel"),
        name="dn_out",
    )(o_f, o_b, z, q_mem, k_mem, v_mem, x2, out_norm.reshape(1, DN_HEAD_DIM), w_out, gain.reshape(1, d))


def _attention_layer(x2, mem, batch, seq, rel_bias, w_in, w_out, mem_gain, w_kv, g_pre, g_post):
    n_groups = len(DILATED_GROUPS)
    tok = n_groups * GROUP_WIDTH
    dils = tuple(dil for _, dil in DILATED_GROUPS)
    w_qkv = w_in[:, :3 * tok].reshape(-1, 3, n_groups, GROUP_WIDTH).transpose(0, 2, 1, 3).astype(BF16)
    group_weights = [w_qkv[:, g].reshape(-1, 3 * GROUP_WIDTH) for g in range(n_groups)]
    *qkvs, q_mem = _att_proj(x2, g_pre, group_weights, w_in[:, 3 * tok:].astype(BF16), dils)
    os_, lses = [], []
    for group, dil in enumerate(dils):
        n_res = min(dil, 2)
        o, lse = _band_attn(qkvs[group], _band_bias(rel_bias, group, dil), batch, seq, dil, n_res,
                            q_tile=512 // n_res)
        os_.append(o)
        lses.append(lse)
    k_mem, v_mem = _mem_kv(mem, mem_gain, w_kv.astype(BF16))
    return _att_out(os_, lses, dils, q_mem, k_mem, v_mem, x2, w_out.astype(BF16), g_post, seq)


def _deltanet_layer(x2, mem, batch, seq, w_in, conv_w, a_log, dt_bias, out_norm, w_out, mem_gain, w_kv,
                    g_pre, g_post):
    d = x2.shape[1]
    n_heads = a_log.shape[1]
    tok = n_heads * DN_HEAD_DIM
    n_gate = 4 * n_heads
    w_gate = w_in[:, 4 * tok:4 * tok + n_gate].reshape(d, 2, 2, n_heads).transpose(0, 2, 1, 3).reshape(d, n_gate)
    w_gate = jnp.pad(w_gate, ((0, 0), (0, LANES - n_gate)))
    w_in_b = w_in.astype(BF16)
    pad_row = lambda p: jnp.pad(p.reshape(1, -1).astype(F32), ((0, 0), (0, LANES - 2 * n_heads)))
    q, k, v, z, q_mem, gb, gbt = _dn_proj(x2, g_pre, w_in_b, w_gate.astype(BF16), w_in_b[:, 4 * tok + n_gate:],
                                          conv_w.astype(F32), pad_row(a_log), pad_row(dt_bias), batch, seq, tok,
                                          2 * n_heads)
    o_f, o_b = _dn_core(q, k, v, gb, gbt, batch, seq)
    k_mem, v_mem = _mem_kv(mem, mem_gain, w_kv.astype(BF16))
    return _dn_out(o_f, o_b, z, q_mem, k_mem, v_mem, x2, out_norm, w_out.astype(BF16), g_post, seq)


def kernel(x, mem, rel_bias, att_w_in, att_w_out, dn_w_in, dn_conv, dn_a_log, dn_dt_bias, dn_out_norm,
           dn_w_out, mem_norm, mem_w_kv, norm_mix_pre, norm_mix_post, norm_ffn_pre, norm_ffn_post,
           ffn_w_gate_up, ffn_w_down):
    batch, seq, d = x.shape
    depth = norm_mix_pre.shape[0]
    n_mixers = 2
    x2 = x.reshape(batch * seq, d)
    for i in range(depth):
        j = i // n_mixers
        if i % n_mixers == 0:
            x2 = _attention_layer(x2, mem, batch, seq, rel_bias, att_w_in[j], att_w_out[j], mem_norm[i],
                                  mem_w_kv[i], norm_mix_pre[i], norm_mix_post[i])
        else:
            x2 = _deltanet_layer(x2, mem, batch, seq, dn_w_in[j], dn_conv[j], dn_a_log[j], dn_dt_bias[j],
                                 dn_out_norm[j], dn_w_out[j], mem_norm[i], mem_w_kv[i], norm_mix_pre[i],
                                 norm_mix_post[i])
        x2 = _ffn(x2, norm_ffn_pre[i], ffn_w_gate_up[i].astype(BF16), ffn_w_down[i].astype(BF16),
                  norm_ffn_post[i])
    return x2.reshape(batch, seq, d)
```

```python
import functools
import math

import jax
import jax.numpy as jnp
import numpy as np
from jax import lax
from jax.experimental import pallas as pl
from jax.experimental.pallas import tpu as pltpu

F32 = jnp.float32
BF16 = jnp.bfloat16

EPS = 1e-6
NEG_INF = -1e30

LANES = 128
VMEM_LIMIT_BYTES = 56 * 1024 * 1024

DILATED_GROUPS = ((128, 1), (512, 4), (2048, 16))
ATT_HEAD_DIM = 64
HEADS_PER_GROUP = 4
GROUP_WIDTH = HEADS_PER_GROUP * ATT_HEAD_DIM
BAND_HALF = 64
REL_BUCKETS = 32
REL_MAX_DIST = 1024
MEM_HEADS = 4
MEM_HEAD_DIM = 64
DN_HEAD_DIM = 128
DN_CONV = 5
DN_CHUNK = 128
SOLVE_LEAF = 16
CONV_ROWS = 64

ATT_Q_BLOCK = 128
ATT_K_WINDOW = ATT_Q_BLOCK + 2 * BAND_HALF


def _params(*semantics):
    return pltpu.CompilerParams(dimension_semantics=semantics, vmem_limit_bytes=VMEM_LIMIT_BYTES)


def _rms(x, gain):
    return x * lax.rsqrt(jnp.mean(x * x, axis=-1, keepdims=True) + EPS) * gain


def _silu(x):
    return x / (1.0 + jnp.exp2(x * (-1.0 / math.log(2.0))))


def _const_spec(shape):
    return pl.BlockSpec(shape, lambda *_: (0,) * len(shape))


def _mem_kv_kernel(mem_ref, g_ref, w_ref, k_ref, v_ref):
    h = _rms(mem_ref[0], g_ref[...]).astype(BF16)
    kv = jnp.dot(h, w_ref[...], preferred_element_type=F32)
    width = k_ref.shape[-1]
    k_ref[0] = kv[:, :width].astype(BF16)
    v_ref[0] = kv[:, width:].astype(BF16)


def _mem_kv(mem, gain, w_kv):
    b, m, d = mem.shape
    width = w_kv.shape[1] // 2
    return pl.pallas_call(
        _mem_kv_kernel,
        grid=(b,),
        in_specs=[pl.BlockSpec((1, m, d), lambda i: (i, 0, 0)), _const_spec((1, d)), _const_spec(w_kv.shape)],
        out_specs=[pl.BlockSpec((1, m, width), lambda i: (i, 0, 0))] * 2,
        out_shape=[jax.ShapeDtypeStruct((b, m, width), BF16)] * 2,
        compiler_params=_params("parallel"),
        name="mem_kv",
    )(mem, gain.reshape(1, d), w_kv)


def _memory_attention(q, k, v):
    lane = lax.broadcasted_iota(jnp.int32, q.shape, 1)
    in_head = [(lane >= h * MEM_HEAD_DIM) & (lane < (h + 1) * MEM_HEAD_DIM) for h in range(MEM_HEADS)]
    logits = [lax.dot_general(jnp.where(mask, q, jnp.zeros_like(q)), k, (((1,), (1,)), ((), ())),
                              preferred_element_type=F32) for mask in in_head]
    probs = []
    for s in logits:
        s = s * (MEM_HEAD_DIM ** -0.5)
        p = jnp.exp(s - jnp.max(s, axis=-1, keepdims=True))
        probs.append((p / jnp.sum(p, axis=-1, keepdims=True)).astype(BF16))
    pvs = [jnp.dot(p, v, preferred_element_type=F32) for p in probs]
    out = pvs[-1]
    for mask, pv in zip(in_head[:-1], pvs[:-1]):
        out = jnp.where(mask, pv, out)
    return out.astype(BF16)


def _memory_specs(q_mem, k_mem, seq, tm):
    per_seq = seq // tm
    kv_spec = pl.BlockSpec((1,) + k_mem.shape[1:], lambda i: (i // per_seq, 0, 0))
    return [pl.BlockSpec((tm, q_mem.shape[1]), lambda i: (i, 0)), kv_spec, kv_spec]


def _t5_bucket(rel):
    half = REL_BUCKETS // 2
    max_exact = half // 2
    n = np.abs(rel)
    large = max_exact + (np.log(np.maximum(n, 1) / max_exact) / math.log(REL_MAX_DIST / max_exact)
                         * (half - max_exact)).astype(np.int64)
    large = np.minimum(large, half - 1)
    return ((rel > 0) * half + np.where(n < max_exact, n, large)).astype(np.int32)


def _band_bias(rel_bias, group, dil):
    period = ATT_Q_BLOCK + ATT_K_WINDOW
    heads = slice(group * HEADS_PER_GROUP, (group + 1) * HEADS_PER_GROUP)
    tables = []
    for variant in range(3):
        rel = np.arange(period) - (ATT_Q_BLOCK - 1) - variant * BAND_HALF
        in_band = np.abs(rel) <= BAND_HALF
        bucket = _t5_bucket(np.clip(rel, -BAND_HALF, BAND_HALF) * dil)
        f = jnp.where(in_band[None, :], jnp.transpose(rel_bias[bucket][:, heads]).astype(F32), NEG_INF)
        flat = jnp.tile(f, (1, ATT_Q_BLOCK))[:, :ATT_Q_BLOCK * (period - 1)]
        skew = flat.reshape(HEADS_PER_GROUP, ATT_Q_BLOCK, period - 1)
        tables.append(skew[:, :, ATT_Q_BLOCK - 1:ATT_Q_BLOCK - 1 + ATT_K_WINDOW])
    return jnp.stack(tables)


def _band_attn_kernel(x_ref, bias_ref, o_ref, lse_ref, *, sub_len, n_res, q_tile):
    lane = lax.broadcasted_iota(jnp.int32, (ATT_Q_BLOCK, LANES), 1)
    low_half = lane < ATT_HEAD_DIM
    n_pairs = GROUP_WIDTH // LANES

    def tile(t, carry):
        blocks = []
        for res in range(n_res):
            for blk in range(q_tile // ATT_Q_BLOCK):
                i0 = pl.multiple_of(t * q_tile + blk * ATT_Q_BLOCK, ATT_Q_BLOCK)
                ws = jnp.clip(i0 - BAND_HALF, 0, sub_len - ATT_K_WINDOW)
                variant = (i0 - ws) // BAND_HALF
                ws = pl.multiple_of(ws, BAND_HALF)
                base = res * 3 * GROUP_WIDTH
                blocks.append(dict(
                    res=res, rows=pl.ds(i0, ATT_Q_BLOCK), variant=variant,
                    q=x_ref[0, pl.ds(i0, ATT_Q_BLOCK), base:base + GROUP_WIDTH]
                    * jnp.asarray(ATT_HEAD_DIM ** -0.5, BF16),
                    k=x_ref[0, pl.ds(ws, ATT_K_WINDOW), base + GROUP_WIDTH:base + 2 * GROUP_WIDTH],
                    v=x_ref[0, pl.ds(ws, ATT_K_WINDOW), base + 2 * GROUP_WIDTH:base + 3 * GROUP_WIDTH]))
        heads = [(b, pair, a) for b in blocks for pair in range(n_pairs) for a in range(2)]
        logits = []
        for b, pair, a in heads:
            cols = slice(pair * LANES, (pair + 1) * LANES)
            q_pair = b["q"][:, cols]
            keep = low_half if a == 0 else jnp.logical_not(low_half)
            qh = jnp.where(keep, q_pair, jnp.zeros_like(q_pair))
            logits.append(_mm_nt(qh, b["k"][:, cols]))
        probs, dens, lses = [], [], []
        for (b, pair, a), s in zip(heads, logits):
            s = s + bias_ref[b["variant"], 2 * pair + a]
            m = jnp.max(s, axis=-1, keepdims=True)
            p = jnp.exp(s - m)
            den = jnp.sum(p, axis=-1, keepdims=True)
            probs.append(p.astype(BF16))
            dens.append(den)
            lses.append(m + jnp.log(den))
        pvs = [_mm(p, b["v"][:, pair * LANES:(pair + 1) * LANES]) for (b, pair, a), p in zip(heads, probs)]
        for idx in range(0, len(heads), 2):
            b, pair, _ = heads[idx]
            cols = pl.ds(b["res"] * GROUP_WIDTH + pair * LANES, LANES)
            out = jnp.where(low_half, pvs[idx] / dens[idx], pvs[idx + 1] / dens[idx + 1])
            lse = jnp.where(low_half, jnp.broadcast_to(lses[idx], (ATT_Q_BLOCK, LANES)),
                            jnp.broadcast_to(lses[idx + 1], (ATT_Q_BLOCK, LANES)))
            o_ref[0, b["rows"], cols] = out.astype(o_ref.dtype)
            lse_ref[0, b["rows"], cols] = lse
        return carry

    lax.fori_loop(0, sub_len // q_tile, tile, 0)


def _band_attn(qkv, bias, batch, seq, dil, n_res, q_tile=256):
    rows, width = qkv.shape
    sub_len = seq // dil
    qkv3 = qkv.reshape(batch, sub_len, width)
    x_spec = pl.BlockSpec((1, sub_len, n_res * 3 * GROUP_WIDTH), lambda b, r: (b, 0, r))
    o_spec = pl.BlockSpec((1, sub_len, n_res * GROUP_WIDTH), lambda b, r: (b, 0, r))
    o, lse = pl.pallas_call(
        functools.partial(_band_attn_kernel, sub_len=sub_len, n_res=n_res, q_tile=q_tile),
        grid=(batch, dil // n_res),
        in_specs=[x_spec, _const_spec(bias.shape)],
        out_specs=[o_spec, o_spec],
        out_shape=[jax.ShapeDtypeStruct((batch, sub_len, dil * GROUP_WIDTH), BF16),
                   jax.ShapeDtypeStruct((batch, sub_len, dil * GROUP_WIDTH), F32)],
        compiler_params=_params("parallel", "parallel"),
        name=f"band_attn_d{dil}",
    )(qkv3, bias)
    return o.reshape(rows, dil * GROUP_WIDTH), lse.reshape(rows, dil * GROUP_WIDTH)


def _att_proj_kernel(x_ref, g_ref, *refs, dils, row_blocks):
    n = len(dils)
    w_ref = refs[0]
    out_refs, qmem_ref, stage_ref = refs[1:n + 1], refs[n + 1], refs[n + 2]
    width = out_refs[0].shape[1] // dils[0]
    tiles = width // LANES
    rows = x_ref.shape[0] // row_blocks
    staged = 0
    for blk in range(row_blocks):
        sl = slice(blk * rows, (blk + 1) * rows)
        h = _rms(x_ref[sl, :], g_ref[...]).astype(BF16)
        y_all = jnp.dot(h, w_ref[...], preferred_element_type=F32)
        for g, (o_ref, dil) in enumerate(zip(out_refs, dils)):
            y = y_all[:, g * width:(g + 1) * width]
            out_rows = slice(blk * rows // dil, (blk + 1) * rows // dil)
            if dil == 1:
                o_ref[out_rows, :] = y.astype(BF16)
            else:
                for c in range(tiles):
                    stage_ref[staged + c] = y[:, c * LANES:(c + 1) * LANES]
                for r in range(dil):
                    for c in range(tiles):
                        lanes = pl.ds(r * width + c * LANES, LANES)
                        o_ref[out_rows, lanes] = stage_ref[
                            staged + c, pl.ds(r, rows // dil, stride=dil), :].astype(BF16)
                staged += tiles
        qmem_ref[sl, :] = y_all[:, n * width:].astype(BF16)


def _att_proj(x2, gain, w_all, width, dils, tm=512, row_blocks=2):
    t, d = x2.shape
    mem_width = w_all.shape[1] - len(dils) * width
    row = lambda i: (i, 0)
    n_staged = row_blocks * sum(dil > 1 for dil in dils) * (width // LANES)
    return pl.pallas_call(
        functools.partial(_att_proj_kernel, dils=dils, row_blocks=row_blocks),
        grid=(t // tm,),
        in_specs=[pl.BlockSpec((tm, d), row), _const_spec((1, d)), _const_spec(w_all.shape)],
        out_specs=[pl.BlockSpec((tm // dil, dil * width), row) for dil in dils]
        + [pl.BlockSpec((tm, mem_width), row)],
        out_shape=[jax.ShapeDtypeStruct((t // dil, dil * width), BF16) for dil in dils]
        + [jax.ShapeDtypeStruct((t, mem_width), BF16)],
        scratch_shapes=[pltpu.VMEM((n_staged, tm // row_blocks, LANES), F32)],
        compiler_params=_params("parallel"),
        name="att_proj",
    )(x2, gain.reshape(1, d), w_all)


def _att_out_kernel(*refs, dils):
    n = len(dils)
    o_refs, l_refs = refs[:n], refs[n:2 * n]
    qmem_ref, kmem_ref, vmem_ref, x_ref, w_ref, g_ref, out_ref, stage_ref = refs[2 * n:]
    tm = x_ref.shape[0]

    def token_major(ref, dil, slot):
        if dil == 1:
            return ref[...].astype(F32)
        tiles = GROUP_WIDTH // LANES
        for r in range(dil):
            for c in range(tiles):
                lanes = pl.ds(r * GROUP_WIDTH + c * LANES, LANES)
                stage_ref[slot * tiles + c, pl.ds(r, tm // dil, stride=dil), :] = ref[:, lanes].astype(F32)
        return jnp.concatenate([stage_ref[slot * tiles + c] for c in range(tiles)], axis=-1)

    lses = [token_major(ref, dil, 2 * i) for i, (ref, dil) in enumerate(zip(l_refs, dils))]
    outs = [token_major(ref, dil, 2 * i + 1) for i, (ref, dil) in enumerate(zip(o_refs, dils))]
    mx = functools.reduce(jnp.maximum, lses)
    es = [jnp.exp(l - mx) for l in lses]
    tot = functools.reduce(lambda a, b: a + b, es)
    parts = [(o * (e / tot)).astype(BF16) for o, e in zip(outs, es)]
    mem_out = _memory_attention(qmem_ref[...], kmem_ref[0], vmem_ref[0])
    mixed = jnp.concatenate(parts + [mem_out], axis=-1)
    y = jnp.dot(mixed, w_ref[...], preferred_element_type=F32)
    out_ref[...] = x_ref[...] + _rms(y, g_ref[...])


def _att_out(os_, lses, dils, q_mem, k_mem, v_mem, x2, w_out, gain, seq, tm=1024):
    t, d = x2.shape
    row = lambda i: (i, 0)
    group_specs = [pl.BlockSpec((tm // dil, dil * GROUP_WIDTH), row) for dil in dils]
    return pl.pallas_call(
        functools.partial(_att_out_kernel, dils=dils),
        grid=(t // tm,),
        in_specs=group_specs + group_specs + _memory_specs(q_mem, k_mem, seq, tm)
        + [pl.BlockSpec((tm, d), row), _const_spec(w_out.shape), _const_spec((1, d))],
        out_specs=pl.BlockSpec((tm, d), row),
        out_shape=jax.ShapeDtypeStruct((t, d), F32),
        scratch_shapes=[pltpu.VMEM((2 * len(dils) * (GROUP_WIDTH // LANES), tm, LANES), F32)],
        compiler_params=_params("parallel"),
        name="att_out",
    )(*os_, *lses, q_mem, k_mem, v_mem, x2, w_out, gain.reshape(1, d))


def _ffn_kernel(x_ref, gpre_ref, wg_ref, wu_ref, wd_ref, gpost_ref, out_ref, *, row_blocks):
    rows = x_ref.shape[0] // row_blocks
    for blk in range(row_blocks):
        sl = slice(blk * rows, (blk + 1) * rows)
        x = x_ref[sl, :]
        h = _rms(x, gpre_ref[...]).astype(BF16)
        gate = jnp.dot(h, wg_ref[...], preferred_element_type=F32)
        up = jnp.dot(h, wu_ref[...], preferred_element_type=F32)
        act = (_silu(gate) * up).astype(BF16)
        y = jnp.dot(act, wd_ref[...], preferred_element_type=F32)
        out_ref[sl, :] = x + _rms(y, gpost_ref[...])


def _ffn(x2, g_pre, w_gate_up, w_down, g_post, tm=512, row_blocks=2):
    t, d = x2.shape
    d_ff = w_down.shape[0]
    row = lambda i: (i, 0)
    return pl.pallas_call(
        functools.partial(_ffn_kernel, row_blocks=row_blocks),
        grid=(t // tm,),
        in_specs=[pl.BlockSpec((tm, d), row), _const_spec((1, d)),
                  pl.BlockSpec((d, d_ff), lambda i: (0, 0)), pl.BlockSpec((d, d_ff), lambda i: (0, 1)),
                  _const_spec(w_down.shape), _const_spec((1, d))],
        out_specs=pl.BlockSpec((tm, d), row),
        out_shape=jax.ShapeDtypeStruct((t, d), F32),
        compiler_params=_params("parallel"),
        name="ffn",
    )(x2, g_pre.reshape(1, d), w_gate_up, w_gate_up, w_down, g_post.reshape(1, d))


def _dn_proj_kernel(x_ref, xprev_ref, xnext_ref, g_ref, wqkv_ref, wz_ref, wgate_ref, wmem_ref, conv_ref,
                    alog_ref, dtb_ref, q_ref, k_ref, v_ref, z_ref, qmem_ref, gb_ref, gbt_ref, ext_ref,
                    *, halo, n_dir_heads):
    i = pl.program_id(1)
    last = pl.num_programs(1) - 1
    tm = x_ref.shape[0]
    width = wqkv_ref.shape[1]
    tok = width // 3
    gain = g_ref[...]
    h = _rms(x_ref[...], gain).astype(BF16)
    h_prev = jnp.where(i > 0, _rms(xprev_ref[...], gain), 0.0).astype(BF16)
    h_next = jnp.where(i < last, _rms(xnext_ref[...], gain), 0.0).astype(BF16)
    h_ext = jnp.concatenate([h_prev, h, h_next], axis=0)
    chunk = 2 * LANES
    for c0 in range(0, width, chunk):
        y_ext = jnp.dot(h_ext, wqkv_ref[:, c0:c0 + chunk], preferred_element_type=F32)
        for half in range(chunk // LANES):
            c1 = c0 + half * LANES
            cols = slice(c1, c1 + LANES)
            ext = ext_ref.at[(c1 // LANES) % ext_ref.shape[0]]
            ext[...] = y_ext[:, half * LANES:(half + 1) * LANES]
            taps = [conv_ref[j:j + 1, cols] for j in range(DN_CONV)]
            which, col = divmod(c1, tok)
            out_ref = (q_ref, k_ref, v_ref)[which]
            for r0 in range(0, tm, CONV_ROWS):
                first = halo + r0 - DN_CONV // 2
                acc = ext[first:first + CONV_ROWS, :] * taps[0]
                for j in range(1, DN_CONV):
                    acc = acc + ext[first + j:first + j + CONV_ROWS, :] * taps[j]
                y = _silu(acc)
                if which < 2:
                    scale = lax.rsqrt(jnp.sum(y * y, axis=-1, keepdims=True) + EPS)
                    y = y * (scale * (DN_HEAD_DIM ** -0.5) if which == 0 else scale)
                out_ref[r0:r0 + CONV_ROWS, col:col + LANES] = y.astype(BF16)
    z_ref[...] = jnp.dot(h, wz_ref[...], preferred_element_type=F32).astype(BF16)
    qmem_ref[...] = jnp.dot(h, wmem_ref[...], preferred_element_type=F32).astype(BF16)
    gate = jnp.dot(h, wgate_ref[...], preferred_element_type=F32)
    shifted = gate + dtb_ref[...]
    softplus = jnp.maximum(shifted, 0.0) + jnp.log(1.0 + jnp.exp(-jnp.abs(shifted)))
    decay = -jnp.exp(alog_ref[...]) * softplus
    beta = 1.0 / (1.0 + jnp.exp(-gate))
    lane = lax.broadcasted_iota(jnp.int32, gate.shape, 1)
    gb = jnp.where(lane < n_dir_heads, decay, beta)
    gb_ref[...] = gb
    gbt_ref[...] = jnp.transpose(gb)[:gbt_ref.shape[0], :]


def _dn_proj(x2, gain, w_in, w_gate, w_mem, conv_w, a_log_row, dt_bias_row, batch, seq, tok, n_dir_heads,
             tm=512, halo=16):
    t, d = x2.shape
    n_gate_rows = 8 * ((2 * n_dir_heads + 7) // 8)
    per_seq = seq // tm
    blocks_per_tile = tm // halo
    n_halo_blocks = t // halo
    mem_width = w_mem.shape[1]

    def cur_map(b, i):
        return (b * per_seq + i, 0)

    def prev_map(b, i):
        return (jnp.maximum((b * per_seq + i) * blocks_per_tile - 1, 0), 0)

    def next_map(b, i):
        return (jnp.minimum((b * per_seq + i + 1) * blocks_per_tile, n_halo_blocks - 1), 0)

    return pl.pallas_call(
        functools.partial(_dn_proj_kernel, halo=halo, n_dir_heads=n_dir_heads),
        grid=(batch, per_seq),
        in_specs=[pl.BlockSpec((tm, d), cur_map), pl.BlockSpec((halo, d), prev_map),
                  pl.BlockSpec((halo, d), next_map), _const_spec((1, d)),
                  pl.BlockSpec((d, 3 * tok), lambda b, i: (0, 0)), pl.BlockSpec((d, tok), lambda b, i: (0, 3)),
                  _const_spec(w_gate.shape), _const_spec(w_mem.shape), _const_spec(conv_w.shape),
                  _const_spec((1, LANES)), _const_spec((1, LANES))],
        out_specs=[pl.BlockSpec((tm, tok), cur_map)] * 4
        + [pl.BlockSpec((tm, mem_width), cur_map), pl.BlockSpec((tm, LANES), cur_map),
           pl.BlockSpec((n_gate_rows, tm), lambda b, i: (0, b * per_seq + i))],
        out_shape=[jax.ShapeDtypeStruct((t, tok), BF16)] * 4
        + [jax.ShapeDtypeStruct((t, mem_width), BF16), jax.ShapeDtypeStruct((t, LANES), F32),
           jax.ShapeDtypeStruct((n_gate_rows, t), F32)],
        scratch_shapes=[pltpu.VMEM((4, tm + 2 * halo, LANES), F32)],
        compiler_params=_params("parallel", "arbitrary"),
        name="dn_proj",
    )(x2, x2, x2, gain.reshape(1, d), w_in, w_in, w_gate, w_mem, conv_w, a_log_row, dt_bias_row)


def _mm(a, b):
    return jnp.dot(a, b, preferred_element_type=F32)


def _mm_nt(a, b):
    return lax.dot_general(a, b, (((1,), (1,)), ((), ())), preferred_element_type=F32)


def _mm_tn(a, b):
    return lax.dot_general(a, b, (((0,), (0,)), ((), ())), preferred_element_type=F32)


def _row_blocks(x, size, parity):
    return jnp.concatenate([x[b * size:(b + 1) * size] for b in range(parity, x.shape[0] // size, 2)], axis=0)


def _interleave_row_blocks(even, odd, size):
    pieces = []
    for b in range(even.shape[0] // size):
        pieces += [even[b * size:(b + 1) * size], odd[b * size:(b + 1) * size]]
    return jnp.concatenate(pieces, axis=0)


def _unit_triangular_inverses(neg_ls, uppers, same_block):
    n = neg_ls[0].shape[0]
    ii = lax.broadcasted_iota(jnp.int32, (n, n), 0)
    jj = lax.broadcasted_iota(jnp.int32, (n, n), 1)
    identity = (ii == jj).astype(F32).astype(BF16)
    zero = jnp.zeros((n, n), BF16)
    plus_identity = lambda m: m + identity
    leaf = same_block[SOLVE_LEAF].astype(F32).astype(BF16)
    xbs = [nl * leaf for nl in neg_ls]
    invs = [plus_identity(xb) for xb in xbs]
    power = 1
    while 2 * power < SOLVE_LEAF:
        xbs = [_mm(xb, xb).astype(BF16) for xb in xbs]
        invs = [_mm(plus_identity(xb), inv).astype(BF16) for xb, inv in zip(xbs, invs)]
        power *= 2
    size = SOLVE_LEAF
    while size < n:
        merge = (same_block[2 * size] & jnp.logical_not(same_block[size])).astype(F32).astype(BF16)
        hot = [0 if upper else 1 for upper in uppers]
        offs = [_row_blocks(nl, size, p) * _row_blocks(merge, size, p) for nl, p in zip(neg_ls, hot)]
        rights = [_mm(off, inv).astype(BF16) for off, inv in zip(offs, invs)]
        half_zero = zero[:n // 2]
        rights = [_interleave_row_blocks(*((r, half_zero) if p == 0 else (half_zero, r)), size)
                  for r, p in zip(rights, hot)]
        moved = [_mm(_row_blocks(inv, size, p), plus_identity(right)).astype(BF16)
                 for inv, right, p in zip(invs, rights, hot)]
        invs = [_interleave_row_blocks(*((m, _row_blocks(inv, size, 1)) if p == 0 else
                                         (_row_blocks(inv, size, 0), m)), size)
                for inv, m, p in zip(invs, moved, hot)]
        size *= 2
    return invs


def _exact_prefix_sums(x, tri, rows):
    hi = x.astype(BF16).astype(F32)
    mid = (x - hi).astype(BF16).astype(F32)
    lo = (x - hi - mid).astype(BF16).astype(F32)
    if rows:
        n = x.shape[0]
        res = _mm(jnp.concatenate([hi, mid, lo], axis=0).astype(BF16), tri)
        return res[:n] + res[n:2 * n] + res[2 * n:]
    quarter = LANES // 4
    packed = hi + pltpu.roll(mid, quarter, 1) + pltpu.roll(lo, 2 * quarter, 1)
    res = _mm(tri, packed.astype(BF16))
    return res + pltpu.roll(res, LANES - quarter, 1) + pltpu.roll(res, LANES - 2 * quarter, 1)


def _dn_core_kernel(qf_ref, kf_ref, vf_ref, gbf_ref, gtf_ref, qb_ref, kb_ref, vb_ref, gbb_ref, gtb_ref,
                    of_ref, ob_ref, state_ref, *, n_heads):
    @pl.when(pl.program_id(1) == 0)
    def _():
        state_ref[...] = jnp.zeros_like(state_ref)

    c = DN_CHUNK
    d = DN_HEAD_DIM
    n_sub = qf_ref.shape[0] // c
    ii = lax.broadcasted_iota(jnp.int32, (c, c), 0)
    jj = lax.broadcasted_iota(jnp.int32, (c, c), 1)
    lower_eq, upper_eq = jj <= ii, jj >= ii
    lower, upper = jj < ii, jj > ii
    same_block = {}
    size = SOLVE_LEAF
    while size <= c:
        shift = int(math.log2(size))
        same_block[size] = (ii >> shift) == (jj >> shift)
        size *= 2

    items = []
    for direction, (q_ref, k_ref, v_ref, gb_ref, gt_ref, o_ref) in enumerate(
            ((qf_ref, kf_ref, vf_ref, gbf_ref, gtf_ref, of_ref),
             (qb_ref, kb_ref, vb_ref, gbb_ref, gtb_ref, ob_ref))):
        reverse = direction == 1
        tri_col = (upper_eq if reverse else lower_eq).astype(BF16)
        tri_row = (lower_eq if reverse else upper_eq).astype(BF16)
        for sub in range(n_sub):
            rows = slice(sub * c, (sub + 1) * c)
            gb = gb_ref[rows, :]
            lane = lax.broadcasted_iota(jnp.int32, gb.shape, 1)
            cum_col = _exact_prefix_sums(jnp.where(lane < LANES // 4, gb, 0.0), tri_col, rows=False)
            cum_row = _exact_prefix_sums(gt_ref[:, rows], tri_row, rows=True)
            for head in range(n_heads):
                g_idx = direction * n_heads + head
                b_idx = 2 * n_heads + g_idx
                cols = slice(head * d, (head + 1) * d)
                items.append(dict(
                    direction=direction, head=head, sub=sub, o_ref=o_ref, rows=rows, cols=cols, reverse=reverse,
                    earlier_eq=upper_eq if reverse else lower_eq,
                    earlier=upper if reverse else lower,
                    q=q_ref[rows, cols], k=k_ref[rows, cols], v=v_ref[rows, cols],
                    g_col=cum_col[:, g_idx:g_idx + 1], beta_col=gb[:, b_idx:b_idx + 1],
                    g_row=cum_row[g_idx:g_idx + 1, :],
                    g_last=cum_col[0:1, g_idx:g_idx + 1] if reverse else cum_col[c - 1:c, g_idx:g_idx + 1]))

    grams = [_mm_nt(jnp.concatenate([it["k"], it["q"]], axis=0), it["k"]) for it in items]
    neg_ls, rhss = [], []
    for it, gram in zip(items, grams):
        g_wide = jnp.broadcast_to(it["g_col"], (c, d))
        beta_wide = jnp.broadcast_to(it["beta_col"], (c, d))
        it["g_last_wide"] = jnp.broadcast_to(it["g_last"], (c, d))
        decay = jnp.exp(jnp.where(it["earlier_eq"], g_wide - it["g_row"], NEG_INF))
        neg_ls.append((jnp.where(it["earlier"], gram[:c] * decay, 0.0) * (-beta_wide)).astype(BF16))
        it["intra"] = jnp.where(it["earlier_eq"], gram[c:] * decay, 0.0).astype(BF16)
        kf = it["k"].astype(F32)
        e_g = jnp.exp(g_wide)
        rhss.append(jnp.concatenate([(it["v"].astype(F32) * beta_wide).astype(BF16),
                                     (kf * (beta_wide * e_g)).astype(BF16)], axis=-1))
        it["qg"] = (it["q"].astype(F32) * e_g).astype(BF16)
        it["k_dec"] = (kf * jnp.exp(it["g_last_wide"] - g_wide)).astype(BF16)
    invs = _unit_triangular_inverses(neg_ls, [it["reverse"] for it in items], same_block)
    for it, inv, rhs in zip(items, invs, rhss):
        sol = _mm(inv, rhs)
        it["u"], it["w"] = sol[:, :d], sol[:, d:].astype(BF16)

    for step in range(n_sub):
        live = [it for it in items if it["sub"] == (n_sub - 1 - step if it["reverse"] else step)]
        states = [state_ref[it["direction"], it["head"]] for it in live]
        wss = [_mm(jnp.concatenate([it["w"], it["qg"]], axis=0), state.astype(BF16))
               for it, state in zip(live, states)]
        v_news = [(it["u"] - ws[:c]).astype(BF16) for it, ws in zip(live, wss)]
        outs = [ws[c:] + _mm(it["intra"], v_new) for it, ws, v_new in zip(live, wss, v_news)]
        updates = [_mm_tn(it["k_dec"], v_new) for it, v_new in zip(live, v_news)]
        for it, out, state, update in zip(live, outs, states, updates):
            it["o_ref"][it["rows"], it["cols"]] = out.astype(BF16)
            state_ref[it["direction"], it["head"]] = state * jnp.exp(it["g_last_wide"]) + update


def _dn_core(q, k, v, gb, gbt, batch, seq, chunks_per_step=2):
    t, tok = q.shape
    n_heads = tok // DN_HEAD_DIM
    rows = chunks_per_step * DN_CHUNK
    steps = seq // rows
    n_rows = gbt.shape[0]

    fwd = lambda b, i: (b * steps + i, 0)
    bwd = lambda b, i: (b * steps + steps - 1 - i, 0)
    fwd_t = lambda b, i: (0, b * steps + i)
    bwd_t = lambda b, i: (0, b * steps + steps - 1 - i)

    def side(tok_map, t_map):
        return [pl.BlockSpec((rows, tok), tok_map)] * 3 + [
            pl.BlockSpec((rows, LANES), tok_map), pl.BlockSpec((n_rows, rows), t_map)]

    return pl.pallas_call(
        functools.partial(_dn_core_kernel, n_heads=n_heads),
        grid=(batch, steps),
        in_specs=side(fwd, fwd_t) + side(bwd, bwd_t),
        out_specs=[pl.BlockSpec((rows, tok), fwd), pl.BlockSpec((rows, tok), bwd)],
        out_shape=[jax.ShapeDtypeStruct((t, tok), BF16)] * 2,
        scratch_shapes=[pltpu.VMEM((2, n_heads, DN_HEAD_DIM, DN_HEAD_DIM), F32)],
        compiler_params=_params("parallel", "arbitrary"),
        name="dn_core",
    )(q, k, v, gb, gbt, q, k, v, gb, gbt)


def _dn_out_kernel(of_ref, ob_ref, z_ref, qmem_ref, kmem_ref, vmem_ref, x_ref, onorm_ref, w_ref, g_ref,
                   out_ref):
    tok = of_ref.shape[1]
    parts = []
    for c0 in range(0, tok, DN_HEAD_DIM):
        cols = slice(c0, c0 + DN_HEAD_DIM)
        o = of_ref[:, cols].astype(F32) + ob_ref[:, cols].astype(F32)
        o = o * lax.rsqrt(jnp.mean(o * o, axis=-1, keepdims=True) + EPS) * onorm_ref[...]
        parts.append((o * _silu(z_ref[:, cols].astype(F32))).astype(BF16))
    mem_out = _memory_attention(qmem_ref[...], kmem_ref[0], vmem_ref[0])
    mixed = jnp.concatenate(parts + [mem_out], axis=-1)
    y = jnp.dot(mixed, w_ref[...], preferred_element_type=F32)
    out_ref[...] = x_ref[...] + _rms(y, g_ref[...])


def _dn_out(o_f, o_b, z, q_mem, k_mem, v_mem, x2, out_norm, w_out, gain, seq, tm=1024):
    t, d = x2.shape
    tok = o_f.shape[1]
    row = lambda i: (i, 0)
    return pl.pallas_call(
        _dn_out_kernel,
        grid=(t // tm,),
        in_specs=[pl.BlockSpec((tm, tok), row)] * 3 + _memory_specs(q_mem, k_mem, seq, tm)
        + [pl.BlockSpec((tm, d), row), _const_spec((1, DN_HEAD_DIM)), _const_spec(w_out.shape),
           _const_spec((1, d))],
        out_specs=pl.BlockSpec((tm, d), row),
        out_shape=jax.ShapeDtypeStruct((t, d), F32),
        compiler_params=_params("parallel"),
        name="dn_out",
    )(o_f, o_b, z, q_mem, k_mem, v_mem, x2, out_norm.reshape(1, DN_HEAD_DIM), w_out, gain.reshape(1, d))


def _attention_layer(x2, mem, batch, seq, rel_bias, w_in, w_out, mem_gain, w_kv, g_pre, g_post):
    n_groups = len(DILATED_GROUPS)
    tok = n_groups * GROUP_WIDTH
    dils = tuple(dil for _, dil in DILATED_GROUPS)
    w_qkv = w_in[:, :3 * tok].reshape(-1, 3, n_groups, GROUP_WIDTH).transpose(0, 2, 1, 3).reshape(-1, 3 * tok)
    w_all = jnp.concatenate([w_qkv, w_in[:, 3 * tok:]], axis=1).astype(BF16)
    *qkvs, q_mem = _att_proj(x2, g_pre, w_all, 3 * GROUP_WIDTH, dils)
    os_, lses = [], []
    for group, dil in enumerate(dils):
        n_res = min(dil, 2)
        o, lse = _band_attn(qkvs[group], _band_bias(rel_bias, group, dil), batch, seq, dil, n_res,
                            q_tile=512 // n_res)
        os_.append(o)
        lses.append(lse)
    k_mem, v_mem = _mem_kv(mem, mem_gain, w_kv.astype(BF16))
    return _att_out(os_, lses, dils, q_mem, k_mem, v_mem, x2, w_out.astype(BF16), g_post, seq)


def _deltanet_layer(x2, mem, batch, seq, w_in, conv_w, a_log, dt_bias, out_norm, w_out, mem_gain, w_kv,
                    g_pre, g_post):
    d = x2.shape[1]
    n_heads = a_log.shape[1]
    tok = n_heads * DN_HEAD_DIM
    n_gate = 4 * n_heads
    w_gate = w_in[:, 4 * tok:4 * tok + n_gate].reshape(d, 2, 2, n_heads).transpose(0, 2, 1, 3).reshape(d, n_gate)
    w_gate = jnp.pad(w_gate, ((0, 0), (0, LANES - n_gate)))
    w_in_b = w_in.astype(BF16)
    pad_row = lambda p: jnp.pad(p.reshape(1, -1).astype(F32), ((0, 0), (0, LANES - 2 * n_heads)))
    q, k, v, z, q_mem, gb, gbt = _dn_proj(x2, g_pre, w_in_b, w_gate.astype(BF16), w_in_b[:, 4 * tok + n_gate:],
                                          conv_w.astype(F32), pad_row(a_log), pad_row(dt_bias), batch, seq, tok,
                                          2 * n_heads)
    o_f, o_b = _dn_core(q, k, v, gb, gbt, batch, seq)
    k_mem, v_mem = _mem_kv(mem, mem_gain, w_kv.astype(BF16))
    return _dn_out(o_f, o_b, z, q_mem, k_mem, v_mem, x2, out_norm, w_out.astype(BF16), g_post, seq)


def kernel(x, mem, rel_bias, att_w_in, att_w_out, dn_w_in, dn_conv, dn_a_log, dn_dt_bias, dn_out_norm,
           dn_w_out, mem_norm, mem_w_kv, norm_mix_pre, norm_mix_post, norm_ffn_pre, norm_ffn_post,
           ffn_w_gate_up, ffn_w_down):
    batch, seq, d = x.shape
    depth = norm_mix_pre.shape[0]
    n_mixers = 2
    x2 = x.reshape(batch * seq, d)
    for i in range(depth):
        j = i // n_mixers
        if i % n_mixers == 0:
            x2 = _attention_layer(x2, mem, batch, seq, rel_bias, att_w_in[j], att_w_out[j], mem_norm[i],
                                  mem_w_kv[i], norm_mix_pre[i], norm_mix_post[i])
        else:
            x2 = _deltanet_layer(x2, mem, batch, seq, dn_w_in[j], dn_conv[j], dn_a_log[j], dn_dt_bias[j],
                                 dn_out_norm[j], dn_w_out[j], mem_norm[i], mem_w_kv[i], norm_mix_pre[i],
                                 norm_mix_post[i])
        x2 = _ffn(x2, norm_ffn_pre[i], ffn_w_gate_up[i].astype(BF16), ffn_w_down[i].astype(BF16),
                  norm_ffn_post[i])
    return x2.reshape(batch, seq, d)
```

```python
import functools
import math

import jax
import jax.numpy as jnp
import numpy as np
from jax import lax
from jax.experimental import pallas as pl
from jax.experimental.pallas import tpu as pltpu

F32 = jnp.float32
BF16 = jnp.bfloat16

EPS = 1e-6
NEG_INF = -1e30

LANES = 128
VMEM_LIMIT_BYTES = 56 * 1024 * 1024

DILATED_GROUPS = ((128, 1), (512, 4), (2048, 16))
ATT_HEAD_DIM = 64
HEADS_PER_GROUP = 4
GROUP_WIDTH = HEADS_PER_GROUP * ATT_HEAD_DIM
BAND_HALF = 64
REL_BUCKETS = 32
REL_MAX_DIST = 1024
MEM_HEADS = 4
MEM_HEAD_DIM = 64
DN_HEAD_DIM = 128
DN_CONV = 5
DN_CHUNK = 128
SOLVE_LEAF = 16
CONV_ROWS = 64

ATT_Q_BLOCK = 128
ATT_K_WINDOW = ATT_Q_BLOCK + 2 * BAND_HALF


def _params(*semantics):
    return pltpu.CompilerParams(dimension_semantics=semantics, vmem_limit_bytes=VMEM_LIMIT_BYTES)


def _rms(x, gain):
    return x * lax.rsqrt(jnp.mean(x * x, axis=-1, keepdims=True) + EPS) * gain


def _silu(x):
    return x / (1.0 + jnp.exp2(x * (-1.0 / math.log(2.0))))


def _const_spec(shape):
    return pl.BlockSpec(shape, lambda *_: (0,) * len(shape))


def _mem_kv_kernel(mem_ref, g_ref, w_ref, k_ref, v_ref):
    h = _rms(mem_ref[0], g_ref[...]).astype(BF16)
    kv = jnp.dot(h, w_ref[...], preferred_element_type=F32)
    width = k_ref.shape[-1]
    k_ref[0] = kv[:, :width].astype(BF16)
    v_ref[0] = kv[:, width:].astype(BF16)


def _mem_kv(mem, gain, w_kv):
    b, m, d = mem.shape
    width = w_kv.shape[1] // 2
    return pl.pallas_call(
        _mem_kv_kernel,
        grid=(b,),
        in_specs=[pl.BlockSpec((1, m, d), lambda i: (i, 0, 0)), _const_spec((1, d)), _const_spec(w_kv.shape)],
        out_specs=[pl.BlockSpec((1, m, width), lambda i: (i, 0, 0))] * 2,
        out_shape=[jax.ShapeDtypeStruct((b, m, width), BF16)] * 2,
        compiler_params=_params("parallel"),
        name="mem_kv",
    )(mem, gain.reshape(1, d), w_kv)


def _memory_attention(q, k, v):
    lane = lax.broadcasted_iota(jnp.int32, q.shape, 1)
    in_head = [(lane >= h * MEM_HEAD_DIM) & (lane < (h + 1) * MEM_HEAD_DIM) for h in range(MEM_HEADS)]
    logits = [lax.dot_general(jnp.where(mask, q, jnp.zeros_like(q)), k, (((1,), (1,)), ((), ())),
                              preferred_element_type=F32) for mask in in_head]
    probs = []
    for s in logits:
        s = s * (MEM_HEAD_DIM ** -0.5)
        p = jnp.exp(s - jnp.max(s, axis=-1, keepdims=True))
        probs.append((p / jnp.sum(p, axis=-1, keepdims=True)).astype(BF16))
    pvs = [jnp.dot(p, v, preferred_element_type=F32) for p in probs]
    out = pvs[-1]
    for mask, pv in zip(in_head[:-1], pvs[:-1]):
        out = jnp.where(mask, pv, out)
    return out.astype(BF16)


def _memory_specs(q_mem, k_mem, seq, tm):
    per_seq = seq // tm
    kv_spec = pl.BlockSpec((1,) + k_mem.shape[1:], lambda i: (i // per_seq, 0, 0))
    return [pl.BlockSpec((tm, q_mem.shape[1]), lambda i: (i, 0)), kv_spec, kv_spec]


def _t5_bucket(rel):
    half = REL_BUCKETS // 2
    max_exact = half // 2
    n = np.abs(rel)
    large = max_exact + (np.log(np.maximum(n, 1) / max_exact) / math.log(REL_MAX_DIST / max_exact)
                         * (half - max_exact)).astype(np.int64)
    large = np.minimum(large, half - 1)
    return ((rel > 0) * half + np.where(n < max_exact, n, large)).astype(np.int32)


def _band_bias(rel_bias, group, dil):
    period = ATT_Q_BLOCK + ATT_K_WINDOW
    heads = slice(group * HEADS_PER_GROUP, (group + 1) * HEADS_PER_GROUP)
    tables = []
    for variant in range(3):
        rel = np.arange(period) - (ATT_Q_BLOCK - 1) - variant * BAND_HALF
        in_band = np.abs(rel) <= BAND_HALF
        bucket = _t5_bucket(np.clip(rel, -BAND_HALF, BAND_HALF) * dil)
        f = jnp.where(in_band[None, :], jnp.transpose(rel_bias[bucket][:, heads]).astype(F32), NEG_INF)
        flat = jnp.tile(f, (1, ATT_Q_BLOCK))[:, :ATT_Q_BLOCK * (period - 1)]
        skew = flat.reshape(HEADS_PER_GROUP, ATT_Q_BLOCK, period - 1)
        tables.append(skew[:, :, ATT_Q_BLOCK - 1:ATT_Q_BLOCK - 1 + ATT_K_WINDOW])
    return jnp.stack(tables)


def _band_attn_kernel(x_ref, bias_ref, o_ref, lse_ref, *, sub_len, n_res, q_tile):
    lane = lax.broadcasted_iota(jnp.int32, (ATT_Q_BLOCK, LANES), 1)
    low_half = lane < ATT_HEAD_DIM
    n_pairs = GROUP_WIDTH // LANES

    def tile(t, carry):
        blocks = []
        for res in range(n_res):
            for blk in range(q_tile // ATT_Q_BLOCK):
                i0 = pl.multiple_of(t * q_tile + blk * ATT_Q_BLOCK, ATT_Q_BLOCK)
                ws = jnp.clip(i0 - BAND_HALF, 0, sub_len - ATT_K_WINDOW)
                variant = (i0 - ws) // BAND_HALF
                ws = pl.multiple_of(ws, BAND_HALF)
                base = res * 3 * GROUP_WIDTH
                blocks.append(dict(
                    res=res, rows=pl.ds(i0, ATT_Q_BLOCK), variant=variant,
                    q=x_ref[0, pl.ds(i0, ATT_Q_BLOCK), base:base + GROUP_WIDTH]
                    * jnp.asarray(ATT_HEAD_DIM ** -0.5, BF16),
                    k=x_ref[0, pl.ds(ws, ATT_K_WINDOW), base + GROUP_WIDTH:base + 2 * GROUP_WIDTH],
                    v=x_ref[0, pl.ds(ws, ATT_K_WINDOW), base + 2 * GROUP_WIDTH:base + 3 * GROUP_WIDTH]))
        heads = [(b, pair, a) for b in blocks for pair in range(n_pairs) for a in range(2)]
        logits = []
        for b, pair, a in heads:
            cols = slice(pair * LANES, (pair + 1) * LANES)
            q_pair = b["q"][:, cols]
            keep = low_half if a == 0 else jnp.logical_not(low_half)
            qh = jnp.where(keep, q_pair, jnp.zeros_like(q_pair))
            logits.append(_mm_nt(qh, b["k"][:, cols]))
        probs, dens, lses = [], [], []
        for (b, pair, a), s in zip(heads, logits):
            s = s + bias_ref[b["variant"], 2 * pair + a]
            m = jnp.max(s, axis=-1, keepdims=True)
            p = jnp.exp(s - m)
            den = jnp.sum(p, axis=-1, keepdims=True)
            probs.append(p.astype(BF16))
            dens.append(den)
            lses.append(m + jnp.log(den))
        pvs = [_mm(p, b["v"][:, pair * LANES:(pair + 1) * LANES]) for (b, pair, a), p in zip(heads, probs)]
        for idx in range(0, len(heads), 2):
            b, pair, _ = heads[idx]
            cols = pl.ds(b["res"] * GROUP_WIDTH + pair * LANES, LANES)
            out = jnp.where(low_half, pvs[idx] / dens[idx], pvs[idx + 1] / dens[idx + 1])
            lse = jnp.where(low_half, jnp.broadcast_to(lses[idx], (ATT_Q_BLOCK, LANES)),
                            jnp.broadcast_to(lses[idx + 1], (ATT_Q_BLOCK, LANES)))
            o_ref[0, b["rows"], cols] = out.astype(o_ref.dtype)
            lse_ref[0, b["rows"], cols] = lse
        return carry

    lax.fori_loop(0, sub_len // q_tile, tile, 0)


def _band_attn(qkv, bias, batch, seq, dil, n_res, q_tile=256):
    rows, width = qkv.shape
    sub_len = seq // dil
    qkv3 = qkv.reshape(batch, sub_len, width)
    x_spec = pl.BlockSpec((1, sub_len, n_res * 3 * GROUP_WIDTH), lambda b, r: (b, 0, r))
    o_spec = pl.BlockSpec((1, sub_len, n_res * GROUP_WIDTH), lambda b, r: (b, 0, r))
    o, lse = pl.pallas_call(
        functools.partial(_band_attn_kernel, sub_len=sub_len, n_res=n_res, q_tile=q_tile),
        grid=(batch, dil // n_res),
        in_specs=[x_spec, _const_spec(bias.shape)],
        out_specs=[o_spec, o_spec],
        out_shape=[jax.ShapeDtypeStruct((batch, sub_len, dil * GROUP_WIDTH), BF16),
                   jax.ShapeDtypeStruct((batch, sub_len, dil * GROUP_WIDTH), F32)],
        compiler_params=_params("parallel", "parallel"),
        name=f"band_attn_d{dil}",
    )(qkv3, bias)
    return o.reshape(rows, dil * GROUP_WIDTH), lse.reshape(rows, dil * GROUP_WIDTH)


def _att_proj_kernel(x_ref, g_ref, *refs, dils):
    n = len(dils)
    w_refs, wmem_ref = refs[:n], refs[n]
    out_refs, qmem_ref, stage_ref = refs[n + 1:2 * n + 1], refs[2 * n + 1], refs[2 * n + 2]
    h = _rms(x_ref[...], g_ref[...]).astype(BF16)
    tm = h.shape[0]
    for w_ref, o_ref, dil in zip(w_refs, out_refs, dils):
        y = jnp.dot(h, w_ref[...], preferred_element_type=F32)
        width = y.shape[1]
        if dil == 1:
            o_ref[...] = y.astype(BF16)
        else:
            for c in range(width // LANES):
                stage_ref[c] = y[:, c * LANES:(c + 1) * LANES]
            for r in range(dil):
                for c in range(width // LANES):
                    lanes = pl.ds(r * width + c * LANES, LANES)
                    o_ref[:, lanes] = stage_ref[c, pl.ds(r, tm // dil, stride=dil), :].astype(BF16)
    qmem_ref[...] = jnp.dot(h, wmem_ref[...], preferred_element_type=F32).astype(BF16)


def _att_proj(x2, gain, group_weights, w_mem, dils, tm=512):
    t, d = x2.shape
    width = group_weights[0].shape[1]
    row = lambda i: (i, 0)
    return pl.pallas_call(
        functools.partial(_att_proj_kernel, dils=dils),
        grid=(t // tm,),
        in_specs=[pl.BlockSpec((tm, d), row), _const_spec((1, d))]
        + [_const_spec(w.shape) for w in group_weights] + [_const_spec(w_mem.shape)],
        out_specs=[pl.BlockSpec((tm // dil, dil * width), row) for dil in dils]
        + [pl.BlockSpec((tm, w_mem.shape[1]), row)],
        out_shape=[jax.ShapeDtypeStruct((t // dil, dil * width), BF16) for dil in dils]
        + [jax.ShapeDtypeStruct((t, w_mem.shape[1]), BF16)],
        scratch_shapes=[pltpu.VMEM((width // LANES, tm, LANES), F32)],
        compiler_params=_params("parallel"),
        name="att_proj",
    )(x2, gain.reshape(1, d), *group_weights, w_mem)


def _att_out_kernel(*refs, dils):
    n = len(dils)
    o_refs, l_refs = refs[:n], refs[n:2 * n]
    qmem_ref, kmem_ref, vmem_ref, x_ref, w_ref, g_ref, out_ref, stage_ref = refs[2 * n:]
    tm = x_ref.shape[0]

    def token_major(ref, dil, slot):
        if dil == 1:
            return ref[...].astype(F32)
        tiles = GROUP_WIDTH // LANES
        for r in range(dil):
            for c in range(tiles):
                lanes = pl.ds(r * GROUP_WIDTH + c * LANES, LANES)
                stage_ref[slot * tiles + c, pl.ds(r, tm // dil, stride=dil), :] = ref[:, lanes].astype(F32)
        return jnp.concatenate([stage_ref[slot * tiles + c] for c in range(tiles)], axis=-1)

    lses = [token_major(ref, dil, 2 * i) for i, (ref, dil) in enumerate(zip(l_refs, dils))]
    outs = [token_major(ref, dil, 2 * i + 1) for i, (ref, dil) in enumerate(zip(o_refs, dils))]
    mx = functools.reduce(jnp.maximum, lses)
    es = [jnp.exp(l - mx) for l in lses]
    tot = functools.reduce(lambda a, b: a + b, es)
    parts = [(o * (e / tot)).astype(BF16) for o, e in zip(outs, es)]
    mem_out = _memory_attention(qmem_ref[...], kmem_ref[0], vmem_ref[0])
    mixed = jnp.concatenate(parts + [mem_out], axis=-1)
    y = jnp.dot(mixed, w_ref[...], preferred_element_type=F32)
    out_ref[...] = x_ref[...] + _rms(y, g_ref[...])


def _att_out(os_, lses, dils, q_mem, k_mem, v_mem, x2, w_out, gain, seq, tm=1024):
    t, d = x2.shape
    row = lambda i: (i, 0)
    group_specs = [pl.BlockSpec((tm // dil, dil * GROUP_WIDTH), row) for dil in dils]
    return pl.pallas_call(
        functools.partial(_att_out_kernel, dils=dils),
        grid=(t // tm,),
        in_specs=group_specs + group_specs + _memory_specs(q_mem, k_mem, seq, tm)
        + [pl.BlockSpec((tm, d), row), _const_spec(w_out.shape), _const_spec((1, d))],
        out_specs=pl.BlockSpec((tm, d), row),
        out_shape=jax.ShapeDtypeStruct((t, d), F32),
        scratch_shapes=[pltpu.VMEM((2 * len(dils) * (GROUP_WIDTH // LANES), tm, LANES), F32)],
        compiler_params=_params("parallel"),
        name="att_out",
    )(*os_, *lses, q_mem, k_mem, v_mem, x2, w_out, gain.reshape(1, d))


def _ffn_kernel(x_ref, gpre_ref, wg_ref, wu_ref, wd_ref, gpost_ref, out_ref, *, row_blocks):
    rows = x_ref.shape[0] // row_blocks
    for blk in range(row_blocks):
        sl = slice(blk * rows, (blk + 1) * rows)
        x = x_ref[sl, :]
        h = _rms(x, gpre_ref[...]).astype(BF16)
        gate = jnp.dot(h, wg_ref[...], preferred_element_type=F32)
        up = jnp.dot(h, wu_ref[...], preferred_element_type=F32)
        act = (_silu(gate) * up).astype(BF16)
        y = jnp.dot(act, wd_ref[...], preferred_element_type=F32)
        out_ref[sl, :] = x + _rms(y, gpost_ref[...])


def _ffn(x2, g_pre, w_gate_up, w_down, g_post, tm=512, row_blocks=2):
    t, d = x2.shape
    d_ff = w_down.shape[0]
    row = lambda i: (i, 0)
    return pl.pallas_call(
        functools.partial(_ffn_kernel, row_blocks=row_blocks),
        grid=(t // tm,),
        in_specs=[pl.BlockSpec((tm, d), row), _const_spec((1, d)),
                  pl.BlockSpec((d, d_ff), lambda i: (0, 0)), pl.BlockSpec((d, d_ff), lambda i: (0, 1)),
                  _const_spec(w_down.shape), _const_spec((1, d))],
        out_specs=pl.BlockSpec((tm, d), row),
        out_shape=jax.ShapeDtypeStruct((t, d), F32),
        compiler_params=_params("parallel"),
        name="ffn",
    )(x2, g_pre.reshape(1, d), w_gate_up, w_gate_up, w_down, g_post.reshape(1, d))


def _dn_proj_kernel(x_ref, xprev_ref, xnext_ref, g_ref, wqkv_ref, wz_ref, wgate_ref, wmem_ref, conv_ref,
                    alog_ref, dtb_ref, q_ref, k_ref, v_ref, z_ref, qmem_ref, gb_ref, gbt_ref, ext_ref,
                    *, halo, n_dir_heads):
    i = pl.program_id(1)
    last = pl.num_programs(1) - 1
    tm = x_ref.shape[0]
    width = wqkv_ref.shape[1]
    tok = width // 3
    gain = g_ref[...]
    h = _rms(x_ref[...], gain).astype(BF16)
    h_prev = jnp.where(i > 0, _rms(xprev_ref[...], gain), 0.0).astype(BF16)
    h_next = jnp.where(i < last, _rms(xnext_ref[...], gain), 0.0).astype(BF16)
    h_ext = jnp.concatenate([h_prev, h, h_next], axis=0)
    chunk = 2 * LANES
    for c0 in range(0, width, chunk):
        y_ext = jnp.dot(h_ext, wqkv_ref[:, c0:c0 + chunk], preferred_element_type=F32)
        for half in range(chunk // LANES):
            c1 = c0 + half * LANES
            cols = slice(c1, c1 + LANES)
            ext = ext_ref.at[(c1 // LANES) % ext_ref.shape[0]]
            ext[...] = y_ext[:, half * LANES:(half + 1) * LANES]
            taps = [conv_ref[j:j + 1, cols] for j in range(DN_CONV)]
            which, col = divmod(c1, tok)
            out_ref = (q_ref, k_ref, v_ref)[which]
            for r0 in range(0, tm, CONV_ROWS):
                first = halo + r0 - DN_CONV // 2
                acc = ext[first:first + CONV_ROWS, :] * taps[0]
                for j in range(1, DN_CONV):
                    acc = acc + ext[first + j:first + j + CONV_ROWS, :] * taps[j]
                y = _silu(acc)
                if which < 2:
                    scale = lax.rsqrt(jnp.sum(y * y, axis=-1, keepdims=True) + EPS)
                    y = y * (scale * (DN_HEAD_DIM ** -0.5) if which == 0 else scale)
                out_ref[r0:r0 + CONV_ROWS, col:col + LANES] = y.astype(BF16)
    z_ref[...] = jnp.dot(h, wz_ref[...], preferred_element_type=F32).astype(BF16)
    qmem_ref[...] = jnp.dot(h, wmem_ref[...], preferred_element_type=F32).astype(BF16)
    gate = jnp.dot(h, wgate_ref[...], preferred_element_type=F32)
    shifted = gate + dtb_ref[...]
    softplus = jnp.maximum(shifted, 0.0) + jnp.log(1.0 + jnp.exp(-jnp.abs(shifted)))
    decay = -jnp.exp(alog_ref[...]) * softplus
    beta = 1.0 / (1.0 + jnp.exp(-gate))
    lane = lax.broadcasted_iota(jnp.int32, gate.shape, 1)
    gb = jnp.where(lane < n_dir_heads, decay, beta)
    gb_ref[...] = gb
    gbt_ref[...] = jnp.transpose(gb)[:gbt_ref.shape[0], :]


def _dn_proj(x2, gain, w_in, w_gate, w_mem, conv_w, a_log_row, dt_bias_row, batch, seq, tok, n_dir_heads,
             tm=512, halo=16):
    t, d = x2.shape
    n_gate_rows = 8 * ((2 * n_dir_heads + 7) // 8)
    per_seq = seq // tm
    blocks_per_tile = tm // halo
    n_halo_blocks = t // halo
    mem_width = w_mem.shape[1]

    def cur_map(b, i):
        return (b * per_seq + i, 0)

    def prev_map(b, i):
        return (jnp.maximum((b * per_seq + i) * blocks_per_tile - 1, 0), 0)

    def next_map(b, i):
        return (jnp.minimum((b * per_seq + i + 1) * blocks_per_tile, n_halo_blocks - 1), 0)

    return pl.pallas_call(
        functools.partial(_dn_proj_kernel, halo=halo, n_dir_heads=n_dir_heads),
        grid=(batch, per_seq),
        in_specs=[pl.BlockSpec((tm, d), cur_map), pl.BlockSpec((halo, d), prev_map),
                  pl.BlockSpec((halo, d), next_map), _const_spec((1, d)),
                  pl.BlockSpec((d, 3 * tok), lambda b, i: (0, 0)), pl.BlockSpec((d, tok), lambda b, i: (0, 3)),
                  _const_spec(w_gate.shape), _const_spec(w_mem.shape), _const_spec(conv_w.shape),
                  _const_spec((1, LANES)), _const_spec((1, LANES))],
        out_specs=[pl.BlockSpec((tm, tok), cur_map)] * 4
        + [pl.BlockSpec((tm, mem_width), cur_map), pl.BlockSpec((tm, LANES), cur_map),
           pl.BlockSpec((n_gate_rows, tm), lambda b, i: (0, b * per_seq + i))],
        out_shape=[jax.ShapeDtypeStruct((t, tok), BF16)] * 4
        + [jax.ShapeDtypeStruct((t, mem_width), BF16), jax.ShapeDtypeStruct((t, LANES), F32),
           jax.ShapeDtypeStruct((n_gate_rows, t), F32)],
        scratch_shapes=[pltpu.VMEM((4, tm + 2 * halo, LANES), F32)],
        compiler_params=_params("parallel", "arbitrary"),
        name="dn_proj",
    )(x2, x2, x2, gain.reshape(1, d), w_in, w_in, w_gate, w_mem, conv_w, a_log_row, dt_bias_row)


def _mm(a, b):
    return jnp.dot(a, b, preferred_element_type=F32)


def _mm_nt(a, b):
    return lax.dot_general(a, b, (((1,), (1,)), ((), ())), preferred_element_type=F32)


def _mm_tn(a, b):
    return lax.dot_general(a, b, (((0,), (0,)), ((), ())), preferred_element_type=F32)


def _row_blocks(x, size, parity):
    return jnp.concatenate([x[b * size:(b + 1) * size] for b in range(parity, x.shape[0] // size, 2)], axis=0)


def _interleave_row_blocks(even, odd, size):
    pieces = []
    for b in range(even.shape[0] // size):
        pieces += [even[b * size:(b + 1) * size], odd[b * size:(b + 1) * size]]
    return jnp.concatenate(pieces, axis=0)


def _unit_triangular_inverses(neg_ls, uppers, same_block):
    n = neg_ls[0].shape[0]
    ii = lax.broadcasted_iota(jnp.int32, (n, n), 0)
    jj = lax.broadcasted_iota(jnp.int32, (n, n), 1)
    identity = (ii == jj).astype(F32).astype(BF16)
    zero = jnp.zeros((n, n), BF16)
    plus_identity = lambda m: m + identity
    leaf = same_block[SOLVE_LEAF].astype(F32).astype(BF16)
    xbs = [nl * leaf for nl in neg_ls]
    invs = [plus_identity(xb) for xb in xbs]
    power = 1
    while 2 * power < SOLVE_LEAF:
        xbs = [_mm(xb, xb).astype(BF16) for xb in xbs]
        invs = [_mm(plus_identity(xb), inv).astype(BF16) for xb, inv in zip(xbs, invs)]
        power *= 2
    size = SOLVE_LEAF
    while size < n:
        merge = (same_block[2 * size] & jnp.logical_not(same_block[size])).astype(F32).astype(BF16)
        hot = [0 if upper else 1 for upper in uppers]
        offs = [_row_blocks(nl, size, p) * _row_blocks(merge, size, p) for nl, p in zip(neg_ls, hot)]
        rights = [_mm(off, inv).astype(BF16) for off, inv in zip(offs, invs)]
        half_zero = zero[:n // 2]
        rights = [_interleave_row_blocks(*((r, half_zero) if p == 0 else (half_zero, r)), size)
                  for r, p in zip(rights, hot)]
        moved = [_mm(_row_blocks(inv, size, p), plus_identity(right)).astype(BF16)
                 for inv, right, p in zip(invs, rights, hot)]
        invs = [_interleave_row_blocks(*((m, _row_blocks(inv, size, 1)) if p == 0 else
                                         (_row_blocks(inv, size, 0), m)), size)
                for inv, m, p in zip(invs, moved, hot)]
        size *= 2
    return invs


def _exact_prefix_sums(x, tri, rows):
    hi = x.astype(BF16).astype(F32)
    mid = (x - hi).astype(BF16).astype(F32)
    lo = (x - hi - mid).astype(BF16).astype(F32)
    if rows:
        n = x.shape[0]
        res = _mm(jnp.concatenate([hi, mid, lo], axis=0).astype(BF16), tri)
        return res[:n] + res[n:2 * n] + res[2 * n:]
    quarter = LANES // 4
    packed = hi + pltpu.roll(mid, quarter, 1) + pltpu.roll(lo, 2 * quarter, 1)
    res = _mm(tri, packed.astype(BF16))
    return res + pltpu.roll(res, LANES - quarter, 1) + pltpu.roll(res, LANES - 2 * quarter, 1)


def _dn_core_kernel(qf_ref, kf_ref, vf_ref, gbf_ref, gtf_ref, qb_ref, kb_ref, vb_ref, gbb_ref, gtb_ref,
                    of_ref, ob_ref, state_ref, *, n_heads):
    @pl.when(pl.program_id(1) == 0)
    def _():
        state_ref[...] = jnp.zeros_like(state_ref)

    c = DN_CHUNK
    d = DN_HEAD_DIM
    n_sub = qf_ref.shape[0] // c
    ii = lax.broadcasted_iota(jnp.int32, (c, c), 0)
    jj = lax.broadcasted_iota(jnp.int32, (c, c), 1)
    lower_eq, upper_eq = jj <= ii, jj >= ii
    lower, upper = jj < ii, jj > ii
    same_block = {}
    size = SOLVE_LEAF
    while size <= c:
        shift = int(math.log2(size))
        same_block[size] = (ii >> shift) == (jj >> shift)
        size *= 2

    items = []
    for direction, (q_ref, k_ref, v_ref, gb_ref, gt_ref, o_ref) in enumerate(
            ((qf_ref, kf_ref, vf_ref, gbf_ref, gtf_ref, of_ref),
             (qb_ref, kb_ref, vb_ref, gbb_ref, gtb_ref, ob_ref))):
        reverse = direction == 1
        tri_col = (upper_eq if reverse else lower_eq).astype(BF16)
        tri_row = (lower_eq if reverse else upper_eq).astype(BF16)
        for sub in range(n_sub):
            rows = slice(sub * c, (sub + 1) * c)
            gb = gb_ref[rows, :]
            lane = lax.broadcasted_iota(jnp.int32, gb.shape, 1)
            cum_col = _exact_prefix_sums(jnp.where(lane < LANES // 4, gb, 0.0), tri_col, rows=False)
            cum_row = _exact_prefix_sums(gt_ref[:, rows], tri_row, rows=True)
            for head in range(n_heads):
                g_idx = direction * n_heads + head
                b_idx = 2 * n_heads + g_idx
                cols = slice(head * d, (head + 1) * d)
                items.append(dict(
                    direction=direction, head=head, sub=sub, o_ref=o_ref, rows=rows, cols=cols, reverse=reverse,
                    earlier_eq=upper_eq if reverse else lower_eq,
                    earlier=upper if reverse else lower,
                    q=q_ref[rows, cols], k=k_ref[rows, cols], v=v_ref[rows, cols],
                    g_col=cum_col[:, g_idx:g_idx + 1], beta_col=gb[:, b_idx:b_idx + 1],
                    g_row=cum_row[g_idx:g_idx + 1, :],
                    g_last=cum_col[0:1, g_idx:g_idx + 1] if reverse else cum_col[c - 1:c, g_idx:g_idx + 1]))

    grams = [_mm_nt(jnp.concatenate([it["k"], it["q"]], axis=0), it["k"]) for it in items]
    neg_ls, rhss = [], []
    for it, gram in zip(items, grams):
        g_wide = jnp.broadcast_to(it["g_col"], (c, d))
        beta_wide = jnp.broadcast_to(it["beta_col"], (c, d))
        it["g_last_wide"] = jnp.broadcast_to(it["g_last"], (c, d))
        decay = jnp.exp(jnp.where(it["earlier_eq"], g_wide - it["g_row"], NEG_INF))
        neg_ls.append((jnp.where(it["earlier"], gram[:c] * decay, 0.0) * (-beta_wide)).astype(BF16))
        it["intra"] = jnp.where(it["earlier_eq"], gram[c:] * decay, 0.0).astype(BF16)
        kf = it["k"].astype(F32)
        e_g = jnp.exp(g_wide)
        rhss.append(jnp.concatenate([(it["v"].astype(F32) * beta_wide).astype(BF16),
                                     (kf * (beta_wide * e_g)).astype(BF16)], axis=-1))
        it["qg"] = (it["q"].astype(F32) * e_g).astype(BF16)
        it["k_dec"] = (kf * jnp.exp(it["g_last_wide"] - g_wide)).astype(BF16)
    invs = _unit_triangular_inverses(neg_ls, [it["reverse"] for it in items], same_block)
    for it, inv, rhs in zip(items, invs, rhss):
        sol = _mm(inv, rhs)
        it["u"], it["w"] = sol[:, :d], sol[:, d:].astype(BF16)

    for step in range(n_sub):
        live = [it for it in items if it["sub"] == (n_sub - 1 - step if it["reverse"] else step)]
        states = [state_ref[it["direction"], it["head"]] for it in live]
        wss = [_mm(jnp.concatenate([it["w"], it["qg"]], axis=0), state.astype(BF16))
               for it, state in zip(live, states)]
        v_news = [(it["u"] - ws[:c]).astype(BF16) for it, ws in zip(live, wss)]
        outs = [ws[c:] + _mm(it["intra"], v_new) for it, ws, v_new in zip(live, wss, v_news)]
        updates = [_mm_tn(it["k_dec"], v_new) for it, v_new in zip(live, v_news)]
        for it, out, state, update in zip(live, outs, states, updates):
            it["o_ref"][it["rows"], it["cols"]] = out.astype(BF16)
            state_ref[it["direction"], it["head"]] = state * jnp.exp(it["g_last_wide"]) + update


def _dn_core(q, k, v, gb, gbt, batch, seq, chunks_per_step=2):
    t, tok = q.shape
    n_heads = tok // DN_HEAD_DIM
    rows = chunks_per_step * DN_CHUNK
    steps = seq // rows
    n_rows = gbt.shape[0]

    fwd = lambda b, i: (b * steps + i, 0)
    bwd = lambda b, i: (b * steps + steps - 1 - i, 0)
    fwd_t = lambda b, i: (0, b * steps + i)
    bwd_t = lambda b, i: (0, b * steps + steps - 1 - i)

    def side(tok_map, t_map):
        return [pl.BlockSpec((rows, tok), tok_map)] * 3 + [
            pl.BlockSpec((rows, LANES), tok_map), pl.BlockSpec((n_rows, rows), t_map)]

    return pl.pallas_call(
        functools.partial(_dn_core_kernel, n_heads=n_heads),
        grid=(batch, steps),
        in_specs=side(fwd, fwd_t) + side(bwd, bwd_t),
        out_specs=[pl.BlockSpec((rows, tok), fwd), pl.BlockSpec((rows, tok), bwd)],
        out_shape=[jax.ShapeDtypeStruct((t, tok), BF16)] * 2,
        scratch_shapes=[pltpu.VMEM((2, n_heads, DN_HEAD_DIM, DN_HEAD_DIM), F32)],
        compiler_params=_params("parallel", "arbitrary"),
        name="dn_core",
    )(q, k, v, gb, gbt, q, k, v, gb, gbt)


def _dn_out_kernel(of_ref, ob_ref, z_ref, qmem_ref, kmem_ref, vmem_ref, x_ref, onorm_ref, w_ref, g_ref,
                   out_ref):
    tok = of_ref.shape[1]
    parts = []
    for c0 in range(0, tok, DN_HEAD_DIM):
        cols = slice(c0, c0 + DN_HEAD_DIM)
        o = of_ref[:, cols].astype(F32) + ob_ref[:, cols].astype(F32)
        o = o * lax.rsqrt(jnp.mean(o * o, axis=-1, keepdims=True) + EPS) * onorm_ref[...]
        parts.append((o * _silu(z_ref[:, cols].astype(F32))).astype(BF16))
    mem_out = _memory_attention(qmem_ref[...], kmem_ref[0], vmem_ref[0])
    mixed = jnp.concatenate(parts + [mem_out], axis=-1)
    y = jnp.dot(mixed, w_ref[...], preferred_element_type=F32)
    out_ref[...] = x_ref[...] + _rms(y, g_ref[...])


def _dn_out(o_f, o_b, z, q_mem, k_mem, v_mem, x2, out_norm, w_out, gain, seq, tm=1024):
    t, d = x2.shape
    tok = o_f.shape[1]
    row = lambda i: (i, 0)
    return pl.pallas_call(
        _dn_out_kernel,
        grid=(t // tm,),
        in_specs=[pl.BlockSpec((tm, tok), row)] * 3 + _memory_specs(q_mem, k_mem, seq, tm)
        + [pl.BlockSpec((tm, d), row), _const_spec((1, DN_HEAD_DIM)), _const_spec(w_out.shape),
           _const_spec((1, d))],
        out_specs=pl.BlockSpec((tm, d), row),
        out_shape=jax.ShapeDtypeStruct((t, d), F32),
        compiler_params=_params("parallel"),
        name="dn_out",
    )(o_f, o_b, z, q_mem, k_mem, v_mem, x2, out_norm.reshape(1, DN_HEAD_DIM), w_out, gain.reshape(1, d))


def _attention_layer(x2, mem, batch, seq, rel_bias, w_in, w_out, mem_gain, w_kv, g_pre, g_post):
    n_groups = len(DILATED_GROUPS)
    tok = n_groups * GROUP_WIDTH
    dils = tuple(dil for _, dil in DILATED_GROUPS)
    w_qkv = w_in[:, :3 * tok].reshape(-1, 3, n_groups, GROUP_WIDTH).transpose(0, 2, 1, 3).astype(BF16)
    group_weights = [w_qkv[:, g].reshape(-1, 3 * GROUP_WIDTH) for g in range(n_groups)]
    *qkvs, q_mem = _att_proj(x2, g_pre, group_weights, w_in[:, 3 * tok:].astype(BF16), dils)
    os_, lses = [], []
    for group, dil in enumerate(dils):
        n_res = min(dil, 2)
        o, lse = _band_attn(qkvs[group], _band_bias(rel_bias, group, dil), batch, seq, dil, n_res,
                            q_tile=512 // n_res)
        os_.append(o)
        lses.append(lse)
    k_mem, v_mem = _mem_kv(mem, mem_gain, w_kv.astype(BF16))
    return _att_out(os_, lses, dils, q_mem, k_mem, v_mem, x2, w_out.astype(BF16), g_post, seq)


def _deltanet_layer(x2, mem, batch, seq, w_in, conv_w, a_log, dt_bias, out_norm, w_out, mem_gain, w_kv,
                    g_pre, g_post):
    d = x2.shape[1]
    n_heads = a_log.shape[1]
    tok = n_heads * DN_HEAD_DIM
    n_gate = 4 * n_heads
    w_gate = w_in[:, 4 * tok:4 * tok + n_gate].reshape(d, 2, 2, n_heads).transpose(0, 2, 1, 3).reshape(d, n_gate)
    w_gate = jnp.pad(w_gate, ((0, 0), (0, LANES - n_gate)))
    w_in_b = w_in.astype(BF16)
    pad_row = lambda p: jnp.pad(p.reshape(1, -1).astype(F32), ((0, 0), (0, LANES - 2 * n_heads)))
    q, k, v, z, q_mem, gb, gbt = _dn_proj(x2, g_pre, w_in_b, w_gate.astype(BF16), w_in_b[:, 4 * tok + n_gate:],
                                          conv_w.astype(F32), pad_row(a_log), pad_row(dt_bias), batch, seq, tok,
                                          2 * n_heads)
    o_f, o_b = _dn_core(q, k, v, gb, gbt, batch, seq)
    k_mem, v_mem = _mem_kv(mem, mem_gain, w_kv.astype(BF16))
    return _dn_out(o_f, o_b, z, q_mem, k_mem, v_mem, x2, out_norm, w_out.astype(BF16), g_post, seq)


def kernel(x, mem, rel_bias, att_w_in, att_w_out, dn_w_in, dn_conv, dn_a_log, dn_dt_bias, dn_out_norm,
           dn_w_out, mem_norm, mem_w_kv, norm_mix_pre, norm_mix_post, norm_ffn_pre, norm_ffn_post,
           ffn_w_gate_up, ffn_w_down):
    batch, seq, d = x.shape
    depth = norm_mix_pre.shape[0]
    n_mixers = 2
    x2 = x.reshape(batch * seq, d)
    for i in range(depth):
        j = i // n_mixers
        if i % n_mixers == 0:
            x2 = _attention_layer(x2, mem, batch, seq, rel_bias, att_w_in[j], att_w_out[j], mem_norm[i],
                                  mem_w_kv[i], norm_mix_pre[i], norm_mix_post[i])
        else:
            x2 = _deltanet_layer(x2, mem, batch, seq, dn_w_in[j], dn_conv[j], dn_a_log[j], dn_dt_bias[j],
                                 dn_out_norm[j], dn_w_out[j], mem_norm[i], mem_w_kv[i], norm_mix_pre[i],
                                 norm_mix_post[i])
        x2 = _ffn(x2, norm_ffn_pre[i], ffn_w_gate_up[i].astype(BF16), ffn_w_down[i].astype(BF16),
                  norm_ffn_post[i])
    return x2.reshape(batch, seq, d)
```

```python
import functools
import math

import jax
import jax.numpy as jnp
import numpy as np
from jax import lax
from jax.experimental import pallas as pl
from jax.experimental.pallas import tpu as pltpu

F32 = jnp.float32
BF16 = jnp.bfloat16

EPS = 1e-6
NEG_INF = -1e30

LANES = 128
VMEM_LIMIT_BYTES = 56 * 1024 * 1024

DILATED_GROUPS = ((128, 1), (512, 4), (2048, 16))
ATT_HEAD_DIM = 64
HEADS_PER_GROUP = 4
GROUP_WIDTH = HEADS_PER_GROUP * ATT_HEAD_DIM
BAND_HALF = 64
REL_BUCKETS = 32
REL_MAX_DIST = 1024
MEM_HEADS = 4
MEM_HEAD_DIM = 64
DN_HEAD_DIM = 128
DN_CONV = 5
DN_CHUNK = 128
SOLVE_LEAF = 16
CONV_ROWS = 64

ATT_Q_BLOCK = 128
ATT_K_WINDOW = ATT_Q_BLOCK + 2 * BAND_HALF


def _params(*semantics):
    return pltpu.CompilerParams(dimension_semantics=semantics, vmem_limit_bytes=VMEM_LIMIT_BYTES)


def _rms(x, gain):
    return x * lax.rsqrt(jnp.mean(x * x, axis=-1, keepdims=True) + EPS) * gain


def _silu(x):
    return x / (1.0 + jnp.exp2(x * (-1.0 / math.log(2.0))))


def _const_spec(shape):
    return pl.BlockSpec(shape, lambda *_: (0,) * len(shape))


def _mem_kv_kernel(mem_ref, g_ref, w_ref, k_ref, v_ref):
    h = _rms(mem_ref[0], g_ref[...]).astype(BF16)
    kv = jnp.dot(h, w_ref[...], preferred_element_type=F32)
    width = k_ref.shape[-1]
    k_ref[0] = kv[:, :width].astype(BF16)
    v_ref[0] = kv[:, width:].astype(BF16)


def _mem_kv(mem, gain, w_kv):
    b, m, d = mem.shape
    width = w_kv.shape[1] // 2
    return pl.pallas_call(
        _mem_kv_kernel,
        grid=(b,),
        in_specs=[pl.BlockSpec((1, m, d), lambda i: (i, 0, 0)), _const_spec((1, d)), _const_spec(w_kv.shape)],
        out_specs=[pl.BlockSpec((1, m, width), lambda i: (i, 0, 0))] * 2,
        out_shape=[jax.ShapeDtypeStruct((b, m, width), BF16)] * 2,
        compiler_params=_params("parallel"),
        name="mem_kv",
    )(mem, gain.reshape(1, d), w_kv)


def _memory_attention(q, k, v):
    lane = lax.broadcasted_iota(jnp.int32, q.shape, 1)
    in_head = [(lane >= h * MEM_HEAD_DIM) & (lane < (h + 1) * MEM_HEAD_DIM) for h in range(MEM_HEADS)]
    logits = [lax.dot_general(jnp.where(mask, q, jnp.zeros_like(q)), k, (((1,), (1,)), ((), ())),
                              preferred_element_type=F32) for mask in in_head]
    probs = []
    for s in logits:
        s = s * (MEM_HEAD_DIM ** -0.5)
        p = jnp.exp(s - jnp.max(s, axis=-1, keepdims=True))
        probs.append((p / jnp.sum(p, axis=-1, keepdims=True)).astype(BF16))
    pvs = [jnp.dot(p, v, preferred_element_type=F32) for p in probs]
    out = pvs[-1]
    for mask, pv in zip(in_head[:-1], pvs[:-1]):
        out = jnp.where(mask, pv, out)
    return out.astype(BF16)


def _memory_specs(q_mem, k_mem, seq, tm):
    per_seq = seq // tm
    kv_spec = pl.BlockSpec((1,) + k_mem.shape[1:], lambda i: (i // per_seq, 0, 0))
    return [pl.BlockSpec((tm, q_mem.shape[1]), lambda i: (i, 0)), kv_spec, kv_spec]


def _t5_bucket(rel):
    half = REL_BUCKETS // 2
    max_exact = half // 2
    n = np.abs(rel)
    large = max_exact + (np.log(np.maximum(n, 1) / max_exact) / math.log(REL_MAX_DIST / max_exact)
                         * (half - max_exact)).astype(np.int64)
    large = np.minimum(large, half - 1)
    return ((rel > 0) * half + np.where(n < max_exact, n, large)).astype(np.int32)


def _band_bias(rel_bias, group, dil):
    period = ATT_Q_BLOCK + ATT_K_WINDOW
    heads = slice(group * HEADS_PER_GROUP, (group + 1) * HEADS_PER_GROUP)
    tables = []
    for variant in range(3):
        rel = np.arange(period) - (ATT_Q_BLOCK - 1) - variant * BAND_HALF
        in_band = np.abs(rel) <= BAND_HALF
        bucket = _t5_bucket(np.clip(rel, -BAND_HALF, BAND_HALF) * dil)
        f = jnp.where(in_band[None, :], jnp.transpose(rel_bias[bucket][:, heads]).astype(F32), NEG_INF)
        flat = jnp.tile(f, (1, ATT_Q_BLOCK))[:, :ATT_Q_BLOCK * (period - 1)]
        skew = flat.reshape(HEADS_PER_GROUP, ATT_Q_BLOCK, period - 1)
        tables.append(skew[:, :, ATT_Q_BLOCK - 1:ATT_Q_BLOCK - 1 + ATT_K_WINDOW])
    return jnp.stack(tables)


def _band_attn_kernel(x_ref, bias_ref, o_ref, lse_ref, *, sub_len, n_res, q_tile):
    lane = lax.broadcasted_iota(jnp.int32, (ATT_Q_BLOCK, LANES), 1)
    low_half = lane < ATT_HEAD_DIM
    n_pairs = GROUP_WIDTH // LANES

    def tile(t, carry):
        blocks = []
        for res in range(n_res):
            for blk in range(q_tile // ATT_Q_BLOCK):
                i0 = pl.multiple_of(t * q_tile + blk * ATT_Q_BLOCK, ATT_Q_BLOCK)
                ws = jnp.clip(i0 - BAND_HALF, 0, sub_len - ATT_K_WINDOW)
                variant = (i0 - ws) // BAND_HALF
                ws = pl.multiple_of(ws, BAND_HALF)
                base = res * 3 * GROUP_WIDTH
                blocks.append(dict(
                    res=res, rows=pl.ds(i0, ATT_Q_BLOCK), variant=variant,
                    q=x_ref[0, pl.ds(i0, ATT_Q_BLOCK), base:base + GROUP_WIDTH]
                    * jnp.asarray(ATT_HEAD_DIM ** -0.5, BF16),
                    k=x_ref[0, pl.ds(ws, ATT_K_WINDOW), base + GROUP_WIDTH:base + 2 * GROUP_WIDTH],
                    v=x_ref[0, pl.ds(ws, ATT_K_WINDOW), base + 2 * GROUP_WIDTH:base + 3 * GROUP_WIDTH]))
        heads = [(b, pair, a) for b in blocks for pair in range(n_pairs) for a in range(2)]
        logits = []
        for b, pair, a in heads:
            cols = slice(pair * LANES, (pair + 1) * LANES)
            q_pair = b["q"][:, cols]
            keep = low_half if a == 0 else jnp.logical_not(low_half)
            qh = jnp.where(keep, q_pair, jnp.zeros_like(q_pair))
            logits.append(_mm_nt(qh, b["k"][:, cols]))
        probs, dens, lses = [], [], []
        for (b, pair, a), s in zip(heads, logits):
            s = s + bias_ref[b["variant"], 2 * pair + a]
            m = jnp.max(s, axis=-1, keepdims=True)
            p = jnp.exp(s - m)
            den = jnp.sum(p, axis=-1, keepdims=True)
            probs.append(p.astype(BF16))
            dens.append(den)
            lses.append(m + jnp.log(den))
        pvs = [_mm(p, b["v"][:, pair * LANES:(pair + 1) * LANES]) for (b, pair, a), p in zip(heads, probs)]
        for idx in range(0, len(heads), 2):
            b, pair, _ = heads[idx]
            cols = pl.ds(b["res"] * GROUP_WIDTH + pair * LANES, LANES)
            out = jnp.where(low_half, pvs[idx] / dens[idx], pvs[idx + 1] / dens[idx + 1])
            lse = jnp.where(low_half, jnp.broadcast_to(lses[idx], (ATT_Q_BLOCK, LANES)),
                            jnp.broadcast_to(lses[idx + 1], (ATT_Q_BLOCK, LANES)))
            o_ref[0, b["rows"], cols] = out.astype(o_ref.dtype)
            lse_ref[0, b["rows"], cols] = lse
        return carry

    lax.fori_loop(0, sub_len // q_tile, tile, 0)


def _band_attn(qkv, bias, batch, seq, dil, n_res, q_tile=256):
    rows, width = qkv.shape
    sub_len = seq // dil
    qkv3 = qkv.reshape(batch, sub_len, width)
    x_spec = pl.BlockSpec((1, sub_len, n_res * 3 * GROUP_WIDTH), lambda b, r: (b, 0, r))
    o_spec = pl.BlockSpec((1, sub_len, n_res * GROUP_WIDTH), lambda b, r: (b, 0, r))
    o, lse = pl.pallas_call(
        functools.partial(_band_attn_kernel, sub_len=sub_len, n_res=n_res, q_tile=q_tile),
        grid=(batch, dil // n_res),
        in_specs=[x_spec, _const_spec(bias.shape)],
        out_specs=[o_spec, o_spec],
        out_shape=[jax.ShapeDtypeStruct((batch, sub_len, dil * GROUP_WIDTH), BF16),
                   jax.ShapeDtypeStruct((batch, sub_len, dil * GROUP_WIDTH), F32)],
        compiler_params=_params("parallel", "parallel"),
        name=f"band_attn_d{dil}",
    )(qkv3, bias)
    return o.reshape(rows, dil * GROUP_WIDTH), lse.reshape(rows, dil * GROUP_WIDTH)


def _att_proj_kernel(x_ref, g_ref, *refs, dils):
    n = len(dils)
    w_refs, wmem_ref = refs[:n], refs[n]
    out_refs, qmem_ref, stage_ref = refs[n + 1:2 * n + 1], refs[2 * n + 1], refs[2 * n + 2]
    h = _rms(x_ref[...], g_ref[...]).astype(BF16)
    tm = h.shape[0]
    for w_ref, o_ref, dil in zip(w_refs, out_refs, dils):
        y = jnp.dot(h, w_ref[...], preferred_element_type=F32)
        width = y.shape[1]
        if dil == 1:
            o_ref[...] = y.astype(BF16)
        else:
            for c in range(width // LANES):
                stage_ref[c] = y[:, c * LANES:(c + 1) * LANES]
            for r in range(dil):
                for c in range(width // LANES):
                    lanes = pl.ds(r * width + c * LANES, LANES)
                    o_ref[:, lanes] = stage_ref[c, pl.ds(r, tm // dil, stride=dil), :].astype(BF16)
    qmem_ref[...] = jnp.dot(h, wmem_ref[...], preferred_element_type=F32).astype(BF16)


def _att_proj(x2, gain, group_weights, w_mem, dils, tm=512):
    t, d = x2.shape
    width = group_weights[0].shape[1]
    row = lambda i: (i, 0)
    return pl.pallas_call(
        functools.partial(_att_proj_kernel, dils=dils),
        grid=(t // tm,),
        in_specs=[pl.BlockSpec((tm, d), row), _const_spec((1, d))]
        + [_const_spec(w.shape) for w in group_weights] + [_const_spec(w_mem.shape)],
        out_specs=[pl.BlockSpec((tm // dil, dil * width), row) for dil in dils]
        + [pl.BlockSpec((tm, w_mem.shape[1]), row)],
        out_shape=[jax.ShapeDtypeStruct((t // dil, dil * width), BF16) for dil in dils]
        + [jax.ShapeDtypeStruct((t, w_mem.shape[1]), BF16)],
        scratch_shapes=[pltpu.VMEM((width // LANES, tm, LANES), F32)],
        compiler_params=_params("parallel"),
        name="att_proj",
    )(x2, gain.reshape(1, d), *group_weights, w_mem)


def _att_out_kernel(*refs, dils):
    n = len(dils)
    o_refs, l_refs = refs[:n], refs[n:2 * n]
    qmem_ref, kmem_ref, vmem_ref, x_ref, w_ref, g_ref, out_ref, stage_ref = refs[2 * n:]
    tm = x_ref.shape[0]

    def token_major(ref, dil, slot):
        if dil == 1:
            return ref[...].astype(F32)
        tiles = GROUP_WIDTH // LANES
        for r in range(dil):
            for c in range(tiles):
                lanes = pl.ds(r * GROUP_WIDTH + c * LANES, LANES)
                stage_ref[slot * tiles + c, pl.ds(r, tm // dil, stride=dil), :] = ref[:, lanes].astype(F32)
        return jnp.concatenate([stage_ref[slot * tiles + c] for c in range(tiles)], axis=-1)

    lses = [token_major(ref, dil, 2 * i) for i, (ref, dil) in enumerate(zip(l_refs, dils))]
    outs = [token_major(ref, dil, 2 * i + 1) for i, (ref, dil) in enumerate(zip(o_refs, dils))]
    mx = functools.reduce(jnp.maximum, lses)
    es = [jnp.exp(l - mx) for l in lses]
    tot = functools.reduce(lambda a, b: a + b, es)
    parts = [(o * (e / tot)).astype(BF16) for o, e in zip(outs, es)]
    mem_out = _memory_attention(qmem_ref[...], kmem_ref[0], vmem_ref[0])
    mixed = jnp.concatenate(parts + [mem_out], axis=-1)
    y = jnp.dot(mixed, w_ref[...], preferred_element_type=F32)
    out_ref[...] = x_ref[...] + _rms(y, g_ref[...])


def _att_out(os_, lses, dils, q_mem, k_mem, v_mem, x2, w_out, gain, seq, tm=1024):
    t, d = x2.shape
    row = lambda i: (i, 0)
    group_specs = [pl.BlockSpec((tm // dil, dil * GROUP_WIDTH), row) for dil in dils]
    return pl.pallas_call(
        functools.partial(_att_out_kernel, dils=dils),
        grid=(t // tm,),
        in_specs=group_specs + group_specs + _memory_specs(q_mem, k_mem, seq, tm)
        + [pl.BlockSpec((tm, d), row), _const_spec(w_out.shape), _const_spec((1, d))],
        out_specs=pl.BlockSpec((tm, d), row),
        out_shape=jax.ShapeDtypeStruct((t, d), F32),
        scratch_shapes=[pltpu.VMEM((2 * len(dils) * (GROUP_WIDTH // LANES), tm, LANES), F32)],
        compiler_params=_params("parallel"),
        name="att_out",
    )(*os_, *lses, q_mem, k_mem, v_mem, x2, w_out, gain.reshape(1, d))


def _ffn_kernel(x_ref, gpre_ref, wg_ref, wu_ref, wd_ref, gpost_ref, out_ref, *, row_blocks):
    rows = x_ref.shape[0] // row_blocks
    for blk in range(row_blocks):
        sl = slice(blk * rows, (blk + 1) * rows)
        x = x_ref[sl, :]
        h = _rms(x, gpre_ref[...]).astype(BF16)
        gate = jnp.dot(h, wg_ref[...], preferred_element_type=F32)
        up = jnp.dot(h, wu_ref[...], preferred_element_type=F32)
        act = (_silu(gate) * up).astype(BF16)
        y = jnp.dot(act, wd_ref[...], preferred_element_type=F32)
        out_ref[sl, :] = x + _rms(y, gpost_ref[...])


def _ffn(x2, g_pre, w_gate_up, w_down, g_post, tm=512, row_blocks=2):
    t, d = x2.shape
    d_ff = w_down.shape[0]
    row = lambda i: (i, 0)
    return pl.pallas_call(
        functools.partial(_ffn_kernel, row_blocks=row_blocks),
        grid=(t // tm,),
        in_specs=[pl.BlockSpec((tm, d), row), _const_spec((1, d)),
                  pl.BlockSpec((d, d_ff), lambda i: (0, 0)), pl.BlockSpec((d, d_ff), lambda i: (0, 1)),
                  _const_spec(w_down.shape), _const_spec((1, d))],
        out_specs=pl.BlockSpec((tm, d), row),
        out_shape=jax.ShapeDtypeStruct((t, d), F32),
        compiler_params=_params("parallel"),
        name="ffn",
    )(x2, g_pre.reshape(1, d), w_gate_up, w_gate_up, w_down, g_post.reshape(1, d))


def _dn_proj_kernel(x_ref, xprev_ref, xnext_ref, g_ref, wqkv_ref, wz_ref, wgate_ref, wmem_ref, conv_ref,
                    alog_ref, dtb_ref, q_ref, k_ref, v_ref, z_ref, qmem_ref, gb_ref, gbt_ref, ext_ref,
                    *, halo, n_dir_heads):
    i = pl.program_id(1)
    last = pl.num_programs(1) - 1
    tm = x_ref.shape[0]
    width = wqkv_ref.shape[1]
    tok = width // 3
    gain = g_ref[...]
    h = _rms(x_ref[...], gain).astype(BF16)
    h_prev = jnp.where(i > 0, _rms(xprev_ref[...], gain), 0.0).astype(BF16)
    h_next = jnp.where(i < last, _rms(xnext_ref[...], gain), 0.0).astype(BF16)
    h_ext = jnp.concatenate([h_prev, h, h_next], axis=0)
    chunk = 2 * LANES
    for c0 in range(0, width, chunk):
        y_ext = jnp.dot(h_ext, wqkv_ref[:, c0:c0 + chunk], preferred_element_type=F32)
        for half in range(chunk // LANES):
            c1 = c0 + half * LANES
            cols = slice(c1, c1 + LANES)
            ext = ext_ref.at[(c1 // LANES) % ext_ref.shape[0]]
            ext[...] = y_ext[:, half * LANES:(half + 1) * LANES]
            taps = [conv_ref[j:j + 1, cols] for j in range(DN_CONV)]
            which, col = divmod(c1, tok)
            out_ref = (q_ref, k_ref, v_ref)[which]
            for r0 in range(0, tm, CONV_ROWS):
                first = halo + r0 - DN_CONV // 2
                acc = ext[first:first + CONV_ROWS, :] * taps[0]
                for j in range(1, DN_CONV):
                    acc = acc + ext[first + j:first + j + CONV_ROWS, :] * taps[j]
                y = _silu(acc)
                if which < 2:
                    scale = lax.rsqrt(jnp.sum(y * y, axis=-1, keepdims=True) + EPS)
                    y = y * (scale * (DN_HEAD_DIM ** -0.5) if which == 0 else scale)
                out_ref[r0:r0 + CONV_ROWS, col:col + LANES] = y.astype(BF16)
    z_ref[...] = jnp.dot(h, wz_ref[...], preferred_element_type=F32).astype(BF16)
    qmem_ref[...] = jnp.dot(h, wmem_ref[...], preferred_element_type=F32).astype(BF16)
    gate = jnp.dot(h, wgate_ref[...], preferred_element_type=F32)
    shifted = gate + dtb_ref[...]
    softplus = jnp.maximum(shifted, 0.0) + jnp.log(1.0 + jnp.exp(-jnp.abs(shifted)))
    decay = -jnp.exp(alog_ref[...]) * softplus
    beta = 1.0 / (1.0 + jnp.exp(-gate))
    lane = lax.broadcasted_iota(jnp.int32, gate.shape, 1)
    gb = jnp.where(lane < n_dir_heads, decay, beta)
    gb_ref[...] = gb
    gbt_ref[...] = jnp.transpose(gb)[:gbt_ref.shape[0], :]


def _dn_proj(x2, gain, w_in, w_gate, w_mem, conv_w, a_log_row, dt_bias_row, batch, seq, tok, n_dir_heads,
             tm=512, halo=16):
    t, d = x2.shape
    n_gate_rows = 8 * ((2 * n_dir_heads + 7) // 8)
    per_seq = seq // tm
    blocks_per_tile = tm // halo
    n_halo_blocks = t // halo
    mem_width = w_mem.shape[1]

    def cur_map(b, i):
        return (b * per_seq + i, 0)

    def prev_map(b, i):
        return (jnp.maximum((b * per_seq + i) * blocks_per_tile - 1, 0), 0)

    def next_map(b, i):
        return (jnp.minimum((b * per_seq + i + 1) * blocks_per_tile, n_halo_blocks - 1), 0)

    return pl.pallas_call(
        functools.partial(_dn_proj_kernel, halo=halo, n_dir_heads=n_dir_heads),
        grid=(batch, per_seq),
        in_specs=[pl.BlockSpec((tm, d), cur_map), pl.BlockSpec((halo, d), prev_map),
                  pl.BlockSpec((halo, d), next_map), _const_spec((1, d)),
                  pl.BlockSpec((d, 3 * tok), lambda b, i: (0, 0)), pl.BlockSpec((d, tok), lambda b, i: (0, 3)),
                  _const_spec(w_gate.shape), _const_spec(w_mem.shape), _const_spec(conv_w.shape),
                  _const_spec((1, LANES)), _const_spec((1, LANES))],
        out_specs=[pl.BlockSpec((tm, tok), cur_map)] * 4
        + [pl.BlockSpec((tm, mem_width), cur_map), pl.BlockSpec((tm, LANES), cur_map),
           pl.BlockSpec((n_gate_rows, tm), lambda b, i: (0, b * per_seq + i))],
        out_shape=[jax.ShapeDtypeStruct((t, tok), BF16)] * 4
        + [jax.ShapeDtypeStruct((t, mem_width), BF16), jax.ShapeDtypeStruct((t, LANES), F32),
           jax.ShapeDtypeStruct((n_gate_rows, t), F32)],
        scratch_shapes=[pltpu.VMEM((4, tm + 2 * halo, LANES), F32)],
        compiler_params=_params("parallel", "arbitrary"),
        name="dn_proj",
    )(x2, x2, x2, gain.reshape(1, d), w_in, w_in, w_gate, w_mem, conv_w, a_log_row, dt_bias_row)


def _mm(a, b):
    return jnp.dot(a, b, preferred_element_type=F32)


def _mm_nt(a, b):
    return lax.dot_general(a, b, (((1,), (1,)), ((), ())), preferred_element_type=F32)


def _mm_tn(a, b):
    return lax.dot_general(a, b, (((0,), (0,)), ((), ())), preferred_element_type=F32)


def _row_blocks(x, size, parity):
    return jnp.concatenate([x[b * size:(b + 1) * size] for b in range(parity, x.shape[0] // size, 2)], axis=0)


def _interleave_row_blocks(even, odd, size):
    pieces = []
    for b in range(even.shape[0] // size):
        pieces += [even[b * size:(b + 1) * size], odd[b * size:(b + 1) * size]]
    return jnp.concatenate(pieces, axis=0)


def _unit_triangular_inverses(neg_ls, uppers, same_block):
    n = neg_ls[0].shape[0]
    ii = lax.broadcasted_iota(jnp.int32, (n, n), 0)
    jj = lax.broadcasted_iota(jnp.int32, (n, n), 1)
    identity = (ii == jj).astype(F32).astype(BF16)
    zero = jnp.zeros((n, n), BF16)
    plus_identity = lambda m: m + identity
    leaf = same_block[SOLVE_LEAF].astype(F32).astype(BF16)
    xbs = [nl * leaf for nl in neg_ls]
    invs = [plus_identity(xb) for xb in xbs]
    power = 1
    while 2 * power < SOLVE_LEAF:
        xbs = [_mm(xb, xb).astype(BF16) for xb in xbs]
        invs = [_mm(plus_identity(xb), inv).astype(BF16) for xb, inv in zip(xbs, invs)]
        power *= 2
    size = SOLVE_LEAF
    while size < n:
        merge = (same_block[2 * size] & jnp.logical_not(same_block[size])).astype(F32).astype(BF16)
        hot = [0 if upper else 1 for upper in uppers]
        offs = [_row_blocks(nl, size, p) * _row_blocks(merge, size, p) for nl, p in zip(neg_ls, hot)]
        rights = [_mm(off, inv).astype(BF16) for off, inv in zip(offs, invs)]
        half_zero = zero[:n // 2]
        rights = [_interleave_row_blocks(*((r, half_zero) if p == 0 else (half_zero, r)), size)
                  for r, p in zip(rights, hot)]
        moved = [_mm(_row_blocks(inv, size, p), plus_identity(right)).astype(BF16)
                 for inv, right, p in zip(invs, rights, hot)]
        invs = [_interleave_row_blocks(*((m, _row_blocks(inv, size, 1)) if p == 0 else
                                         (_row_blocks(inv, size, 0), m)), size)
                for inv, m, p in zip(invs, moved, hot)]
        size *= 2
    return invs


def _exact_prefix_sums(x, tri, rows):
    hi = x.astype(BF16).astype(F32)
    mid = (x - hi).astype(BF16).astype(F32)
    lo = (x - hi - mid).astype(BF16).astype(F32)
    if rows:
        n = x.shape[0]
        res = _mm(jnp.concatenate([hi, mid, lo], axis=0).astype(BF16), tri)
        return res[:n] + res[n:2 * n] + res[2 * n:]
    quarter = LANES // 4
    packed = hi + pltpu.roll(mid, quarter, 1) + pltpu.roll(lo, 2 * quarter, 1)
    res = _mm(tri, packed.astype(BF16))
    return res + pltpu.roll(res, LANES - quarter, 1) + pltpu.roll(res, LANES - 2 * quarter, 1)


def _dn_core_kernel(qf_ref, kf_ref, vf_ref, gbf_ref, gtf_ref, qb_ref, kb_ref, vb_ref, gbb_ref, gtb_ref,
                    of_ref, ob_ref, state_ref, *, n_heads):
    @pl.when(pl.program_id(1) == 0)
    def _():
        state_ref[...] = jnp.zeros_like(state_ref)

    c = DN_CHUNK
    d = DN_HEAD_DIM
    n_sub = qf_ref.shape[0] // c
    ii = lax.broadcasted_iota(jnp.int32, (c, c), 0)
    jj = lax.broadcasted_iota(jnp.int32, (c, c), 1)
    lower_eq, upper_eq = jj <= ii, jj >= ii
    lower, upper = jj < ii, jj > ii
    same_block = {}
    size = SOLVE_LEAF
    while size <= c:
        shift = int(math.log2(size))
        same_block[size] = (ii >> shift) == (jj >> shift)
        size *= 2

    items = []
    for direction, (q_ref, k_ref, v_ref, gb_ref, gt_ref, o_ref) in enumerate(
            ((qf_ref, kf_ref, vf_ref, gbf_ref, gtf_ref, of_ref),
             (qb_ref, kb_ref, vb_ref, gbb_ref, gtb_ref, ob_ref))):
        reverse = direction == 1
        tri_col = (upper_eq if reverse else lower_eq).astype(BF16)
        tri_row = (lower_eq if reverse else upper_eq).astype(BF16)
        for sub in range(n_sub):
            rows = slice(sub * c, (sub + 1) * c)
            gb = gb_ref[rows, :]
            lane = lax.broadcasted_iota(jnp.int32, gb.shape, 1)
            cum_col = _exact_prefix_sums(jnp.where(lane < LANES // 4, gb, 0.0), tri_col, rows=False)
            cum_row = _exact_prefix_sums(gt_ref[:, rows], tri_row, rows=True)
            for head in range(n_heads):
                g_idx = direction * n_heads + head
                b_idx = 2 * n_heads + g_idx
                cols = slice(head * d, (head + 1) * d)
                items.append(dict(
                    direction=direction, head=head, sub=sub, o_ref=o_ref, rows=rows, cols=cols, reverse=reverse,
                    earlier_eq=upper_eq if reverse else lower_eq,
                    earlier=upper if reverse else lower,
                    q=q_ref[rows, cols], k=k_ref[rows, cols], v=v_ref[rows, cols],
                    g_col=cum_col[:, g_idx:g_idx + 1], beta_col=gb[:, b_idx:b_idx + 1],
                    g_row=cum_row[g_idx:g_idx + 1, :],
                    g_last=cum_col[0:1, g_idx:g_idx + 1] if reverse else cum_col[c - 1:c, g_idx:g_idx + 1]))

    grams = [_mm_nt(jnp.concatenate([it["k"], it["q"]], axis=0), it["k"]) for it in items]
    neg_ls, rhss = [], []
    for it, gram in zip(items, grams):
        g_wide = jnp.broadcast_to(it["g_col"], (c, d))
        beta_wide = jnp.broadcast_to(it["beta_col"], (c, d))
        it["g_last_wide"] = jnp.broadcast_to(it["g_last"], (c, d))
        decay = jnp.exp(jnp.where(it["earlier_eq"], g_wide - it["g_row"], NEG_INF))
        neg_ls.append((jnp.where(it["earlier"], gram[:c] * decay, 0.0) * (-beta_wide)).astype(BF16))
        it["intra"] = jnp.where(it["earlier_eq"], gram[c:] * decay, 0.0).astype(BF16)
        kf = it["k"].astype(F32)
        e_g = jnp.exp(g_wide)
        rhss.append(jnp.concatenate([(it["v"].astype(F32) * beta_wide).astype(BF16),
                                     (kf * (beta_wide * e_g)).astype(BF16)], axis=-1))
        it["qg"] = (it["q"].astype(F32) * e_g).astype(BF16)
        it["k_dec"] = (kf * jnp.exp(it["g_last_wide"] - g_wide)).astype(BF16)
    invs = _unit_triangular_inverses(neg_ls, [it["reverse"] for it in items], same_block)
    for it, inv, rhs in zip(items, invs, rhss):
        sol = _mm(inv, rhs)
        it["u"], it["w"] = sol[:, :d], sol[:, d:].astype(BF16)

    for step in range(n_sub):
        live = [it for it in items if it["sub"] == (n_sub - 1 - step if it["reverse"] else step)]
        states = [state_ref[it["direction"], it["head"]] for it in live]
        wss = [_mm(jnp.concatenate([it["w"], it["qg"]], axis=0), state.astype(BF16))
               for it, state in zip(live, states)]
        v_news = [(it["u"] - ws[:c]).astype(BF16) for it, ws in zip(live, wss)]
        outs = [ws[c:] + _mm(it["intra"], v_new) for it, ws, v_new in zip(live, wss, v_news)]
        updates = [_mm_tn(it["k_dec"], v_new) for it, v_new in zip(live, v_news)]
        for it, out, state, update in zip(live, outs, states, updates):
            it["o_ref"][it["rows"], it["cols"]] = out.astype(BF16)
            state_ref[it["direction"], it["head"]] = state * jnp.exp(it["g_last_wide"]) + update


def _dn_core(q, k, v, gb, gbt, batch, seq, chunks_per_step=4):
    t, tok = q.shape
    n_heads = tok // DN_HEAD_DIM
    rows = chunks_per_step * DN_CHUNK
    steps = seq // rows
    n_rows = gbt.shape[0]

    fwd = lambda b, i: (b * steps + i, 0)
    bwd = lambda b, i: (b * steps + steps - 1 - i, 0)
    fwd_t = lambda b, i: (0, b * steps + i)
    bwd_t = lambda b, i: (0, b * steps + steps - 1 - i)

    def side(tok_map, t_map):
        return [pl.BlockSpec((rows, tok), tok_map)] * 3 + [
            pl.BlockSpec((rows, LANES), tok_map), pl.BlockSpec((n_rows, rows), t_map)]

    return pl.pallas_call(
        functools.partial(_dn_core_kernel, n_heads=n_heads),
        grid=(batch, steps),
        in_specs=side(fwd, fwd_t) + side(bwd, bwd_t),
        out_specs=[pl.BlockSpec((rows, tok), fwd), pl.BlockSpec((rows, tok), bwd)],
        out_shape=[jax.ShapeDtypeStruct((t, tok), BF16)] * 2,
        scratch_shapes=[pltpu.VMEM((2, n_heads, DN_HEAD_DIM, DN_HEAD_DIM), F32)],
        compiler_params=_params("parallel", "arbitrary"),
        name="dn_core",
    )(q, k, v, gb, gbt, q, k, v, gb, gbt)


def _dn_out_kernel(of_ref, ob_ref, z_ref, qmem_ref, kmem_ref, vmem_ref, x_ref, onorm_ref, w_ref, g_ref,
                   out_ref):
    tok = of_ref.shape[1]
    parts = []
    for c0 in range(0, tok, DN_HEAD_DIM):
        cols = slice(c0, c0 + DN_HEAD_DIM)
        o = of_ref[:, cols].astype(F32) + ob_ref[:, cols].astype(F32)
        o = o * lax.rsqrt(jnp.mean(o * o, axis=-1, keepdims=True) + EPS) * onorm_ref[...]
        parts.append((o * _silu(z_ref[:, cols].astype(F32))).astype(BF16))
    mem_out = _memory_attention(qmem_ref[...], kmem_ref[0], vmem_ref[0])
    mixed = jnp.concatenate(parts + [mem_out], axis=-1)
    y = jnp.dot(mixed, w_ref[...], preferred_element_type=F32)
    out_ref[...] = x_ref[...] + _rms(y, g_ref[...])


def _dn_out(o_f, o_b, z, q_mem, k_mem, v_mem, x2, out_norm, w_out, gain, seq, tm=1024):
    t, d = x2.shape
    tok = o_f.shape[1]
    row = lambda i: (i, 0)
    return pl.pallas_call(
        _dn_out_kernel,
        grid=(t // tm,),
        in_specs=[pl.BlockSpec((tm, tok), row)] * 3 + _memory_specs(q_mem, k_mem, seq, tm)
        + [pl.BlockSpec((tm, d), row), _const_spec((1, DN_HEAD_DIM)), _const_spec(w_out.shape),
           _const_spec((1, d))],
        out_specs=pl.BlockSpec((tm, d), row),
        out_shape=jax.ShapeDtypeStruct((t, d), F32),
        compiler_params=_params("parallel"),
        name="dn_out",
    )(o_f, o_b, z, q_mem, k_mem, v_mem, x2, out_norm.reshape(1, DN_HEAD_DIM), w_out, gain.reshape(1, d))


def _attention_layer(x2, mem, batch, seq, rel_bias, w_in, w_out, mem_gain, w_kv, g_pre, g_post):
    n_groups = len(DILATED_GROUPS)
    tok = n_groups * GROUP_WIDTH
    dils = tuple(dil for _, dil in DILATED_GROUPS)
    w_qkv = w_in[:, :3 * tok].reshape(-1, 3, n_groups, GROUP_WIDTH).transpose(0, 2, 1, 3).astype(BF16)
    group_weights = [w_qkv[:, g].reshape(-1, 3 * GROUP_WIDTH) for g in range(n_groups)]
    *qkvs, q_mem = _att_proj(x2, g_pre, group_weights, w_in[:, 3 * tok:].astype(BF16), dils)
    os_, lses = [], []
    for group, dil in enumerate(dils):
        n_res = min(dil, 2)
        o, lse = _band_attn(qkvs[group], _band_bias(rel_bias, group, dil), batch, seq, dil, n_res,
                            q_tile=512 // n_res)
        os_.append(o)
        lses.append(lse)
    k_mem, v_mem = _mem_kv(mem, mem_gain, w_kv.astype(BF16))
    return _att_out(os_, lses, dils, q_mem, k_mem, v_mem, x2, w_out.astype(BF16), g_post, seq)


def _deltanet_layer(x2, mem, batch, seq, w_in, conv_w, a_log, dt_bias, out_norm, w_out, mem_gain, w_kv,
                    g_pre, g_post):
    d = x2.shape[1]
    n_heads = a_log.shape[1]
    tok = n_heads * DN_HEAD_DIM
    n_gate = 4 * n_heads
    w_gate = w_in[:, 4 * tok:4 * tok + n_gate].reshape(d, 2, 2, n_heads).transpose(0, 2, 1, 3).reshape(d, n_gate)
    w_gate = jnp.pad(w_gate, ((0, 0), (0, LANES - n_gate)))
    w_in_b = w_in.astype(BF16)
    pad_row = lambda p: jnp.pad(p.reshape(1, -1).astype(F32), ((0, 0), (0, LANES - 2 * n_heads)))
    q, k, v, z, q_mem, gb, gbt = _dn_proj(x2, g_pre, w_in_b, w_gate.astype(BF16), w_in_b[:, 4 * tok + n_gate:],
                                          conv_w.astype(F32), pad_row(a_log), pad_row(dt_bias), batch, seq, tok,
                                          2 * n_heads)
    o_f, o_b = _dn_core(q, k, v, gb, gbt, batch, seq)
    k_mem, v_mem = _mem_kv(mem, mem_gain, w_kv.astype(BF16))
    return _dn_out(o_f, o_b, z, q_mem, k_mem, v_mem, x2, out_norm, w_out.astype(BF16), g_post, seq)


def kernel(x, mem, rel_bias, att_w_in, att_w_out, dn_w_in, dn_conv, dn_a_log, dn_dt_bias, dn_out_norm,
           dn_w_out, mem_norm, mem_w_kv, norm_mix_pre, norm_mix_post, norm_ffn_pre, norm_ffn_post,
           ffn_w_gate_up, ffn_w_down):
    batch, seq, d = x.shape
    depth = norm_mix_pre.shape[0]
    n_mixers = 2
    x2 = x.reshape(batch * seq, d)
    for i in range(depth):
        j = i // n_mixers
        if i % n_mixers == 0:
            x2 = _attention_layer(x2, mem, batch, seq, rel_bias, att_w_in[j], att_w_out[j], mem_norm[i],
                                  mem_w_kv[i], norm_mix_pre[i], norm_mix_post[i])
        else:
            x2 = _deltanet_layer(x2, mem, batch, seq, dn_w_in[j], dn_conv[j], dn_a_log[j], dn_dt_bias[j],
                                 dn_out_norm[j], dn_w_out[j], mem_norm[i], mem_w_kv[i], norm_mix_pre[i],
                                 norm_mix_post[i])
        x2 = _ffn(x2, norm_ffn_pre[i], ffn_w_gate_up[i].astype(BF16), ffn_w_down[i].astype(BF16),
                  norm_ffn_post[i])
    return x2.reshape(batch, seq, d)
```

```python
import functools
import math

import jax
import jax.numpy as jnp
import numpy as np
from jax import lax
from jax.experimental import pallas as pl
from jax.experimental.pallas import tpu as pltpu

F32 = jnp.float32
BF16 = jnp.bfloat16

EPS = 1e-6
NEG_INF = -1e30

LANES = 128
VMEM_LIMIT_BYTES = 56 * 1024 * 1024

DILATED_GROUPS = ((128, 1), (512, 4), (2048, 16))
ATT_HEAD_DIM = 64
HEADS_PER_GROUP = 4
GROUP_WIDTH = HEADS_PER_GROUP * ATT_HEAD_DIM
BAND_HALF = 64
REL_BUCKETS = 32
REL_MAX_DIST = 1024
MEM_HEADS = 4
MEM_HEAD_DIM = 64
DN_HEAD_DIM = 128
DN_CONV = 5
DN_CHUNK = 128
SOLVE_LEAF = 16
CONV_ROWS = 64

ATT_Q_BLOCK = 128
ATT_K_WINDOW = ATT_Q_BLOCK + 2 * BAND_HALF


def _params(*semantics):
    return pltpu.CompilerParams(dimension_semantics=semantics, vmem_limit_bytes=VMEM_LIMIT_BYTES)


def _rms(x, gain):
    return x * lax.rsqrt(jnp.mean(x * x, axis=-1, keepdims=True) + EPS) * gain


def _silu(x):
    return x / (1.0 + jnp.exp2(x * (-1.0 / math.log(2.0))))


def _const_spec(shape):
    return pl.BlockSpec(shape, lambda *_: (0,) * len(shape))


def _mem_kv_kernel(mem_ref, g_ref, w_ref, k_ref, v_ref):
    h = _rms(mem_ref[0], g_ref[...]).astype(BF16)
    kv = jnp.dot(h, w_ref[...], preferred_element_type=F32)
    width = k_ref.shape[-1]
    k_ref[0] = kv[:, :width].astype(BF16)
    v_ref[0] = kv[:, width:].astype(BF16)


def _mem_kv(mem, gain, w_kv):
    b, m, d = mem.shape
    width = w_kv.shape[1] // 2
    return pl.pallas_call(
        _mem_kv_kernel,
        grid=(b,),
        in_specs=[pl.BlockSpec((1, m, d), lambda i: (i, 0, 0)), _const_spec((1, d)), _const_spec(w_kv.shape)],
        out_specs=[pl.BlockSpec((1, m, width), lambda i: (i, 0, 0))] * 2,
        out_shape=[jax.ShapeDtypeStruct((b, m, width), BF16)] * 2,
        compiler_params=_params("parallel"),
        name="mem_kv",
    )(mem, gain.reshape(1, d), w_kv)


def _memory_attention(q, k, v):
    lane = lax.broadcasted_iota(jnp.int32, q.shape, 1)
    in_head = [(lane >= h * MEM_HEAD_DIM) & (lane < (h + 1) * MEM_HEAD_DIM) for h in range(MEM_HEADS)]
    logits = [lax.dot_general(jnp.where(mask, q, jnp.zeros_like(q)), k, (((1,), (1,)), ((), ())),
                              preferred_element_type=F32) for mask in in_head]
    probs = []
    for s in logits:
        s = s * (MEM_HEAD_DIM ** -0.5)
        p = jnp.exp(s - jnp.max(s, axis=-1, keepdims=True))
        probs.append((p / jnp.sum(p, axis=-1, keepdims=True)).astype(BF16))
    pvs = [jnp.dot(p, v, preferred_element_type=F32) for p in probs]
    out = pvs[-1]
    for mask, pv in zip(in_head[:-1], pvs[:-1]):
        out = jnp.where(mask, pv, out)
    return out.astype(BF16)


def _memory_specs(q_mem, k_mem, seq, tm):
    per_seq = seq // tm
    kv_spec = pl.BlockSpec((1,) + k_mem.shape[1:], lambda i: (i // per_seq, 0, 0))
    return [pl.BlockSpec((tm, q_mem.shape[1]), lambda i: (i, 0)), kv_spec, kv_spec]


def _t5_bucket(rel):
    half = REL_BUCKETS // 2
    max_exact = half // 2
    n = np.abs(rel)
    large = max_exact + (np.log(np.maximum(n, 1) / max_exact) / math.log(REL_MAX_DIST / max_exact)
                         * (half - max_exact)).astype(np.int64)
    large = np.minimum(large, half - 1)
    return ((rel > 0) * half + np.where(n < max_exact, n, large)).astype(np.int32)


def _band_bias(rel_bias, group, dil):
    period = ATT_Q_BLOCK + ATT_K_WINDOW
    heads = slice(group * HEADS_PER_GROUP, (group + 1) * HEADS_PER_GROUP)
    tables = []
    for variant in range(3):
        rel = np.arange(period) - (ATT_Q_BLOCK - 1) - variant * BAND_HALF
        in_band = np.abs(rel) <= BAND_HALF
        bucket = _t5_bucket(np.clip(rel, -BAND_HALF, BAND_HALF) * dil)
        f = jnp.where(in_band[None, :], jnp.transpose(rel_bias[bucket][:, heads]).astype(F32), NEG_INF)
        flat = jnp.tile(f, (1, ATT_Q_BLOCK))[:, :ATT_Q_BLOCK * (period - 1)]
        skew = flat.reshape(HEADS_PER_GROUP, ATT_Q_BLOCK, period - 1)
        tables.append(skew[:, :, ATT_Q_BLOCK - 1:ATT_Q_BLOCK - 1 + ATT_K_WINDOW])
    return jnp.stack(tables)


def _band_attn_kernel(x_ref, bias_ref, o_ref, lse_ref, *, sub_len, n_res, q_tile):
    lane = lax.broadcasted_iota(jnp.int32, (ATT_Q_BLOCK, LANES), 1)
    low_half = lane < ATT_HEAD_DIM
    n_pairs = GROUP_WIDTH // LANES

    def tile(t, carry):
        blocks = []
        for res in range(n_res):
            for blk in range(q_tile // ATT_Q_BLOCK):
                i0 = pl.multiple_of(t * q_tile + blk * ATT_Q_BLOCK, ATT_Q_BLOCK)
                ws = jnp.clip(i0 - BAND_HALF, 0, sub_len - ATT_K_WINDOW)
                variant = (i0 - ws) // BAND_HALF
                ws = pl.multiple_of(ws, BAND_HALF)
                base = res * 3 * GROUP_WIDTH
                blocks.append(dict(
                    res=res, rows=pl.ds(i0, ATT_Q_BLOCK), variant=variant,
                    q=x_ref[0, pl.ds(i0, ATT_Q_BLOCK), base:base + GROUP_WIDTH]
                    * jnp.asarray(ATT_HEAD_DIM ** -0.5, BF16),
                    k=x_ref[0, pl.ds(ws, ATT_K_WINDOW), base + GROUP_WIDTH:base + 2 * GROUP_WIDTH],
                    v=x_ref[0, pl.ds(ws, ATT_K_WINDOW), base + 2 * GROUP_WIDTH:base + 3 * GROUP_WIDTH]))
        heads = [(b, pair, a) for b in blocks for pair in range(n_pairs) for a in range(2)]
        logits = []
        for b, pair, a in heads:
            cols = slice(pair * LANES, (pair + 1) * LANES)
            q_pair = b["q"][:, cols]
            keep = low_half if a == 0 else jnp.logical_not(low_half)
            qh = jnp.where(keep, q_pair, jnp.zeros_like(q_pair))
            logits.append(_mm_nt(qh, b["k"][:, cols]))
        probs, dens, lses = [], [], []
        for (b, pair, a), s in zip(heads, logits):
            s = s + bias_ref[b["variant"], 2 * pair + a]
            m = jnp.max(s, axis=-1, keepdims=True)
            p = jnp.exp(s - m)
            den = jnp.sum(p, axis=-1, keepdims=True)
            probs.append(p.astype(BF16))
            dens.append(den)
            lses.append(m + jnp.log(den))
        pvs = [_mm(p, b["v"][:, pair * LANES:(pair + 1) * LANES]) for (b, pair, a), p in zip(heads, probs)]
        for idx in range(0, len(heads), 2):
            b, pair, _ = heads[idx]
            cols = pl.ds(b["res"] * GROUP_WIDTH + pair * LANES, LANES)
            out = jnp.where(low_half, pvs[idx] / dens[idx], pvs[idx + 1] / dens[idx + 1])
            lse = jnp.where(low_half, jnp.broadcast_to(lses[idx], (ATT_Q_BLOCK, LANES)),
                            jnp.broadcast_to(lses[idx + 1], (ATT_Q_BLOCK, LANES)))
            o_ref[0, b["rows"], cols] = out.astype(o_ref.dtype)
            lse_ref[0, b["rows"], cols] = lse
        return carry

    lax.fori_loop(0, sub_len // q_tile, tile, 0)


def _band_attn(qkv, bias, batch, seq, dil, n_res, q_tile=256):
    rows, width = qkv.shape
    sub_len = seq // dil
    qkv3 = qkv.reshape(batch, sub_len, width)
    x_spec = pl.BlockSpec((1, sub_len, n_res * 3 * GROUP_WIDTH), lambda b, r: (b, 0, r))
    o_spec = pl.BlockSpec((1, sub_len, n_res * GROUP_WIDTH), lambda b, r: (b, 0, r))
    o, lse = pl.pallas_call(
        functools.partial(_band_attn_kernel, sub_len=sub_len, n_res=n_res, q_tile=q_tile),
        grid=(batch, dil // n_res),
        in_specs=[x_spec, _const_spec(bias.shape)],
        out_specs=[o_spec, o_spec],
        out_shape=[jax.ShapeDtypeStruct((batch, sub_len, dil * GROUP_WIDTH), BF16),
                   jax.ShapeDtypeStruct((batch, sub_len, dil * GROUP_WIDTH), F32)],
        compiler_params=_params("parallel", "parallel"),
        name=f"band_attn_d{dil}",
    )(qkv3, bias)
    return o.reshape(rows, dil * GROUP_WIDTH), lse.reshape(rows, dil * GROUP_WIDTH)


def _att_proj_kernel(x_ref, g_ref, *refs, dils):
    n = len(dils)
    w_refs, wmem_ref = refs[:n], refs[n]
    out_refs, qmem_ref, stage_ref = refs[n + 1:2 * n + 1], refs[2 * n + 1], refs[2 * n + 2]
    h = _rms(x_ref[...], g_ref[...]).astype(BF16)
    tm = h.shape[0]
    for w_ref, o_ref, dil in zip(w_refs, out_refs, dils):
        y = jnp.dot(h, w_ref[...], preferred_element_type=F32)
        width = y.shape[1]
        if dil == 1:
            o_ref[...] = y.astype(BF16)
        else:
            for c in range(width // LANES):
                stage_ref[c] = y[:, c * LANES:(c + 1) * LANES]
            for r in range(dil):
                for c in range(width // LANES):
                    lanes = pl.ds(r * width + c * LANES, LANES)
                    o_ref[:, lanes] = stage_ref[c, pl.ds(r, tm // dil, stride=dil), :].astype(BF16)
    qmem_ref[...] = jnp.dot(h, wmem_ref[...], preferred_element_type=F32).astype(BF16)


def _att_proj(x2, gain, group_weights, w_mem, dils, tm=512):
    t, d = x2.shape
    width = group_weights[0].shape[1]
    row = lambda i: (i, 0)
    return pl.pallas_call(
        functools.partial(_att_proj_kernel, dils=dils),
        grid=(t // tm,),
        in_specs=[pl.BlockSpec((tm, d), row), _const_spec((1, d))]
        + [_const_spec(w.shape) for w in group_weights] + [_const_spec(w_mem.shape)],
        out_specs=[pl.BlockSpec((tm // dil, dil * width), row) for dil in dils]
        + [pl.BlockSpec((tm, w_mem.shape[1]), row)],
        out_shape=[jax.ShapeDtypeStruct((t // dil, dil * width), BF16) for dil in dils]
        + [jax.ShapeDtypeStruct((t, w_mem.shape[1]), BF16)],
        scratch_shapes=[pltpu.VMEM((width // LANES, tm, LANES), F32)],
        compiler_params=_params("parallel"),
        name="att_proj",
    )(x2, gain.reshape(1, d), *group_weights, w_mem)


def _att_out_kernel(*refs, dils):
    n = len(dils)
    o_refs, l_refs = refs[:n], refs[n:2 * n]
    qmem_ref, kmem_ref, vmem_ref, x_ref, w_ref, g_ref, out_ref, stage_ref = refs[2 * n:]
    tm = x_ref.shape[0]

    def token_major(ref, dil, slot):
        if dil == 1:
            return ref[...].astype(F32)
        tiles = GROUP_WIDTH // LANES
        for r in range(dil):
            for c in range(tiles):
                lanes = pl.ds(r * GROUP_WIDTH + c * LANES, LANES)
                stage_ref[slot * tiles + c, pl.ds(r, tm // dil, stride=dil), :] = ref[:, lanes].astype(F32)
        return jnp.concatenate([stage_ref[slot * tiles + c] for c in range(tiles)], axis=-1)

    lses = [token_major(ref, dil, 2 * i) for i, (ref, dil) in enumerate(zip(l_refs, dils))]
    outs = [token_major(ref, dil, 2 * i + 1) for i, (ref, dil) in enumerate(zip(o_refs, dils))]
    mx = functools.reduce(jnp.maximum, lses)
    es = [jnp.exp(l - mx) for l in lses]
    tot = functools.reduce(lambda a, b: a + b, es)
    parts = [(o * (e / tot)).astype(BF16) for o, e in zip(outs, es)]
    mem_out = _memory_attention(qmem_ref[...], kmem_ref[0], vmem_ref[0])
    mixed = jnp.concatenate(parts + [mem_out], axis=-1)
    y = jnp.dot(mixed, w_ref[...], preferred_element_type=F32)
    out_ref[...] = x_ref[...] + _rms(y, g_ref[...])


def _att_out(os_, lses, dils, q_mem, k_mem, v_mem, x2, w_out, gain, seq, tm=1024):
    t, d = x2.shape
    row = lambda i: (i, 0)
    group_specs = [pl.BlockSpec((tm // dil, dil * GROUP_WIDTH), row) for dil in dils]
    return pl.pallas_call(
        functools.partial(_att_out_kernel, dils=dils),
        grid=(t // tm,),
        in_specs=group_specs + group_specs + _memory_specs(q_mem, k_mem, seq, tm)
        + [pl.BlockSpec((tm, d), row), _const_spec(w_out.shape), _const_spec((1, d))],
        out_specs=pl.BlockSpec((tm, d), row),
        out_shape=jax.ShapeDtypeStruct((t, d), F32),
        scratch_shapes=[pltpu.VMEM((2 * len(dils) * (GROUP_WIDTH // LANES), tm, LANES), F32)],
        compiler_params=_params("parallel"),
        name="att_out",
    )(*os_, *lses, q_mem, k_mem, v_mem, x2, w_out, gain.reshape(1, d))


def _ffn_kernel(x_ref, gpre_ref, wg_ref, wu_ref, wd_ref, gpost_ref, out_ref, *, row_blocks):
    rows = x_ref.shape[0] // row_blocks
    for blk in range(row_blocks):
        sl = slice(blk * rows, (blk + 1) * rows)
        x = x_ref[sl, :]
        h = _rms(x, gpre_ref[...]).astype(BF16)
        gate = jnp.dot(h, wg_ref[...], preferred_element_type=F32)
        up = jnp.dot(h, wu_ref[...], preferred_element_type=F32)
        act = (_silu(gate) * up).astype(BF16)
        y = jnp.dot(act, wd_ref[...], preferred_element_type=F32)
        out_ref[sl, :] = x + _rms(y, gpost_ref[...])


def _ffn(x2, g_pre, w_gate_up, w_down, g_post, tm=512, row_blocks=2):
    t, d = x2.shape
    d_ff = w_down.shape[0]
    row = lambda i: (i, 0)
    return pl.pallas_call(
        functools.partial(_ffn_kernel, row_blocks=row_blocks),
        grid=(t // tm,),
        in_specs=[pl.BlockSpec((tm, d), row), _const_spec((1, d)),
                  pl.BlockSpec((d, d_ff), lambda i: (0, 0)), pl.BlockSpec((d, d_ff), lambda i: (0, 1)),
                  _const_spec(w_down.shape), _const_spec((1, d))],
        out_specs=pl.BlockSpec((tm, d), row),
        out_shape=jax.ShapeDtypeStruct((t, d), F32),
        compiler_params=_params("parallel"),
        name="ffn",
    )(x2, g_pre.reshape(1, d), w_gate_up, w_gate_up, w_down, g_post.reshape(1, d))


def _dn_proj_kernel(x_ref, xprev_ref, xnext_ref, g_ref, wqkv_ref, wz_ref, wgate_ref, wmem_ref, conv_ref,
                    alog_ref, dtb_ref, q_ref, k_ref, v_ref, z_ref, qmem_ref, gb_ref, gbt_ref, ext_ref,
                    *, halo, n_dir_heads):
    i = pl.program_id(1)
    last = pl.num_programs(1) - 1
    tm = x_ref.shape[0]
    width = wqkv_ref.shape[1]
    tok = width // 3
    gain = g_ref[...]
    h = _rms(x_ref[...], gain).astype(BF16)
    h_prev = jnp.where(i > 0, _rms(xprev_ref[...], gain), 0.0).astype(BF16)
    h_next = jnp.where(i < last, _rms(xnext_ref[...], gain), 0.0).astype(BF16)
    h_ext = jnp.concatenate([h_prev, h, h_next], axis=0)
    chunk = 2 * LANES
    for c0 in range(0, width, chunk):
        y_ext = jnp.dot(h_ext, wqkv_ref[:, c0:c0 + chunk], preferred_element_type=F32)
        for half in range(chunk // LANES):
            c1 = c0 + half * LANES
            cols = slice(c1, c1 + LANES)
            ext = ext_ref.at[(c1 // LANES) % ext_ref.shape[0]]
            ext[...] = y_ext[:, half * LANES:(half + 1) * LANES]
            taps = [conv_ref[j:j + 1, cols] for j in range(DN_CONV)]
            which, col = divmod(c1, tok)
            out_ref = (q_ref, k_ref, v_ref)[which]
            for r0 in range(0, tm, CONV_ROWS):
                first = halo + r0 - DN_CONV // 2
                acc = ext[first:first + CONV_ROWS, :] * taps[0]
                for j in range(1, DN_CONV):
                    acc = acc + ext[first + j:first + j + CONV_ROWS, :] * taps[j]
                y = _silu(acc)
                if which < 2:
                    scale = lax.rsqrt(jnp.sum(y * y, axis=-1, keepdims=True) + EPS)
                    y = y * (scale * (DN_HEAD_DIM ** -0.5) if which == 0 else scale)
                out_ref[r0:r0 + CONV_ROWS, col:col + LANES] = y.astype(BF16)
    z_ref[...] = jnp.dot(h, wz_ref[...], preferred_element_type=F32).astype(BF16)
    qmem_ref[...] = jnp.dot(h, wmem_ref[...], preferred_element_type=F32).astype(BF16)
    gate = jnp.dot(h, wgate_ref[...], preferred_element_type=F32)
    shifted = gate + dtb_ref[...]
    softplus = jnp.maximum(shifted, 0.0) + jnp.log(1.0 + jnp.exp(-jnp.abs(shifted)))
    decay = -jnp.exp(alog_ref[...]) * softplus
    beta = 1.0 / (1.0 + jnp.exp(-gate))
    lane = lax.broadcasted_iota(jnp.int32, gate.shape, 1)
    gb = jnp.where(lane < n_dir_heads, decay, beta)
    gb_ref[...] = gb
    gbt_ref[...] = jnp.transpose(gb)[:gbt_ref.shape[0], :]


def _dn_proj(x2, gain, w_in, w_gate, w_mem, conv_w, a_log_row, dt_bias_row, batch, seq, tok, n_dir_heads,
             tm=512, halo=16):
    t, d = x2.shape
    n_gate_rows = 8 * ((2 * n_dir_heads + 7) // 8)
    per_seq = seq // tm
    blocks_per_tile = tm // halo
    n_halo_blocks = t // halo
    mem_width = w_mem.shape[1]

    def cur_map(b, i):
        return (b * per_seq + i, 0)

    def prev_map(b, i):
        return (jnp.maximum((b * per_seq + i) * blocks_per_tile - 1, 0), 0)

    def next_map(b, i):
        return (jnp.minimum((b * per_seq + i + 1) * blocks_per_tile, n_halo_blocks - 1), 0)

    return pl.pallas_call(
        functools.partial(_dn_proj_kernel, halo=halo, n_dir_heads=n_dir_heads),
        grid=(batch, per_seq),
        in_specs=[pl.BlockSpec((tm, d), cur_map), pl.BlockSpec((halo, d), prev_map),
                  pl.BlockSpec((halo, d), next_map), _const_spec((1, d)),
                  pl.BlockSpec((d, 3 * tok), lambda b, i: (0, 0)), pl.BlockSpec((d, tok), lambda b, i: (0, 3)),
                  _const_spec(w_gate.shape), _const_spec(w_mem.shape), _const_spec(conv_w.shape),
                  _const_spec((1, LANES)), _const_spec((1, LANES))],
        out_specs=[pl.BlockSpec((tm, tok), cur_map)] * 4
        + [pl.BlockSpec((tm, mem_width), cur_map), pl.BlockSpec((tm, LANES), cur_map),
           pl.BlockSpec((n_gate_rows, tm), lambda b, i: (0, b * per_seq + i))],
        out_shape=[jax.ShapeDtypeStruct((t, tok), BF16)] * 4
        + [jax.ShapeDtypeStruct((t, mem_width), BF16), jax.ShapeDtypeStruct((t, LANES), F32),
           jax.ShapeDtypeStruct((n_gate_rows, t), F32)],
        scratch_shapes=[pltpu.VMEM((4, tm + 2 * halo, LANES), F32)],
        compiler_params=_params("parallel", "arbitrary"),
        name="dn_proj",
    )(x2, x2, x2, gain.reshape(1, d), w_in, w_in, w_gate, w_mem, conv_w, a_log_row, dt_bias_row)


def _mm(a, b):
    return jnp.dot(a, b, preferred_element_type=F32)


def _mm_nt(a, b):
    return lax.dot_general(a, b, (((1,), (1,)), ((), ())), preferred_element_type=F32)


def _mm_tn(a, b):
    return lax.dot_general(a, b, (((0,), (0,)), ((), ())), preferred_element_type=F32)


def _row_blocks(x, size, parity):
    return jnp.concatenate([x[b * size:(b + 1) * size] for b in range(parity, x.shape[0] // size, 2)], axis=0)


def _interleave_row_blocks(even, odd, size):
    pieces = []
    for b in range(even.shape[0] // size):
        pieces += [even[b * size:(b + 1) * size], odd[b * size:(b + 1) * size]]
    return jnp.concatenate(pieces, axis=0)


def _unit_triangular_inverses(neg_ls, uppers, same_block):
    n = neg_ls[0].shape[0]
    ii = lax.broadcasted_iota(jnp.int32, (n, n), 0)
    jj = lax.broadcasted_iota(jnp.int32, (n, n), 1)
    identity = (ii == jj).astype(F32).astype(BF16)
    zero = jnp.zeros((n, n), BF16)
    plus_identity = lambda m: m + identity
    leaf = same_block[SOLVE_LEAF].astype(F32).astype(BF16)
    xbs = [nl * leaf for nl in neg_ls]
    invs = [plus_identity(xb) for xb in xbs]
    power = 1
    while 2 * power < SOLVE_LEAF:
        xbs = [_mm(xb, xb).astype(BF16) for xb in xbs]
        invs = [_mm(plus_identity(xb), inv).astype(BF16) for xb, inv in zip(xbs, invs)]
        power *= 2
    size = SOLVE_LEAF
    while size < n:
        merge = (same_block[2 * size] & jnp.logical_not(same_block[size])).astype(F32).astype(BF16)
        hot = [0 if upper else 1 for upper in uppers]
        offs = [_row_blocks(nl, size, p) * _row_blocks(merge, size, p) for nl, p in zip(neg_ls, hot)]
        rights = [_mm(off, inv).astype(BF16) for off, inv in zip(offs, invs)]
        half_zero = zero[:n // 2]
        rights = [_interleave_row_blocks(*((r, half_zero) if p == 0 else (half_zero, r)), size)
                  for r, p in zip(rights, hot)]
        moved = [_mm(_row_blocks(inv, size, p), plus_identity(right)).astype(BF16)
                 for inv, right, p in zip(invs, rights, hot)]
        invs = [_interleave_row_blocks(*((m, _row_blocks(inv, size, 1)) if p == 0 else
                                         (_row_blocks(inv, size, 0), m)), size)
                for inv, m, p in zip(invs, moved, hot)]
        size *= 2
    return invs


def _exact_prefix_sums(x, tri, rows):
    hi = x.astype(BF16).astype(F32)
    mid = (x - hi).astype(BF16).astype(F32)
    lo = (x - hi - mid).astype(BF16).astype(F32)
    if rows:
        n = x.shape[0]
        res = _mm(jnp.concatenate([hi, mid, lo], axis=0).astype(BF16), tri)
        return res[:n] + res[n:2 * n] + res[2 * n:]
    quarter = LANES // 4
    packed = hi + pltpu.roll(mid, quarter, 1) + pltpu.roll(lo, 2 * quarter, 1)
    res = _mm(tri, packed.astype(BF16))
    return res + pltpu.roll(res, LANES - quarter, 1) + pltpu.roll(res, LANES - 2 * quarter, 1)


def _dn_core_kernel(qf_ref, kf_ref, vf_ref, gbf_ref, gtf_ref, qb_ref, kb_ref, vb_ref, gbb_ref, gtb_ref,
                    of_ref, ob_ref, state_ref, *, n_heads):
    @pl.when(pl.program_id(1) == 0)
    def _():
        state_ref[...] = jnp.zeros_like(state_ref)

    c = DN_CHUNK
    d = DN_HEAD_DIM
    n_sub = qf_ref.shape[0] // c
    ii = lax.broadcasted_iota(jnp.int32, (c, c), 0)
    jj = lax.broadcasted_iota(jnp.int32, (c, c), 1)
    lower_eq, upper_eq = jj <= ii, jj >= ii
    lower, upper = jj < ii, jj > ii
    same_block = {}
    size = SOLVE_LEAF
    while size <= c:
        shift = int(math.log2(size))
        same_block[size] = (ii >> shift) == (jj >> shift)
        size *= 2

    items = []
    for direction, (q_ref, k_ref, v_ref, gb_ref, gt_ref, o_ref) in enumerate(
            ((qf_ref, kf_ref, vf_ref, gbf_ref, gtf_ref, of_ref),
             (qb_ref, kb_ref, vb_ref, gbb_ref, gtb_ref, ob_ref))):
        reverse = direction == 1
        tri_col = (upper_eq if reverse else lower_eq).astype(BF16)
        tri_row = (lower_eq if reverse else upper_eq).astype(BF16)
        for sub in range(n_sub):
            rows = slice(sub * c, (sub + 1) * c)
            gb = gb_ref[rows, :]
            lane = lax.broadcasted_iota(jnp.int32, gb.shape, 1)
            cum_col = _exact_prefix_sums(jnp.where(lane < LANES // 4, gb, 0.0), tri_col, rows=False)
            cum_row = _exact_prefix_sums(gt_ref[:, rows], tri_row, rows=True)
            for head in range(n_heads):
                g_idx = direction * n_heads + head
                b_idx = 2 * n_heads + g_idx
                cols = slice(head * d, (head + 1) * d)
                items.append(dict(
                    direction=direction, head=head, sub=sub, o_ref=o_ref, rows=rows, cols=cols, reverse=reverse,
                    earlier_eq=upper_eq if reverse else lower_eq,
                    earlier=upper if reverse else lower,
                    q=q_ref[rows, cols], k=k_ref[rows, cols], v=v_ref[rows, cols],
                    g_col=cum_col[:, g_idx:g_idx + 1], beta_col=gb[:, b_idx:b_idx + 1],
                    g_row=cum_row[g_idx:g_idx + 1, :],
                    g_last=cum_col[0:1, g_idx:g_idx + 1] if reverse else cum_col[c - 1:c, g_idx:g_idx + 1]))

    grams = [_mm_nt(jnp.concatenate([it["k"], it["q"]], axis=0), it["k"]) for it in items]
    neg_ls = []
    for it, gram in zip(items, grams):
        g_wide = jnp.broadcast_to(it["g_col"], (c, d))
        beta_wide = jnp.broadcast_to(it["beta_col"], (c, d))
        it["g_last_wide"] = jnp.broadcast_to(it["g_last"], (c, d))
        decay = jnp.exp(jnp.where(it["earlier_eq"], g_wide - it["g_row"], NEG_INF))
        neg_ls.append((jnp.where(it["earlier"], gram[:c] * decay, 0.0) * (-beta_wide)).astype(BF16))
        it["intra"] = jnp.where(it["earlier_eq"], gram[c:] * decay, 0.0).astype(BF16)
        kf = it["k"].astype(F32)
        e_g = jnp.exp(g_wide)
        it["v_beta"] = (it["v"].astype(F32) * beta_wide).astype(BF16)
        it["kq"] = jnp.concatenate([(kf * (beta_wide * e_g)).astype(BF16),
                                    (it["q"].astype(F32) * e_g).astype(BF16)], axis=0)
        it["k_dec"] = (kf * jnp.exp(it["g_last_wide"] - g_wide)).astype(BF16)
    invs = _unit_triangular_inverses(neg_ls, [it["reverse"] for it in items], same_block)
    for it, inv in zip(items, invs):
        it["inv"] = inv

    for step in range(n_sub):
        live = [it for it in items if it["sub"] == (n_sub - 1 - step if it["reverse"] else step)]
        states = [state_ref[it["direction"], it["head"]] for it in live]
        wss = [_mm(it["kq"], state.astype(BF16)) for it, state in zip(live, states)]
        v_news = [_mm(it["inv"], (it["v_beta"].astype(F32) - ws[:c]).astype(BF16)).astype(BF16)
                  for it, ws in zip(live, wss)]
        outs = [ws[c:] + _mm(it["intra"], v_new) for it, ws, v_new in zip(live, wss, v_news)]
        updates = [_mm_tn(it["k_dec"], v_new) for it, v_new in zip(live, v_news)]
        for it, out, state, update in zip(live, outs, states, updates):
            it["o_ref"][it["rows"], it["cols"]] = out.astype(BF16)
            state_ref[it["direction"], it["head"]] = state * jnp.exp(it["g_last_wide"]) + update


def _dn_core(q, k, v, gb, gbt, batch, seq, chunks_per_step=4):
    t, tok = q.shape
    n_heads = tok // DN_HEAD_DIM
    rows = chunks_per_step * DN_CHUNK
    steps = seq // rows
    n_rows = gbt.shape[0]

    fwd = lambda b, i: (b * steps + i, 0)
    bwd = lambda b, i: (b * steps + steps - 1 - i, 0)
    fwd_t = lambda b, i: (0, b * steps + i)
    bwd_t = lambda b, i: (0, b * steps + steps - 1 - i)

    def side(tok_map, t_map):
        return [pl.BlockSpec((rows, tok), tok_map)] * 3 + [
            pl.BlockSpec((rows, LANES), tok_map), pl.BlockSpec((n_rows, rows), t_map)]

    return pl.pallas_call(
        functools.partial(_dn_core_kernel, n_heads=n_heads),
        grid=(batch, steps),
        in_specs=side(fwd, fwd_t) + side(bwd, bwd_t),
        out_specs=[pl.BlockSpec((rows, tok), fwd), pl.BlockSpec((rows, tok), bwd)],
        out_shape=[jax.ShapeDtypeStruct((t, tok), BF16)] * 2,
        scratch_shapes=[pltpu.VMEM((2, n_heads, DN_HEAD_DIM, DN_HEAD_DIM), F32)],
        compiler_params=_params("parallel", "arbitrary"),
        name="dn_core",
    )(q, k, v, gb, gbt, q, k, v, gb, gbt)


def _dn_out_kernel(of_ref, ob_ref, z_ref, qmem_ref, kmem_ref, vmem_ref, x_ref, onorm_ref, w_ref, g_ref,
                   out_ref):
    tok = of_ref.shape[1]
    parts = []
    for c0 in range(0, tok, DN_HEAD_DIM):
        cols = slice(c0, c0 + DN_HEAD_DIM)
        o = of_ref[:, cols].astype(F32) + ob_ref[:, cols].astype(F32)
        o = o * lax.rsqrt(jnp.mean(o * o, axis=-1, keepdims=True) + EPS) * onorm_ref[...]
        parts.append((o * _silu(z_ref[:, cols].astype(F32))).astype(BF16))
    mem_out = _memory_attention(qmem_ref[...], kmem_ref[0], vmem_ref[0])
    mixed = jnp.concatenate(parts + [mem_out], axis=-1)
    y = jnp.dot(mixed, w_ref[...], preferred_element_type=F32)
    out_ref[...] = x_ref[...] + _rms(y, g_ref[...])


def _dn_out(o_f, o_b, z, q_mem, k_mem, v_mem, x2, out_norm, w_out, gain, seq, tm=1024):
    t, d = x2.shape
    tok = o_f.shape[1]
    row = lambda i: (i, 0)
    return pl.pallas_call(
        _dn_out_kernel,
        grid=(t // tm,),
        in_specs=[pl.BlockSpec((tm, tok), row)] * 3 + _memory_specs(q_mem, k_mem, seq, tm)
        + [pl.BlockSpec((tm, d), row), _const_spec((1, DN_HEAD_DIM)), _const_spec(w_out.shape),
           _const_spec((1, d))],
        out_specs=pl.BlockSpec((tm, d), row),
        out_shape=jax.ShapeDtypeStruct((t, d), F32),
        compiler_params=_params("parallel"),
        name="dn_out",
    )(o_f, o_b, z, q_mem, k_mem, v_mem, x2, out_norm.reshape(1, DN_HEAD_DIM), w_out, gain.reshape(1, d))


def _attention_layer(x2, mem, batch, seq, rel_bias, w_in, w_out, mem_gain, w_kv, g_pre, g_post):
    n_groups = len(DILATED_GROUPS)
    tok = n_groups * GROUP_WIDTH
    dils = tuple(dil for _, dil in DILATED_GROUPS)
    w_qkv = w_in[:, :3 * tok].reshape(-1, 3, n_groups, GROUP_WIDTH).transpose(0, 2, 1, 3).astype(BF16)
    group_weights = [w_qkv[:, g].reshape(-1, 3 * GROUP_WIDTH) for g in range(n_groups)]
    *qkvs, q_mem = _att_proj(x2, g_pre, group_weights, w_in[:, 3 * tok:].astype(BF16), dils)
    os_, lses = [], []
    for group, dil in enumerate(dils):
        n_res = min(dil, 2)
        o, lse = _band_attn(qkvs[group], _band_bias(rel_bias, group, dil), batch, seq, dil, n_res,
                            q_tile=512 // n_res)
        os_.append(o)
        lses.append(lse)
    k_mem, v_mem = _mem_kv(mem, mem_gain, w_kv.astype(BF16))
    return _att_out(os_, lses, dils, q_mem, k_mem, v_mem, x2, w_out.astype(BF16), g_post, seq)


def _deltanet_layer(x2, mem, batch, seq, w_in, conv_w, a_log, dt_bias, out_norm, w_out, mem_gain, w_kv,
                    g_pre, g_post):
    d = x2.shape[1]
    n_heads = a_log.shape[1]
    tok = n_heads * DN_HEAD_DIM
    n_gate = 4 * n_heads
    w_gate = w_in[:, 4 * tok:4 * tok + n_gate].reshape(d, 2, 2, n_heads).transpose(0, 2, 1, 3).reshape(d, n_gate)
    w_gate = jnp.pad(w_gate, ((0, 0), (0, LANES - n_gate)))
    w_in_b = w_in.astype(BF16)
    pad_row = lambda p: jnp.pad(p.reshape(1, -1).astype(F32), ((0, 0), (0, LANES - 2 * n_heads)))
    q, k, v, z, q_mem, gb, gbt = _dn_proj(x2, g_pre, w_in_b, w_gate.astype(BF16), w_in_b[:, 4 * tok + n_gate:],
                                          conv_w.astype(F32), pad_row(a_log), pad_row(dt_bias), batch, seq, tok,
                                          2 * n_heads)
    o_f, o_b = _dn_core(q, k, v, gb, gbt, batch, seq)
    k_mem, v_mem = _mem_kv(mem, mem_gain, w_kv.astype(BF16))
    return _dn_out(o_f, o_b, z, q_mem, k_mem, v_mem, x2, out_norm, w_out.astype(BF16), g_post, seq)


def kernel(x, mem, rel_bias, att_w_in, att_w_out, dn_w_in, dn_conv, dn_a_log, dn_dt_bias, dn_out_norm,
           dn_w_out, mem_norm, mem_w_kv, norm_mix_pre, norm_mix_post, norm_ffn_pre, norm_ffn_post,
           ffn_w_gate_up, ffn_w_down):
    batch, seq, d = x.shape
    depth = norm_mix_pre.shape[0]
    n_mixers = 2
    x2 = x.reshape(batch * seq, d)
    for i in range(depth):
        j = i // n_mixers
        if i % n_mixers == 0:
            x2 = _attention_layer(x2, mem, batch, seq, rel_bias, att_w_in[j], att_w_out[j], mem_norm[i],
                                  mem_w_kv[i], norm_mix_pre[i], norm_mix_post[i])
        else:
            x2 = _deltanet_layer(x2, mem, batch, seq, dn_w_in[j], dn_conv[j], dn_a_log[j], dn_dt_bias[j],
                                 dn_out_norm[j], dn_w_out[j], mem_norm[i], mem_w_kv[i], norm_mix_pre[i],
                                 norm_mix_post[i])
        x2 = _ffn(x2, norm_ffn_pre[i], ffn_w_gate_up[i].astype(BF16), ffn_w_down[i].astype(BF16),
                  norm_ffn_post[i])
    return x2.reshape(batch, seq, d)
```

```python
import functools
import math

import jax
import jax.numpy as jnp
import numpy as np
from jax import lax
from jax.experimental import pallas as pl
from jax.experimental.pallas import tpu as pltpu

F32 = jnp.float32
BF16 = jnp.bfloat16

EPS = 1e-6
NEG_INF = -1e30

LANES = 128
VMEM_LIMIT_BYTES = 56 * 1024 * 1024

DILATED_GROUPS = ((128, 1), (512, 4), (2048, 16))
ATT_HEAD_DIM = 64
HEADS_PER_GROUP = 4
GROUP_WIDTH = HEADS_PER_GROUP * ATT_HEAD_DIM
BAND_HALF = 64
REL_BUCKETS = 32
REL_MAX_DIST = 1024
MEM_HEADS = 4
MEM_HEAD_DIM = 64
DN_HEAD_DIM = 128
DN_CONV = 5
DN_CHUNK = 128
SOLVE_LEAF = 16
CONV_ROWS = 64

ATT_Q_BLOCK = 128
ATT_K_WINDOW = ATT_Q_BLOCK + 2 * BAND_HALF


def _params(*semantics):
    return pltpu.CompilerParams(dimension_semantics=semantics, vmem_limit_bytes=VMEM_LIMIT_BYTES)


def _rms(x, gain):
    return x * lax.rsqrt(jnp.mean(x * x, axis=-1, keepdims=True) + EPS) * gain


def _silu(x):
    return x / (1.0 + jnp.exp2(x * (-1.0 / math.log(2.0))))


def _const_spec(shape):
    return pl.BlockSpec(shape, lambda *_: (0,) * len(shape))


def _mem_kv_kernel(mem_ref, g_ref, w_ref, k_ref, v_ref):
    h = _rms(mem_ref[0], g_ref[...]).astype(BF16)
    kv = jnp.dot(h, w_ref[...], preferred_element_type=F32)
    width = k_ref.shape[-1]
    k_ref[0] = kv[:, :width].astype(BF16)
    v_ref[0] = kv[:, width:].astype(BF16)


def _mem_kv(mem, gain, w_kv):
    b, m, d = mem.shape
    width = w_kv.shape[1] // 2
    return pl.pallas_call(
        _mem_kv_kernel,
        grid=(b,),
        in_specs=[pl.BlockSpec((1, m, d), lambda i: (i, 0, 0)), _const_spec((1, d)), _const_spec(w_kv.shape)],
        out_specs=[pl.BlockSpec((1, m, width), lambda i: (i, 0, 0))] * 2,
        out_shape=[jax.ShapeDtypeStruct((b, m, width), BF16)] * 2,
        compiler_params=_params("parallel"),
        name="mem_kv",
    )(mem, gain.reshape(1, d), w_kv)


def _memory_attention(q, k, v):
    lane = lax.broadcasted_iota(jnp.int32, q.shape, 1)
    in_head = [(lane >= h * MEM_HEAD_DIM) & (lane < (h + 1) * MEM_HEAD_DIM) for h in range(MEM_HEADS)]
    logits = [lax.dot_general(jnp.where(mask, q, jnp.zeros_like(q)), k, (((1,), (1,)), ((), ())),
                              preferred_element_type=F32) for mask in in_head]
    probs = []
    for s in logits:
        s = s * (MEM_HEAD_DIM ** -0.5)
        p = jnp.exp(s - jnp.max(s, axis=-1, keepdims=True))
        probs.append((p / jnp.sum(p, axis=-1, keepdims=True)).astype(BF16))
    pvs = [jnp.dot(p, v, preferred_element_type=F32) for p in probs]
    out = pvs[-1]
    for mask, pv in zip(in_head[:-1], pvs[:-1]):
        out = jnp.where(mask, pv, out)
    return out.astype(BF16)


def _memory_specs(q_mem, k_mem, seq, tm):
    per_seq = seq // tm
    kv_spec = pl.BlockSpec((1,) + k_mem.shape[1:], lambda i: (i // per_seq, 0, 0))
    return [pl.BlockSpec((tm, q_mem.shape[1]), lambda i: (i, 0)), kv_spec, kv_spec]


def _t5_bucket(rel):
    half = REL_BUCKETS // 2
    max_exact = half // 2
    n = np.abs(rel)
    large = max_exact + (np.log(np.maximum(n, 1) / max_exact) / math.log(REL_MAX_DIST / max_exact)
                         * (half - max_exact)).astype(np.int64)
    large = np.minimum(large, half - 1)
    return ((rel > 0) * half + np.where(n < max_exact, n, large)).astype(np.int32)


def _band_bias(rel_bias, group, dil):
    period = ATT_Q_BLOCK + ATT_K_WINDOW
    heads = slice(group * HEADS_PER_GROUP, (group + 1) * HEADS_PER_GROUP)
    tables = []
    for variant in range(3):
        rel = np.arange(period) - (ATT_Q_BLOCK - 1) - variant * BAND_HALF
        in_band = np.abs(rel) <= BAND_HALF
        bucket = _t5_bucket(np.clip(rel, -BAND_HALF, BAND_HALF) * dil)
        f = jnp.where(in_band[None, :], jnp.transpose(rel_bias[bucket][:, heads]).astype(F32), NEG_INF)
        flat = jnp.tile(f, (1, ATT_Q_BLOCK))[:, :ATT_Q_BLOCK * (period - 1)]
        skew = flat.reshape(HEADS_PER_GROUP, ATT_Q_BLOCK, period - 1)
        tables.append(skew[:, :, ATT_Q_BLOCK - 1:ATT_Q_BLOCK - 1 + ATT_K_WINDOW])
    return jnp.stack(tables)


def _band_attn_kernel(x_ref, bias_ref, o_ref, lse_ref, *, sub_len, n_res, q_tile):
    lane = lax.broadcasted_iota(jnp.int32, (ATT_Q_BLOCK, LANES), 1)
    low_half = lane < ATT_HEAD_DIM
    n_pairs = GROUP_WIDTH // LANES

    def tile(t, carry):
        blocks = []
        for res in range(n_res):
            for blk in range(q_tile // ATT_Q_BLOCK):
                i0 = pl.multiple_of(t * q_tile + blk * ATT_Q_BLOCK, ATT_Q_BLOCK)
                ws = jnp.clip(i0 - BAND_HALF, 0, sub_len - ATT_K_WINDOW)
                variant = (i0 - ws) // BAND_HALF
                ws = pl.multiple_of(ws, BAND_HALF)
                base = res * 3 * GROUP_WIDTH
                blocks.append(dict(
                    res=res, rows=pl.ds(i0, ATT_Q_BLOCK), variant=variant,
                    q=x_ref[0, pl.ds(i0, ATT_Q_BLOCK), base:base + GROUP_WIDTH]
                    * jnp.asarray(ATT_HEAD_DIM ** -0.5, BF16),
                    k=x_ref[0, pl.ds(ws, ATT_K_WINDOW), base + GROUP_WIDTH:base + 2 * GROUP_WIDTH],
                    v=x_ref[0, pl.ds(ws, ATT_K_WINDOW), base + 2 * GROUP_WIDTH:base + 3 * GROUP_WIDTH]))
        heads = [(b, pair, a) for b in blocks for pair in range(n_pairs) for a in range(2)]
        logits = []
        for b, pair, a in heads:
            cols = slice(pair * LANES, (pair + 1) * LANES)
            q_pair = b["q"][:, cols]
            keep = low_half if a == 0 else jnp.logical_not(low_half)
            qh = jnp.where(keep, q_pair, jnp.zeros_like(q_pair))
            logits.append(_mm_nt(qh, b["k"][:, cols]))
        probs, dens, lses = [], [], []
        for (b, pair, a), s in zip(heads, logits):
            s = s + bias_ref[b["variant"], 2 * pair + a]
            m = jnp.max(s, axis=-1, keepdims=True)
            p = jnp.exp(s - m)
            den = jnp.sum(p, axis=-1, keepdims=True)
            probs.append(p.astype(BF16))
            dens.append(den)
            lses.append(m + jnp.log(den))
        pvs = [_mm(p, b["v"][:, pair * LANES:(pair + 1) * LANES]) for (b, pair, a), p in zip(heads, probs)]
        for idx in range(0, len(heads), 2):
            b, pair, _ = heads[idx]
            cols = pl.ds(b["res"] * GROUP_WIDTH + pair * LANES, LANES)
            out = jnp.where(low_half, pvs[idx] / dens[idx], pvs[idx + 1] / dens[idx + 1])
            lse = jnp.where(low_half, jnp.broadcast_to(lses[idx], (ATT_Q_BLOCK, LANES)),
                            jnp.broadcast_to(lses[idx + 1], (ATT_Q_BLOCK, LANES)))
            o_ref[0, b["rows"], cols] = out.astype(o_ref.dtype)
            lse_ref[0, b["rows"], cols] = lse
        return carry

    lax.fori_loop(0, sub_len // q_tile, tile, 0)


def _band_attn(qkv, bias, batch, seq, dil, n_res, q_tile=256):
    rows, width = qkv.shape
    sub_len = seq // dil
    qkv3 = qkv.reshape(batch, sub_len, width)
    x_spec = pl.BlockSpec((1, sub_len, n_res * 3 * GROUP_WIDTH), lambda b, r: (b, 0, r))
    o_spec = pl.BlockSpec((1, sub_len, n_res * GROUP_WIDTH), lambda b, r: (b, 0, r))
    o, lse = pl.pallas_call(
        functools.partial(_band_attn_kernel, sub_len=sub_len, n_res=n_res, q_tile=q_tile),
        grid=(batch, dil // n_res),
        in_specs=[x_spec, _const_spec(bias.shape)],
        out_specs=[o_spec, o_spec],
        out_shape=[jax.ShapeDtypeStruct((batch, sub_len, dil * GROUP_WIDTH), BF16),
                   jax.ShapeDtypeStruct((batch, sub_len, dil * GROUP_WIDTH), F32)],
        compiler_params=_params("parallel", "parallel"),
        name=f"band_attn_d{dil}",
    )(qkv3, bias)
    return o.reshape(rows, dil * GROUP_WIDTH), lse.reshape(rows, dil * GROUP_WIDTH)


def _att_proj_kernel(x_ref, g_ref, *refs, dils):
    n = len(dils)
    w_refs, wmem_ref = refs[:n], refs[n]
    out_refs, qmem_ref, stage_ref = refs[n + 1:2 * n + 1], refs[2 * n + 1], refs[2 * n + 2]
    h = _rms(x_ref[...], g_ref[...]).astype(BF16)
    tm = h.shape[0]
    for w_ref, o_ref, dil in zip(w_refs, out_refs, dils):
        y = jnp.dot(h, w_ref[...], preferred_element_type=F32)
        width = y.shape[1]
        if dil == 1:
            o_ref[...] = y.astype(BF16)
        else:
            for c in range(width // LANES):
                stage_ref[c] = y[:, c * LANES:(c + 1) * LANES]
            for r in range(dil):
                for c in range(width // LANES):
                    lanes = pl.ds(r * width + c * LANES, LANES)
                    o_ref[:, lanes] = stage_ref[c, pl.ds(r, tm // dil, stride=dil), :].astype(BF16)
    qmem_ref[...] = jnp.dot(h, wmem_ref[...], preferred_element_type=F32).astype(BF16)


def _att_proj(x2, gain, group_weights, w_mem, dils, tm=512):
    t, d = x2.shape
    width = group_weights[0].shape[1]
    row = lambda i: (i, 0)
    return pl.pallas_call(
        functools.partial(_att_proj_kernel, dils=dils),
        grid=(t // tm,),
        in_specs=[pl.BlockSpec((tm, d), row), _const_spec((1, d))]
        + [_const_spec(w.shape) for w in group_weights] + [_const_spec(w_mem.shape)],
        out_specs=[pl.BlockSpec((tm // dil, dil * width), row) for dil in dils]
        + [pl.BlockSpec((tm, w_mem.shape[1]), row)],
        out_shape=[jax.ShapeDtypeStruct((t // dil, dil * width), BF16) for dil in dils]
        + [jax.ShapeDtypeStruct((t, w_mem.shape[1]), BF16)],
        scratch_shapes=[pltpu.VMEM((width // LANES, tm, LANES), F32)],
        compiler_params=_params("parallel"),
        name="att_proj",
    )(x2, gain.reshape(1, d), *group_weights, w_mem)


def _att_out_kernel(*refs, dils):
    n = len(dils)
    o_refs, l_refs = refs[:n], refs[n:2 * n]
    qmem_ref, kmem_ref, vmem_ref, x_ref, w_ref, g_ref, out_ref, stage_ref = refs[2 * n:]
    tm = x_ref.shape[0]

    def token_major(ref, dil, slot):
        if dil == 1:
            return ref[...].astype(F32)
        tiles = GROUP_WIDTH // LANES
        for r in range(dil):
            for c in range(tiles):
                lanes = pl.ds(r * GROUP_WIDTH + c * LANES, LANES)
                stage_ref[slot * tiles + c, pl.ds(r, tm // dil, stride=dil), :] = ref[:, lanes].astype(F32)
        return jnp.concatenate([stage_ref[slot * tiles + c] for c in range(tiles)], axis=-1)

    lses = [token_major(ref, dil, 2 * i) for i, (ref, dil) in enumerate(zip(l_refs, dils))]
    outs = [token_major(ref, dil, 2 * i + 1) for i, (ref, dil) in enumerate(zip(o_refs, dils))]
    mx = functools.reduce(jnp.maximum, lses)
    es = [jnp.exp(l - mx) for l in lses]
    tot = functools.reduce(lambda a, b: a + b, es)
    parts = [(o * (e / tot)).astype(BF16) for o, e in zip(outs, es)]
    mem_out = _memory_attention(qmem_ref[...], kmem_ref[0], vmem_ref[0])
    mixed = jnp.concatenate(parts + [mem_out], axis=-1)
    y = jnp.dot(mixed, w_ref[...], preferred_element_type=F32)
    out_ref[...] = x_ref[...] + _rms(y, g_ref[...])


def _att_out(os_, lses, dils, q_mem, k_mem, v_mem, x2, w_out, gain, seq, tm=1024):
    t, d = x2.shape
    row = lambda i: (i, 0)
    group_specs = [pl.BlockSpec((tm // dil, dil * GROUP_WIDTH), row) for dil in dils]
    return pl.pallas_call(
        functools.partial(_att_out_kernel, dils=dils),
        grid=(t // tm,),
        in_specs=group_specs + group_specs + _memory_specs(q_mem, k_mem, seq, tm)
        + [pl.BlockSpec((tm, d), row), _const_spec(w_out.shape), _const_spec((1, d))],
        out_specs=pl.BlockSpec((tm, d), row),
        out_shape=jax.ShapeDtypeStruct((t, d), F32),
        scratch_shapes=[pltpu.VMEM((2 * len(dils) * (GROUP_WIDTH // LANES), tm, LANES), F32)],
        compiler_params=_params("parallel"),
        name="att_out",
    )(*os_, *lses, q_mem, k_mem, v_mem, x2, w_out, gain.reshape(1, d))


def _ffn_kernel(x_ref, gpre_ref, wg_ref, wu_ref, wd_ref, gpost_ref, out_ref, *, row_blocks):
    rows = x_ref.shape[0] // row_blocks
    for blk in range(row_blocks):
        sl = slice(blk * rows, (blk + 1) * rows)
        x = x_ref[sl, :]
        h = _rms(x, gpre_ref[...]).astype(BF16)
        gate = jnp.dot(h, wg_ref[...], preferred_element_type=F32)
        up = jnp.dot(h, wu_ref[...], preferred_element_type=F32)
        act = (_silu(gate) * up).astype(BF16)
        y = jnp.dot(act, wd_ref[...], preferred_element_type=F32)
        out_ref[sl, :] = x + _rms(y, gpost_ref[...])


def _ffn(x2, g_pre, w_gate_up, w_down, g_post, tm=512, row_blocks=2):
    t, d = x2.shape
    d_ff = w_down.shape[0]
    row = lambda i: (i, 0)
    return pl.pallas_call(
        functools.partial(_ffn_kernel, row_blocks=row_blocks),
        grid=(t // tm,),
        in_specs=[pl.BlockSpec((tm, d), row), _const_spec((1, d)),
                  pl.BlockSpec((d, d_ff), lambda i: (0, 0)), pl.BlockSpec((d, d_ff), lambda i: (0, 1)),
                  _const_spec(w_down.shape), _const_spec((1, d))],
        out_specs=pl.BlockSpec((tm, d), row),
        out_shape=jax.ShapeDtypeStruct((t, d), F32),
        compiler_params=_params("parallel"),
        name="ffn",
    )(x2, g_pre.reshape(1, d), w_gate_up, w_gate_up, w_down, g_post.reshape(1, d))


def _dn_proj_kernel(x_ref, xprev_ref, xnext_ref, g_ref, wqkv_ref, wgate_ref, conv_ref,
                    alog_ref, dtb_ref, q_ref, k_ref, v_ref, gb_ref, gbt_ref, ext_ref,
                    *, halo, n_dir_heads):
    i = pl.program_id(1)
    last = pl.num_programs(1) - 1
    tm = x_ref.shape[0]
    width = wqkv_ref.shape[1]
    tok = width // 3
    gain = g_ref[...]
    h = _rms(x_ref[...], gain).astype(BF16)
    h_prev = jnp.where(i > 0, _rms(xprev_ref[...], gain), 0.0).astype(BF16)
    h_next = jnp.where(i < last, _rms(xnext_ref[...], gain), 0.0).astype(BF16)
    h_ext = jnp.concatenate([h_prev, h, h_next], axis=0)
    chunk = 2 * LANES
    for c0 in range(0, width, chunk):
        y_ext = jnp.dot(h_ext, wqkv_ref[:, c0:c0 + chunk], preferred_element_type=F32)
        for half in range(chunk // LANES):
            c1 = c0 + half * LANES
            cols = slice(c1, c1 + LANES)
            ext = ext_ref.at[(c1 // LANES) % ext_ref.shape[0]]
            ext[...] = y_ext[:, half * LANES:(half + 1) * LANES]
            taps = [conv_ref[j:j + 1, cols] for j in range(DN_CONV)]
            which, col = divmod(c1, tok)
            out_ref = (q_ref, k_ref, v_ref)[which]
            for r0 in range(0, tm, CONV_ROWS):
                first = halo + r0 - DN_CONV // 2
                acc = ext[first:first + CONV_ROWS, :] * taps[0]
                for j in range(1, DN_CONV):
                    acc = acc + ext[first + j:first + j + CONV_ROWS, :] * taps[j]
                y = _silu(acc)
                if which < 2:
                    scale = lax.rsqrt(jnp.sum(y * y, axis=-1, keepdims=True) + EPS)
                    y = y * (scale * (DN_HEAD_DIM ** -0.5) if which == 0 else scale)
                out_ref[r0:r0 + CONV_ROWS, col:col + LANES] = y.astype(BF16)
    gate = jnp.dot(h, wgate_ref[...], preferred_element_type=F32)
    shifted = gate + dtb_ref[...]
    softplus = jnp.maximum(shifted, 0.0) + jnp.log(1.0 + jnp.exp(-jnp.abs(shifted)))
    decay = -jnp.exp(alog_ref[...]) * softplus
    beta = 1.0 / (1.0 + jnp.exp(-gate))
    lane = lax.broadcasted_iota(jnp.int32, gate.shape, 1)
    gb = jnp.where(lane < n_dir_heads, decay, beta)
    gb_ref[...] = gb
    gbt_ref[...] = jnp.transpose(gb)[:gbt_ref.shape[0], :]


def _dn_proj(x2, gain, w_in, w_gate, conv_w, a_log_row, dt_bias_row, batch, seq, tok, n_dir_heads,
             tm=512, halo=16):
    t, d = x2.shape
    n_gate_rows = 8 * ((2 * n_dir_heads + 7) // 8)
    per_seq = seq // tm
    blocks_per_tile = tm // halo
    n_halo_blocks = t // halo

    def cur_map(b, i):
        return (b * per_seq + i, 0)

    def prev_map(b, i):
        return (jnp.maximum((b * per_seq + i) * blocks_per_tile - 1, 0), 0)

    def next_map(b, i):
        return (jnp.minimum((b * per_seq + i + 1) * blocks_per_tile, n_halo_blocks - 1), 0)

    return pl.pallas_call(
        functools.partial(_dn_proj_kernel, halo=halo, n_dir_heads=n_dir_heads),
        grid=(batch, per_seq),
        in_specs=[pl.BlockSpec((tm, d), cur_map), pl.BlockSpec((halo, d), prev_map),
                  pl.BlockSpec((halo, d), next_map), _const_spec((1, d)),
                  pl.BlockSpec((d, 3 * tok), lambda b, i: (0, 0)), _const_spec(w_gate.shape),
                  _const_spec(conv_w.shape), _const_spec((1, LANES)), _const_spec((1, LANES))],
        out_specs=[pl.BlockSpec((tm, tok), cur_map)] * 3
        + [pl.BlockSpec((tm, LANES), cur_map), pl.BlockSpec((n_gate_rows, tm), lambda b, i: (0, b * per_seq + i))],
        out_shape=[jax.ShapeDtypeStruct((t, tok), BF16)] * 3
        + [jax.ShapeDtypeStruct((t, LANES), F32), jax.ShapeDtypeStruct((n_gate_rows, t), F32)],
        scratch_shapes=[pltpu.VMEM((4, tm + 2 * halo, LANES), F32)],
        compiler_params=_params("parallel", "arbitrary"),
        name="dn_proj",
    )(x2, x2, x2, gain.reshape(1, d), w_in, w_gate, conv_w, a_log_row, dt_bias_row)


def _mm(a, b):
    return jnp.dot(a, b, preferred_element_type=F32)


def _mm_nt(a, b):
    return lax.dot_general(a, b, (((1,), (1,)), ((), ())), preferred_element_type=F32)


def _mm_tn(a, b):
    return lax.dot_general(a, b, (((0,), (0,)), ((), ())), preferred_element_type=F32)


def _row_blocks(x, size, parity):
    return jnp.concatenate([x[b * size:(b + 1) * size] for b in range(parity, x.shape[0] // size, 2)], axis=0)


def _interleave_row_blocks(even, odd, size):
    pieces = []
    for b in range(even.shape[0] // size):
        pieces += [even[b * size:(b + 1) * size], odd[b * size:(b + 1) * size]]
    return jnp.concatenate(pieces, axis=0)


def _unit_triangular_inverses(neg_ls, uppers, same_block):
    n = neg_ls[0].shape[0]
    ii = lax.broadcasted_iota(jnp.int32, (n, n), 0)
    jj = lax.broadcasted_iota(jnp.int32, (n, n), 1)
    identity = (ii == jj).astype(F32).astype(BF16)
    zero = jnp.zeros((n, n), BF16)
    plus_identity = lambda m: m + identity
    leaf = same_block[SOLVE_LEAF].astype(F32).astype(BF16)
    xbs = [nl * leaf for nl in neg_ls]
    invs = [plus_identity(xb) for xb in xbs]
    power = 1
    while 2 * power < SOLVE_LEAF:
        xbs = [_mm(xb, xb).astype(BF16) for xb in xbs]
        invs = [_mm(plus_identity(xb), inv).astype(BF16) for xb, inv in zip(xbs, invs)]
        power *= 2
    size = SOLVE_LEAF
    while size < n:
        merge = (same_block[2 * size] & jnp.logical_not(same_block[size])).astype(F32).astype(BF16)
        hot = [0 if upper else 1 for upper in uppers]
        offs = [_row_blocks(nl, size, p) * _row_blocks(merge, size, p) for nl, p in zip(neg_ls, hot)]
        rights = [_mm(off, inv).astype(BF16) for off, inv in zip(offs, invs)]
        half_zero = zero[:n // 2]
        rights = [_interleave_row_blocks(*((r, half_zero) if p == 0 else (half_zero, r)), size)
                  for r, p in zip(rights, hot)]
        moved = [_mm(_row_blocks(inv, size, p), plus_identity(right)).astype(BF16)
                 for inv, right, p in zip(invs, rights, hot)]
        invs = [_interleave_row_blocks(*((m, _row_blocks(inv, size, 1)) if p == 0 else
                                         (_row_blocks(inv, size, 0), m)), size)
                for inv, m, p in zip(invs, moved, hot)]
        size *= 2
    return invs


def _exact_prefix_sums(x, tri, rows):
    hi = x.astype(BF16).astype(F32)
    mid = (x - hi).astype(BF16).astype(F32)
    lo = (x - hi - mid).astype(BF16).astype(F32)
    if rows:
        n = x.shape[0]
        res = _mm(jnp.concatenate([hi, mid, lo], axis=0).astype(BF16), tri)
        return res[:n] + res[n:2 * n] + res[2 * n:]
    quarter = LANES // 4
    packed = hi + pltpu.roll(mid, quarter, 1) + pltpu.roll(lo, 2 * quarter, 1)
    res = _mm(tri, packed.astype(BF16))
    return res + pltpu.roll(res, LANES - quarter, 1) + pltpu.roll(res, LANES - 2 * quarter, 1)


def _dn_core_kernel(qf_ref, kf_ref, vf_ref, gbf_ref, gtf_ref, qb_ref, kb_ref, vb_ref, gbb_ref, gtb_ref,
                    of_ref, ob_ref, state_ref, *, n_heads):
    @pl.when(pl.program_id(1) == 0)
    def _():
        state_ref[...] = jnp.zeros_like(state_ref)

    c = DN_CHUNK
    d = DN_HEAD_DIM
    n_sub = qf_ref.shape[0] // c
    ii = lax.broadcasted_iota(jnp.int32, (c, c), 0)
    jj = lax.broadcasted_iota(jnp.int32, (c, c), 1)
    lower_eq, upper_eq = jj <= ii, jj >= ii
    lower, upper = jj < ii, jj > ii
    same_block = {}
    size = SOLVE_LEAF
    while size <= c:
        shift = int(math.log2(size))
        same_block[size] = (ii >> shift) == (jj >> shift)
        size *= 2

    items = []
    for direction, (q_ref, k_ref, v_ref, gb_ref, gt_ref, o_ref) in enumerate(
            ((qf_ref, kf_ref, vf_ref, gbf_ref, gtf_ref, of_ref),
             (qb_ref, kb_ref, vb_ref, gbb_ref, gtb_ref, ob_ref))):
        reverse = direction == 1
        tri_col = (upper_eq if reverse else lower_eq).astype(BF16)
        tri_row = (lower_eq if reverse else upper_eq).astype(BF16)
        for sub in range(n_sub):
            rows = slice(sub * c, (sub + 1) * c)
            gb = gb_ref[rows, :]
            lane = lax.broadcasted_iota(jnp.int32, gb.shape, 1)
            cum_col = _exact_prefix_sums(jnp.where(lane < LANES // 4, gb, 0.0), tri_col, rows=False)
            cum_row = _exact_prefix_sums(gt_ref[:, rows], tri_row, rows=True)
            for head in range(n_heads):
                g_idx = direction * n_heads + head
                b_idx = 2 * n_heads + g_idx
                cols = slice(head * d, (head + 1) * d)
                items.append(dict(
                    direction=direction, head=head, sub=sub, o_ref=o_ref, rows=rows, cols=cols, reverse=reverse,
                    earlier_eq=upper_eq if reverse else lower_eq,
                    earlier=upper if reverse else lower,
                    q=q_ref[rows, cols], k=k_ref[rows, cols], v=v_ref[rows, cols],
                    g_col=cum_col[:, g_idx:g_idx + 1], beta_col=gb[:, b_idx:b_idx + 1],
                    g_row=cum_row[g_idx:g_idx + 1, :],
                    g_last=cum_col[0:1, g_idx:g_idx + 1] if reverse else cum_col[c - 1:c, g_idx:g_idx + 1]))

    grams = [_mm_nt(jnp.concatenate([it["k"], it["q"]], axis=0), it["k"]) for it in items]
    neg_ls, rhss = [], []
    for it, gram in zip(items, grams):
        g_wide = jnp.broadcast_to(it["g_col"], (c, d))
        beta_wide = jnp.broadcast_to(it["beta_col"], (c, d))
        it["g_last_wide"] = jnp.broadcast_to(it["g_last"], (c, d))
        decay = jnp.exp(jnp.where(it["earlier_eq"], g_wide - it["g_row"], NEG_INF))
        neg_ls.append((jnp.where(it["earlier"], gram[:c] * decay, 0.0) * (-beta_wide)).astype(BF16))
        it["intra"] = jnp.where(it["earlier_eq"], gram[c:] * decay, 0.0).astype(BF16)
        kf = it["k"].astype(F32)
        e_g = jnp.exp(g_wide)
        rhss.append(jnp.concatenate([(it["v"].astype(F32) * beta_wide).astype(BF16),
                                     (kf * (beta_wide * e_g)).astype(BF16)], axis=-1))
        it["qg"] = (it["q"].astype(F32) * e_g).astype(BF16)
        it["k_dec"] = (kf * jnp.exp(it["g_last_wide"] - g_wide)).astype(BF16)
    invs = _unit_triangular_inverses(neg_ls, [it["reverse"] for it in items], same_block)
    for it, inv, rhs in zip(items, invs, rhss):
        sol = _mm(inv, rhs)
        it["u"], it["w"] = sol[:, :d], sol[:, d:].astype(BF16)

    for step in range(n_sub):
        live = [it for it in items if it["sub"] == (n_sub - 1 - step if it["reverse"] else step)]
        states = [state_ref[it["direction"], it["head"]] for it in live]
        wss = [_mm(jnp.concatenate([it["w"], it["qg"]], axis=0), state.astype(BF16))
               for it, state in zip(live, states)]
        v_news = [(it["u"] - ws[:c]).astype(BF16) for it, ws in zip(live, wss)]
        outs = [ws[c:] + _mm(it["intra"], v_new) for it, ws, v_new in zip(live, wss, v_news)]
        updates = [_mm_tn(it["k_dec"], v_new) for it, v_new in zip(live, v_news)]
        for it, out, state, update in zip(live, outs, states, updates):
            it["o_ref"][it["rows"], it["cols"]] = out.astype(BF16)
            state_ref[it["direction"], it["head"]] = state * jnp.exp(it["g_last_wide"]) + update


def _dn_core(q, k, v, gb, gbt, batch, seq, chunks_per_step=4):
    t, tok = q.shape
    n_heads = tok // DN_HEAD_DIM
    rows = chunks_per_step * DN_CHUNK
    steps = seq // rows
    n_rows = gbt.shape[0]

    fwd = lambda b, i: (b * steps + i, 0)
    bwd = lambda b, i: (b * steps + steps - 1 - i, 0)
    fwd_t = lambda b, i: (0, b * steps + i)
    bwd_t = lambda b, i: (0, b * steps + steps - 1 - i)

    def side(tok_map, t_map):
        return [pl.BlockSpec((rows, tok), tok_map)] * 3 + [
            pl.BlockSpec((rows, LANES), tok_map), pl.BlockSpec((n_rows, rows), t_map)]

    return pl.pallas_call(
        functools.partial(_dn_core_kernel, n_heads=n_heads),
        grid=(batch, steps),
        in_specs=side(fwd, fwd_t) + side(bwd, bwd_t),
        out_specs=[pl.BlockSpec((rows, tok), fwd), pl.BlockSpec((rows, tok), bwd)],
        out_shape=[jax.ShapeDtypeStruct((t, tok), BF16)] * 2,
        scratch_shapes=[pltpu.VMEM((2, n_heads, DN_HEAD_DIM, DN_HEAD_DIM), F32)],
        compiler_params=_params("parallel", "arbitrary"),
        name="dn_core",
    )(q, k, v, gb, gbt, q, k, v, gb, gbt)


def _dn_out_kernel(of_ref, ob_ref, kmem_ref, vmem_ref, x_ref, gpre_ref, wz_ref, wmem_ref, onorm_ref, w_ref,
                   g_ref, out_ref):
    tok = of_ref.shape[1]
    x = x_ref[...]
    h = _rms(x, gpre_ref[...]).astype(BF16)
    z = jnp.dot(h, wz_ref[...], preferred_element_type=F32)
    q_mem = jnp.dot(h, wmem_ref[...], preferred_element_type=F32).astype(BF16)
    parts = []
    for c0 in range(0, tok, DN_HEAD_DIM):
        cols = slice(c0, c0 + DN_HEAD_DIM)
        o = of_ref[:, cols].astype(F32) + ob_ref[:, cols].astype(F32)
        o = o * lax.rsqrt(jnp.mean(o * o, axis=-1, keepdims=True) + EPS) * onorm_ref[...]
        parts.append((o * _silu(z[:, cols])).astype(BF16))
    mem_out = _memory_attention(q_mem, kmem_ref[0], vmem_ref[0])
    mixed = jnp.concatenate(parts + [mem_out], axis=-1)
    y = jnp.dot(mixed, w_ref[...], preferred_element_type=F32)
    out_ref[...] = x + _rms(y, g_ref[...])


def _dn_out(o_f, o_b, k_mem, v_mem, x2, g_pre, w_in, w_mem, out_norm, w_out, gain, seq, tm=1024):
    t, d = x2.shape
    tok = o_f.shape[1]
    row = lambda i: (i, 0)
    per_seq = seq // tm
    kv_spec = pl.BlockSpec((1,) + k_mem.shape[1:], lambda i: (i // per_seq, 0, 0))
    return pl.pallas_call(
        _dn_out_kernel,
        grid=(t // tm,),
        in_specs=[pl.BlockSpec((tm, tok), row)] * 2 + [kv_spec, kv_spec]
        + [pl.BlockSpec((tm, d), row), _const_spec((1, d)), pl.BlockSpec((d, tok), lambda i: (0, 3)),
           _const_spec(w_mem.shape), _const_spec((1, DN_HEAD_DIM)), _const_spec(w_out.shape),
           _const_spec((1, d))],
        out_specs=pl.BlockSpec((tm, d), row),
        out_shape=jax.ShapeDtypeStruct((t, d), F32),
        compiler_params=_params("parallel"),
        name="dn_out",
    )(o_f, o_b, k_mem, v_mem, x2, g_pre.reshape(1, d), w_in, w_mem, out_norm.reshape(1, DN_HEAD_DIM), w_out,
      gain.reshape(1, d))


def _attention_layer(x2, mem, batch, seq, rel_bias, w_in, w_out, mem_gain, w_kv, g_pre, g_post):
    n_groups = len(DILATED_GROUPS)
    tok = n_groups * GROUP_WIDTH
    dils = tuple(dil for _, dil in DILATED_GROUPS)
    w_qkv = w_in[:, :3 * tok].reshape(-1, 3, n_groups, GROUP_WIDTH).transpose(0, 2, 1, 3).astype(BF16)
    group_weights = [w_qkv[:, g].reshape(-1, 3 * GROUP_WIDTH) for g in range(n_groups)]
    *qkvs, q_mem = _att_proj(x2, g_pre, group_weights, w_in[:, 3 * tok:].astype(BF16), dils)
    os_, lses = [], []
    for group, dil in enumerate(dils):
        n_res = min(dil, 2)
        o, lse = _band_attn(qkvs[group], _band_bias(rel_bias, group, dil), batch, seq, dil, n_res,
                            q_tile=512 // n_res)
        os_.append(o)
        lses.append(lse)
    k_mem, v_mem = _mem_kv(mem, mem_gain, w_kv.astype(BF16))
    return _att_out(os_, lses, dils, q_mem, k_mem, v_mem, x2, w_out.astype(BF16), g_post, seq)


def _deltanet_layer(x2, mem, batch, seq, w_in, conv_w, a_log, dt_bias, out_norm, w_out, mem_gain, w_kv,
                    g_pre, g_post):
    d = x2.shape[1]
    n_heads = a_log.shape[1]
    tok = n_heads * DN_HEAD_DIM
    n_gate = 4 * n_heads
    w_gate = w_in[:, 4 * tok:4 * tok + n_gate].reshape(d, 2, 2, n_heads).transpose(0, 2, 1, 3).reshape(d, n_gate)
    w_gate = jnp.pad(w_gate, ((0, 0), (0, LANES - n_gate)))
    w_in_b = w_in.astype(BF16)
    pad_row = lambda p: jnp.pad(p.reshape(1, -1).astype(F32), ((0, 0), (0, LANES - 2 * n_heads)))
    q, k, v, gb, gbt = _dn_proj(x2, g_pre, w_in_b, w_gate.astype(BF16), conv_w.astype(F32), pad_row(a_log),
                                pad_row(dt_bias), batch, seq, tok, 2 * n_heads)
    o_f, o_b = _dn_core(q, k, v, gb, gbt, batch, seq)
    k_mem, v_mem = _mem_kv(mem, mem_gain, w_kv.astype(BF16))
    return _dn_out(o_f, o_b, k_mem, v_mem, x2, g_pre, w_in_b, w_in_b[:, 4 * tok + n_gate:], out_norm,
                   w_out.astype(BF16), g_post, seq)


def kernel(x, mem, rel_bias, att_w_in, att_w_out, dn_w_in, dn_conv, dn_a_log, dn_dt_bias, dn_out_norm,
           dn_w_out, mem_norm, mem_w_kv, norm_mix_pre, norm_mix_post, norm_ffn_pre, norm_ffn_post,
           ffn_w_gate_up, ffn_w_down):
    batch, seq, d = x.shape
    depth = norm_mix_pre.shape[0]
    n_mixers = 2
    x2 = x.reshape(batch * seq, d)
    for i in range(depth):
        j = i // n_mixers
        if i % n_mixers == 0:
            x2 = _attention_layer(x2, mem, batch, seq, rel_bias, att_w_in[j], att_w_out[j], mem_norm[i],
                                  mem_w_kv[i], norm_mix_pre[i], norm_mix_post[i])
        else:
            x2 = _deltanet_layer(x2, mem, batch, seq, dn_w_in[j], dn_conv[j], dn_a_log[j], dn_dt_bias[j],
                                 dn_out_norm[j], dn_w_out[j], mem_norm[i], mem_w_kv[i], norm_mix_pre[i],
                                 norm_mix_post[i])
        x2 = _ffn(x2, norm_ffn_pre[i], ffn_w_gate_up[i].astype(BF16), ffn_w_down[i].astype(BF16),
                  norm_ffn_post[i])
    return x2.reshape(batch, seq, d)
```

```python
import functools
import math

import jax
import jax.numpy as jnp
import numpy as np
from jax import lax
from jax.experimental import pallas as pl
from jax.experimental.pallas import tpu as pltpu

F32 = jnp.float32
BF16 = jnp.bfloat16

EPS = 1e-6
NEG_INF = -1e30

LANES = 128
F32_SUBLANES = 8
BF16_SUBLANES = 16
VMEM_LIMIT_BYTES = 56 * 1024 * 1024

DILATED_GROUPS = ((128, 1), (512, 4), (2048, 16))
ATT_HEAD_DIM = 64
HEADS_PER_GROUP = 4
GROUP_WIDTH = HEADS_PER_GROUP * ATT_HEAD_DIM
BAND_HALF = 64
REL_BUCKETS = 32
REL_MAX_DIST = 1024
MEM_HEADS = 4
MEM_HEAD_DIM = 64
DN_HEAD_DIM = 128
DN_CONV = 5
DN_CHUNK = 128
SOLVE_LEAF = 16
CONV_ROWS = 64

ATT_Q_BLOCK = 128
ATT_K_WINDOW = ATT_Q_BLOCK + 2 * BAND_HALF


def _params(*semantics):
    return pltpu.CompilerParams(dimension_semantics=semantics, vmem_limit_bytes=VMEM_LIMIT_BYTES)


def _rms(x, gain):
    return x * lax.rsqrt(jnp.mean(x * x, axis=-1, keepdims=True) + EPS) * gain


def _silu(x):
    return x / (1.0 + jnp.exp2(x * (-1.0 / math.log(2.0))))


def _const_spec(shape):
    return pl.BlockSpec(shape, lambda *_: (0,) * len(shape))


def _mem_kv_kernel(mem_ref, g_ref, w_ref, k_ref, v_ref):
    h = _rms(mem_ref[0], g_ref[...]).astype(BF16)
    kv = jnp.dot(h, w_ref[...], preferred_element_type=F32)
    width = k_ref.shape[-1]
    k_ref[0] = kv[:, :width].astype(BF16)
    v_ref[0] = kv[:, width:].astype(BF16)


def _mem_kv(mem, gain, w_kv):
    b, m, d = mem.shape
    width = w_kv.shape[1] // 2
    return pl.pallas_call(
        _mem_kv_kernel,
        grid=(b,),
        in_specs=[pl.BlockSpec((1, m, d), lambda i: (i, 0, 0)), _const_spec((1, d)), _const_spec(w_kv.shape)],
        out_specs=[pl.BlockSpec((1, m, width), lambda i: (i, 0, 0))] * 2,
        out_shape=[jax.ShapeDtypeStruct((b, m, width), BF16)] * 2,
        compiler_params=_params("parallel"),
        name="mem_kv",
    )(mem, gain.reshape(1, d), w_kv)


def _memory_attention(q, k, v):
    lane = lax.broadcasted_iota(jnp.int32, q.shape, 1)
    in_head = [(lane >= h * MEM_HEAD_DIM) & (lane < (h + 1) * MEM_HEAD_DIM) for h in range(MEM_HEADS)]
    q = q * jnp.asarray(MEM_HEAD_DIM ** -0.5, BF16)
    logits = [lax.dot_general(jnp.where(mask, q, jnp.zeros_like(q)), k, (((1,), (1,)), ((), ())),
                              preferred_element_type=F32) for mask in in_head]
    probs = []
    for s in logits:
        p = jnp.exp(s - jnp.max(s, axis=-1, keepdims=True))
        probs.append((p / jnp.sum(p, axis=-1, keepdims=True)).astype(BF16))
    pvs = [jnp.dot(p, v, preferred_element_type=F32) for p in probs]
    out = pvs[-1]
    for mask, pv in zip(in_head[:-1], pvs[:-1]):
        out = jnp.where(mask, pv, out)
    return out.astype(BF16)


def _memory_specs(q_mem, k_mem, seq, tm):
    per_seq = seq // tm
    kv_spec = pl.BlockSpec((1,) + k_mem.shape[1:], lambda i: (i // per_seq, 0, 0))
    return [pl.BlockSpec((tm, q_mem.shape[1]), lambda i: (i, 0)), kv_spec, kv_spec]


def _t5_bucket(rel):
    half = REL_BUCKETS // 2
    max_exact = half // 2
    n = np.abs(rel)
    large = max_exact + (np.log(np.maximum(n, 1) / max_exact) / math.log(REL_MAX_DIST / max_exact)
                         * (half - max_exact)).astype(np.int64)
    large = np.minimum(large, half - 1)
    return ((rel > 0) * half + np.where(n < max_exact, n, large)).astype(np.int32)


def _band_bias(rel_bias, group, dil):
    period = ATT_Q_BLOCK + ATT_K_WINDOW
    heads = slice(group * HEADS_PER_GROUP, (group + 1) * HEADS_PER_GROUP)
    tables = []
    for variant in range(3):
        rel = np.arange(period) - (ATT_Q_BLOCK - 1) - variant * BAND_HALF
        in_band = np.abs(rel) <= BAND_HALF
        bucket = _t5_bucket(np.clip(rel, -BAND_HALF, BAND_HALF) * dil)
        f = jnp.where(in_band[None, :], jnp.transpose(rel_bias[bucket][:, heads]).astype(F32), NEG_INF)
        flat = jnp.tile(f, (1, ATT_Q_BLOCK))[:, :ATT_Q_BLOCK * (period - 1)]
        skew = flat.reshape(HEADS_PER_GROUP, ATT_Q_BLOCK, period - 1)
        tables.append(skew[:, :, ATT_Q_BLOCK - 1:ATT_Q_BLOCK - 1 + ATT_K_WINDOW])
    return jnp.stack(tables)


def _band_attn_kernel(x_ref, bias_ref, o_ref, lse_ref, *, sub_len, n_res, q_tile):
    lane = lax.broadcasted_iota(jnp.int32, (ATT_Q_BLOCK, LANES), 1)
    low_half = lane < ATT_HEAD_DIM
    n_pairs = GROUP_WIDTH // LANES

    def tile(t, carry):
        blocks = []
        for res in range(n_res):
            for blk in range(q_tile // ATT_Q_BLOCK):
                i0 = pl.multiple_of(t * q_tile + blk * ATT_Q_BLOCK, ATT_Q_BLOCK)
                ws = jnp.clip(i0 - BAND_HALF, 0, sub_len - ATT_K_WINDOW)
                variant = (i0 - ws) // BAND_HALF
                ws = pl.multiple_of(ws, BAND_HALF)
                base = res * 3 * GROUP_WIDTH
                blocks.append(dict(
                    res=res, rows=pl.ds(i0, ATT_Q_BLOCK), variant=variant,
                    q=x_ref[0, pl.ds(i0, ATT_Q_BLOCK), base:base + GROUP_WIDTH]
                    * jnp.asarray(ATT_HEAD_DIM ** -0.5, BF16),
                    k=x_ref[0, pl.ds(ws, ATT_K_WINDOW), base + GROUP_WIDTH:base + 2 * GROUP_WIDTH],
                    v=x_ref[0, pl.ds(ws, ATT_K_WINDOW), base + 2 * GROUP_WIDTH:base + 3 * GROUP_WIDTH]))
        heads = [(b, pair, a) for b in blocks for pair in range(n_pairs) for a in range(2)]
        logits = []
        for b, pair, a in heads:
            cols = slice(pair * LANES, (pair + 1) * LANES)
            q_pair = b["q"][:, cols]
            keep = low_half if a == 0 else jnp.logical_not(low_half)
            qh = jnp.where(keep, q_pair, jnp.zeros_like(q_pair))
            logits.append(_mm_nt(qh, b["k"][:, cols]))
        probs, dens, lses = [], [], []
        for (b, pair, a), s in zip(heads, logits):
            s = s + bias_ref[b["variant"], 2 * pair + a]
            m = jnp.max(s, axis=-1, keepdims=True)
            p = jnp.exp(s - m)
            den = jnp.sum(p, axis=-1, keepdims=True)
            probs.append(p.astype(BF16))
            dens.append(den)
            lses.append(m + jnp.log(den))
        pvs = [_mm(p, b["v"][:, pair * LANES:(pair + 1) * LANES]) for (b, pair, a), p in zip(heads, probs)]
        for idx in range(0, len(heads), 2):
            b, pair, _ = heads[idx]
            cols = pl.ds(b["res"] * GROUP_WIDTH + pair * LANES, LANES)
            out = jnp.where(low_half, pvs[idx] / dens[idx], pvs[idx + 1] / dens[idx + 1])
            lse = jnp.where(low_half, jnp.broadcast_to(lses[idx], (ATT_Q_BLOCK, LANES)),
                            jnp.broadcast_to(lses[idx + 1], (ATT_Q_BLOCK, LANES)))
            o_ref[0, b["rows"], cols] = out.astype(o_ref.dtype)
            lse_ref[0, b["rows"], cols] = lse
        return carry

    lax.fori_loop(0, sub_len // q_tile, tile, 0)


def _band_attn(qkv, bias, batch, seq, dil, n_res, q_tile=256):
    rows, width = qkv.shape
    sub_len = seq // dil
    qkv3 = qkv.reshape(batch, sub_len, width)
    x_spec = pl.BlockSpec((1, sub_len, n_res * 3 * GROUP_WIDTH), lambda b, r: (b, 0, r))
    o_spec = pl.BlockSpec((1, sub_len, n_res * GROUP_WIDTH), lambda b, r: (b, 0, r))
    o, lse = pl.pallas_call(
        functools.partial(_band_attn_kernel, sub_len=sub_len, n_res=n_res, q_tile=q_tile),
        grid=(batch, dil // n_res),
        in_specs=[x_spec, _const_spec(bias.shape)],
        out_specs=[o_spec, o_spec],
        out_shape=[jax.ShapeDtypeStruct((batch, sub_len, dil * GROUP_WIDTH), BF16),
                   jax.ShapeDtypeStruct((batch, sub_len, dil * GROUP_WIDTH), F32)],
        compiler_params=_params("parallel", "parallel"),
        name=f"band_attn_d{dil}",
    )(qkv3, bias)
    return o.reshape(rows, dil * GROUP_WIDTH), lse.reshape(rows, dil * GROUP_WIDTH)


def _att_proj_kernel(x_ref, g_ref, *refs, dils):
    n = len(dils)
    w_refs, wmem_ref = refs[:n], refs[n]
    out_refs, qmem_ref, stage_ref = refs[n + 1:2 * n + 1], refs[2 * n + 1], refs[2 * n + 2]
    h = _rms(x_ref[...], g_ref[...]).astype(BF16)
    tm = h.shape[0]
    for w_ref, o_ref, dil in zip(w_refs, out_refs, dils):
        y = jnp.dot(h, w_ref[...], preferred_element_type=F32)
        width = y.shape[1]
        if dil == 1:
            o_ref[...] = y.astype(BF16)
        else:
            for c in range(width // LANES):
                stage_ref[c] = y[:, c * LANES:(c + 1) * LANES]
            for r in range(dil):
                for c in range(width // LANES):
                    lanes = pl.ds(r * width + c * LANES, LANES)
                    o_ref[:, lanes] = stage_ref[c, pl.ds(r, tm // dil, stride=dil), :].astype(BF16)
    qmem_ref[...] = jnp.dot(h, wmem_ref[...], preferred_element_type=F32).astype(BF16)


def _att_proj(x2, gain, group_weights, w_mem, dils, tm=512):
    t, d = x2.shape
    width = group_weights[0].shape[1]
    row = lambda i: (i, 0)
    return pl.pallas_call(
        functools.partial(_att_proj_kernel, dils=dils),
        grid=(t // tm,),
        in_specs=[pl.BlockSpec((tm, d), row), _const_spec((1, d))]
        + [_const_spec(w.shape) for w in group_weights] + [_const_spec(w_mem.shape)],
        out_specs=[pl.BlockSpec((tm // dil, dil * width), row) for dil in dils]
        + [pl.BlockSpec((tm, w_mem.shape[1]), row)],
        out_shape=[jax.ShapeDtypeStruct((t // dil, dil * width), BF16) for dil in dils]
        + [jax.ShapeDtypeStruct((t, w_mem.shape[1]), BF16)],
        scratch_shapes=[pltpu.VMEM((width // LANES, tm, LANES), F32)],
        compiler_params=_params("parallel"),
        name="att_proj",
    )(x2, gain.reshape(1, d), *group_weights, w_mem)


def _att_out_kernel(*refs, dils):
    n = len(dils)
    o_refs, l_refs = refs[:n], refs[n:2 * n]
    qmem_ref, kmem_ref, vmem_ref, x_ref, w_ref, g_ref, out_ref, stage_ref = refs[2 * n:]
    tm = x_ref.shape[0]

    def token_major(ref, dil, slot):
        if dil == 1:
            return ref[...].astype(F32)
        tiles = GROUP_WIDTH // LANES
        for r in range(dil):
            for c in range(tiles):
                lanes = pl.ds(r * GROUP_WIDTH + c * LANES, LANES)
                stage_ref[slot * tiles + c, pl.ds(r, tm // dil, stride=dil), :] = ref[:, lanes].astype(F32)
        return jnp.concatenate([stage_ref[slot * tiles + c] for c in range(tiles)], axis=-1)

    lses = [token_major(ref, dil, 2 * i) for i, (ref, dil) in enumerate(zip(l_refs, dils))]
    outs = [token_major(ref, dil, 2 * i + 1) for i, (ref, dil) in enumerate(zip(o_refs, dils))]
    mx = functools.reduce(jnp.maximum, lses)
    es = [jnp.exp(l - mx) for l in lses]
    tot = functools.reduce(lambda a, b: a + b, es)
    parts = [(o * (e / tot)).astype(BF16) for o, e in zip(outs, es)]
    mem_out = _memory_attention(qmem_ref[...], kmem_ref[0], vmem_ref[0])
    mixed = jnp.concatenate(parts + [mem_out], axis=-1)
    y = jnp.dot(mixed, w_ref[...], preferred_element_type=F32)
    out_ref[...] = x_ref[...] + _rms(y, g_ref[...])


def _att_out(os_, lses, dils, q_mem, k_mem, v_mem, x2, w_out, gain, seq, tm=1024):
    t, d = x2.shape
    row = lambda i: (i, 0)
    group_specs = [pl.BlockSpec((tm // dil, dil * GROUP_WIDTH), row) for dil in dils]
    return pl.pallas_call(
        functools.partial(_att_out_kernel, dils=dils),
        grid=(t // tm,),
        in_specs=group_specs + group_specs + _memory_specs(q_mem, k_mem, seq, tm)
        + [pl.BlockSpec((tm, d), row), _const_spec(w_out.shape), _const_spec((1, d))],
        out_specs=pl.BlockSpec((tm, d), row),
        out_shape=jax.ShapeDtypeStruct((t, d), F32),
        scratch_shapes=[pltpu.VMEM((2 * len(dils) * (GROUP_WIDTH // LANES), tm, LANES), F32)],
        compiler_params=_params("parallel"),
        name="att_out",
    )(*os_, *lses, q_mem, k_mem, v_mem, x2, w_out, gain.reshape(1, d))


def _ffn_kernel(x_ref, gpre_ref, wg_ref, wu_ref, wd_ref, gpost_ref, out_ref, *, row_blocks):
    rows = x_ref.shape[0] // row_blocks
    for blk in range(row_blocks):
        sl = slice(blk * rows, (blk + 1) * rows)
        x = x_ref[sl, :]
        h = _rms(x, gpre_ref[...]).astype(BF16)
        gate = jnp.dot(h, wg_ref[...], preferred_element_type=F32)
        up = jnp.dot(h, wu_ref[...], preferred_element_type=F32)
        act = (_silu(gate) * up).astype(BF16)
        y = jnp.dot(act, wd_ref[...], preferred_element_type=F32)
        out_ref[sl, :] = x + _rms(y, gpost_ref[...])


def _ffn(x2, g_pre, w_gate_up, w_down, g_post, tm=512, row_blocks=2):
    t, d = x2.shape
    d_ff = w_down.shape[0]
    row = lambda i: (i, 0)
    return pl.pallas_call(
        functools.partial(_ffn_kernel, row_blocks=row_blocks),
        grid=(t // tm,),
        in_specs=[pl.BlockSpec((tm, d), row), _const_spec((1, d)),
                  pl.BlockSpec((d, d_ff), lambda i: (0, 0)), pl.BlockSpec((d, d_ff), lambda i: (0, 1)),
                  _const_spec(w_down.shape), _const_spec((1, d))],
        out_specs=pl.BlockSpec((tm, d), row),
        out_shape=jax.ShapeDtypeStruct((t, d), F32),
        compiler_params=_params("parallel"),
        name="ffn",
    )(x2, g_pre.reshape(1, d), w_gate_up, w_gate_up, w_down, g_post.reshape(1, d))


def _dn_proj_kernel(x_ref, xprev_ref, xnext_ref, g_ref, wqkv_ref, wgate_ref, conv_ref,
                    alog_ref, dtb_ref, q_ref, k_ref, v_ref, gb_ref, gbt_ref, ext_ref,
                    *, halo, n_dir_heads):
    i = pl.program_id(1)
    last = pl.num_programs(1) - 1
    tm = x_ref.shape[0]
    width = wqkv_ref.shape[1]
    tok = width // 3
    gain = g_ref[...]
    h = _rms(x_ref[...], gain).astype(BF16)
    h_prev = jnp.where(i > 0, _rms(xprev_ref[...], gain), 0.0).astype(BF16)
    h_next = jnp.where(i < last, _rms(xnext_ref[...], gain), 0.0).astype(BF16)
    h_ext = jnp.concatenate([h_prev, h, h_next], axis=0)
    chunk = 2 * LANES
    for c0 in range(0, width, chunk):
        y_ext = jnp.dot(h_ext, wqkv_ref[:, c0:c0 + chunk], preferred_element_type=F32)
        for half in range(chunk // LANES):
            c1 = c0 + half * LANES
            cols = slice(c1, c1 + LANES)
            ext = ext_ref.at[(c1 // LANES) % ext_ref.shape[0]]
            ext[...] = y_ext[:, half * LANES:(half + 1) * LANES]
            taps = [conv_ref[j:j + 1, cols] for j in range(DN_CONV)]
            which, col = divmod(c1, tok)
            out_ref = (q_ref, k_ref, v_ref)[which]
            for r0 in range(0, tm, CONV_ROWS):
                first = halo + r0 - DN_CONV // 2
                acc = ext[first:first + CONV_ROWS, :] * taps[0]
                for j in range(1, DN_CONV):
                    acc = acc + ext[first + j:first + j + CONV_ROWS, :] * taps[j]
                y = _silu(acc)
                if which < 2:
                    scale = lax.rsqrt(jnp.sum(y * y, axis=-1, keepdims=True) + EPS)
                    y = y * (scale * (DN_HEAD_DIM ** -0.5) if which == 0 else scale)
                out_ref[r0:r0 + CONV_ROWS, col:col + LANES] = y.astype(BF16)
    gate = jnp.dot(h, wgate_ref[...], preferred_element_type=F32)
    shifted = gate + dtb_ref[...]
    softplus = jnp.maximum(shifted, 0.0) + jnp.log(1.0 + jnp.exp(-jnp.abs(shifted)))
    decay = -jnp.exp(alog_ref[...]) * softplus
    beta = 1.0 / (1.0 + jnp.exp(-gate))
    lane = lax.broadcasted_iota(jnp.int32, gate.shape, 1)
    gb = jnp.where(lane < n_dir_heads, decay, beta)
    gb_ref[...] = gb
    gbt_ref[...] = jnp.transpose(gb)[:gbt_ref.shape[0], :]


def _dn_proj(x2, gain, w_in, w_gate, conv_w, a_log_row, dt_bias_row, batch, seq, tok, n_dir_heads,
             tm=512, halo=BF16_SUBLANES):
    t, d = x2.shape
    n_gate_rows = F32_SUBLANES * pl.cdiv(2 * n_dir_heads, F32_SUBLANES)
    per_seq = seq // tm
    blocks_per_tile = tm // halo
    n_halo_blocks = t // halo

    def cur_map(b, i):
        return (b * per_seq + i, 0)

    def prev_map(b, i):
        return (jnp.maximum((b * per_seq + i) * blocks_per_tile - 1, 0), 0)

    def next_map(b, i):
        return (jnp.minimum((b * per_seq + i + 1) * blocks_per_tile, n_halo_blocks - 1), 0)

    return pl.pallas_call(
        functools.partial(_dn_proj_kernel, halo=halo, n_dir_heads=n_dir_heads),
        grid=(batch, per_seq),
        in_specs=[pl.BlockSpec((tm, d), cur_map), pl.BlockSpec((halo, d), prev_map),
                  pl.BlockSpec((halo, d), next_map), _const_spec((1, d)),
                  pl.BlockSpec((d, 3 * tok), lambda b, i: (0, 0)), _const_spec(w_gate.shape),
                  _const_spec(conv_w.shape), _const_spec((1, LANES)), _const_spec((1, LANES))],
        out_specs=[pl.BlockSpec((tm, tok), cur_map)] * 3
        + [pl.BlockSpec((tm, LANES), cur_map), pl.BlockSpec((n_gate_rows, tm), lambda b, i: (0, b * per_seq + i))],
        out_shape=[jax.ShapeDtypeStruct((t, tok), BF16)] * 3
        + [jax.ShapeDtypeStruct((t, LANES), F32), jax.ShapeDtypeStruct((n_gate_rows, t), F32)],
        scratch_shapes=[pltpu.VMEM((4, tm + 2 * halo, LANES), F32)],
        compiler_params=_params("parallel", "arbitrary"),
        name="dn_proj",
    )(x2, x2, x2, gain.reshape(1, d), w_in, w_gate, conv_w, a_log_row, dt_bias_row)


def _mm(a, b):
    return jnp.dot(a, b, preferred_element_type=F32)


def _mm_nt(a, b):
    return lax.dot_general(a, b, (((1,), (1,)), ((), ())), preferred_element_type=F32)


def _mm_tn(a, b):
    return lax.dot_general(a, b, (((0,), (0,)), ((), ())), preferred_element_type=F32)


def _row_blocks(x, size, parity):
    return jnp.concatenate([x[b * size:(b + 1) * size] for b in range(parity, x.shape[0] // size, 2)], axis=0)


def _interleave_row_blocks(even, odd, size):
    pieces = []
    for b in range(even.shape[0] // size):
        pieces += [even[b * size:(b + 1) * size], odd[b * size:(b + 1) * size]]
    return jnp.concatenate(pieces, axis=0)


def _unit_triangular_inverses(neg_ls, uppers, same_block):
    n = neg_ls[0].shape[0]
    ii = lax.broadcasted_iota(jnp.int32, (n, n), 0)
    jj = lax.broadcasted_iota(jnp.int32, (n, n), 1)
    identity = (ii == jj).astype(F32).astype(BF16)
    zero = jnp.zeros((n, n), BF16)
    plus_identity = lambda m: m + identity
    leaf = same_block[SOLVE_LEAF].astype(F32).astype(BF16)
    xbs = [nl * leaf for nl in neg_ls]
    invs = [plus_identity(xb) for xb in xbs]
    power = 1
    while 2 * power < SOLVE_LEAF:
        xbs = [_mm(xb, xb).astype(BF16) for xb in xbs]
        invs = [_mm(plus_identity(xb), inv).astype(BF16) for xb, inv in zip(xbs, invs)]
        power *= 2
    size = SOLVE_LEAF
    while size < n:
        merge = (same_block[2 * size] & jnp.logical_not(same_block[size])).astype(F32).astype(BF16)
        hot = [0 if upper else 1 for upper in uppers]
        offs = [_row_blocks(nl, size, p) * _row_blocks(merge, size, p) for nl, p in zip(neg_ls, hot)]
        rights = [_mm(off, inv).astype(BF16) for off, inv in zip(offs, invs)]
        half_zero = zero[:n // 2]
        rights = [_interleave_row_blocks(*((r, half_zero) if p == 0 else (half_zero, r)), size)
                  for r, p in zip(rights, hot)]
        moved = [_mm(_row_blocks(inv, size, p), plus_identity(right)).astype(BF16)
                 for inv, right, p in zip(invs, rights, hot)]
        invs = [_interleave_row_blocks(*((m, _row_blocks(inv, size, 1)) if p == 0 else
                                         (_row_blocks(inv, size, 0), m)), size)
                for inv, m, p in zip(invs, moved, hot)]
        size *= 2
    return invs


def _exact_prefix_sums(x, tri, rows):
    hi = x.astype(BF16).astype(F32)
    mid = (x - hi).astype(BF16).astype(F32)
    lo = (x - hi - mid).astype(BF16).astype(F32)
    if rows:
        n = x.shape[0]
        res = _mm(jnp.concatenate([hi, mid, lo], axis=0).astype(BF16), tri)
        return res[:n] + res[n:2 * n] + res[2 * n:]
    quarter = LANES // 4
    packed = hi + pltpu.roll(mid, quarter, 1) + pltpu.roll(lo, 2 * quarter, 1)
    res = _mm(tri, packed.astype(BF16))
    return res + pltpu.roll(res, LANES - quarter, 1) + pltpu.roll(res, LANES - 2 * quarter, 1)


def _dn_core_kernel(qf_ref, kf_ref, vf_ref, gbf_ref, gtf_ref, qb_ref, kb_ref, vb_ref, gbb_ref, gtb_ref,
                    of_ref, ob_ref, state_ref, *, n_heads):
    @pl.when(pl.program_id(1) == 0)
    def _():
        state_ref[...] = jnp.zeros_like(state_ref)

    c = DN_CHUNK
    d = DN_HEAD_DIM
    n_sub = qf_ref.shape[0] // c
    ii = lax.broadcasted_iota(jnp.int32, (c, c), 0)
    jj = lax.broadcasted_iota(jnp.int32, (c, c), 1)
    lower_eq, upper_eq = jj <= ii, jj >= ii
    lower, upper = jj < ii, jj > ii
    same_block = {}
    size = SOLVE_LEAF
    while size <= c:
        shift = int(math.log2(size))
        same_block[size] = (ii >> shift) == (jj >> shift)
        size *= 2

    items = []
    for direction, (q_ref, k_ref, v_ref, gb_ref, gt_ref, o_ref) in enumerate(
            ((qf_ref, kf_ref, vf_ref, gbf_ref, gtf_ref, of_ref),
             (qb_ref, kb_ref, vb_ref, gbb_ref, gtb_ref, ob_ref))):
        reverse = direction == 1
        tri_col = (upper_eq if reverse else lower_eq).astype(BF16)
        tri_row = (lower_eq if reverse else upper_eq).astype(BF16)
        for sub in range(n_sub):
            rows = slice(sub * c, (sub + 1) * c)
            gb = gb_ref[rows, :]
            lane = lax.broadcasted_iota(jnp.int32, gb.shape, 1)
            cum_col = _exact_prefix_sums(jnp.where(lane < LANES // 4, gb, 0.0), tri_col, rows=False)
            cum_row = _exact_prefix_sums(gt_ref[:, rows], tri_row, rows=True)
            for head in range(n_heads):
                g_idx = direction * n_heads + head
                b_idx = 2 * n_heads + g_idx
                cols = slice(head * d, (head + 1) * d)
                items.append(dict(
                    direction=direction, head=head, sub=sub, o_ref=o_ref, rows=rows, cols=cols, reverse=reverse,
                    earlier_eq=upper_eq if reverse else lower_eq,
                    earlier=upper if reverse else lower,
                    q=q_ref[rows, cols], k=k_ref[rows, cols], v=v_ref[rows, cols],
                    g_col=cum_col[:, g_idx:g_idx + 1], beta_col=gb[:, b_idx:b_idx + 1],
                    g_row=cum_row[g_idx:g_idx + 1, :],
                    g_last=cum_col[0:1, g_idx:g_idx + 1] if reverse else cum_col[c - 1:c, g_idx:g_idx + 1]))

    grams = [_mm_nt(jnp.concatenate([it["k"], it["q"]], axis=0), it["k"]) for it in items]
    neg_ls, rhss = [], []
    for it, gram in zip(items, grams):
        g_wide = jnp.broadcast_to(it["g_col"], (c, d))
        beta_wide = jnp.broadcast_to(it["beta_col"], (c, d))
        it["g_last_wide"] = jnp.broadcast_to(it["g_last"], (c, d))
        decay = jnp.exp(jnp.where(it["earlier_eq"], g_wide - it["g_row"], NEG_INF))
        neg_ls.append((jnp.where(it["earlier"], gram[:c] * decay, 0.0) * (-beta_wide)).astype(BF16))
        it["intra"] = jnp.where(it["earlier_eq"], gram[c:] * decay, 0.0).astype(BF16)
        kf = it["k"].astype(F32)
        e_g = jnp.exp(g_wide)
        rhss.append(jnp.concatenate([(it["v"].astype(F32) * beta_wide).astype(BF16),
                                     (kf * (beta_wide * e_g)).astype(BF16)], axis=-1))
        it["qg"] = (it["q"].astype(F32) * e_g).astype(BF16)
        it["k_dec"] = (kf * jnp.exp(it["g_last_wide"] - g_wide)).astype(BF16)
    invs = _unit_triangular_inverses(neg_ls, [it["reverse"] for it in items], same_block)
    for it, inv, rhs in zip(items, invs, rhss):
        sol = _mm(inv, rhs)
        it["u"], it["w"] = sol[:, :d], sol[:, d:].astype(BF16)

    for step in range(n_sub):
        live = [it for it in items if it["sub"] == (n_sub - 1 - step if it["reverse"] else step)]
        states = [state_ref[it["direction"], it["head"]] for it in live]
        wss = [_mm(jnp.concatenate([it["w"], it["qg"]], axis=0), state.astype(BF16))
               for it, state in zip(live, states)]
        v_news = [(it["u"] - ws[:c]).astype(BF16) for it, ws in zip(live, wss)]
        outs = [ws[c:] + _mm(it["intra"], v_new) for it, ws, v_new in zip(live, wss, v_news)]
        updates = [_mm_tn(it["k_dec"], v_new) for it, v_new in zip(live, v_news)]
        for it, out, state, update in zip(live, outs, states, updates):
            it["o_ref"][it["rows"], it["cols"]] = out.astype(BF16)
            state_ref[it["direction"], it["head"]] = state * jnp.exp(it["g_last_wide"]) + update


def _dn_core(q, k, v, gb, gbt, batch, seq, chunks_per_step=4):
    t, tok = q.shape
    n_heads = tok // DN_HEAD_DIM
    rows = chunks_per_step * DN_CHUNK
    assert DN_CHUNK == DN_HEAD_DIM == LANES and seq % rows == 0 and 4 * n_heads <= LANES // 4
    steps = seq // rows
    n_rows = gbt.shape[0]

    fwd = lambda b, i: (b * steps + i, 0)
    bwd = lambda b, i: (b * steps + steps - 1 - i, 0)
    fwd_t = lambda b, i: (0, b * steps + i)
    bwd_t = lambda b, i: (0, b * steps + steps - 1 - i)

    def side(tok_map, t_map):
        return [pl.BlockSpec((rows, tok), tok_map)] * 3 + [
            pl.BlockSpec((rows, LANES), tok_map), pl.BlockSpec((n_rows, rows), t_map)]

    return pl.pallas_call(
        functools.partial(_dn_core_kernel, n_heads=n_heads),
        grid=(batch, steps),
        in_specs=side(fwd, fwd_t) + side(bwd, bwd_t),
        out_specs=[pl.BlockSpec((rows, tok), fwd), pl.BlockSpec((rows, tok), bwd)],
        out_shape=[jax.ShapeDtypeStruct((t, tok), BF16)] * 2,
        scratch_shapes=[pltpu.VMEM((2, n_heads, DN_HEAD_DIM, DN_HEAD_DIM), F32)],
        compiler_params=_params("parallel", "arbitrary"),
        name="dn_core",
    )(q, k, v, gb, gbt, q, k, v, gb, gbt)


def _dn_out_kernel(of_ref, ob_ref, kmem_ref, vmem_ref, x_ref, gpre_ref, wz_ref, wmem_ref, onorm_ref, w_ref,
                   g_ref, out_ref):
    tok = of_ref.shape[1]
    x = x_ref[...]
    h = _rms(x, gpre_ref[...]).astype(BF16)
    z = jnp.dot(h, wz_ref[...], preferred_element_type=F32)
    q_mem = jnp.dot(h, wmem_ref[...], preferred_element_type=F32).astype(BF16)
    parts = []
    for c0 in range(0, tok, DN_HEAD_DIM):
        cols = slice(c0, c0 + DN_HEAD_DIM)
        o = of_ref[:, cols].astype(F32) + ob_ref[:, cols].astype(F32)
        o = o * lax.rsqrt(jnp.mean(o * o, axis=-1, keepdims=True) + EPS) * onorm_ref[...]
        parts.append((o * _silu(z[:, cols])).astype(BF16))
    mem_out = _memory_attention(q_mem, kmem_ref[0], vmem_ref[0])
    mixed = jnp.concatenate(parts + [mem_out], axis=-1)
    y = jnp.dot(mixed, w_ref[...], preferred_element_type=F32)
    out_ref[...] = x + _rms(y, g_ref[...])


def _dn_out(o_f, o_b, k_mem, v_mem, x2, g_pre, w_in, w_mem, out_norm, w_out, gain, seq, tm=1024):
    t, d = x2.shape
    tok = o_f.shape[1]
    row = lambda i: (i, 0)
    per_seq = seq // tm
    kv_spec = pl.BlockSpec((1,) + k_mem.shape[1:], lambda i: (i // per_seq, 0, 0))
    return pl.pallas_call(
        _dn_out_kernel,
        grid=(t // tm,),
        in_specs=[pl.BlockSpec((tm, tok), row)] * 2 + [kv_spec, kv_spec]
        + [pl.BlockSpec((tm, d), row), _const_spec((1, d)), pl.BlockSpec((d, tok), lambda i: (0, 3)),
           _const_spec(w_mem.shape), _const_spec((1, DN_HEAD_DIM)), _const_spec(w_out.shape),
           _const_spec((1, d))],
        out_specs=pl.BlockSpec((tm, d), row),
        out_shape=jax.ShapeDtypeStruct((t, d), F32),
        compiler_params=_params("parallel"),
        name="dn_out",
    )(o_f, o_b, k_mem, v_mem, x2, g_pre.reshape(1, d), w_in, w_mem, out_norm.reshape(1, DN_HEAD_DIM), w_out,
      gain.reshape(1, d))


def _attention_layer(x2, mem, batch, seq, rel_bias, w_in, w_out, mem_gain, w_kv, g_pre, g_post):
    n_groups = len(DILATED_GROUPS)
    tok = n_groups * GROUP_WIDTH
    dils = tuple(dil for _, dil in DILATED_GROUPS)
    assert all(window // (2 * dil) == BAND_HALF for window, dil in DILATED_GROUPS)
    assert w_in.shape[1] == 3 * tok + MEM_HEADS * MEM_HEAD_DIM and rel_bias.shape == (REL_BUCKETS, tok // ATT_HEAD_DIM)
    assert all(seq % (dil * ATT_K_WINDOW) == 0 for dil in dils)
    w_qkv = w_in[:, :3 * tok].reshape(-1, 3, n_groups, GROUP_WIDTH).transpose(0, 2, 1, 3).astype(BF16)
    group_weights = [w_qkv[:, g].reshape(-1, 3 * GROUP_WIDTH) for g in range(n_groups)]
    *qkvs, q_mem = _att_proj(x2, g_pre, group_weights, w_in[:, 3 * tok:].astype(BF16), dils)
    os_, lses = [], []
    for group, dil in enumerate(dils):
        n_res = min(dil, 2)
        o, lse = _band_attn(qkvs[group], _band_bias(rel_bias, group, dil), batch, seq, dil, n_res,
                            q_tile=512 // n_res)
        os_.append(o)
        lses.append(lse)
    k_mem, v_mem = _mem_kv(mem, mem_gain, w_kv.astype(BF16))
    return _att_out(os_, lses, dils, q_mem, k_mem, v_mem, x2, w_out.astype(BF16), g_post, seq)


def _deltanet_layer(x2, mem, batch, seq, w_in, conv_w, a_log, dt_bias, out_norm, w_out, mem_gain, w_kv,
                    g_pre, g_post):
    d = x2.shape[1]
    n_heads = a_log.shape[1]
    tok = n_heads * DN_HEAD_DIM
    n_gate = 4 * n_heads
    w_gate = w_in[:, 4 * tok:4 * tok + n_gate].reshape(d, 2, 2, n_heads).transpose(0, 2, 1, 3).reshape(d, n_gate)
    w_gate = jnp.pad(w_gate, ((0, 0), (0, LANES - n_gate)))
    w_in_b = w_in.astype(BF16)
    pad_row = lambda p: jnp.pad(p.reshape(1, -1).astype(F32), ((0, 0), (0, LANES - 2 * n_heads)))
    q, k, v, gb, gbt = _dn_proj(x2, g_pre, w_in_b, w_gate.astype(BF16), conv_w.astype(F32), pad_row(a_log),
                                pad_row(dt_bias), batch, seq, tok, 2 * n_heads)
    o_f, o_b = _dn_core(q, k, v, gb, gbt, batch, seq)
    k_mem, v_mem = _mem_kv(mem, mem_gain, w_kv.astype(BF16))
    return _dn_out(o_f, o_b, k_mem, v_mem, x2, g_pre, w_in_b, w_in_b[:, 4 * tok + n_gate:], out_norm,
                   w_out.astype(BF16), g_post, seq)


def kernel(x, mem, rel_bias, att_w_in, att_w_out, dn_w_in, dn_conv, dn_a_log, dn_dt_bias, dn_out_norm,
           dn_w_out, mem_norm, mem_w_kv, norm_mix_pre, norm_mix_post, norm_ffn_pre, norm_ffn_post,
           ffn_w_gate_up, ffn_w_down):
    batch, seq, d = x.shape
    depth = norm_mix_pre.shape[0]
    n_mixers = 2
    x2 = x.reshape(batch * seq, d)
    for i in range(depth):
        j = i // n_mixers
        if i % n_mixers == 0:
            x2 = _attention_layer(x2, mem, batch, seq, rel_bias, att_w_in[j], att_w_out[j], mem_norm[i],
                                  mem_w_kv[i], norm_mix_pre[i], norm_mix_post[i])
        else:
            x2 = _deltanet_layer(x2, mem, batch, seq, dn_w_in[j], dn_conv[j], dn_a_log[j], dn_dt_bias[j],
                                 dn_out_norm[j], dn_w_out[j], mem_norm[i], mem_w_kv[i], norm_mix_pre[i],
                                 norm_mix_post[i])
        x2 = _ffn(x2, norm_ffn_pre[i], ffn_w_gate_up[i].astype(BF16), ffn_w_down[i].astype(BF16),
                  norm_ffn_post[i])
    return x2.reshape(batch, seq, d)
```

```python
import functools
import math

import jax
import jax.numpy as jnp
import numpy as np
from jax import lax
from jax.experimental import pallas as pl
from jax.experimental.pallas import tpu as pltpu

F32 = jnp.float32
BF16 = jnp.bfloat16

EPS = 1e-6
NEG_INF = -1e30

LANES = 128
F32_SUBLANES = 8
BF16_SUBLANES = 16
VMEM_LIMIT_BYTES = 56 * 1024 * 1024

DILATED_GROUPS = ((128, 1), (512, 4), (2048, 16))
ATT_HEAD_DIM = 64
HEADS_PER_GROUP = 4
GROUP_WIDTH = HEADS_PER_GROUP * ATT_HEAD_DIM
BAND_HALF = 64
REL_BUCKETS = 32
REL_MAX_DIST = 1024
MEM_HEADS = 4
MEM_HEAD_DIM = 64
DN_HEAD_DIM = 128
DN_CONV = 5
DN_CHUNK = 128
SOLVE_LEAF = 16
CONV_ROWS = 64

ATT_Q_BLOCK = 128
ATT_K_WINDOW = ATT_Q_BLOCK + 2 * BAND_HALF


def _params(*semantics):
    return pltpu.CompilerParams(dimension_semantics=semantics, vmem_limit_bytes=VMEM_LIMIT_BYTES)


def _rms(x, gain):
    return x * lax.rsqrt(jnp.mean(x * x, axis=-1, keepdims=True) + EPS) * gain


def _silu(x):
    return x / (1.0 + jnp.exp2(x * (-1.0 / math.log(2.0))))


def _const_spec(shape):
    return pl.BlockSpec(shape, lambda *_: (0,) * len(shape))


def _mem_kv_kernel(mem_ref, g_ref, w_ref, k_ref, v_ref):
    h = _rms(mem_ref[0], g_ref[...]).astype(BF16)
    kv = jnp.dot(h, w_ref[...], preferred_element_type=F32)
    width = k_ref.shape[-1]
    k_ref[0] = kv[:, :width].astype(BF16)
    v_ref[0] = kv[:, width:].astype(BF16)


def _mem_kv(mem, gain, w_kv):
    b, m, d = mem.shape
    width = w_kv.shape[1] // 2
    return pl.pallas_call(
        _mem_kv_kernel,
        grid=(b,),
        in_specs=[pl.BlockSpec((1, m, d), lambda i: (i, 0, 0)), _const_spec((1, d)), _const_spec(w_kv.shape)],
        out_specs=[pl.BlockSpec((1, m, width), lambda i: (i, 0, 0))] * 2,
        out_shape=[jax.ShapeDtypeStruct((b, m, width), BF16)] * 2,
        compiler_params=_params("parallel"),
        name="mem_kv",
    )(mem, gain.reshape(1, d), w_kv)


def _memory_attention(q, k, v):
    lane = lax.broadcasted_iota(jnp.int32, q.shape, 1)
    in_head = [(lane >= h * MEM_HEAD_DIM) & (lane < (h + 1) * MEM_HEAD_DIM) for h in range(MEM_HEADS)]
    q = q * jnp.asarray(MEM_HEAD_DIM ** -0.5, BF16)
    logits = [lax.dot_general(jnp.where(mask, q, jnp.zeros_like(q)), k, (((1,), (1,)), ((), ())),
                              preferred_element_type=F32) for mask in in_head]
    probs = []
    for s in logits:
        p = jnp.exp(s - jnp.max(s, axis=-1, keepdims=True))
        probs.append((p / jnp.sum(p, axis=-1, keepdims=True)).astype(BF16))
    pvs = [jnp.dot(p, v, preferred_element_type=F32) for p in probs]
    out = pvs[-1]
    for mask, pv in zip(in_head[:-1], pvs[:-1]):
        out = jnp.where(mask, pv, out)
    return out.astype(BF16)


def _memory_specs(q_mem, k_mem, seq, tm):
    per_seq = seq // tm
    kv_spec = pl.BlockSpec((1,) + k_mem.shape[1:], lambda i: (i // per_seq, 0, 0))
    return [pl.BlockSpec((tm, q_mem.shape[1]), lambda i: (i, 0)), kv_spec, kv_spec]


def _t5_bucket(rel):
    half = REL_BUCKETS // 2
    max_exact = half // 2
    n = np.abs(rel)
    large = max_exact + (np.log(np.maximum(n, 1) / max_exact) / math.log(REL_MAX_DIST / max_exact)
                         * (half - max_exact)).astype(np.int64)
    large = np.minimum(large, half - 1)
    return ((rel > 0) * half + np.where(n < max_exact, n, large)).astype(np.int32)


def _band_bias(rel_bias, group, dil):
    period = ATT_Q_BLOCK + ATT_K_WINDOW
    heads = slice(group * HEADS_PER_GROUP, (group + 1) * HEADS_PER_GROUP)
    tables = []
    for variant in range(3):
        rel = np.arange(period) - (ATT_Q_BLOCK - 1) - variant * BAND_HALF
        in_band = np.abs(rel) <= BAND_HALF
        bucket = _t5_bucket(np.clip(rel, -BAND_HALF, BAND_HALF) * dil)
        f = jnp.where(in_band[None, :], jnp.transpose(rel_bias[bucket][:, heads]).astype(F32), NEG_INF)
        flat = jnp.tile(f, (1, ATT_Q_BLOCK))[:, :ATT_Q_BLOCK * (period - 1)]
        skew = flat.reshape(HEADS_PER_GROUP, ATT_Q_BLOCK, period - 1)
        tables.append(skew[:, :, ATT_Q_BLOCK - 1:ATT_Q_BLOCK - 1 + ATT_K_WINDOW])
    return jnp.stack(tables)


def _band_attn_kernel(x_ref, bias_ref, o_ref, lse_ref, *, sub_len, n_res, q_tile):
    lane = lax.broadcasted_iota(jnp.int32, (ATT_Q_BLOCK, LANES), 1)
    low_half = lane < ATT_HEAD_DIM
    n_pairs = GROUP_WIDTH // LANES

    def tile(t, carry):
        blocks = []
        for res in range(n_res):
            for blk in range(q_tile // ATT_Q_BLOCK):
                i0 = pl.multiple_of(t * q_tile + blk * ATT_Q_BLOCK, ATT_Q_BLOCK)
                ws = jnp.clip(i0 - BAND_HALF, 0, sub_len - ATT_K_WINDOW)
                variant = (i0 - ws) // BAND_HALF
                ws = pl.multiple_of(ws, BAND_HALF)
                base = res * 3 * GROUP_WIDTH
                blocks.append(dict(
                    res=res, rows=pl.ds(i0, ATT_Q_BLOCK), variant=variant,
                    q=x_ref[0, pl.ds(i0, ATT_Q_BLOCK), base:base + GROUP_WIDTH]
                    * jnp.asarray(ATT_HEAD_DIM ** -0.5, BF16),
                    k=x_ref[0, pl.ds(ws, ATT_K_WINDOW), base + GROUP_WIDTH:base + 2 * GROUP_WIDTH],
                    v=x_ref[0, pl.ds(ws, ATT_K_WINDOW), base + 2 * GROUP_WIDTH:base + 3 * GROUP_WIDTH]))
        heads = [(b, pair, a) for b in blocks for pair in range(n_pairs) for a in range(2)]
        logits = []
        for b, pair, a in heads:
            cols = slice(pair * LANES, (pair + 1) * LANES)
            q_pair = b["q"][:, cols]
            keep = low_half if a == 0 else jnp.logical_not(low_half)
            qh = jnp.where(keep, q_pair, jnp.zeros_like(q_pair))
            logits.append(_mm_nt(qh, b["k"][:, cols]))
        probs, dens, lses = [], [], []
        for (b, pair, a), s in zip(heads, logits):
            s = s + bias_ref[b["variant"], 2 * pair + a]
            m = jnp.max(s, axis=-1, keepdims=True)
            p = jnp.exp(s - m)
            den = jnp.sum(p, axis=-1, keepdims=True)
            probs.append(p.astype(BF16))
            dens.append(den)
            lses.append(m + jnp.log(den))
        pvs = [_mm(p, b["v"][:, pair * LANES:(pair + 1) * LANES]) for (b, pair, a), p in zip(heads, probs)]
        for idx in range(0, len(heads), 2):
            b, pair, _ = heads[idx]
            cols = pl.ds(b["res"] * GROUP_WIDTH + pair * LANES, LANES)
            out = jnp.where(low_half, pvs[idx] / dens[idx], pvs[idx + 1] / dens[idx + 1])
            lse = jnp.where(low_half, jnp.broadcast_to(lses[idx], (ATT_Q_BLOCK, LANES)),
                            jnp.broadcast_to(lses[idx + 1], (ATT_Q_BLOCK, LANES)))
            o_ref[0, b["rows"], cols] = out.astype(o_ref.dtype)
            lse_ref[0, b["rows"], cols] = lse
        return carry

    lax.fori_loop(0, sub_len // q_tile, tile, 0)


def _band_attn(qkv, bias, batch, seq, dil, n_res, q_tile=256):
    rows, width = qkv.shape
    sub_len = seq // dil
    qkv3 = qkv.reshape(batch, sub_len, width)
    x_spec = pl.BlockSpec((1, sub_len, n_res * 3 * GROUP_WIDTH), lambda b, r: (b, 0, r))
    o_spec = pl.BlockSpec((1, sub_len, n_res * GROUP_WIDTH), lambda b, r: (b, 0, r))
    o, lse = pl.pallas_call(
        functools.partial(_band_attn_kernel, sub_len=sub_len, n_res=n_res, q_tile=q_tile),
        grid=(batch, dil // n_res),
        in_specs=[x_spec, _const_spec(bias.shape)],
        out_specs=[o_spec, o_spec],
        out_shape=[jax.ShapeDtypeStruct((batch, sub_len, dil * GROUP_WIDTH), BF16),
                   jax.ShapeDtypeStruct((batch, sub_len, dil * GROUP_WIDTH), F32)],
        compiler_params=_params("parallel", "parallel"),
        name=f"band_attn_d{dil}",
    )(qkv3, bias)
    return o.reshape(rows, dil * GROUP_WIDTH), lse.reshape(rows, dil * GROUP_WIDTH)


def _att_proj_kernel(x_ref, g_ref, *refs, dils):
    n = len(dils)
    w_refs, wmem_ref = refs[:n], refs[n]
    out_refs, qmem_ref, stage_ref = refs[n + 1:2 * n + 1], refs[2 * n + 1], refs[2 * n + 2]
    h = _rms(x_ref[...], g_ref[...]).astype(BF16)
    tm = h.shape[0]
    for w_ref, o_ref, dil in zip(w_refs, out_refs, dils):
        y = jnp.dot(h, w_ref[...], preferred_element_type=F32)
        width = y.shape[1]
        if dil == 1:
            o_ref[...] = y.astype(BF16)
        else:
            for c in range(width // LANES):
                stage_ref[c] = y[:, c * LANES:(c + 1) * LANES]
            for r in range(dil):
                for c in range(width // LANES):
                    lanes = pl.ds(r * width + c * LANES, LANES)
                    o_ref[:, lanes] = stage_ref[c, pl.ds(r, tm // dil, stride=dil), :].astype(BF16)
    qmem_ref[...] = jnp.dot(h, wmem_ref[...], preferred_element_type=F32).astype(BF16)


def _att_proj(x2, gain, group_weights, w_mem, dils, tm=512):
    t, d = x2.shape
    width = group_weights[0].shape[1]
    row = lambda i: (i, 0)
    return pl.pallas_call(
        functools.partial(_att_proj_kernel, dils=dils),
        grid=(t // tm,),
        in_specs=[pl.BlockSpec((tm, d), row), _const_spec((1, d))]
        + [_const_spec(w.shape) for w in group_weights] + [_const_spec(w_mem.shape)],
        out_specs=[pl.BlockSpec((tm // dil, dil * width), row) for dil in dils]
        + [pl.BlockSpec((tm, w_mem.shape[1]), row)],
        out_shape=[jax.ShapeDtypeStruct((t // dil, dil * width), BF16) for dil in dils]
        + [jax.ShapeDtypeStruct((t, w_mem.shape[1]), BF16)],
        scratch_shapes=[pltpu.VMEM((width // LANES, tm, LANES), F32)],
        compiler_params=_params("parallel"),
        name="att_proj",
    )(x2, gain.reshape(1, d), *group_weights, w_mem)


def _att_out_kernel(*refs, dils):
    n = len(dils)
    o_refs, l_refs = refs[:n], refs[n:2 * n]
    qmem_ref, kmem_ref, vmem_ref, x_ref, w_ref, g_ref, out_ref, stage_ref = refs[2 * n:]
    tm = x_ref.shape[0]

    def token_major(ref, dil, slot):
        if dil == 1:
            return ref[...].astype(F32)
        tiles = GROUP_WIDTH // LANES
        for r in range(dil):
            for c in range(tiles):
                lanes = pl.ds(r * GROUP_WIDTH + c * LANES, LANES)
                stage_ref[slot * tiles + c, pl.ds(r, tm // dil, stride=dil), :] = ref[:, lanes].astype(F32)
        return jnp.concatenate([stage_ref[slot * tiles + c] for c in range(tiles)], axis=-1)

    lses = [token_major(ref, dil, 2 * i) for i, (ref, dil) in enumerate(zip(l_refs, dils))]
    outs = [token_major(ref, dil, 2 * i + 1) for i, (ref, dil) in enumerate(zip(o_refs, dils))]
    mx = functools.reduce(jnp.maximum, lses)
    es = [jnp.exp(l - mx) for l in lses]
    tot = functools.reduce(lambda a, b: a + b, es)
    parts = [(o * (e / tot)).astype(BF16) for o, e in zip(outs, es)]
    mem_out = _memory_attention(qmem_ref[...], kmem_ref[0], vmem_ref[0])
    mixed = jnp.concatenate(parts + [mem_out], axis=-1)
    y = jnp.dot(mixed, w_ref[...], preferred_element_type=F32)
    out_ref[...] = x_ref[...] + _rms(y, g_ref[...])


def _att_out(os_, lses, dils, q_mem, k_mem, v_mem, x2, w_out, gain, seq, tm=1024):
    t, d = x2.shape
    row = lambda i: (i, 0)
    group_specs = [pl.BlockSpec((tm // dil, dil * GROUP_WIDTH), row) for dil in dils]
    return pl.pallas_call(
        functools.partial(_att_out_kernel, dils=dils),
        grid=(t // tm,),
        in_specs=group_specs + group_specs + _memory_specs(q_mem, k_mem, seq, tm)
        + [pl.BlockSpec((tm, d), row), _const_spec(w_out.shape), _const_spec((1, d))],
        out_specs=pl.BlockSpec((tm, d), row),
        out_shape=jax.ShapeDtypeStruct((t, d), F32),
        scratch_shapes=[pltpu.VMEM((2 * len(dils) * (GROUP_WIDTH // LANES), tm, LANES), F32)],
        compiler_params=_params("parallel"),
        name="att_out",
    )(*os_, *lses, q_mem, k_mem, v_mem, x2, w_out, gain.reshape(1, d))


def _ffn_rows(x, gpre_ref, wg_ref, wu_ref, wd_ref, gpost_ref):
    h = _rms(x, gpre_ref[...]).astype(BF16)
    gate = jnp.dot(h, wg_ref[...], preferred_element_type=F32)
    up = jnp.dot(h, wu_ref[...], preferred_element_type=F32)
    act = (_silu(gate) * up).astype(BF16)
    y = jnp.dot(act, wd_ref[...], preferred_element_type=F32)
    return x + _rms(y, gpost_ref[...])


def _ffn_kernel(x_ref, gpre_ref, wg_ref, wu_ref, wd_ref, gpost_ref, out_ref, *, row_blocks):
    rows = x_ref.shape[0] // row_blocks
    for blk in range(row_blocks):
        sl = slice(blk * rows, (blk + 1) * rows)
        out_ref[sl, :] = _ffn_rows(x_ref[sl, :], gpre_ref, wg_ref, wu_ref, wd_ref, gpost_ref)


def _ffn(x2, g_pre, w_gate_up, w_down, g_post, tm=512, row_blocks=2):
    t, d = x2.shape
    d_ff = w_down.shape[0]
    row = lambda i: (i, 0)
    return pl.pallas_call(
        functools.partial(_ffn_kernel, row_blocks=row_blocks),
        grid=(t // tm,),
        in_specs=[pl.BlockSpec((tm, d), row), _const_spec((1, d)),
                  pl.BlockSpec((d, d_ff), lambda i: (0, 0)), pl.BlockSpec((d, d_ff), lambda i: (0, 1)),
                  _const_spec(w_down.shape), _const_spec((1, d))],
        out_specs=pl.BlockSpec((tm, d), row),
        out_shape=jax.ShapeDtypeStruct((t, d), F32),
        compiler_params=_params("parallel"),
        name="ffn",
    )(x2, g_pre.reshape(1, d), w_gate_up, w_gate_up, w_down, g_post.reshape(1, d))


def _dn_proj_kernel(x_ref, xprev_ref, xnext_ref, g_ref, wqkv_ref, wgate_ref, conv_ref,
                    alog_ref, dtb_ref, q_ref, k_ref, v_ref, gb_ref, gbt_ref, ext_ref,
                    *, halo, n_dir_heads):
    i = pl.program_id(1)
    last = pl.num_programs(1) - 1
    tm = x_ref.shape[0]
    width = wqkv_ref.shape[1]
    tok = width // 3
    gain = g_ref[...]
    h = _rms(x_ref[...], gain).astype(BF16)
    h_prev = jnp.where(i > 0, _rms(xprev_ref[...], gain), 0.0).astype(BF16)
    h_next = jnp.where(i < last, _rms(xnext_ref[...], gain), 0.0).astype(BF16)
    h_ext = jnp.concatenate([h_prev, h, h_next], axis=0)
    chunk = 2 * LANES
    for c0 in range(0, width, chunk):
        y_ext = jnp.dot(h_ext, wqkv_ref[:, c0:c0 + chunk], preferred_element_type=F32)
        for half in range(chunk // LANES):
            c1 = c0 + half * LANES
            cols = slice(c1, c1 + LANES)
            ext = ext_ref.at[(c1 // LANES) % ext_ref.shape[0]]
            ext[...] = y_ext[:, half * LANES:(half + 1) * LANES]
            taps = [conv_ref[j:j + 1, cols] for j in range(DN_CONV)]
            which, col = divmod(c1, tok)
            out_ref = (q_ref, k_ref, v_ref)[which]
            for r0 in range(0, tm, CONV_ROWS):
                first = halo + r0 - DN_CONV // 2
                acc = ext[first:first + CONV_ROWS, :] * taps[0]
                for j in range(1, DN_CONV):
                    acc = acc + ext[first + j:first + j + CONV_ROWS, :] * taps[j]
                y = _silu(acc)
                if which < 2:
                    scale = lax.rsqrt(jnp.sum(y * y, axis=-1, keepdims=True) + EPS)
                    y = y * (scale * (DN_HEAD_DIM ** -0.5) if which == 0 else scale)
                out_ref[r0:r0 + CONV_ROWS, col:col + LANES] = y.astype(BF16)
    gate = jnp.dot(h, wgate_ref[...], preferred_element_type=F32)
    shifted = gate + dtb_ref[...]
    softplus = jnp.maximum(shifted, 0.0) + jnp.log(1.0 + jnp.exp(-jnp.abs(shifted)))
    decay = -jnp.exp(alog_ref[...]) * softplus
    beta = 1.0 / (1.0 + jnp.exp(-gate))
    lane = lax.broadcasted_iota(jnp.int32, gate.shape, 1)
    gb = jnp.where(lane < n_dir_heads, decay, beta)
    gb_ref[...] = gb
    gbt_ref[...] = jnp.transpose(gb)[:gbt_ref.shape[0], :]


def _dn_proj(x2, gain, w_in, w_gate, conv_w, a_log_row, dt_bias_row, batch, seq, tok, n_dir_heads,
             tm=512, halo=BF16_SUBLANES):
    t, d = x2.shape
    n_gate_rows = F32_SUBLANES * pl.cdiv(2 * n_dir_heads, F32_SUBLANES)
    per_seq = seq // tm
    blocks_per_tile = tm // halo
    n_halo_blocks = t // halo

    def cur_map(b, i):
        return (b * per_seq + i, 0)

    def prev_map(b, i):
        return (jnp.maximum((b * per_seq + i) * blocks_per_tile - 1, 0), 0)

    def next_map(b, i):
        return (jnp.minimum((b * per_seq + i + 1) * blocks_per_tile, n_halo_blocks - 1), 0)

    return pl.pallas_call(
        functools.partial(_dn_proj_kernel, halo=halo, n_dir_heads=n_dir_heads),
        grid=(batch, per_seq),
        in_specs=[pl.BlockSpec((tm, d), cur_map), pl.BlockSpec((halo, d), prev_map),
                  pl.BlockSpec((halo, d), next_map), _const_spec((1, d)),
                  pl.BlockSpec((d, 3 * tok), lambda b, i: (0, 0)), _const_spec(w_gate.shape),
                  _const_spec(conv_w.shape), _const_spec((1, LANES)), _const_spec((1, LANES))],
        out_specs=[pl.BlockSpec((tm, tok), cur_map)] * 3
        + [pl.BlockSpec((tm, LANES), cur_map), pl.BlockSpec((n_gate_rows, tm), lambda b, i: (0, b * per_seq + i))],
        out_shape=[jax.ShapeDtypeStruct((t, tok), BF16)] * 3
        + [jax.ShapeDtypeStruct((t, LANES), F32), jax.ShapeDtypeStruct((n_gate_rows, t), F32)],
        scratch_shapes=[pltpu.VMEM((4, tm + 2 * halo, LANES), F32)],
        compiler_params=_params("parallel", "arbitrary"),
        name="dn_proj",
    )(x2, x2, x2, gain.reshape(1, d), w_in, w_gate, conv_w, a_log_row, dt_bias_row)


def _mm(a, b):
    return jnp.dot(a, b, preferred_element_type=F32)


def _mm_nt(a, b):
    return lax.dot_general(a, b, (((1,), (1,)), ((), ())), preferred_element_type=F32)


def _mm_tn(a, b):
    return lax.dot_general(a, b, (((0,), (0,)), ((), ())), preferred_element_type=F32)


def _row_blocks(x, size, parity):
    return jnp.concatenate([x[b * size:(b + 1) * size] for b in range(parity, x.shape[0] // size, 2)], axis=0)


def _interleave_row_blocks(even, odd, size):
    pieces = []
    for b in range(even.shape[0] // size):
        pieces += [even[b * size:(b + 1) * size], odd[b * size:(b + 1) * size]]
    return jnp.concatenate(pieces, axis=0)


def _unit_triangular_inverses(neg_ls, uppers, same_block):
    n = neg_ls[0].shape[0]
    ii = lax.broadcasted_iota(jnp.int32, (n, n), 0)
    jj = lax.broadcasted_iota(jnp.int32, (n, n), 1)
    identity = (ii == jj).astype(F32).astype(BF16)
    zero = jnp.zeros((n, n), BF16)
    plus_identity = lambda m: m + identity
    leaf = same_block[SOLVE_LEAF].astype(F32).astype(BF16)
    xbs = [nl * leaf for nl in neg_ls]
    invs = [plus_identity(xb) for xb in xbs]
    power = 1
    while 2 * power < SOLVE_LEAF:
        xbs = [_mm(xb, xb).astype(BF16) for xb in xbs]
        invs = [_mm(plus_identity(xb), inv).astype(BF16) for xb, inv in zip(xbs, invs)]
        power *= 2
    size = SOLVE_LEAF
    while size < n:
        merge = (same_block[2 * size] & jnp.logical_not(same_block[size])).astype(F32).astype(BF16)
        hot = [0 if upper else 1 for upper in uppers]
        offs = [_row_blocks(nl, size, p) * _row_blocks(merge, size, p) for nl, p in zip(neg_ls, hot)]
        rights = [_mm(off, inv).astype(BF16) for off, inv in zip(offs, invs)]
        half_zero = zero[:n // 2]
        rights = [_interleave_row_blocks(*((r, half_zero) if p == 0 else (half_zero, r)), size)
                  for r, p in zip(rights, hot)]
        moved = [_mm(_row_blocks(inv, size, p), plus_identity(right)).astype(BF16)
                 for inv, right, p in zip(invs, rights, hot)]
        invs = [_interleave_row_blocks(*((m, _row_blocks(inv, size, 1)) if p == 0 else
                                         (_row_blocks(inv, size, 0), m)), size)
                for inv, m, p in zip(invs, moved, hot)]
        size *= 2
    return invs


def _exact_prefix_sums(x, tri, rows):
    hi = x.astype(BF16).astype(F32)
    mid = (x - hi).astype(BF16).astype(F32)
    lo = (x - hi - mid).astype(BF16).astype(F32)
    if rows:
        n = x.shape[0]
        res = _mm(jnp.concatenate([hi, mid, lo], axis=0).astype(BF16), tri)
        return res[:n] + res[n:2 * n] + res[2 * n:]
    quarter = LANES // 4
    packed = hi + pltpu.roll(mid, quarter, 1) + pltpu.roll(lo, 2 * quarter, 1)
    res = _mm(tri, packed.astype(BF16))
    return res + pltpu.roll(res, LANES - quarter, 1) + pltpu.roll(res, LANES - 2 * quarter, 1)


def _dn_core_kernel(qf_ref, kf_ref, vf_ref, gbf_ref, gtf_ref, qb_ref, kb_ref, vb_ref, gbb_ref, gtb_ref,
                    of_ref, ob_ref, state_ref, *, n_heads):
    @pl.when(pl.program_id(1) == 0)
    def _():
        state_ref[...] = jnp.zeros_like(state_ref)

    c = DN_CHUNK
    d = DN_HEAD_DIM
    n_sub = qf_ref.shape[0] // c
    ii = lax.broadcasted_iota(jnp.int32, (c, c), 0)
    jj = lax.broadcasted_iota(jnp.int32, (c, c), 1)
    lower_eq, upper_eq = jj <= ii, jj >= ii
    lower, upper = jj < ii, jj > ii
    same_block = {}
    size = SOLVE_LEAF
    while size <= c:
        shift = int(math.log2(size))
        same_block[size] = (ii >> shift) == (jj >> shift)
        size *= 2

    items = []
    for direction, (q_ref, k_ref, v_ref, gb_ref, gt_ref, o_ref) in enumerate(
            ((qf_ref, kf_ref, vf_ref, gbf_ref, gtf_ref, of_ref),
             (qb_ref, kb_ref, vb_ref, gbb_ref, gtb_ref, ob_ref))):
        reverse = direction == 1
        tri_col = (upper_eq if reverse else lower_eq).astype(BF16)
        tri_row = (lower_eq if reverse else upper_eq).astype(BF16)
        for sub in range(n_sub):
            rows = slice(sub * c, (sub + 1) * c)
            gb = gb_ref[rows, :]
            lane = lax.broadcasted_iota(jnp.int32, gb.shape, 1)
            cum_col = _exact_prefix_sums(jnp.where(lane < LANES // 4, gb, 0.0), tri_col, rows=False)
            cum_row = _exact_prefix_sums(gt_ref[:, rows], tri_row, rows=True)
            for head in range(n_heads):
                g_idx = direction * n_heads + head
                b_idx = 2 * n_heads + g_idx
                cols = slice(head * d, (head + 1) * d)
                items.append(dict(
                    direction=direction, head=head, sub=sub, o_ref=o_ref, rows=rows, cols=cols, reverse=reverse,
                    earlier_eq=upper_eq if reverse else lower_eq,
                    earlier=upper if reverse else lower,
                    q=q_ref[rows, cols], k=k_ref[rows, cols], v=v_ref[rows, cols],
                    g_col=cum_col[:, g_idx:g_idx + 1], beta_col=gb[:, b_idx:b_idx + 1],
                    g_row=cum_row[g_idx:g_idx + 1, :],
                    g_last=cum_col[0:1, g_idx:g_idx + 1] if reverse else cum_col[c - 1:c, g_idx:g_idx + 1]))

    grams = [_mm_nt(jnp.concatenate([it["k"], it["q"]], axis=0), it["k"]) for it in items]
    neg_ls, rhss = [], []
    for it, gram in zip(items, grams):
        g_wide = jnp.broadcast_to(it["g_col"], (c, d))
        beta_wide = jnp.broadcast_to(it["beta_col"], (c, d))
        it["g_last_wide"] = jnp.broadcast_to(it["g_last"], (c, d))
        decay = jnp.exp(jnp.where(it["earlier_eq"], g_wide - it["g_row"], NEG_INF))
        neg_ls.append((jnp.where(it["earlier"], gram[:c] * decay, 0.0) * (-beta_wide)).astype(BF16))
        it["intra"] = jnp.where(it["earlier_eq"], gram[c:] * decay, 0.0).astype(BF16)
        kf = it["k"].astype(F32)
        e_g = jnp.exp(g_wide)
        rhss.append(jnp.concatenate([(it["v"].astype(F32) * beta_wide).astype(BF16),
                                     (kf * (beta_wide * e_g)).astype(BF16)], axis=-1))
        it["qg"] = (it["q"].astype(F32) * e_g).astype(BF16)
        it["k_dec"] = (kf * jnp.exp(it["g_last_wide"] - g_wide)).astype(BF16)
    invs = _unit_triangular_inverses(neg_ls, [it["reverse"] for it in items], same_block)
    for it, inv, rhs in zip(items, invs, rhss):
        sol = _mm(inv, rhs)
        it["u"], it["w"] = sol[:, :d], sol[:, d:].astype(BF16)

    for step in range(n_sub):
        live = [it for it in items if it["sub"] == (n_sub - 1 - step if it["reverse"] else step)]
        states = [state_ref[it["direction"], it["head"]] for it in live]
        wss = [_mm(jnp.concatenate([it["w"], it["qg"]], axis=0), state.astype(BF16))
               for it, state in zip(live, states)]
        v_news = [(it["u"] - ws[:c]).astype(BF16) for it, ws in zip(live, wss)]
        outs = [ws[c:] + _mm(it["intra"], v_new) for it, ws, v_new in zip(live, wss, v_news)]
        updates = [_mm_tn(it["k_dec"], v_new) for it, v_new in zip(live, v_news)]
        for it, out, state, update in zip(live, outs, states, updates):
            it["o_ref"][it["rows"], it["cols"]] = out.astype(BF16)
            state_ref[it["direction"], it["head"]] = state * jnp.exp(it["g_last_wide"]) + update


def _dn_core(q, k, v, gb, gbt, batch, seq, chunks_per_step=4):
    t, tok = q.shape
    n_heads = tok // DN_HEAD_DIM
    rows = chunks_per_step * DN_CHUNK
    assert DN_CHUNK == DN_HEAD_DIM == LANES and seq % rows == 0 and 4 * n_heads <= LANES // 4
    steps = seq // rows
    n_rows = gbt.shape[0]

    fwd = lambda b, i: (b * steps + i, 0)
    bwd = lambda b, i: (b * steps + steps - 1 - i, 0)
    fwd_t = lambda b, i: (0, b * steps + i)
    bwd_t = lambda b, i: (0, b * steps + steps - 1 - i)

    def side(tok_map, t_map):
        return [pl.BlockSpec((rows, tok), tok_map)] * 3 + [
            pl.BlockSpec((rows, LANES), tok_map), pl.BlockSpec((n_rows, rows), t_map)]

    return pl.pallas_call(
        functools.partial(_dn_core_kernel, n_heads=n_heads),
        grid=(batch, steps),
        in_specs=side(fwd, fwd_t) + side(bwd, bwd_t),
        out_specs=[pl.BlockSpec((rows, tok), fwd), pl.BlockSpec((rows, tok), bwd)],
        out_shape=[jax.ShapeDtypeStruct((t, tok), BF16)] * 2,
        scratch_shapes=[pltpu.VMEM((2, n_heads, DN_HEAD_DIM, DN_HEAD_DIM), F32)],
        compiler_params=_params("parallel", "arbitrary"),
        name="dn_core",
    )(q, k, v, gb, gbt, q, k, v, gb, gbt)


def _dn_tail_kernel(of_ref, ob_ref, kmem_ref, vmem_ref, x_ref, gpre_ref, wz_ref, wmem_ref, onorm_ref, w_ref,
                    g_ref, fpre_ref, wg_ref, wu_ref, wd_ref, fpost_ref, out_ref, *, row_blocks):
    tok = of_ref.shape[1]
    rows = x_ref.shape[0] // row_blocks
    for blk in range(row_blocks):
        sl = slice(blk * rows, (blk + 1) * rows)
        x = x_ref[sl, :]
        h = _rms(x, gpre_ref[...]).astype(BF16)
        z = jnp.dot(h, wz_ref[...], preferred_element_type=F32)
        q_mem = jnp.dot(h, wmem_ref[...], preferred_element_type=F32).astype(BF16)
        parts = []
        for c0 in range(0, tok, DN_HEAD_DIM):
            cols = slice(c0, c0 + DN_HEAD_DIM)
            o = of_ref[sl, cols].astype(F32) + ob_ref[sl, cols].astype(F32)
            o = o * lax.rsqrt(jnp.mean(o * o, axis=-1, keepdims=True) + EPS) * onorm_ref[...]
            parts.append((o * _silu(z[:, cols])).astype(BF16))
        mem_out = _memory_attention(q_mem, kmem_ref[0], vmem_ref[0])
        mixed = jnp.concatenate(parts + [mem_out], axis=-1)
        y = jnp.dot(mixed, w_ref[...], preferred_element_type=F32)
        x_mid = x + _rms(y, g_ref[...])
        out_ref[sl, :] = _ffn_rows(x_mid, fpre_ref, wg_ref, wu_ref, wd_ref, fpost_ref)


def _dn_tail(o_f, o_b, k_mem, v_mem, x2, g_pre, w_in, w_mem, out_norm, w_out, gain, ffn_pre, w_gate_up, w_down,
             ffn_post, seq, tm=512, row_blocks=2):
    t, d = x2.shape
    tok = o_f.shape[1]
    d_ff = w_down.shape[0]
    row = lambda i: (i, 0)
    per_seq = seq // tm
    kv_spec = pl.BlockSpec((1,) + k_mem.shape[1:], lambda i: (i // per_seq, 0, 0))
    return pl.pallas_call(
        functools.partial(_dn_tail_kernel, row_blocks=row_blocks),
        grid=(t // tm,),
        in_specs=[pl.BlockSpec((tm, tok), row)] * 2 + [kv_spec, kv_spec]
        + [pl.BlockSpec((tm, d), row), _const_spec((1, d)), pl.BlockSpec((d, tok), lambda i: (0, 3)),
           _const_spec(w_mem.shape), _const_spec((1, DN_HEAD_DIM)), _const_spec(w_out.shape),
           _const_spec((1, d)), _const_spec((1, d)),
           pl.BlockSpec((d, d_ff), lambda i: (0, 0)), pl.BlockSpec((d, d_ff), lambda i: (0, 1)),
           _const_spec(w_down.shape), _const_spec((1, d))],
        out_specs=pl.BlockSpec((tm, d), row),
        out_shape=jax.ShapeDtypeStruct((t, d), F32),
        compiler_params=_params("parallel"),
        name="dn_tail",
    )(o_f, o_b, k_mem, v_mem, x2, g_pre.reshape(1, d), w_in, w_mem, out_norm.reshape(1, DN_HEAD_DIM), w_out,
      gain.reshape(1, d), ffn_pre.reshape(1, d), w_gate_up, w_gate_up, w_down, ffn_post.reshape(1, d))


def _attention_layer(x2, mem, batch, seq, rel_bias, w_in, w_out, mem_gain, w_kv, g_pre, g_post):
    n_groups = len(DILATED_GROUPS)
    tok = n_groups * GROUP_WIDTH
    dils = tuple(dil for _, dil in DILATED_GROUPS)
    assert all(window // (2 * dil) == BAND_HALF for window, dil in DILATED_GROUPS)
    assert w_in.shape[1] == 3 * tok + MEM_HEADS * MEM_HEAD_DIM and rel_bias.shape == (REL_BUCKETS, tok // ATT_HEAD_DIM)
    assert all(seq % (dil * ATT_K_WINDOW) == 0 for dil in dils)
    w_qkv = w_in[:, :3 * tok].reshape(-1, 3, n_groups, GROUP_WIDTH).transpose(0, 2, 1, 3).astype(BF16)
    group_weights = [w_qkv[:, g].reshape(-1, 3 * GROUP_WIDTH) for g in range(n_groups)]
    *qkvs, q_mem = _att_proj(x2, g_pre, group_weights, w_in[:, 3 * tok:].astype(BF16), dils)
    os_, lses = [], []
    for group, dil in enumerate(dils):
        n_res = min(dil, 2)
        o, lse = _band_attn(qkvs[group], _band_bias(rel_bias, group, dil), batch, seq, dil, n_res,
                            q_tile=512 // n_res)
        os_.append(o)
        lses.append(lse)
    k_mem, v_mem = _mem_kv(mem, mem_gain, w_kv.astype(BF16))
    return _att_out(os_, lses, dils, q_mem, k_mem, v_mem, x2, w_out.astype(BF16), g_post, seq)


def _deltanet_layer(x2, mem, batch, seq, w_in, conv_w, a_log, dt_bias, out_norm, w_out, mem_gain, w_kv,
                    g_pre, g_post, ffn_pre, ffn_w_gate_up, ffn_w_down, ffn_post):
    d = x2.shape[1]
    n_heads = a_log.shape[1]
    tok = n_heads * DN_HEAD_DIM
    n_gate = 4 * n_heads
    w_gate = w_in[:, 4 * tok:4 * tok + n_gate].reshape(d, 2, 2, n_heads).transpose(0, 2, 1, 3).reshape(d, n_gate)
    w_gate = jnp.pad(w_gate, ((0, 0), (0, LANES - n_gate)))
    w_in_b = w_in.astype(BF16)
    pad_row = lambda p: jnp.pad(p.reshape(1, -1).astype(F32), ((0, 0), (0, LANES - 2 * n_heads)))
    q, k, v, gb, gbt = _dn_proj(x2, g_pre, w_in_b, w_gate.astype(BF16), conv_w.astype(F32), pad_row(a_log),
                                pad_row(dt_bias), batch, seq, tok, 2 * n_heads)
    o_f, o_b = _dn_core(q, k, v, gb, gbt, batch, seq)
    k_mem, v_mem = _mem_kv(mem, mem_gain, w_kv.astype(BF16))
    return _dn_tail(o_f, o_b, k_mem, v_mem, x2, g_pre, w_in_b, w_in_b[:, 4 * tok + n_gate:], out_norm,
                    w_out.astype(BF16), g_post, ffn_pre, ffn_w_gate_up.astype(BF16), ffn_w_down.astype(BF16),
                    ffn_post, seq)


def kernel(x, mem, rel_bias, att_w_in, att_w_out, dn_w_in, dn_conv, dn_a_log, dn_dt_bias, dn_out_norm,
           dn_w_out, mem_norm, mem_w_kv, norm_mix_pre, norm_mix_post, norm_ffn_pre, norm_ffn_post,
           ffn_w_gate_up, ffn_w_down):
    batch, seq, d = x.shape
    depth = norm_mix_pre.shape[0]
    n_mixers = 2
    x2 = x.reshape(batch * seq, d)
    for i in range(depth):
        j = i // n_mixers
        if i % n_mixers == 0:
            x2 = _attention_layer(x2, mem, batch, seq, rel_bias, att_w_in[j], att_w_out[j], mem_norm[i],
                                  mem_w_kv[i], norm_mix_pre[i], norm_mix_post[i])
            x2 = _ffn(x2, norm_ffn_pre[i], ffn_w_gate_up[i].astype(BF16), ffn_w_down[i].astype(BF16),
                      norm_ffn_post[i])
        else:
            x2 = _deltanet_layer(x2, mem, batch, seq, dn_w_in[j], dn_conv[j], dn_a_log[j], dn_dt_bias[j],
                                 dn_out_norm[j], dn_w_out[j], mem_norm[i], mem_w_kv[i], norm_mix_pre[i],
                                 norm_mix_post[i], norm_ffn_pre[i], ffn_w_gate_up[i], ffn_w_down[i],
                                 norm_ffn_post[i])
    return x2.reshape(batch, seq, d)
```

```python
import functools
import math

import jax
import jax.numpy as jnp
import numpy as np
from jax import lax
from jax.experimental import pallas as pl
from jax.experimental.pallas import tpu as pltpu

F32 = jnp.float32
BF16 = jnp.bfloat16

EPS = 1e-6
NEG_INF = -1e30

LANES = 128
F32_SUBLANES = 8
BF16_SUBLANES = 16
VMEM_LIMIT_BYTES = 56 * 1024 * 1024

DILATED_GROUPS = ((128, 1), (512, 4), (2048, 16))
ATT_HEAD_DIM = 64
HEADS_PER_GROUP = 4
GROUP_WIDTH = HEADS_PER_GROUP * ATT_HEAD_DIM
BAND_HALF = 64
REL_BUCKETS = 32
REL_MAX_DIST = 1024
MEM_HEADS = 4
MEM_HEAD_DIM = 64
DN_HEAD_DIM = 128
DN_CONV = 5
DN_CHUNK = 128
SOLVE_LEAF = 16
CONV_ROWS = 64
RING_SLOTS = 3

ATT_Q_BLOCK = 128
ATT_K_WINDOW = ATT_Q_BLOCK + 2 * BAND_HALF


def _params(*semantics):
    return pltpu.CompilerParams(dimension_semantics=semantics, vmem_limit_bytes=VMEM_LIMIT_BYTES)


def _rms(x, gain):
    return x * lax.rsqrt(jnp.mean(x * x, axis=-1, keepdims=True) + EPS) * gain


def _silu(x):
    return x / (1.0 + jnp.exp2(x * (-1.0 / math.log(2.0))))


def _const_spec(shape):
    return pl.BlockSpec(shape, lambda *_: (0,) * len(shape))


def _mem_kv_kernel(mem_ref, g_ref, w_ref, k_ref, v_ref):
    h = _rms(mem_ref[0], g_ref[...]).astype(BF16)
    kv = jnp.dot(h, w_ref[...], preferred_element_type=F32)
    width = k_ref.shape[-1]
    k_ref[0] = kv[:, :width].astype(BF16)
    v_ref[0] = kv[:, width:].astype(BF16)


def _mem_kv(mem, gain, w_kv):
    b, m, d = mem.shape
    width = w_kv.shape[1] // 2
    return pl.pallas_call(
        _mem_kv_kernel,
        grid=(b,),
        in_specs=[pl.BlockSpec((1, m, d), lambda i: (i, 0, 0)), _const_spec((1, d)), _const_spec(w_kv.shape)],
        out_specs=[pl.BlockSpec((1, m, width), lambda i: (i, 0, 0))] * 2,
        out_shape=[jax.ShapeDtypeStruct((b, m, width), BF16)] * 2,
        compiler_params=_params("parallel"),
        name="mem_kv",
    )(mem, gain.reshape(1, d), w_kv)


def _memory_attention(q, k, v):
    lane = lax.broadcasted_iota(jnp.int32, q.shape, 1)
    in_head = [(lane >= h * MEM_HEAD_DIM) & (lane < (h + 1) * MEM_HEAD_DIM) for h in range(MEM_HEADS)]
    q = q * jnp.asarray(MEM_HEAD_DIM ** -0.5, BF16)
    logits = [lax.dot_general(jnp.where(mask, q, jnp.zeros_like(q)), k, (((1,), (1,)), ((), ())),
                              preferred_element_type=F32) for mask in in_head]
    probs = []
    for s in logits:
        p = jnp.exp(s - jnp.max(s, axis=-1, keepdims=True))
        probs.append((p / jnp.sum(p, axis=-1, keepdims=True)).astype(BF16))
    pvs = [jnp.dot(p, v, preferred_element_type=F32) for p in probs]
    out = pvs[-1]
    for mask, pv in zip(in_head[:-1], pvs[:-1]):
        out = jnp.where(mask, pv, out)
    return out.astype(BF16)


def _memory_specs(q_mem, k_mem, seq, tm):
    per_seq = seq // tm
    kv_spec = pl.BlockSpec((1,) + k_mem.shape[1:], lambda i: (i // per_seq, 0, 0))
    return [pl.BlockSpec((tm, q_mem.shape[1]), lambda i: (i, 0)), kv_spec, kv_spec]


def _t5_bucket(rel):
    half = REL_BUCKETS // 2
    max_exact = half // 2
    n = np.abs(rel)
    large = max_exact + (np.log(np.maximum(n, 1) / max_exact) / math.log(REL_MAX_DIST / max_exact)
                         * (half - max_exact)).astype(np.int64)
    large = np.minimum(large, half - 1)
    return ((rel > 0) * half + np.where(n < max_exact, n, large)).astype(np.int32)


def _band_bias(rel_bias, group, dil):
    period = ATT_Q_BLOCK + ATT_K_WINDOW
    heads = slice(group * HEADS_PER_GROUP, (group + 1) * HEADS_PER_GROUP)
    tables = []
    for variant in range(3):
        rel = np.arange(period) - (ATT_Q_BLOCK - 1) - variant * BAND_HALF
        in_band = np.abs(rel) <= BAND_HALF
        bucket = _t5_bucket(np.clip(rel, -BAND_HALF, BAND_HALF) * dil)
        f = jnp.where(in_band[None, :], jnp.transpose(rel_bias[bucket][:, heads]).astype(F32), NEG_INF)
        flat = jnp.tile(f, (1, ATT_Q_BLOCK))[:, :ATT_Q_BLOCK * (period - 1)]
        skew = flat.reshape(HEADS_PER_GROUP, ATT_Q_BLOCK, period - 1)
        tables.append(skew[:, :, ATT_Q_BLOCK - 1:ATT_Q_BLOCK - 1 + ATT_K_WINDOW])
    return jnp.stack(tables)


def _band_attn_kernel(x_ref, bias_ref, o_ref, lse_ref, *, sub_len, n_res, q_tile):
    lane = lax.broadcasted_iota(jnp.int32, (ATT_Q_BLOCK, LANES), 1)
    low_half = lane < ATT_HEAD_DIM
    n_pairs = GROUP_WIDTH // LANES

    def tile(t, carry):
        blocks = []
        for res in range(n_res):
            for blk in range(q_tile // ATT_Q_BLOCK):
                i0 = pl.multiple_of(t * q_tile + blk * ATT_Q_BLOCK, ATT_Q_BLOCK)
                ws = jnp.clip(i0 - BAND_HALF, 0, sub_len - ATT_K_WINDOW)
                variant = (i0 - ws) // BAND_HALF
                ws = pl.multiple_of(ws, BAND_HALF)
                base = res * 3 * GROUP_WIDTH
                blocks.append(dict(
                    res=res, rows=pl.ds(i0, ATT_Q_BLOCK), variant=variant,
                    q=x_ref[0, pl.ds(i0, ATT_Q_BLOCK), base:base + GROUP_WIDTH]
                    * jnp.asarray(ATT_HEAD_DIM ** -0.5, BF16),
                    k=x_ref[0, pl.ds(ws, ATT_K_WINDOW), base + GROUP_WIDTH:base + 2 * GROUP_WIDTH],
                    v=x_ref[0, pl.ds(ws, ATT_K_WINDOW), base + 2 * GROUP_WIDTH:base + 3 * GROUP_WIDTH]))
        heads = [(b, pair, a) for b in blocks for pair in range(n_pairs) for a in range(2)]
        logits = []
        for b, pair, a in heads:
            cols = slice(pair * LANES, (pair + 1) * LANES)
            q_pair = b["q"][:, cols]
            keep = low_half if a == 0 else jnp.logical_not(low_half)
            qh = jnp.where(keep, q_pair, jnp.zeros_like(q_pair))
            logits.append(_mm_nt(qh, b["k"][:, cols]))
        probs, dens, lses = [], [], []
        for (b, pair, a), s in zip(heads, logits):
            s = s + bias_ref[b["variant"], 2 * pair + a]
            m = jnp.max(s, axis=-1, keepdims=True)
            p = jnp.exp(s - m)
            den = jnp.sum(p, axis=-1, keepdims=True)
            probs.append(p.astype(BF16))
            dens.append(den)
            lses.append(m + jnp.log(den))
        pvs = [_mm(p, b["v"][:, pair * LANES:(pair + 1) * LANES]) for (b, pair, a), p in zip(heads, probs)]
        for idx in range(0, len(heads), 2):
            b, pair, _ = heads[idx]
            cols = pl.ds(b["res"] * GROUP_WIDTH + pair * LANES, LANES)
            out = jnp.where(low_half, pvs[idx] / dens[idx], pvs[idx + 1] / dens[idx + 1])
            lse = jnp.where(low_half, jnp.broadcast_to(lses[idx], (ATT_Q_BLOCK, LANES)),
                            jnp.broadcast_to(lses[idx + 1], (ATT_Q_BLOCK, LANES)))
            o_ref[0, b["rows"], cols] = out.astype(o_ref.dtype)
            lse_ref[0, b["rows"], cols] = lse
        return carry

    lax.fori_loop(0, sub_len // q_tile, tile, 0)


def _band_attn(qkv, bias, batch, seq, dil, n_res, q_tile=256):
    rows, width = qkv.shape
    sub_len = seq // dil
    qkv3 = qkv.reshape(batch, sub_len, width)
    x_spec = pl.BlockSpec((1, sub_len, n_res * 3 * GROUP_WIDTH), lambda b, r: (b, 0, r))
    o_spec = pl.BlockSpec((1, sub_len, n_res * GROUP_WIDTH), lambda b, r: (b, 0, r))
    o, lse = pl.pallas_call(
        functools.partial(_band_attn_kernel, sub_len=sub_len, n_res=n_res, q_tile=q_tile),
        grid=(batch, dil // n_res),
        in_specs=[x_spec, _const_spec(bias.shape)],
        out_specs=[o_spec, o_spec],
        out_shape=[jax.ShapeDtypeStruct((batch, sub_len, dil * GROUP_WIDTH), BF16),
                   jax.ShapeDtypeStruct((batch, sub_len, dil * GROUP_WIDTH), F32)],
        compiler_params=_params("parallel", "parallel"),
        name=f"band_attn_d{dil}",
    )(qkv3, bias)
    return o.reshape(rows, dil * GROUP_WIDTH), lse.reshape(rows, dil * GROUP_WIDTH)


def _att_proj_kernel(x_ref, g_ref, *refs, dils):
    n = len(dils)
    w_refs, wmem_ref = refs[:n], refs[n]
    out_refs, qmem_ref, stage_ref = refs[n + 1:2 * n + 1], refs[2 * n + 1], refs[2 * n + 2]
    h = _rms(x_ref[...], g_ref[...]).astype(BF16)
    tm = h.shape[0]
    for w_ref, o_ref, dil in zip(w_refs, out_refs, dils):
        y = jnp.dot(h, w_ref[...], preferred_element_type=F32)
        width = y.shape[1]
        if dil == 1:
            o_ref[...] = y.astype(BF16)
        else:
            for c in range(width // LANES):
                stage_ref[c] = y[:, c * LANES:(c + 1) * LANES]
            for r in range(dil):
                for c in range(width // LANES):
                    lanes = pl.ds(r * width + c * LANES, LANES)
                    o_ref[:, lanes] = stage_ref[c, pl.ds(r, tm // dil, stride=dil), :].astype(BF16)
    qmem_ref[...] = jnp.dot(h, wmem_ref[...], preferred_element_type=F32).astype(BF16)


def _att_proj(x2, gain, group_weights, w_mem, dils, tm=512):
    t, d = x2.shape
    width = group_weights[0].shape[1]
    row = lambda i: (i, 0)
    return pl.pallas_call(
        functools.partial(_att_proj_kernel, dils=dils),
        grid=(t // tm,),
        in_specs=[pl.BlockSpec((tm, d), row), _const_spec((1, d))]
        + [_const_spec(w.shape) for w in group_weights] + [_const_spec(w_mem.shape)],
        out_specs=[pl.BlockSpec((tm // dil, dil * width), row) for dil in dils]
        + [pl.BlockSpec((tm, w_mem.shape[1]), row)],
        out_shape=[jax.ShapeDtypeStruct((t // dil, dil * width), BF16) for dil in dils]
        + [jax.ShapeDtypeStruct((t, w_mem.shape[1]), BF16)],
        scratch_shapes=[pltpu.VMEM((width // LANES, tm, LANES), F32)],
        compiler_params=_params("parallel"),
        name="att_proj",
    )(x2, gain.reshape(1, d), *group_weights, w_mem)


def _att_out_kernel(*refs, dils):
    n = len(dils)
    o_refs, l_refs = refs[:n], refs[n:2 * n]
    qmem_ref, kmem_ref, vmem_ref, x_ref, w_ref, g_ref, out_ref, stage_ref = refs[2 * n:]
    tm = x_ref.shape[0]

    def token_major(ref, dil, slot):
        if dil == 1:
            return ref[...].astype(F32)
        tiles = GROUP_WIDTH // LANES
        for r in range(dil):
            for c in range(tiles):
                lanes = pl.ds(r * GROUP_WIDTH + c * LANES, LANES)
                stage_ref[slot * tiles + c, pl.ds(r, tm // dil, stride=dil), :] = ref[:, lanes].astype(F32)
        return jnp.concatenate([stage_ref[slot * tiles + c] for c in range(tiles)], axis=-1)

    lses = [token_major(ref, dil, 2 * i) for i, (ref, dil) in enumerate(zip(l_refs, dils))]
    outs = [token_major(ref, dil, 2 * i + 1) for i, (ref, dil) in enumerate(zip(o_refs, dils))]
    mx = functools.reduce(jnp.maximum, lses)
    es = [jnp.exp(l - mx) for l in lses]
    tot = functools.reduce(lambda a, b: a + b, es)
    parts = [(o * (e / tot)).astype(BF16) for o, e in zip(outs, es)]
    mem_out = _memory_attention(qmem_ref[...], kmem_ref[0], vmem_ref[0])
    mixed = jnp.concatenate(parts + [mem_out], axis=-1)
    y = jnp.dot(mixed, w_ref[...], preferred_element_type=F32)
    out_ref[...] = x_ref[...] + _rms(y, g_ref[...])


def _att_out(os_, lses, dils, q_mem, k_mem, v_mem, x2, w_out, gain, seq, tm=1024):
    t, d = x2.shape
    row = lambda i: (i, 0)
    group_specs = [pl.BlockSpec((tm // dil, dil * GROUP_WIDTH), row) for dil in dils]
    return pl.pallas_call(
        functools.partial(_att_out_kernel, dils=dils),
        grid=(t // tm,),
        in_specs=group_specs + group_specs + _memory_specs(q_mem, k_mem, seq, tm)
        + [pl.BlockSpec((tm, d), row), _const_spec(w_out.shape), _const_spec((1, d))],
        out_specs=pl.BlockSpec((tm, d), row),
        out_shape=jax.ShapeDtypeStruct((t, d), F32),
        scratch_shapes=[pltpu.VMEM((2 * len(dils) * (GROUP_WIDTH // LANES), tm, LANES), F32)],
        compiler_params=_params("parallel"),
        name="att_out",
    )(*os_, *lses, q_mem, k_mem, v_mem, x2, w_out, gain.reshape(1, d))


def _ffn_kernel(x_ref, gpre_ref, wg_ref, wu_ref, wd_ref, gpost_ref, out_ref, *, row_blocks):
    rows = x_ref.shape[0] // row_blocks
    for blk in range(row_blocks):
        sl = slice(blk * rows, (blk + 1) * rows)
        x = x_ref[sl, :]
        h = _rms(x, gpre_ref[...]).astype(BF16)
        gate = jnp.dot(h, wg_ref[...], preferred_element_type=F32)
        up = jnp.dot(h, wu_ref[...], preferred_element_type=F32)
        act = (_silu(gate) * up).astype(BF16)
        y = jnp.dot(act, wd_ref[...], preferred_element_type=F32)
        out_ref[sl, :] = x + _rms(y, gpost_ref[...])


def _ffn(x2, g_pre, w_gate_up, w_down, g_post, tm=512, row_blocks=2):
    t, d = x2.shape
    d_ff = w_down.shape[0]
    row = lambda i: (i, 0)
    return pl.pallas_call(
        functools.partial(_ffn_kernel, row_blocks=row_blocks),
        grid=(t // tm,),
        in_specs=[pl.BlockSpec((tm, d), row), _const_spec((1, d)),
                  pl.BlockSpec((d, d_ff), lambda i: (0, 0)), pl.BlockSpec((d, d_ff), lambda i: (0, 1)),
                  _const_spec(w_down.shape), _const_spec((1, d))],
        out_specs=pl.BlockSpec((tm, d), row),
        out_shape=jax.ShapeDtypeStruct((t, d), F32),
        compiler_params=_params("parallel"),
        name="ffn",
    )(x2, g_pre.reshape(1, d), w_gate_up, w_gate_up, w_down, g_post.reshape(1, d))


def _dn_proj_kernel(x_hbm, xprev_ref, xnext_ref, g_ref, wqkv_ref, wgate_ref, conv_ref,
                    alog_ref, dtb_ref, q_ref, k_ref, v_ref, gb_ref, gbt_ref, ext_ref, ring_ref, ring_sem,
                    *, halo, n_dir_heads, per_seq):
    step = pl.program_id(0)
    n_steps = pl.num_programs(0)
    tm = ring_ref.shape[1]

    def tile_copy(tile):
        slot = tile % RING_SLOTS
        return pltpu.make_async_copy(x_hbm.at[pl.ds(tile * tm, tm), :], ring_ref.at[slot], ring_sem.at[slot])

    @pl.when(step == 0)
    def _():
        for tile in range(RING_SLOTS - 1):
            tile_copy(tile).start()

    @pl.when(step + RING_SLOTS - 1 < n_steps)
    def _():
        tile_copy(step + RING_SLOTS - 1).start()

    tile_copy(step).wait()
    x_ref = ring_ref.at[step % RING_SLOTS]
    i = step % per_seq
    last = per_seq - 1
    width = wqkv_ref.shape[1]
    tok = width // 3
    gain = g_ref[...]
    h = _rms(x_ref[...], gain).astype(BF16)
    h_prev = jnp.where(i > 0, _rms(xprev_ref[...], gain), 0.0).astype(BF16)
    h_next = jnp.where(i < last, _rms(xnext_ref[...], gain), 0.0).astype(BF16)
    h_ext = jnp.concatenate([h_prev, h, h_next], axis=0)
    chunk = 2 * LANES
    for c0 in range(0, width, chunk):
        y_ext = jnp.dot(h_ext, wqkv_ref[:, c0:c0 + chunk], preferred_element_type=F32)
        for half in range(chunk // LANES):
            c1 = c0 + half * LANES
            cols = slice(c1, c1 + LANES)
            ext = ext_ref.at[(c1 // LANES) % ext_ref.shape[0]]
            ext[...] = y_ext[:, half * LANES:(half + 1) * LANES]
            taps = [conv_ref[j:j + 1, cols] for j in range(DN_CONV)]
            which, col = divmod(c1, tok)
            out_ref = (q_ref, k_ref, v_ref)[which]
            for r0 in range(0, tm, CONV_ROWS):
                first = halo + r0 - DN_CONV // 2
                acc = ext[first:first + CONV_ROWS, :] * taps[0]
                for j in range(1, DN_CONV):
                    acc = acc + ext[first + j:first + j + CONV_ROWS, :] * taps[j]
                y = _silu(acc)
                if which < 2:
                    scale = lax.rsqrt(jnp.sum(y * y, axis=-1, keepdims=True) + EPS)
                    y = y * (scale * (DN_HEAD_DIM ** -0.5) if which == 0 else scale)
                out_ref[r0:r0 + CONV_ROWS, col:col + LANES] = y.astype(BF16)
    gate = jnp.dot(h, wgate_ref[...], preferred_element_type=F32)
    shifted = gate + dtb_ref[...]
    softplus = jnp.maximum(shifted, 0.0) + jnp.log(1.0 + jnp.exp(-jnp.abs(shifted)))
    decay = -jnp.exp(alog_ref[...]) * softplus
    beta = 1.0 / (1.0 + jnp.exp(-gate))
    lane = lax.broadcasted_iota(jnp.int32, gate.shape, 1)
    gb = jnp.where(lane < n_dir_heads, decay, beta)
    gb_ref[...] = gb
    gbt_ref[...] = jnp.transpose(gb)[:gbt_ref.shape[0], :]


def _dn_proj(x2, gain, w_in, w_gate, conv_w, a_log_row, dt_bias_row, batch, seq, tok, n_dir_heads,
             tm=512, halo=BF16_SUBLANES):
    t, d = x2.shape
    n_gate_rows = F32_SUBLANES * pl.cdiv(2 * n_dir_heads, F32_SUBLANES)
    per_seq = seq // tm
    blocks_per_tile = tm // halo
    n_halo_blocks = t // halo

    assert t // tm >= RING_SLOTS

    def cur_map(s):
        return (s, 0)

    def prev_map(s):
        return (jnp.maximum(s * blocks_per_tile - 1, 0), 0)

    def next_map(s):
        return (jnp.minimum((s + 1) * blocks_per_tile, n_halo_blocks - 1), 0)

    return pl.pallas_call(
        functools.partial(_dn_proj_kernel, halo=halo, n_dir_heads=n_dir_heads, per_seq=per_seq),
        grid=(t // tm,),
        in_specs=[pl.BlockSpec(memory_space=pl.ANY), pl.BlockSpec((halo, d), prev_map),
                  pl.BlockSpec((halo, d), next_map), _const_spec((1, d)),
                  pl.BlockSpec((d, 3 * tok), lambda s: (0, 0)), _const_spec(w_gate.shape),
                  _const_spec(conv_w.shape), _const_spec((1, LANES)), _const_spec((1, LANES))],
        out_specs=[pl.BlockSpec((tm, tok), cur_map)] * 3
        + [pl.BlockSpec((tm, LANES), cur_map), pl.BlockSpec((n_gate_rows, tm), lambda s: (0, s))],
        out_shape=[jax.ShapeDtypeStruct((t, tok), BF16)] * 3
        + [jax.ShapeDtypeStruct((t, LANES), F32), jax.ShapeDtypeStruct((n_gate_rows, t), F32)],
        scratch_shapes=[pltpu.VMEM((4, tm + 2 * halo, LANES), F32), pltpu.VMEM((RING_SLOTS, tm, d), F32),
                        pltpu.SemaphoreType.DMA((RING_SLOTS,))],
        compiler_params=_params("arbitrary"),
        name="dn_proj",
    )(x2, x2, x2, gain.reshape(1, d), w_in, w_gate, conv_w, a_log_row, dt_bias_row)


def _mm(a, b):
    return jnp.dot(a, b, preferred_element_type=F32)


def _mm_nt(a, b):
    return lax.dot_general(a, b, (((1,), (1,)), ((), ())), preferred_element_type=F32)


def _mm_tn(a, b):
    return lax.dot_general(a, b, (((0,), (0,)), ((), ())), preferred_element_type=F32)


def _row_blocks(x, size, parity):
    return jnp.concatenate([x[b * size:(b + 1) * size] for b in range(parity, x.shape[0] // size, 2)], axis=0)


def _interleave_row_blocks(even, odd, size):
    pieces = []
    for b in range(even.shape[0] // size):
        pieces += [even[b * size:(b + 1) * size], odd[b * size:(b + 1) * size]]
    return jnp.concatenate(pieces, axis=0)


def _unit_triangular_inverses(neg_ls, uppers, same_block):
    n = neg_ls[0].shape[0]
    ii = lax.broadcasted_iota(jnp.int32, (n, n), 0)
    jj = lax.broadcasted_iota(jnp.int32, (n, n), 1)
    identity = (ii == jj).astype(F32).astype(BF16)
    zero = jnp.zeros((n, n), BF16)
    plus_identity = lambda m: m + identity
    leaf = same_block[SOLVE_LEAF].astype(F32).astype(BF16)
    xbs = [nl * leaf for nl in neg_ls]
    invs = [plus_identity(xb) for xb in xbs]
    power = 1
    while 2 * power < SOLVE_LEAF:
        xbs = [_mm(xb, xb).astype(BF16) for xb in xbs]
        invs = [_mm(plus_identity(xb), inv).astype(BF16) for xb, inv in zip(xbs, invs)]
        power *= 2
    size = SOLVE_LEAF
    while size < n:
        merge = (same_block[2 * size] & jnp.logical_not(same_block[size])).astype(F32).astype(BF16)
        hot = [0 if upper else 1 for upper in uppers]
        offs = [_row_blocks(nl, size, p) * _row_blocks(merge, size, p) for nl, p in zip(neg_ls, hot)]
        rights = [_mm(off, inv).astype(BF16) for off, inv in zip(offs, invs)]
        half_zero = zero[:n // 2]
        rights = [_interleave_row_blocks(*((r, half_zero) if p == 0 else (half_zero, r)), size)
                  for r, p in zip(rights, hot)]
        moved = [_mm(_row_blocks(inv, size, p), plus_identity(right)).astype(BF16)
                 for inv, right, p in zip(invs, rights, hot)]
        invs = [_interleave_row_blocks(*((m, _row_blocks(inv, size, 1)) if p == 0 else
                                         (_row_blocks(inv, size, 0), m)), size)
                for inv, m, p in zip(invs, moved, hot)]
        size *= 2
    return invs


def _exact_prefix_sums(x, tri, rows):
    hi = x.astype(BF16).astype(F32)
    mid = (x - hi).astype(BF16).astype(F32)
    lo = (x - hi - mid).astype(BF16).astype(F32)
    if rows:
        n = x.shape[0]
        res = _mm(jnp.concatenate([hi, mid, lo], axis=0).astype(BF16), tri)
        return res[:n] + res[n:2 * n] + res[2 * n:]
    quarter = LANES // 4
    packed = hi + pltpu.roll(mid, quarter, 1) + pltpu.roll(lo, 2 * quarter, 1)
    res = _mm(tri, packed.astype(BF16))
    return res + pltpu.roll(res, LANES - quarter, 1) + pltpu.roll(res, LANES - 2 * quarter, 1)


def _dn_core_kernel(qf_ref, kf_ref, vf_ref, gbf_ref, gtf_ref, qb_ref, kb_ref, vb_ref, gbb_ref, gtb_ref,
                    of_ref, ob_ref, state_ref, *, n_heads):
    @pl.when(pl.program_id(1) == 0)
    def _():
        state_ref[...] = jnp.zeros_like(state_ref)

    c = DN_CHUNK
    d = DN_HEAD_DIM
    n_sub = qf_ref.shape[0] // c
    ii = lax.broadcasted_iota(jnp.int32, (c, c), 0)
    jj = lax.broadcasted_iota(jnp.int32, (c, c), 1)
    lower_eq, upper_eq = jj <= ii, jj >= ii
    lower, upper = jj < ii, jj > ii
    same_block = {}
    size = SOLVE_LEAF
    while size <= c:
        shift = int(math.log2(size))
        same_block[size] = (ii >> shift) == (jj >> shift)
        size *= 2

    items = []
    for direction, (q_ref, k_ref, v_ref, gb_ref, gt_ref, o_ref) in enumerate(
            ((qf_ref, kf_ref, vf_ref, gbf_ref, gtf_ref, of_ref),
             (qb_ref, kb_ref, vb_ref, gbb_ref, gtb_ref, ob_ref))):
        reverse = direction == 1
        tri_col = (upper_eq if reverse else lower_eq).astype(BF16)
        tri_row = (lower_eq if reverse else upper_eq).astype(BF16)
        for sub in range(n_sub):
            rows = slice(sub * c, (sub + 1) * c)
            gb = gb_ref[rows, :]
            lane = lax.broadcasted_iota(jnp.int32, gb.shape, 1)
            cum_col = _exact_prefix_sums(jnp.where(lane < LANES // 4, gb, 0.0), tri_col, rows=False)
            cum_row = _exact_prefix_sums(gt_ref[:, rows], tri_row, rows=True)
            for head in range(n_heads):
                g_idx = direction * n_heads + head
                b_idx = 2 * n_heads + g_idx
                cols = slice(head * d, (head + 1) * d)
                items.append(dict(
                    direction=direction, head=head, sub=sub, o_ref=o_ref, rows=rows, cols=cols, reverse=reverse,
                    earlier_eq=upper_eq if reverse else lower_eq,
                    earlier=upper if reverse else lower,
                    q=q_ref[rows, cols], k=k_ref[rows, cols], v=v_ref[rows, cols],
                    g_col=cum_col[:, g_idx:g_idx + 1], beta_col=gb[:, b_idx:b_idx + 1],
                    g_row=cum_row[g_idx:g_idx + 1, :],
                    g_last=cum_col[0:1, g_idx:g_idx + 1] if reverse else cum_col[c - 1:c, g_idx:g_idx + 1]))

    grams = [_mm_nt(jnp.concatenate([it["k"], it["q"]], axis=0), it["k"]) for it in items]
    neg_ls, rhss = [], []
    for it, gram in zip(items, grams):
        g_wide = jnp.broadcast_to(it["g_col"], (c, d))
        beta_wide = jnp.broadcast_to(it["beta_col"], (c, d))
        it["g_last_wide"] = jnp.broadcast_to(it["g_last"], (c, d))
        decay = jnp.exp(jnp.where(it["earlier_eq"], g_wide - it["g_row"], NEG_INF))
        neg_ls.append((jnp.where(it["earlier"], gram[:c] * decay, 0.0) * (-beta_wide)).astype(BF16))
        it["intra"] = jnp.where(it["earlier_eq"], gram[c:] * decay, 0.0).astype(BF16)
        kf = it["k"].astype(F32)
        e_g = jnp.exp(g_wide)
        rhss.append(jnp.concatenate([(it["v"].astype(F32) * beta_wide).astype(BF16),
                                     (kf * (beta_wide * e_g)).astype(BF16)], axis=-1))
        it["qg"] = (it["q"].astype(F32) * e_g).astype(BF16)
        it["k_dec"] = (kf * jnp.exp(it["g_last_wide"] - g_wide)).astype(BF16)
    invs = _unit_triangular_inverses(neg_ls, [it["reverse"] for it in items], same_block)
    for it, inv, rhs in zip(items, invs, rhss):
        sol = _mm(inv, rhs)
        it["u"], it["w"] = sol[:, :d], sol[:, d:].astype(BF16)

    for step in range(n_sub):
        live = [it for it in items if it["sub"] == (n_sub - 1 - step if it["reverse"] else step)]
        states = [state_ref[it["direction"], it["head"]] for it in live]
        wss = [_mm(jnp.concatenate([it["w"], it["qg"]], axis=0), state.astype(BF16))
               for it, state in zip(live, states)]
        v_news = [(it["u"] - ws[:c]).astype(BF16) for it, ws in zip(live, wss)]
        outs = [ws[c:] + _mm(it["intra"], v_new) for it, ws, v_new in zip(live, wss, v_news)]
        updates = [_mm_tn(it["k_dec"], v_new) for it, v_new in zip(live, v_news)]
        for it, out, state, update in zip(live, outs, states, updates):
            it["o_ref"][it["rows"], it["cols"]] = out.astype(BF16)
            state_ref[it["direction"], it["head"]] = state * jnp.exp(it["g_last_wide"]) + update


def _dn_core(q, k, v, gb, gbt, batch, seq, chunks_per_step=4):
    t, tok = q.shape
    n_heads = tok // DN_HEAD_DIM
    rows = chunks_per_step * DN_CHUNK
    assert DN_CHUNK == DN_HEAD_DIM == LANES and seq % rows == 0 and 4 * n_heads <= LANES // 4
    steps = seq // rows
    n_rows = gbt.shape[0]

    fwd = lambda b, i: (b * steps + i, 0)
    bwd = lambda b, i: (b * steps + steps - 1 - i, 0)
    fwd_t = lambda b, i: (0, b * steps + i)
    bwd_t = lambda b, i: (0, b * steps + steps - 1 - i)

    def side(tok_map, t_map):
        return [pl.BlockSpec((rows, tok), tok_map)] * 3 + [
            pl.BlockSpec((rows, LANES), tok_map), pl.BlockSpec((n_rows, rows), t_map)]

    return pl.pallas_call(
        functools.partial(_dn_core_kernel, n_heads=n_heads),
        grid=(batch, steps),
        in_specs=side(fwd, fwd_t) + side(bwd, bwd_t),
        out_specs=[pl.BlockSpec((rows, tok), fwd), pl.BlockSpec((rows, tok), bwd)],
        out_shape=[jax.ShapeDtypeStruct((t, tok), BF16)] * 2,
        scratch_shapes=[pltpu.VMEM((2, n_heads, DN_HEAD_DIM, DN_HEAD_DIM), F32)],
        compiler_params=_params("parallel", "arbitrary"),
        name="dn_core",
    )(q, k, v, gb, gbt, q, k, v, gb, gbt)


def _dn_out_kernel(of_ref, ob_ref, kmem_ref, vmem_ref, x_ref, gpre_ref, wz_ref, wmem_ref, onorm_ref, w_ref,
                   g_ref, out_ref):
    tok = of_ref.shape[1]
    x = x_ref[...]
    h = _rms(x, gpre_ref[...]).astype(BF16)
    z = jnp.dot(h, wz_ref[...], preferred_element_type=F32)
    q_mem = jnp.dot(h, wmem_ref[...], preferred_element_type=F32).astype(BF16)
    parts = []
    for c0 in range(0, tok, DN_HEAD_DIM):
        cols = slice(c0, c0 + DN_HEAD_DIM)
        o = of_ref[:, cols].astype(F32) + ob_ref[:, cols].astype(F32)
        o = o * lax.rsqrt(jnp.mean(o * o, axis=-1, keepdims=True) + EPS) * onorm_ref[...]
        parts.append((o * _silu(z[:, cols])).astype(BF16))
    mem_out = _memory_attention(q_mem, kmem_ref[0], vmem_ref[0])
    mixed = jnp.concatenate(parts + [mem_out], axis=-1)
    y = jnp.dot(mixed, w_ref[...], preferred_element_type=F32)
    out_ref[...] = x + _rms(y, g_ref[...])


def _dn_out(o_f, o_b, k_mem, v_mem, x2, g_pre, w_in, w_mem, out_norm, w_out, gain, seq, tm=1024):
    t, d = x2.shape
    tok = o_f.shape[1]
    row = lambda i: (i, 0)
    per_seq = seq // tm
    kv_spec = pl.BlockSpec((1,) + k_mem.shape[1:], lambda i: (i // per_seq, 0, 0))
    return pl.pallas_call(
        _dn_out_kernel,
        grid=(t // tm,),
        in_specs=[pl.BlockSpec((tm, tok), row)] * 2 + [kv_spec, kv_spec]
        + [pl.BlockSpec((tm, d), row), _const_spec((1, d)), pl.BlockSpec((d, tok), lambda i: (0, 3)),
           _const_spec(w_mem.shape), _const_spec((1, DN_HEAD_DIM)), _const_spec(w_out.shape),
           _const_spec((1, d))],
        out_specs=pl.BlockSpec((tm, d), row),
        out_shape=jax.ShapeDtypeStruct((t, d), F32),
        compiler_params=_params("parallel"),
        name="dn_out",
    )(o_f, o_b, k_mem, v_mem, x2, g_pre.reshape(1, d), w_in, w_mem, out_norm.reshape(1, DN_HEAD_DIM), w_out,
      gain.reshape(1, d))


def _attention_layer(x2, mem, batch, seq, rel_bias, w_in, w_out, mem_gain, w_kv, g_pre, g_post):
    n_groups = len(DILATED_GROUPS)
    tok = n_groups * GROUP_WIDTH
    dils = tuple(dil for _, dil in DILATED_GROUPS)
    assert all(window // (2 * dil) == BAND_HALF for window, dil in DILATED_GROUPS)
    assert w_in.shape[1] == 3 * tok + MEM_HEADS * MEM_HEAD_DIM and rel_bias.shape == (REL_BUCKETS, tok // ATT_HEAD_DIM)
    assert all(seq % (dil * ATT_K_WINDOW) == 0 for dil in dils)
    w_qkv = w_in[:, :3 * tok].reshape(-1, 3, n_groups, GROUP_WIDTH).transpose(0, 2, 1, 3).astype(BF16)
    group_weights = [w_qkv[:, g].reshape(-1, 3 * GROUP_WIDTH) for g in range(n_groups)]
    *qkvs, q_mem = _att_proj(x2, g_pre, group_weights, w_in[:, 3 * tok:].astype(BF16), dils)
    os_, lses = [], []
    for group, dil in enumerate(dils):
        n_res = min(dil, 2)
        o, lse = _band_attn(qkvs[group], _band_bias(rel_bias, group, dil), batch, seq, dil, n_res,
                            q_tile=512 // n_res)
        os_.append(o)
        lses.append(lse)
    k_mem, v_mem = _mem_kv(mem, mem_gain, w_kv.astype(BF16))
    return _att_out(os_, lses, dils, q_mem, k_mem, v_mem, x2, w_out.astype(BF16), g_post, seq)


def _deltanet_layer(x2, mem, batch, seq, w_in, conv_w, a_log, dt_bias, out_norm, w_out, mem_gain, w_kv,
                    g_pre, g_post):
    d = x2.shape[1]
    n_heads = a_log.shape[1]
    tok = n_heads * DN_HEAD_DIM
    n_gate = 4 * n_heads
    w_gate = w_in[:, 4 * tok:4 * tok + n_gate].reshape(d, 2, 2, n_heads).transpose(0, 2, 1, 3).reshape(d, n_gate)
    w_gate = jnp.pad(w_gate, ((0, 0), (0, LANES - n_gate)))
    w_in_b = w_in.astype(BF16)
    pad_row = lambda p: jnp.pad(p.reshape(1, -1).astype(F32), ((0, 0), (0, LANES - 2 * n_heads)))
    q, k, v, gb, gbt = _dn_proj(x2, g_pre, w_in_b, w_gate.astype(BF16), conv_w.astype(F32), pad_row(a_log),
                                pad_row(dt_bias), batch, seq, tok, 2 * n_heads)
    o_f, o_b = _dn_core(q, k, v, gb, gbt, batch, seq)
    k_mem, v_mem = _mem_kv(mem, mem_gain, w_kv.astype(BF16))
    return _dn_out(o_f, o_b, k_mem, v_mem, x2, g_pre, w_in_b, w_in_b[:, 4 * tok + n_gate:], out_norm,
                   w_out.astype(BF16), g_post, seq)


def kernel(x, mem, rel_bias, att_w_in, att_w_out, dn_w_in, dn_conv, dn_a_log, dn_dt_bias, dn_out_norm,
           dn_w_out, mem_norm, mem_w_kv, norm_mix_pre, norm_mix_post, norm_ffn_pre, norm_ffn_post,
           ffn_w_gate_up, ffn_w_down):
    batch, seq, d = x.shape
    depth = norm_mix_pre.shape[0]
    n_mixers = 2
    x2 = x.reshape(batch * seq, d)
    for i in range(depth):
        j = i // n_mixers
        if i % n_mixers == 0:
            x2 = _attention_layer(x2, mem, batch, seq, rel_bias, att_w_in[j], att_w_out[j], mem_norm[i],
                                  mem_w_kv[i], norm_mix_pre[i], norm_mix_post[i])
        else:
            x2 = _deltanet_layer(x2, mem, batch, seq, dn_w_in[j], dn_conv[j], dn_a_log[j], dn_dt_bias[j],
                                 dn_out_norm[j], dn_w_out[j], mem_norm[i], mem_w_kv[i], norm_mix_pre[i],
                                 norm_mix_post[i])
        x2 = _ffn(x2, norm_ffn_pre[i], ffn_w_gate_up[i].astype(BF16), ffn_w_down[i].astype(BF16),
                  norm_ffn_post[i])
    return x2.reshape(batch, seq, d)
```

```python
import functools
import math

import jax
import jax.numpy as jnp
import numpy as np
from jax import lax
from jax.experimental import pallas as pl
from jax.experimental.pallas import tpu as pltpu

F32 = jnp.float32
BF16 = jnp.bfloat16

EPS = 1e-6
NEG_INF = -1e30

LANES = 128
F32_SUBLANES = 8
BF16_SUBLANES = 16
VMEM_LIMIT_BYTES = 56 * 1024 * 1024

DILATED_GROUPS = ((128, 1), (512, 4), (2048, 16))
ATT_HEAD_DIM = 64
HEADS_PER_GROUP = 4
GROUP_WIDTH = HEADS_PER_GROUP * ATT_HEAD_DIM
BAND_HALF = 64
REL_BUCKETS = 32
REL_MAX_DIST = 1024
MEM_HEADS = 4
MEM_HEAD_DIM = 64
DN_HEAD_DIM = 128
DN_CONV = 5
DN_CHUNK = 128
SOLVE_LEAF = 16
CONV_ROWS = 64

ATT_Q_BLOCK = 128
ATT_K_WINDOW = ATT_Q_BLOCK + 2 * BAND_HALF


def _params(*semantics):
    return pltpu.CompilerParams(dimension_semantics=semantics, vmem_limit_bytes=VMEM_LIMIT_BYTES)


def _rms(x, gain):
    return x * lax.rsqrt(jnp.mean(x * x, axis=-1, keepdims=True) + EPS) * gain


def _silu(x):
    return x / (1.0 + jnp.exp2(x * (-1.0 / math.log(2.0))))


def _const_spec(shape):
    return pl.BlockSpec(shape, lambda *_: (0,) * len(shape))


def _mem_kv_kernel(mem_ref, g_ref, w_ref, k_ref, v_ref):
    h = _rms(mem_ref[0], g_ref[0]).astype(BF16)
    kv = jnp.dot(h, w_ref[0], preferred_element_type=F32)
    width = k_ref.shape[-1]
    k_ref[0, 0] = kv[:, :width].astype(BF16)
    v_ref[0, 0] = kv[:, width:].astype(BF16)


def _mem_kv(mem, gains, w_kv):
    b, m, d = mem.shape
    depth = w_kv.shape[0]
    width = w_kv.shape[2] // 2
    out_spec = pl.BlockSpec((1, 1, m, width), lambda layer, i: (layer, i, 0, 0))
    return pl.pallas_call(
        _mem_kv_kernel,
        grid=(depth, b),
        in_specs=[pl.BlockSpec((1, m, d), lambda layer, i: (i, 0, 0)),
                  pl.BlockSpec((1, 1, d), lambda layer, i: (layer, 0, 0)),
                  pl.BlockSpec((1, d, 2 * width), lambda layer, i: (layer, 0, 0))],
        out_specs=[out_spec, out_spec],
        out_shape=[jax.ShapeDtypeStruct((depth, b, m, width), BF16)] * 2,
        compiler_params=_params("parallel", "parallel"),
        name="mem_kv",
    )(mem, gains.reshape(depth, 1, d), w_kv)


def _memory_attention(q, k, v):
    lane = lax.broadcasted_iota(jnp.int32, q.shape, 1)
    in_head = [(lane >= h * MEM_HEAD_DIM) & (lane < (h + 1) * MEM_HEAD_DIM) for h in range(MEM_HEADS)]
    q = q * jnp.asarray(MEM_HEAD_DIM ** -0.5, BF16)
    logits = [lax.dot_general(jnp.where(mask, q, jnp.zeros_like(q)), k, (((1,), (1,)), ((), ())),
                              preferred_element_type=F32) for mask in in_head]
    probs = []
    for s in logits:
        p = jnp.exp(s - jnp.max(s, axis=-1, keepdims=True))
        probs.append((p / jnp.sum(p, axis=-1, keepdims=True)).astype(BF16))
    pvs = [jnp.dot(p, v, preferred_element_type=F32) for p in probs]
    out = pvs[-1]
    for mask, pv in zip(in_head[:-1], pvs[:-1]):
        out = jnp.where(mask, pv, out)
    return out.astype(BF16)


def _memory_specs(q_mem, k_mem, seq, tm):
    per_seq = seq // tm
    kv_spec = pl.BlockSpec((1,) + k_mem.shape[1:], lambda i: (i // per_seq, 0, 0))
    return [pl.BlockSpec((tm, q_mem.shape[1]), lambda i: (i, 0)), kv_spec, kv_spec]


def _t5_bucket(rel):
    half = REL_BUCKETS // 2
    max_exact = half // 2
    n = np.abs(rel)
    large = max_exact + (np.log(np.maximum(n, 1) / max_exact) / math.log(REL_MAX_DIST / max_exact)
                         * (half - max_exact)).astype(np.int64)
    large = np.minimum(large, half - 1)
    return ((rel > 0) * half + np.where(n < max_exact, n, large)).astype(np.int32)


def _band_bias(rel_bias, group, dil):
    period = ATT_Q_BLOCK + ATT_K_WINDOW
    heads = slice(group * HEADS_PER_GROUP, (group + 1) * HEADS_PER_GROUP)
    tables = []
    for variant in range(3):
        rel = np.arange(period) - (ATT_Q_BLOCK - 1) - variant * BAND_HALF
        in_band = np.abs(rel) <= BAND_HALF
        bucket = _t5_bucket(np.clip(rel, -BAND_HALF, BAND_HALF) * dil)
        f = jnp.where(in_band[None, :], jnp.transpose(rel_bias[bucket][:, heads]).astype(F32), NEG_INF)
        flat = jnp.tile(f, (1, ATT_Q_BLOCK))[:, :ATT_Q_BLOCK * (period - 1)]
        skew = flat.reshape(HEADS_PER_GROUP, ATT_Q_BLOCK, period - 1)
        tables.append(skew[:, :, ATT_Q_BLOCK - 1:ATT_Q_BLOCK - 1 + ATT_K_WINDOW])
    return jnp.stack(tables)


def _band_attn_kernel(x_ref, bias_ref, o_ref, lse_ref, *, sub_len, n_res, q_tile):
    lane = lax.broadcasted_iota(jnp.int32, (ATT_Q_BLOCK, LANES), 1)
    low_half = lane < ATT_HEAD_DIM
    n_pairs = GROUP_WIDTH // LANES

    def tile(t, carry):
        blocks = []
        for res in range(n_res):
            for blk in range(q_tile // ATT_Q_BLOCK):
                i0 = pl.multiple_of(t * q_tile + blk * ATT_Q_BLOCK, ATT_Q_BLOCK)
                ws = jnp.clip(i0 - BAND_HALF, 0, sub_len - ATT_K_WINDOW)
                variant = (i0 - ws) // BAND_HALF
                ws = pl.multiple_of(ws, BAND_HALF)
                base = res * 3 * GROUP_WIDTH
                blocks.append(dict(
                    res=res, rows=pl.ds(i0, ATT_Q_BLOCK), variant=variant,
                    q=x_ref[0, pl.ds(i0, ATT_Q_BLOCK), base:base + GROUP_WIDTH]
                    * jnp.asarray(ATT_HEAD_DIM ** -0.5, BF16),
                    k=x_ref[0, pl.ds(ws, ATT_K_WINDOW), base + GROUP_WIDTH:base + 2 * GROUP_WIDTH],
                    v=x_ref[0, pl.ds(ws, ATT_K_WINDOW), base + 2 * GROUP_WIDTH:base + 3 * GROUP_WIDTH]))
        heads = [(b, pair, a) for b in blocks for pair in range(n_pairs) for a in range(2)]
        logits = []
        for b, pair, a in heads:
            cols = slice(pair * LANES, (pair + 1) * LANES)
            q_pair = b["q"][:, cols]
            keep = low_half if a == 0 else jnp.logical_not(low_half)
            qh = jnp.where(keep, q_pair, jnp.zeros_like(q_pair))
            logits.append(_mm_nt(qh, b["k"][:, cols]))
        probs, dens, lses = [], [], []
        for (b, pair, a), s in zip(heads, logits):
            s = s + bias_ref[b["variant"], 2 * pair + a]
            m = jnp.max(s, axis=-1, keepdims=True)
            p = jnp.exp(s - m)
            den = jnp.sum(p, axis=-1, keepdims=True)
            probs.append(p.astype(BF16))
            dens.append(den)
            lses.append(m + jnp.log(den))
        pvs = [_mm(p, b["v"][:, pair * LANES:(pair + 1) * LANES]) for (b, pair, a), p in zip(heads, probs)]
        for idx in range(0, len(heads), 2):
            b, pair, _ = heads[idx]
            cols = pl.ds(b["res"] * GROUP_WIDTH + pair * LANES, LANES)
            out = jnp.where(low_half, pvs[idx] / dens[idx], pvs[idx + 1] / dens[idx + 1])
            lse = jnp.where(low_half, jnp.broadcast_to(lses[idx], (ATT_Q_BLOCK, LANES)),
                            jnp.broadcast_to(lses[idx + 1], (ATT_Q_BLOCK, LANES)))
            o_ref[0, b["rows"], cols] = out.astype(o_ref.dtype)
            lse_ref[0, b["rows"], cols] = lse
        return carry

    lax.fori_loop(0, sub_len // q_tile, tile, 0)


def _band_attn(qkv, bias, batch, seq, dil, n_res, q_tile=256):
    rows, width = qkv.shape
    sub_len = seq // dil
    qkv3 = qkv.reshape(batch, sub_len, width)
    x_spec = pl.BlockSpec((1, sub_len, n_res * 3 * GROUP_WIDTH), lambda b, r: (b, 0, r))
    o_spec = pl.BlockSpec((1, sub_len, n_res * GROUP_WIDTH), lambda b, r: (b, 0, r))
    o, lse = pl.pallas_call(
        functools.partial(_band_attn_kernel, sub_len=sub_len, n_res=n_res, q_tile=q_tile),
        grid=(batch, dil // n_res),
        in_specs=[x_spec, _const_spec(bias.shape)],
        out_specs=[o_spec, o_spec],
        out_shape=[jax.ShapeDtypeStruct((batch, sub_len, dil * GROUP_WIDTH), BF16),
                   jax.ShapeDtypeStruct((batch, sub_len, dil * GROUP_WIDTH), F32)],
        compiler_params=_params("parallel", "parallel"),
        name=f"band_attn_d{dil}",
    )(qkv3, bias)
    return o.reshape(rows, dil * GROUP_WIDTH), lse.reshape(rows, dil * GROUP_WIDTH)


def _att_proj_kernel(x_ref, g_ref, *refs, dils):
    n = len(dils)
    w_refs, wmem_ref = refs[:n], refs[n]
    out_refs, qmem_ref, stage_ref = refs[n + 1:2 * n + 1], refs[2 * n + 1], refs[2 * n + 2]
    h = _rms(x_ref[...], g_ref[...]).astype(BF16)
    tm = h.shape[0]
    for w_ref, o_ref, dil in zip(w_refs, out_refs, dils):
        y = jnp.dot(h, w_ref[...], preferred_element_type=F32)
        width = y.shape[1]
        if dil == 1:
            o_ref[...] = y.astype(BF16)
        else:
            for c in range(width // LANES):
                stage_ref[c] = y[:, c * LANES:(c + 1) * LANES]
            for r in range(dil):
                for c in range(width // LANES):
                    lanes = pl.ds(r * width + c * LANES, LANES)
                    o_ref[:, lanes] = stage_ref[c, pl.ds(r, tm // dil, stride=dil), :].astype(BF16)
    qmem_ref[...] = jnp.dot(h, wmem_ref[...], preferred_element_type=F32).astype(BF16)


def _att_proj(x2, gain, group_weights, w_mem, dils, tm=512):
    t, d = x2.shape
    width = group_weights[0].shape[1]
    row = lambda i: (i, 0)
    return pl.pallas_call(
        functools.partial(_att_proj_kernel, dils=dils),
        grid=(t // tm,),
        in_specs=[pl.BlockSpec((tm, d), row), _const_spec((1, d))]
        + [_const_spec(w.shape) for w in group_weights] + [_const_spec(w_mem.shape)],
        out_specs=[pl.BlockSpec((tm // dil, dil * width), row) for dil in dils]
        + [pl.BlockSpec((tm, w_mem.shape[1]), row)],
        out_shape=[jax.ShapeDtypeStruct((t // dil, dil * width), BF16) for dil in dils]
        + [jax.ShapeDtypeStruct((t, w_mem.shape[1]), BF16)],
        scratch_shapes=[pltpu.VMEM((width // LANES, tm, LANES), F32)],
        compiler_params=_params("parallel"),
        name="att_proj",
    )(x2, gain.reshape(1, d), *group_weights, w_mem)


def _att_out_kernel(*refs, dils):
    n = len(dils)
    o_refs, l_refs = refs[:n], refs[n:2 * n]
    qmem_ref, kmem_ref, vmem_ref, x_ref, w_ref, g_ref, out_ref, stage_ref = refs[2 * n:]
    tm = x_ref.shape[0]

    def token_major(ref, dil, slot):
        if dil == 1:
            return ref[...].astype(F32)
        tiles = GROUP_WIDTH // LANES
        for r in range(dil):
            for c in range(tiles):
                lanes = pl.ds(r * GROUP_WIDTH + c * LANES, LANES)
                stage_ref[slot * tiles + c, pl.ds(r, tm // dil, stride=dil), :] = ref[:, lanes].astype(F32)
        return jnp.concatenate([stage_ref[slot * tiles + c] for c in range(tiles)], axis=-1)

    lses = [token_major(ref, dil, 2 * i) for i, (ref, dil) in enumerate(zip(l_refs, dils))]
    outs = [token_major(ref, dil, 2 * i + 1) for i, (ref, dil) in enumerate(zip(o_refs, dils))]
    mx = functools.reduce(jnp.maximum, lses)
    es = [jnp.exp(l - mx) for l in lses]
    tot = functools.reduce(lambda a, b: a + b, es)
    parts = [(o * (e / tot)).astype(BF16) for o, e in zip(outs, es)]
    mem_out = _memory_attention(qmem_ref[...], kmem_ref[0], vmem_ref[0])
    mixed = jnp.concatenate(parts + [mem_out], axis=-1)
    y = jnp.dot(mixed, w_ref[...], preferred_element_type=F32)
    out_ref[...] = x_ref[...] + _rms(y, g_ref[...])


def _att_out(os_, lses, dils, q_mem, k_mem, v_mem, x2, w_out, gain, seq, tm=1024):
    t, d = x2.shape
    row = lambda i: (i, 0)
    group_specs = [pl.BlockSpec((tm // dil, dil * GROUP_WIDTH), row) for dil in dils]
    return pl.pallas_call(
        functools.partial(_att_out_kernel, dils=dils),
        grid=(t // tm,),
        in_specs=group_specs + group_specs + _memory_specs(q_mem, k_mem, seq, tm)
        + [pl.BlockSpec((tm, d), row), _const_spec(w_out.shape), _const_spec((1, d))],
        out_specs=pl.BlockSpec((tm, d), row),
        out_shape=jax.ShapeDtypeStruct((t, d), F32),
        scratch_shapes=[pltpu.VMEM((2 * len(dils) * (GROUP_WIDTH // LANES), tm, LANES), F32)],
        compiler_params=_params("parallel"),
        name="att_out",
    )(*os_, *lses, q_mem, k_mem, v_mem, x2, w_out, gain.reshape(1, d))


def _ffn_kernel(x_ref, gpre_ref, wg_ref, wu_ref, wd_ref, gpost_ref, out_ref, *, row_blocks):
    rows = x_ref.shape[0] // row_blocks
    for blk in range(row_blocks):
        sl = slice(blk * rows, (blk + 1) * rows)
        x = x_ref[sl, :]
        h = _rms(x, gpre_ref[...]).astype(BF16)
        gate = jnp.dot(h, wg_ref[...], preferred_element_type=F32)
        up = jnp.dot(h, wu_ref[...], preferred_element_type=F32)
        act = (_silu(gate) * up).astype(BF16)
        y = jnp.dot(act, wd_ref[...], preferred_element_type=F32)
        out_ref[sl, :] = x + _rms(y, gpost_ref[...])


def _ffn(x2, g_pre, w_gate_up, w_down, g_post, tm=512, row_blocks=2):
    t, d = x2.shape
    d_ff = w_down.shape[0]
    row = lambda i: (i, 0)
    return pl.pallas_call(
        functools.partial(_ffn_kernel, row_blocks=row_blocks),
        grid=(t // tm,),
        in_specs=[pl.BlockSpec((tm, d), row), _const_spec((1, d)),
                  pl.BlockSpec((d, d_ff), lambda i: (0, 0)), pl.BlockSpec((d, d_ff), lambda i: (0, 1)),
                  _const_spec(w_down.shape), _const_spec((1, d))],
        out_specs=pl.BlockSpec((tm, d), row),
        out_shape=jax.ShapeDtypeStruct((t, d), F32),
        compiler_params=_params("parallel"),
        name="ffn",
    )(x2, g_pre.reshape(1, d), w_gate_up, w_gate_up, w_down, g_post.reshape(1, d))


def _dn_proj_kernel(x_ref, xprev_ref, xnext_ref, g_ref, wqkv_ref, wgate_ref, conv_ref,
                    alog_ref, dtb_ref, q_ref, k_ref, v_ref, gb_ref, gbt_ref, ext_ref,
                    *, halo, n_dir_heads):
    i = pl.program_id(1)
    last = pl.num_programs(1) - 1
    tm = x_ref.shape[0]
    width = wqkv_ref.shape[1]
    tok = width // 3
    gain = g_ref[...]
    h = _rms(x_ref[...], gain).astype(BF16)
    h_prev = jnp.where(i > 0, _rms(xprev_ref[...], gain), 0.0).astype(BF16)
    h_next = jnp.where(i < last, _rms(xnext_ref[...], gain), 0.0).astype(BF16)
    h_ext = jnp.concatenate([h_prev, h, h_next], axis=0)
    chunk = 2 * LANES
    for c0 in range(0, width, chunk):
        y_ext = jnp.dot(h_ext, wqkv_ref[:, c0:c0 + chunk], preferred_element_type=F32)
        for half in range(chunk // LANES):
            c1 = c0 + half * LANES
            cols = slice(c1, c1 + LANES)
            ext = ext_ref.at[(c1 // LANES) % ext_ref.shape[0]]
            ext[...] = y_ext[:, half * LANES:(half + 1) * LANES]
            taps = [conv_ref[j:j + 1, cols] for j in range(DN_CONV)]
            which, col = divmod(c1, tok)
            out_ref = (q_ref, k_ref, v_ref)[which]
            for r0 in range(0, tm, CONV_ROWS):
                first = halo + r0 - DN_CONV // 2
                acc = ext[first:first + CONV_ROWS, :] * taps[0]
                for j in range(1, DN_CONV):
                    acc = acc + ext[first + j:first + j + CONV_ROWS, :] * taps[j]
                y = _silu(acc)
                if which < 2:
                    scale = lax.rsqrt(jnp.sum(y * y, axis=-1, keepdims=True) + EPS)
                    y = y * (scale * (DN_HEAD_DIM ** -0.5) if which == 0 else scale)
                out_ref[r0:r0 + CONV_ROWS, col:col + LANES] = y.astype(BF16)
    gate = jnp.dot(h, wgate_ref[...], preferred_element_type=F32)
    shifted = gate + dtb_ref[...]
    softplus = jnp.maximum(shifted, 0.0) + jnp.log(1.0 + jnp.exp(-jnp.abs(shifted)))
    decay = -jnp.exp(alog_ref[...]) * softplus
    beta = 1.0 / (1.0 + jnp.exp(-gate))
    lane = lax.broadcasted_iota(jnp.int32, gate.shape, 1)
    gb = jnp.where(lane < n_dir_heads, decay, beta)
    gb_ref[...] = gb
    gbt_ref[...] = jnp.transpose(gb)[:gbt_ref.shape[0], :]


def _dn_proj(x2, gain, w_in, w_gate, conv_w, a_log_row, dt_bias_row, batch, seq, tok, n_dir_heads,
             tm=512, halo=BF16_SUBLANES):
    t, d = x2.shape
    n_gate_rows = F32_SUBLANES * pl.cdiv(2 * n_dir_heads, F32_SUBLANES)
    per_seq = seq // tm
    blocks_per_tile = tm // halo
    n_halo_blocks = t // halo

    def cur_map(b, i):
        return (b * per_seq + i, 0)

    def prev_map(b, i):
        return (jnp.maximum((b * per_seq + i) * blocks_per_tile - 1, 0), 0)

    def next_map(b, i):
        return (jnp.minimum((b * per_seq + i + 1) * blocks_per_tile, n_halo_blocks - 1), 0)

    return pl.pallas_call(
        functools.partial(_dn_proj_kernel, halo=halo, n_dir_heads=n_dir_heads),
        grid=(batch, per_seq),
        in_specs=[pl.BlockSpec((tm, d), cur_map), pl.BlockSpec((halo, d), prev_map),
                  pl.BlockSpec((halo, d), next_map), _const_spec((1, d)),
                  pl.BlockSpec((d, 3 * tok), lambda b, i: (0, 0)), _const_spec(w_gate.shape),
                  _const_spec(conv_w.shape), _const_spec((1, LANES)), _const_spec((1, LANES))],
        out_specs=[pl.BlockSpec((tm, tok), cur_map)] * 3
        + [pl.BlockSpec((tm, LANES), cur_map), pl.BlockSpec((n_gate_rows, tm), lambda b, i: (0, b * per_seq + i))],
        out_shape=[jax.ShapeDtypeStruct((t, tok), BF16)] * 3
        + [jax.ShapeDtypeStruct((t, LANES), F32), jax.ShapeDtypeStruct((n_gate_rows, t), F32)],
        scratch_shapes=[pltpu.VMEM((4, tm + 2 * halo, LANES), F32)],
        compiler_params=_params("parallel", "arbitrary"),
        name="dn_proj",
    )(x2, x2, x2, gain.reshape(1, d), w_in, w_gate, conv_w, a_log_row, dt_bias_row)


def _mm(a, b):
    return jnp.dot(a, b, preferred_element_type=F32)


def _mm_nt(a, b):
    return lax.dot_general(a, b, (((1,), (1,)), ((), ())), preferred_element_type=F32)


def _mm_tn(a, b):
    return lax.dot_general(a, b, (((0,), (0,)), ((), ())), preferred_element_type=F32)


def _row_blocks(x, size, parity):
    return jnp.concatenate([x[b * size:(b + 1) * size] for b in range(parity, x.shape[0] // size, 2)], axis=0)


def _interleave_row_blocks(even, odd, size):
    pieces = []
    for b in range(even.shape[0] // size):
        pieces += [even[b * size:(b + 1) * size], odd[b * size:(b + 1) * size]]
    return jnp.concatenate(pieces, axis=0)


def _unit_triangular_inverses(neg_ls, uppers, same_block):
    n = neg_ls[0].shape[0]
    ii = lax.broadcasted_iota(jnp.int32, (n, n), 0)
    jj = lax.broadcasted_iota(jnp.int32, (n, n), 1)
    identity = (ii == jj).astype(F32).astype(BF16)
    zero = jnp.zeros((n, n), BF16)
    plus_identity = lambda m: m + identity
    leaf = same_block[SOLVE_LEAF].astype(F32).astype(BF16)
    xbs = [nl * leaf for nl in neg_ls]
    invs = [plus_identity(xb) for xb in xbs]
    power = 1
    while 2 * power < SOLVE_LEAF:
        xbs = [_mm(xb, xb).astype(BF16) for xb in xbs]
        invs = [_mm(plus_identity(xb), inv).astype(BF16) for xb, inv in zip(xbs, invs)]
        power *= 2
    size = SOLVE_LEAF
    while size < n:
        merge = (same_block[2 * size] & jnp.logical_not(same_block[size])).astype(F32).astype(BF16)
        hot = [0 if upper else 1 for upper in uppers]
        offs = [_row_blocks(nl, size, p) * _row_blocks(merge, size, p) for nl, p in zip(neg_ls, hot)]
        rights = [_mm(off, inv).astype(BF16) for off, inv in zip(offs, invs)]
        half_zero = zero[:n // 2]
        rights = [_interleave_row_blocks(*((r, half_zero) if p == 0 else (half_zero, r)), size)
                  for r, p in zip(rights, hot)]
        moved = [_mm(_row_blocks(inv, size, p), plus_identity(right)).astype(BF16)
                 for inv, right, p in zip(invs, rights, hot)]
        invs = [_interleave_row_blocks(*((m, _row_blocks(inv, size, 1)) if p == 0 else
                                         (_row_blocks(inv, size, 0), m)), size)
                for inv, m, p in zip(invs, moved, hot)]
        size *= 2
    return invs


def _exact_prefix_sums(x, tri, rows):
    hi = x.astype(BF16).astype(F32)
    mid = (x - hi).astype(BF16).astype(F32)
    lo = (x - hi - mid).astype(BF16).astype(F32)
    if rows:
        n = x.shape[0]
        res = _mm(jnp.concatenate([hi, mid, lo], axis=0).astype(BF16), tri)
        return res[:n] + res[n:2 * n] + res[2 * n:]
    quarter = LANES // 4
    packed = hi + pltpu.roll(mid, quarter, 1) + pltpu.roll(lo, 2 * quarter, 1)
    res = _mm(tri, packed.astype(BF16))
    return res + pltpu.roll(res, LANES - quarter, 1) + pltpu.roll(res, LANES - 2 * quarter, 1)


def _dn_core_kernel(qf_ref, kf_ref, vf_ref, gbf_ref, gtf_ref, qb_ref, kb_ref, vb_ref, gbb_ref, gtb_ref,
                    of_ref, ob_ref, state_ref, *, n_heads):
    @pl.when(pl.program_id(1) == 0)
    def _():
        state_ref[...] = jnp.zeros_like(state_ref)

    c = DN_CHUNK
    d = DN_HEAD_DIM
    n_sub = qf_ref.shape[0] // c
    ii = lax.broadcasted_iota(jnp.int32, (c, c), 0)
    jj = lax.broadcasted_iota(jnp.int32, (c, c), 1)
    lower_eq, upper_eq = jj <= ii, jj >= ii
    lower, upper = jj < ii, jj > ii
    same_block = {}
    size = SOLVE_LEAF
    while size <= c:
        shift = int(math.log2(size))
        same_block[size] = (ii >> shift) == (jj >> shift)
        size *= 2

    items = []
    for direction, (q_ref, k_ref, v_ref, gb_ref, gt_ref, o_ref) in enumerate(
            ((qf_ref, kf_ref, vf_ref, gbf_ref, gtf_ref, of_ref),
             (qb_ref, kb_ref, vb_ref, gbb_ref, gtb_ref, ob_ref))):
        reverse = direction == 1
        tri_col = (upper_eq if reverse else lower_eq).astype(BF16)
        tri_row = (lower_eq if reverse else upper_eq).astype(BF16)
        for sub in range(n_sub):
            rows = slice(sub * c, (sub + 1) * c)
            gb = gb_ref[rows, :]
            lane = lax.broadcasted_iota(jnp.int32, gb.shape, 1)
            cum_col = _exact_prefix_sums(jnp.where(lane < LANES // 4, gb, 0.0), tri_col, rows=False)
            cum_row = _exact_prefix_sums(gt_ref[:, rows], tri_row, rows=True)
            for head in range(n_heads):
                g_idx = direction * n_heads + head
                b_idx = 2 * n_heads + g_idx
                cols = slice(head * d, (head + 1) * d)
                items.append(dict(
                    direction=direction, head=head, sub=sub, o_ref=o_ref, rows=rows, cols=cols, reverse=reverse,
                    earlier_eq=upper_eq if reverse else lower_eq,
                    earlier=upper if reverse else lower,
                    q=q_ref[rows, cols], k=k_ref[rows, cols], v=v_ref[rows, cols],
                    g_col=cum_col[:, g_idx:g_idx + 1], beta_col=gb[:, b_idx:b_idx + 1],
                    g_row=cum_row[g_idx:g_idx + 1, :],
                    g_last=cum_col[0:1, g_idx:g_idx + 1] if reverse else cum_col[c - 1:c, g_idx:g_idx + 1]))

    grams = [_mm_nt(jnp.concatenate([it["k"], it["q"]], axis=0), it["k"]) for it in items]
    neg_ls, rhss = [], []
    for it, gram in zip(items, grams):
        g_wide = jnp.broadcast_to(it["g_col"], (c, d))
        beta_wide = jnp.broadcast_to(it["beta_col"], (c, d))
        it["g_last_wide"] = jnp.broadcast_to(it["g_last"], (c, d))
        decay = jnp.exp(jnp.where(it["earlier_eq"], g_wide - it["g_row"], NEG_INF))
        neg_ls.append((jnp.where(it["earlier"], gram[:c] * decay, 0.0) * (-beta_wide)).astype(BF16))
        it["intra"] = jnp.where(it["earlier_eq"], gram[c:] * decay, 0.0).astype(BF16)
        kf = it["k"].astype(F32)
        e_g = jnp.exp(g_wide)
        rhss.append(jnp.concatenate([(it["v"].astype(F32) * beta_wide).astype(BF16),
                                     (kf * (beta_wide * e_g)).astype(BF16)], axis=-1))
        it["qg"] = (it["q"].astype(F32) * e_g).astype(BF16)
        it["k_dec"] = (kf * jnp.exp(it["g_last_wide"] - g_wide)).astype(BF16)
    invs = _unit_triangular_inverses(neg_ls, [it["reverse"] for it in items], same_block)
    for it, inv, rhs in zip(items, invs, rhss):
        sol = _mm(inv, rhs)
        it["u"], it["w"] = sol[:, :d], sol[:, d:].astype(BF16)

    for step in range(n_sub):
        live = [it for it in items if it["sub"] == (n_sub - 1 - step if it["reverse"] else step)]
        states = [state_ref[it["direction"], it["head"]] for it in live]
        wss = [_mm(jnp.concatenate([it["w"], it["qg"]], axis=0), state.astype(BF16))
               for it, state in zip(live, states)]
        v_news = [(it["u"] - ws[:c]).astype(BF16) for it, ws in zip(live, wss)]
        outs = [ws[c:] + _mm(it["intra"], v_new) for it, ws, v_new in zip(live, wss, v_news)]
        updates = [_mm_tn(it["k_dec"], v_new) for it, v_new in zip(live, v_news)]
        for it, out, state, update in zip(live, outs, states, updates):
            it["o_ref"][it["rows"], it["cols"]] = out.astype(BF16)
            state_ref[it["direction"], it["head"]] = state * jnp.exp(it["g_last_wide"]) + update


def _dn_core(q, k, v, gb, gbt, batch, seq, chunks_per_step=4):
    t, tok = q.shape
    n_heads = tok // DN_HEAD_DIM
    rows = chunks_per_step * DN_CHUNK
    assert DN_CHUNK == DN_HEAD_DIM == LANES and seq % rows == 0 and 4 * n_heads <= LANES // 4
    steps = seq // rows
    n_rows = gbt.shape[0]

    fwd = lambda b, i: (b * steps + i, 0)
    bwd = lambda b, i: (b * steps + steps - 1 - i, 0)
    fwd_t = lambda b, i: (0, b * steps + i)
    bwd_t = lambda b, i: (0, b * steps + steps - 1 - i)

    def side(tok_map, t_map):
        return [pl.BlockSpec((rows, tok), tok_map)] * 3 + [
            pl.BlockSpec((rows, LANES), tok_map), pl.BlockSpec((n_rows, rows), t_map)]

    return pl.pallas_call(
        functools.partial(_dn_core_kernel, n_heads=n_heads),
        grid=(batch, steps),
        in_specs=side(fwd, fwd_t) + side(bwd, bwd_t),
        out_specs=[pl.BlockSpec((rows, tok), fwd), pl.BlockSpec((rows, tok), bwd)],
        out_shape=[jax.ShapeDtypeStruct((t, tok), BF16)] * 2,
        scratch_shapes=[pltpu.VMEM((2, n_heads, DN_HEAD_DIM, DN_HEAD_DIM), F32)],
        compiler_params=_params("parallel", "arbitrary"),
        name="dn_core",
    )(q, k, v, gb, gbt, q, k, v, gb, gbt)


def _dn_out_kernel(of_ref, ob_ref, kmem_ref, vmem_ref, x_ref, gpre_ref, wz_ref, wmem_ref, onorm_ref, w_ref,
                   g_ref, out_ref):
    tok = of_ref.shape[1]
    x = x_ref[...]
    h = _rms(x, gpre_ref[...]).astype(BF16)
    z = jnp.dot(h, wz_ref[...], preferred_element_type=F32)
    q_mem = jnp.dot(h, wmem_ref[...], preferred_element_type=F32).astype(BF16)
    parts = []
    for c0 in range(0, tok, DN_HEAD_DIM):
        cols = slice(c0, c0 + DN_HEAD_DIM)
        o = of_ref[:, cols].astype(F32) + ob_ref[:, cols].astype(F32)
        o = o * lax.rsqrt(jnp.mean(o * o, axis=-1, keepdims=True) + EPS) * onorm_ref[...]
        parts.append((o * _silu(z[:, cols])).astype(BF16))
    mem_out = _memory_attention(q_mem, kmem_ref[0], vmem_ref[0])
    mixed = jnp.concatenate(parts + [mem_out], axis=-1)
    y = jnp.dot(mixed, w_ref[...], preferred_element_type=F32)
    out_ref[...] = x + _rms(y, g_ref[...])


def _dn_out(o_f, o_b, k_mem, v_mem, x2, g_pre, w_in, w_mem, out_norm, w_out, gain, seq, tm=1024):
    t, d = x2.shape
    tok = o_f.shape[1]
    row = lambda i: (i, 0)
    per_seq = seq // tm
    kv_spec = pl.BlockSpec((1,) + k_mem.shape[1:], lambda i: (i // per_seq, 0, 0))
    return pl.pallas_call(
        _dn_out_kernel,
        grid=(t // tm,),
        in_specs=[pl.BlockSpec((tm, tok), row)] * 2 + [kv_spec, kv_spec]
        + [pl.BlockSpec((tm, d), row), _const_spec((1, d)), pl.BlockSpec((d, tok), lambda i: (0, 3)),
           _const_spec(w_mem.shape), _const_spec((1, DN_HEAD_DIM)), _const_spec(w_out.shape),
           _const_spec((1, d))],
        out_specs=pl.BlockSpec((tm, d), row),
        out_shape=jax.ShapeDtypeStruct((t, d), F32),
        compiler_params=_params("parallel"),
        name="dn_out",
    )(o_f, o_b, k_mem, v_mem, x2, g_pre.reshape(1, d), w_in, w_mem, out_norm.reshape(1, DN_HEAD_DIM), w_out,
      gain.reshape(1, d))


def _attention_layer(x2, k_mem, v_mem, batch, seq, rel_bias, w_in, w_out, g_pre, g_post):
    n_groups = len(DILATED_GROUPS)
    tok = n_groups * GROUP_WIDTH
    dils = tuple(dil for _, dil in DILATED_GROUPS)
    assert all(window // (2 * dil) == BAND_HALF for window, dil in DILATED_GROUPS)
    assert w_in.shape[1] == 3 * tok + MEM_HEADS * MEM_HEAD_DIM and rel_bias.shape == (REL_BUCKETS, tok // ATT_HEAD_DIM)
    assert all(seq % (dil * ATT_K_WINDOW) == 0 for dil in dils)
    w_qkv = w_in[:, :3 * tok].reshape(-1, 3, n_groups, GROUP_WIDTH).transpose(0, 2, 1, 3).astype(BF16)
    group_weights = [w_qkv[:, g].reshape(-1, 3 * GROUP_WIDTH) for g in range(n_groups)]
    *qkvs, q_mem = _att_proj(x2, g_pre, group_weights, w_in[:, 3 * tok:].astype(BF16), dils)
    os_, lses = [], []
    for group, dil in enumerate(dils):
        n_res = min(dil, 2)
        o, lse = _band_attn(qkvs[group], _band_bias(rel_bias, group, dil), batch, seq, dil, n_res,
                            q_tile=512 // n_res)
        os_.append(o)
        lses.append(lse)
    return _att_out(os_, lses, dils, q_mem, k_mem, v_mem, x2, w_out.astype(BF16), g_post, seq)


def _deltanet_layer(x2, k_mem, v_mem, batch, seq, w_in, conv_w, a_log, dt_bias, out_norm, w_out,
                    g_pre, g_post):
    d = x2.shape[1]
    n_heads = a_log.shape[1]
    tok = n_heads * DN_HEAD_DIM
    n_gate = 4 * n_heads
    w_gate = w_in[:, 4 * tok:4 * tok + n_gate].reshape(d, 2, 2, n_heads).transpose(0, 2, 1, 3).reshape(d, n_gate)
    w_gate = jnp.pad(w_gate, ((0, 0), (0, LANES - n_gate)))
    w_in_b = w_in.astype(BF16)
    pad_row = lambda p: jnp.pad(p.reshape(1, -1).astype(F32), ((0, 0), (0, LANES - 2 * n_heads)))
    q, k, v, gb, gbt = _dn_proj(x2, g_pre, w_in_b, w_gate.astype(BF16), conv_w.astype(F32), pad_row(a_log),
                                pad_row(dt_bias), batch, seq, tok, 2 * n_heads)
    o_f, o_b = _dn_core(q, k, v, gb, gbt, batch, seq)
    return _dn_out(o_f, o_b, k_mem, v_mem, x2, g_pre, w_in_b, w_in_b[:, 4 * tok + n_gate:], out_norm,
                   w_out.astype(BF16), g_post, seq)


def kernel(x, mem, rel_bias, att_w_in, att_w_out, dn_w_in, dn_conv, dn_a_log, dn_dt_bias, dn_out_norm,
           dn_w_out, mem_norm, mem_w_kv, norm_mix_pre, norm_mix_post, norm_ffn_pre, norm_ffn_post,
           ffn_w_gate_up, ffn_w_down):
    batch, seq, d = x.shape
    depth = norm_mix_pre.shape[0]
    n_mixers = 2
    x2 = x.reshape(batch * seq, d)
    k_mem, v_mem = _mem_kv(mem, mem_norm, mem_w_kv.astype(BF16))
    for i in range(depth):
        j = i // n_mixers
        if i % n_mixers == 0:
            x2 = _attention_layer(x2, k_mem[i], v_mem[i], batch, seq, rel_bias, att_w_in[j], att_w_out[j],
                                  norm_mix_pre[i], norm_mix_post[i])
        else:
            x2 = _deltanet_layer(x2, k_mem[i], v_mem[i], batch, seq, dn_w_in[j], dn_conv[j], dn_a_log[j],
                                 dn_dt_bias[j], dn_out_norm[j], dn_w_out[j], norm_mix_pre[i], norm_mix_post[i])
        x2 = _ffn(x2, norm_ffn_pre[i], ffn_w_gate_up[i].astype(BF16), ffn_w_down[i].astype(BF16),
                  norm_ffn_post[i])
    return x2.reshape(batch, seq, d)
```
